```python
import math
import jax
import jax.numpy as jnp
from jax import lax
import numpy as np

D_MODEL = 1024
BATCH = 4
SEQ = 8192
DEPTH = 4

GRID_W = 64
CTX_LEN = 256
N_EVEN = (DEPTH + 1) // 2
N_ODD = DEPTH // 2
EPS = 1e-6
NEG_INF = -1e30
N_MOD = 6

HY_CH = D_MODEL // 2
HY_ORDER = 2
SHORT_CONV = 3
HY_BANDS = 16
HY_EMB = 1 + 2 * HY_BANDS
HY_FFN = 64
HY_WINDOW_SHIFT = 0.05
HY_DECAY_TARGET = 1e-2
HY_FAST_PCT = 0.3
HY_SLOW_PCT = 1.5
S5_CH = D_MODEL // 2
S5_GROUP = 16
S5_NG = S5_CH // S5_GROUP
S5_P = 64
S5_DT_MIN = 1e-3
S5_DT_MAX = 1e-1
EV_IN = (HY_ORDER + 1) * HY_CH + S5_CH
EV_MIX = HY_CH + S5_CH

MLA_HEADS = 8
MLA_NOPE = 64
MLA_ROPE = 32
MLA_V = 64
Q_LORA = 256
KV_LORA = 128
GQA_HEADS = 8
GQA_KV = 2
GQA_HD = 64
WINDOW = 128
BLK = 128
Q_BLK = 128
ROPE_BASE = 10000.0
OFF_KR = Q_LORA + KV_LORA
OFF_GQ = OFF_KR + MLA_ROPE
OFF_GK = OFF_GQ + GQA_HEADS * GQA_HD
OFF_GV = OFF_GK + GQA_KV * GQA_HD
OD_IN = OFF_GV + GQA_KV * GQA_HD
OD_MIX = MLA_HEADS * MLA_V + GQA_HEADS * GQA_HD
MLA_SCALE = (MLA_NOPE + MLA_ROPE) ** -0.5
GQA_SCALE = GQA_HD ** -0.5

D_FF = 2816
N_EXP = 8
TOP_K = 2
EXP_FF = 3584

kernel_name = 'hybrid_flow_backbone'


def rmsnorm(x, g):
    xf = x.astype(jnp.float32)
    y = xf * lax.rsqrt(jnp.mean(xf * xf, axis=-1, keepdims=True) + EPS)
    return (y * g.astype(jnp.float32)).astype(x.dtype)


def ada_norm(x, g, shift, scale):
    return rmsnorm(x, g) * (1 + scale) + shift


def swiglu(y, wg, wu, wd):
    return (jax.nn.silu(y @ wg) * (y @ wu)) @ wd


def moe_ffn(y, router, wg, wu, wd):
    logits = (y @ router).astype(jnp.float32)
    top_val, top_idx = lax.top_k(logits, TOP_K)
    weights = jax.nn.softmax(top_val, axis=-1)
    gates = jnp.sum(jax.nn.one_hot(top_idx, N_EXP, dtype=jnp.float32) * weights[..., None], axis=-2).astype(y.dtype)
    out = jnp.zeros_like(y)
    for e in range(N_EXP):
        out = out + gates[..., e:e + 1] * swiglu(y, wg[e], wu[e], wd[e])
    return out


def axial_rope_tables(L, dim, dtype):
    rows = L // GRID_W
    row = jnp.repeat(jnp.arange(rows), GRID_W).astype(jnp.float32)
    col = jnp.tile(jnp.arange(GRID_W), rows).astype(jnp.float32)
    nf = dim // 4
    inv = ROPE_BASE ** (-jnp.arange(nf, dtype=jnp.float32) / nf)
    ar = row[:, None] * inv[None, :]
    ac = col[:, None] * inv[None, :]
    return ((jnp.cos(ar).astype(dtype), jnp.sin(ar).astype(dtype)),
            (jnp.cos(ac).astype(dtype), jnp.sin(ac).astype(dtype)))


def _rotate(x, cos, sin):
    x1, x2 = jnp.split(x, 2, axis=-1)
    return jnp.concatenate([x1 * cos - x2 * sin, x2 * cos + x1 * sin], axis=-1)


def axial_rope(x, tables):
    (cr, sr), (cc, sc) = tables
    e = lambda t: t[None, :, None, :]
    xr, xc = jnp.split(x, 2, axis=-1)
    return jnp.concatenate([_rotate(xr, e(cr), e(sr)), _rotate(xc, e(cc), e(sc))], axis=-1)


def short_conv(u, w, b):
    L = u.shape[1]
    pad = SHORT_CONV // 2
    up = jnp.pad(u, ((0, 0), (pad, SHORT_CONV - 1 - pad), (0, 0)))
    out = b
    for j in range(SHORT_CONV):
        out = out + up[:, j:j + L] * w[j]
    return out


def hyena_filters(L, w1, b1, w2, b2, w3, freq, decay):
    f32 = jnp.float32
    t = jnp.arange(L, dtype=f32)
    t01 = t / L
    bands = jnp.linspace(1e-4, HY_BANDS - 1, HY_BANDS, dtype=f32)
    ang = (2.0 * math.pi / L) * t[:, None] * bands[None, :]
    feats = jnp.concatenate([t01[:, None], jnp.cos(ang), -jnp.sin(ang)], axis=-1)
    fr = freq.astype(f32)
    hid = jnp.sin(fr * (feats @ w1.astype(f32) + b1.astype(f32)))
    hid = jnp.sin(fr * (hid @ w2.astype(f32) + b2.astype(f32)))
    h = (hid @ w3.astype(f32)).reshape(L, 2, HY_ORDER, HY_CH)
    window = jnp.exp(-t01[:, None, None] * jnp.abs(decay.astype(f32))[None]) + HY_WINDOW_SHIFT
    h = h * window[:, None]
    fwd, bwd = h[:, 0], h[:, 1]
    k = jnp.concatenate([fwd, jnp.zeros((1, HY_ORDER, HY_CH), f32), bwd[:0:-1]], axis=0)
    return k / jnp.sum(jnp.abs(k), axis=0, keepdims=True)


def fft_long_conv(u, k):
    n = k.shape[0]
    L = u.shape[1]
    spec = jnp.fft.rfft(u, n=n, axis=1) * jnp.fft.rfft(k, axis=0)[None]
    return jnp.fft.irfft(spec, n=n, axis=1)[:, :L]


def hyena_sequence(z, conv_w, conv_b, w1, b1, w2, b2, w3, freq, decay, bias):
    L = z.shape[1]
    zc = short_conv(z, conv_w, conv_b).astype(jnp.float32)
    v, *gates = jnp.split(zc, HY_ORDER + 1, axis=-1)
    k = hyena_filters(L, w1, b1, w2, b2, w3, freq, decay)
    bias = bias.astype(jnp.float32)
    y = v
    for o, gate in enumerate(gates):
        y = gate * (fft_long_conv(y, k[:, o]) + y * bias[o])
    return y.astype(z.dtype)


def s5_discretize(a_re, a_im, log_dt, b_re, b_im):
    f32 = jnp.float32
    lam = lax.complex(jnp.minimum(a_re.astype(f32), -1e-4), a_im.astype(f32))
    dt = jnp.exp(log_dt.astype(f32))[:, None]
    abar = jnp.exp(lam * dt)
    bbar = ((abar - 1.0) / lam)[..., None] * lax.complex(b_re.astype(f32), b_im.astype(f32))
    return abar, bbar


def _linear_recurrence(e1, e2):
    a1, b1 = e1
    a2, b2 = e2
    return a1 * a2, a2 * b1 + b2


def s5_scan(abar, bu, s0):
    if s0 is not None:
        bu = bu.at[:, 0].add(abar * s0)
    a = jnp.broadcast_to(abar, (1, bu.shape[1]) + abar.shape)
    _, states = lax.associative_scan(_linear_recurrence, (a, bu), axis=1)
    return states


def s5_mixer(u_lat, u_ctx, a_re, a_im, log_dt, b_re, b_im, c_re, c_im, d_skip, w_glu, need_ctx):
    f32 = jnp.float32
    grouped = lambda u: u.astype(f32).reshape(u.shape[0], u.shape[1], S5_NG, S5_GROUP)
    ul, uc = grouped(u_lat), grouped(u_ctx)
    y_lat = jnp.zeros(ul.shape, f32)
    y_ctx = jnp.zeros(uc.shape, f32)
    for d in range(2):
        abar, bbar = s5_discretize(a_re[d], a_im[d], log_dt[d], b_re[d], b_im[d])
        cmat = lax.complex(c_re[d].astype(f32), c_im[d].astype(f32))
        orient = (lambda s: s) if d == 0 else (lambda s: jnp.flip(s, axis=1))
        st_c = s5_scan(abar, jnp.einsum('gpc,blgc->blgp', bbar, orient(uc)), None)
        st_l = s5_scan(abar, jnp.einsum('gpc,blgc->blgp', bbar, orient(ul)), st_c[:, -1])
        y_lat = y_lat + orient(jnp.real(jnp.einsum('gcp,blgp->blgc', cmat, st_l)))
        if need_ctx:
            y_ctx = y_ctx + orient(jnp.real(jnp.einsum('gcp,blgp->blgc', cmat, st_c)))
    dsk = d_skip.astype(f32).reshape(S5_NG, S5_GROUP)
    wg = w_glu.astype(f32)

    def readout(y, u):
        B, L = y.shape[:2]
        y = jax.nn.gelu((y + dsk * u).reshape(B, L, S5_CH))
        return y * jax.nn.sigmoid(y @ wg)

    out_l = readout(y_lat, ul).astype(u_lat.dtype)
    out_c = readout(y_ctx, uc).astype(u_ctx.dtype) if need_ctx else None
    return out_l, out_c


def even_mixer(yl, yc, w_in, conv_w, conv_b, hy_params, s5_params, w_out, need_ctx):
    hy_cols = (HY_ORDER + 1) * HY_CH
    zl = yl @ w_in
    zc = yc @ w_in
    hl = hyena_sequence(zl[..., :hy_cols], conv_w, conv_b, *hy_params)
    sl, sc = s5_mixer(zl[..., hy_cols:], zc[..., hy_cols:], *s5_params, need_ctx)
    out_l = jnp.concatenate([hl, sl], axis=-1) @ w_out
    out_c = None
    if need_ctx:
        hc = hyena_sequence(zc[..., :hy_cols], conv_w, conv_b, *hy_params)
        out_c = jnp.concatenate([hc, sc], axis=-1) @ w_out
    return out_l, out_c


def attend(q, k, v, scale):
    s = jnp.einsum('bqhd,bkhd->bhqk', q, k).astype(jnp.float32) * scale
    p = jax.nn.softmax(s, axis=-1).astype(v.dtype)
    return jnp.einsum('bhqk,bkhd->bqhd', p, v)


def blocked_attend(q, k, v, scale):
    B, L, H, dk = q.shape
    nb = L // Q_BLK
    qb = q.reshape(B, nb, Q_BLK, H, dk).transpose(1, 0, 2, 3, 4)
    out = lax.map(lambda qi: attend(qi, k, v, scale), qb)
    return out.transpose(1, 0, 2, 3, 4).reshape(B, L, H, v.shape[-1])


def window_sink_attention(q, k, v, kc, vc, sink, scale):
    B, L, H, hd = q.shape
    KVH = k.shape[2]
    G = H // KVH
    nb = L // BLK
    f32 = jnp.float32
    qb = q.reshape(B, nb, BLK, KVH, G, hd)
    pad = ((0, 0), (BLK, BLK), (0, 0), (0, 0))
    kp = jnp.pad(k, pad).reshape(B, nb + 2, BLK, KVH, hd)
    vp = jnp.pad(v, pad).reshape(B, nb + 2, BLK, KVH, hd)
    kband = jnp.concatenate([kp[:, :-2], kp[:, 1:-1], kp[:, 2:]], axis=2)
    vband = jnp.concatenate([vp[:, :-2], vp[:, 1:-1], vp[:, 2:]], axis=2)
    qpos = jnp.arange(nb)[:, None] * BLK + jnp.arange(BLK)[None, :]
    kpos = jnp.arange(nb)[:, None] * BLK - BLK + jnp.arange(3 * BLK)[None, :]
    valid = ((jnp.abs(qpos[:, :, None] - kpos[:, None, :]) <= WINDOW)
             & (kpos[:, None, :] >= 0) & (kpos[:, None, :] < L))
    s_band = jnp.einsum('bnqkgd,bnjkd->bnkgqj', qb, kband).astype(f32) * scale
    s_band = jnp.where(valid[None, :, None, None], s_band, NEG_INF)
    s_ctx = jnp.einsum('bnqkgd,bckd->bnkgqc', qb, kc).astype(f32) * scale
    s_sink = jnp.broadcast_to(sink.astype(f32).reshape(KVH, G)[None, None, :, :, None, None], s_band.shape[:-1] + (1,))
    p = jax.nn.softmax(jnp.concatenate([s_band, s_ctx, s_sink], axis=-1), axis=-1)
    nband = 3 * BLK
    nctx = kc.shape[1]
    p_band = p[..., :nband].astype(v.dtype)
    p_ctx = p[..., nband:nband + nctx].astype(v.dtype)
    out = (jnp.einsum('bnkgqj,bnjkd->bnqkgd', p_band, vband)
           + jnp.einsum('bnkgqc,bckd->bnqkgd', p_ctx, vc))
    return out.reshape(B, L, H, hd)


def ctx_sink_attention(q, k, v, sink, scale):
    B, C, H, hd = q.shape
    KVH = k.shape[2]
    G = H // KVH
    f32 = jnp.float32
    qg = q.reshape(B, C, KVH, G, hd)
    s = jnp.einsum('bqkgd,bckd->bkgqc', qg, k).astype(f32) * scale
    s_sink = jnp.broadcast_to(sink.astype(f32).reshape(KVH, G)[None, :, :, None, None], s.shape[:-1] + (1,))
    p = jax.nn.softmax(jnp.concatenate([s, s_sink], axis=-1), axis=-1)[..., :-1].astype(v.dtype)
    return jnp.einsum('bkgqc,bckd->bqkgd', p, v).reshape(B, C, H, hd)


def odd_mixer(yl, yc, w_in, q_norm, w_uq, kv_norm, w_ukv, sink, w_out, rope_mla, rope_gqa, need_ctx):
    zl = yl @ w_in
    zc = yc @ w_in

    def mla_q(z):
        B, L = z.shape[:2]
        return (rmsnorm(z[..., :Q_LORA], q_norm) @ w_uq).reshape(B, L, MLA_HEADS, MLA_NOPE + MLA_ROPE)

    def mla_kv(z):
        B, L = z.shape[:2]
        kv = (rmsnorm(z[..., Q_LORA:OFF_KR], kv_norm) @ w_ukv).reshape(B, L, MLA_HEADS, MLA_NOPE + MLA_V)
        k_rope = z[..., OFF_KR:OFF_GQ].reshape(B, L, 1, MLA_ROPE)
        return kv[..., :MLA_NOPE], k_rope, kv[..., MLA_NOPE:]

    def mla_keys(k_nope, k_rope):
        return jnp.concatenate([k_nope, jnp.broadcast_to(k_rope, k_nope.shape[:3] + (MLA_ROPE,))], axis=-1)

    def gqa_q(z):
        B, L = z.shape[:2]
        return z[..., OFF_GQ:OFF_GK].reshape(B, L, GQA_HEADS, GQA_HD)

    def gqa_kv(z):
        B, L = z.shape[:2]
        return (z[..., OFF_GK:OFF_GV].reshape(B, L, GQA_KV, GQA_HD),
                z[..., OFF_GV:].reshape(B, L, GQA_KV, GQA_HD))

    B, L = yl.shape[:2]
    ql = mla_q(zl)
    ql = jnp.concatenate([ql[..., :MLA_NOPE], axial_rope(ql[..., MLA_NOPE:], rope_mla)], axis=-1)
    kl_nope, krl, vl = mla_kv(zl)
    kc_nope, krc, vc = mla_kv(zc)
    kl = mla_keys(kl_nope, axial_rope(krl, rope_mla))
    kc = mla_keys(kc_nope, krc)
    mla_l = blocked_attend(ql, jnp.concatenate([kc, kl], axis=1), jnp.concatenate([vc, vl], axis=1), MLA_SCALE)
    gkl, gvl = gqa_kv(zl)
    gkc, gvc = gqa_kv(zc)
    gql = axial_rope(gqa_q(zl), rope_gqa)
    gqa_l = window_sink_attention(gql, axial_rope(gkl, rope_gqa), gvl, gkc, gvc, sink, GQA_SCALE)
    out_l = jnp.concatenate([mla_l.reshape(B, L, -1), gqa_l.reshape(B, L, -1)], axis=-1) @ w_out
    out_c = None
    if need_ctx:
        C = yc.shape[1]
        mla_c = attend(mla_q(zc), kc, vc, MLA_SCALE)
        gqa_c = ctx_sink_attention(gqa_q(zc), gkc, gvc, sink, GQA_SCALE)
        out_c = jnp.concatenate([mla_c.reshape(yc.shape[0], C, -1), gqa_c.reshape(yc.shape[0], C, -1)], axis=-1) @ w_out
    return out_l, out_c


def setup_inputs(seed: int = 0) -> dict:
    key = jax.random.key(seed)
    ks = iter(jax.random.split(key, 64))
    f32 = jnp.float32
    D = D_MODEL

    def nrm(shape, scale=1.0):
        return jax.random.normal(next(ks), shape, f32) * scale

    def gain(shape):
        return 1.0 + nrm(shape, 0.02)

    decay0 = jnp.linspace(-math.log(HY_DECAY_TARGET) / HY_SLOW_PCT, -math.log(HY_DECAY_TARGET) / HY_FAST_PCT, HY_CH, dtype=f32)
    return {
        'x': nrm((BATCH, SEQ, D)),
        'c': nrm((BATCH, D)),
        'ctx': nrm((BATCH, CTX_LEN, D)),
        'c_ctx': nrm((D,)),
        'mod_w': nrm((DEPTH, D, N_MOD * D), 0.5 * D ** -0.5),
        'mod_b': nrm((DEPTH, N_MOD * D), 0.02),
        'norm_mix': gain((DEPTH, D)),
        'norm_ffn': gain((DEPTH, D)),
        'final_norm': gain((D,)),
        'ev_w_in': nrm((N_EVEN, D, EV_IN), D ** -0.5),
        'ev_conv_w': nrm((N_EVEN, SHORT_CONV, (HY_ORDER + 1) * HY_CH), SHORT_CONV ** -0.5),
        'ev_conv_b': nrm((N_EVEN, (HY_ORDER + 1) * HY_CH), 0.02),
        'hy_w1': nrm((N_EVEN, HY_EMB, HY_FFN), HY_EMB ** -0.5),
        'hy_b1': nrm((N_EVEN, HY_FFN), 0.1),
        'hy_w2': nrm((N_EVEN, HY_FFN, HY_FFN), HY_FFN ** -0.5),
        'hy_b2': nrm((N_EVEN, HY_FFN), 0.1),
        'hy_w3': nrm((N_EVEN, HY_FFN, 2 * HY_ORDER * HY_CH), HY_FFN ** -0.5),
        'hy_freq': 1.0 + nrm((N_EVEN, HY_FFN), 0.1),
        'hy_decay': decay0 + nrm((N_EVEN, HY_ORDER, HY_CH), 0.1),
        'hy_bias': nrm((N_EVEN, HY_ORDER, HY_CH)),
        's5_a_re': -0.5 + nrm((N_EVEN, 2, S5_NG, S5_P), 0.01),
        's5_a_im': math.pi * jnp.arange(S5_P, dtype=f32) + nrm((N_EVEN, 2, S5_NG, S5_P), 0.01),
        's5_log_dt': jax.random.uniform(next(ks), (N_EVEN, 2, S5_NG), f32, math.log(S5_DT_MIN), math.log(S5_DT_MAX)),
        's5_b_re': nrm((N_EVEN, 2, S5_NG, S5_P, S5_GROUP), (2 * S5_GROUP) ** -0.5),
        's5_b_im': nrm((N_EVEN, 2, S5_NG, S5_P, S5_GROUP), (2 * S5_GROUP) ** -0.5),
        's5_c_re': nrm((N_EVEN, 2, S5_NG, S5_GROUP, S5_P), (2 * S5_P) ** -0.5),
        's5_c_im': nrm((N_EVEN, 2, S5_NG, S5_GROUP, S5_P), (2 * S5_P) ** -0.5),
        's5_d': nrm((N_EVEN, S5_CH)),
        's5_w_glu': nrm((N_EVEN, S5_CH, S5_CH), S5_CH ** -0.5),
        'ev_w_out': nrm((N_EVEN, EV_MIX, D), EV_MIX ** -0.5),
        'ff_w_gate': nrm((N_EVEN, D, D_FF), D ** -0.5),
        'ff_w_up': nrm((N_EVEN, D, D_FF), D ** -0.5),
        'ff_w_down': nrm((N_EVEN, D_FF, D), D_FF ** -0.5),
        'od_w_in': nrm((N_ODD, D, OD_IN), D ** -0.5),
        'mla_q_norm': gain((N_ODD, Q_LORA)),
        'mla_w_uq': nrm((N_ODD, Q_LORA, MLA_HEADS * (MLA_NOPE + MLA_ROPE)), Q_LORA ** -0.5),
        'mla_kv_norm': gain((N_ODD, KV_LORA)),
        'mla_w_ukv': nrm((N_ODD, KV_LORA, MLA_HEADS * (MLA_NOPE + MLA_V)), KV_LORA ** -0.5),
        'gqa_sink': nrm((N_ODD, GQA_HEADS), 0.5),
        'od_w_out': nrm((N_ODD, OD_MIX, D), OD_MIX ** -0.5),
        'moe_router': nrm((N_ODD, D, N_EXP), D ** -0.5),
        'moe_w_gate': nrm((N_ODD, N_EXP, D, EXP_FF), D ** -0.5),
        'moe_w_up': nrm((N_ODD, N_EXP, D, EXP_FF), D ** -0.5),
        'moe_w_down': nrm((N_ODD, N_EXP, EXP_FF, D), EXP_FF ** -0.5),
    }


def reference(x, c, ctx, c_ctx, mod_w, mod_b, norm_mix, norm_ffn, final_norm,
              ev_w_in, ev_conv_w, ev_conv_b, hy_w1, hy_b1, hy_w2, hy_b2, hy_w3, hy_freq, hy_decay, hy_bias,
              s5_a_re, s5_a_im, s5_log_dt, s5_b_re, s5_b_im, s5_c_re, s5_c_im, s5_d, s5_w_glu, ev_w_out,
              ff_w_gate, ff_w_up, ff_w_down,
              od_w_in, mla_q_norm, mla_w_uq, mla_kv_norm, mla_w_ukv, gqa_sink, od_w_out,
              moe_router, moe_w_gate, moe_w_up, moe_w_down):
    L = x.shape[1]
    rope_mla = axial_rope_tables(L, MLA_ROPE, x.dtype)
    rope_gqa = axial_rope_tables(L, GQA_HD, x.dtype)
    s_lat = jax.nn.silu(c)
    s_ctx = jax.nn.silu(c_ctx)
    h, g = x, ctx
    for l in range(DEPTH):
        i = l // 2
        need_ctx = l < DEPTH - 1
        m_lat = jnp.split((s_lat @ mod_w[l] + mod_b[l])[:, None, :], N_MOD, axis=-1)
        m_ctx = jnp.split((s_ctx @ mod_w[l] + mod_b[l])[None, None, :], N_MOD, axis=-1)
        yl = ada_norm(h, norm_mix[l], m_lat[0], m_lat[1])
        yc = ada_norm(g, norm_mix[l], m_ctx[0], m_ctx[1])
        if l % 2 == 0:
            hy_params = (hy_w1[i], hy_b1[i], hy_w2[i], hy_b2[i], hy_w3[i], hy_freq[i], hy_decay[i], hy_bias[i])
            s5_params = (s5_a_re[i], s5_a_im[i], s5_log_dt[i], s5_b_re[i], s5_b_im[i],
                         s5_c_re[i], s5_c_im[i], s5_d[i], s5_w_glu[i])
            ol, oc = even_mixer(yl, yc, ev_w_in[i], ev_conv_w[i], ev_conv_b[i], hy_params, s5_params,
                                ev_w_out[i], need_ctx)

            def ffn(y):
                return swiglu(y, ff_w_gate[i], ff_w_up[i], ff_w_down[i])
        else:
            ol, oc = odd_mixer(yl, yc, od_w_in[i], mla_q_norm[i], mla_w_uq[i], mla_kv_norm[i], mla_w_ukv[i],
                               gqa_sink[i], od_w_out[i], rope_mla, rope_gqa, need_ctx)

            def ffn(y):
                return moe_ffn(y, moe_router[i], moe_w_gate[i], moe_w_up[i], moe_w_down[i])
        h = h + m_lat[2] * ol
        h = h + m_lat[5] * ffn(ada_norm(h, norm_ffn[l], m_lat[3], m_lat[4]))
        if need_ctx:
            g = g + m_ctx[2] * oc
            g = g + m_ctx[5] * ffn(ada_norm(g, norm_ffn[l], m_ctx[3], m_ctx[4]))
    return rmsnorm(h, final_norm)
```

```python
import functools
import math

import jax
import jax.numpy as jnp
from jax import lax
from jax.experimental import pallas as pl
from jax.experimental.pallas import tpu as pltpu

F32 = jnp.float32
BF16 = jnp.bfloat16
HI = lax.Precision.HIGHEST

D = 1024
DEPTH = 4
GRID_W = 64
EPS = 1e-6
NEG = -1e30
N_MOD = 6

HY_CH = 512
HY_ORDER = 2
HY_BANDS = 16
HY_EMB = 1 + 2 * HY_BANDS
HY_FFN = 64
HY_SHIFT = 0.05
HY_COLS = (HY_ORDER + 1) * HY_CH
S5_CH = 512
S5_GROUP = 16
S5_NG = S5_CH // S5_GROUP
S5_P = 64
S5_T = 16
S5_SEG = 8
EV_IN = HY_COLS + S5_CH

MLA_HEADS = 8
MLA_NOPE = 64
MLA_ROPE = 32
MLA_V = 64
Q_LORA = 256
KV_LORA = 128
GQA_HEADS = 8
GQA_KV = 2
GQA_HD = 64
WINDOW = 128
BLK = 128
ROPE_BASE = 10000.0
MLA_SCALE = (MLA_NOPE + MLA_ROPE) ** -0.5
GQA_SCALE = GQA_HD ** -0.5
HEAD_PAD = 128
OD_IN_PAD = 1280

D_FF = 2816
N_EXP = 8
EXP_FF = 3584

LANE = 128
FFT_N2 = 128
VMEM_LIMIT = 56 * 1024 * 1024


def _cparams(sem):
    return pltpu.CompilerParams(dimension_semantics=sem, vmem_limit_bytes=VMEM_LIMIT)


def _pick(n, cands):
    for c in cands:
        if n % c == 0:
            return c
    raise ValueError(f"no tile for {n} in {cands}")


def _mod_kernel(ct_ref, w_ref, b_ref, o_ref, *, nrows):
    c = ct_ref[...]
    s = c * jax.nn.sigmoid(c)
    w = w_ref[...]
    rows = [jnp.sum(w * s[:, r:r + 1], axis=0, keepdims=True) for r in range(nrows)]
    rows.append(jnp.zeros((8 - nrows, w.shape[1]), F32))
    o_ref[...] = jnp.concatenate(rows, axis=0) + b_ref[...]


def _modulations(cond_t, nrows, mod_w, mod_b):
    tn = 1536
    out = pl.pallas_call(
        functools.partial(_mod_kernel, nrows=nrows),
        grid=(DEPTH, N_MOD * D // tn),
        in_specs=[pl.BlockSpec((D, 8), lambda l, j: (0, 0)),
                  pl.BlockSpec((None, D, tn), lambda l, j: (l, 0, j)),
                  pl.BlockSpec((None, 1, tn), lambda l, j: (l, 0, j))],
        out_specs=pl.BlockSpec((None, 8, tn), lambda l, j: (l, 0, j)),
        out_shape=jax.ShapeDtypeStruct((DEPTH, 8, N_MOD * D), F32),
        compiler_params=_cparams(("arbitrary", "arbitrary")),
    )(cond_t, mod_w, mod_b.reshape(DEPTH, 1, N_MOD * D))
    return out.reshape(DEPTH, 8, N_MOD, D)


class _Tok:
    def __init__(self, B, L, CTX, tm):
        assert L % tm == 0 and (B * CTX) % tm == 0
        self.B, self.L, self.CTX, self.tm = B, L, CTX, tm
        self.n_lat = B * L // tm
        self.n_all = self.n_lat + B * CTX // tm
        self.T = B * (L + CTX)
        self.per_seq = L // tm

    def mod_row(self, i):
        return jnp.where(i < self.n_lat, i // self.per_seq, self.B)


def _ada_norm(x, gain, mod, shift_idx, scale_idx):
    y = x * lax.rsqrt(jnp.mean(x * x, axis=-1, keepdims=True) + EPS) * gain
    return y * (1.0 + mod[scale_idx:scale_idx + 1, :]) + mod[shift_idx:shift_idx + 1, :]


def _norm_mm_kernel(h_ref, mod_ref, g_ref, w_ref, o_ref, y_scr):
    @pl.when(pl.program_id(1) == 0)
    def _():
        y_scr[...] = _ada_norm(h_ref[...], g_ref[...], mod_ref[...], 0, 1).astype(BF16)

    o_ref[...] = jnp.dot(y_scr[...], w_ref[...], preferred_element_type=F32).astype(o_ref.dtype)


def _norm_matmul(tok, h, mods_l, gain, w, tn, out_dtype=F32):
    tm, n = tok.tm, w.shape[1]
    return pl.pallas_call(
        _norm_mm_kernel,
        grid=(tok.n_all, n // tn),
        in_specs=[pl.BlockSpec((tm, D), lambda i, j: (i, 0)),
                  pl.BlockSpec((None, N_MOD, D), lambda i, j: (tok.mod_row(i), 0, 0)),
                  pl.BlockSpec((1, D), lambda i, j: (0, 0)),
                  pl.BlockSpec((D, tn), lambda i, j: (0, j))],
        out_specs=pl.BlockSpec((tm, tn), lambda i, j: (i, j)),
        out_shape=jax.ShapeDtypeStruct((tok.T, n), out_dtype),
        scratch_shapes=[pltpu.VMEM((tm, D), BF16)],
        compiler_params=_cparams(("arbitrary", "arbitrary")),
    )(h, mods_l, gain.reshape(1, D), w)


def _short_conv_kernel(z_ref, zp_ref, zn_ref, w_ref, b_ref, v_ref, x1_ref, x2_ref, *, tm, n_lat, L, CTX):
    i = pl.program_id(0)
    is_lat = i < n_lat
    seqlen = jnp.where(is_lat, L, CTX)
    off = jnp.where(is_lat, i * tm, (i - n_lat) * tm)
    first = lax.rem(off, seqlen) == 0
    last = lax.rem(off + tm, seqlen) == 0
    z = z_ref[...]
    prev_row = jnp.where(first, 0.0, zp_ref[7:8, :])
    next_row = jnp.where(last, 0.0, zn_ref[0:1, :])
    rid = lax.broadcasted_iota(jnp.int32, z.shape, 0)
    zm1 = jnp.where(rid == 0, prev_row, pltpu.roll(z, 1, axis=0))
    zp1 = jnp.where(rid == tm - 1, next_row, pltpu.roll(z, tm - 1, axis=0))
    out = b_ref[...] + zm1 * w_ref[0:1, :] + z * w_ref[1:2, :] + zp1 * w_ref[2:3, :]
    v_ref[...] = out[:, :HY_CH]
    x1_ref[...] = out[:, HY_CH:2 * HY_CH]
    x2_ref[...] = out[:, 2 * HY_CH:]


def _short_conv(tok, z, conv_w, conv_b):
    tm = _pick(math.gcd(tok.L, tok.CTX), (256, 128))
    n_lat = tok.B * tok.L // tm
    n_all = tok.T // tm
    r8 = tm // 8
    kern = functools.partial(_short_conv_kernel, tm=tm, n_lat=n_lat, L=tok.L, CTX=tok.CTX)
    o = jax.ShapeDtypeStruct((tok.T, HY_CH), F32)
    return pl.pallas_call(
        kern,
        grid=(n_all,),
        in_specs=[pl.BlockSpec((tm, HY_COLS), lambda i: (i, 0)),
                  pl.BlockSpec((8, HY_COLS), lambda i: (jnp.maximum(i * r8 - 1, 0), 0)),
                  pl.BlockSpec((8, HY_COLS), lambda i: (jnp.minimum((i + 1) * r8, n_all * r8 - 1), 0)),
                  pl.BlockSpec((8, HY_COLS), lambda i: (0, 0)),
                  pl.BlockSpec((1, HY_COLS), lambda i: (0, 0))],
        out_specs=[pl.BlockSpec((tm, HY_CH), lambda i: (i, 0))] * 3,
        out_shape=[o, o, o],
        compiler_params=_cparams(("arbitrary",)),
    )(z, z, z, jnp.pad(conv_w, ((0, 8 - conv_w.shape[0]), (0, 0))), conv_b.reshape(1, HY_COLS))


def _filter_kernel(f_ref, w1_ref, b1_ref, w2_ref, b2_ref, w3_ref, fr_ref, dec_ref, k_ref, s_ref):
    @pl.when(pl.program_id(0) == 0)
    def _():
        s_ref[...] = jnp.zeros_like(s_ref)

    f = f_ref[...]
    fr = fr_ref[...]
    hid = jnp.sin(fr * (jnp.dot(f, w1_ref[...], precision=HI, preferred_element_type=F32) + b1_ref[...]))
    hid = jnp.sin(fr * (jnp.dot(hid, w2_ref[...], precision=HI, preferred_element_type=F32) + b2_ref[...]))
    h = jnp.dot(hid, w3_ref[...], precision=HI, preferred_element_type=F32)
    t01 = f[:, 0:1]
    valid = f[:, LANE - 1:LANE]
    k = h * (jnp.exp(-t01 * jnp.abs(dec_ref[...])) + HY_SHIFT) * valid
    k_ref[0] = k[:, :HY_CH]
    k_ref[1] = k[:, HY_CH:]
    s_ref[...] += jnp.sum(jnp.abs(k), axis=0, keepdims=True)


def _hyena_filters(L, n, w1, b1, w2, b2, w3, freq, decay):
    row = jnp.arange(n)
    fwd = row < L
    bwd = row > n - L
    t = jnp.where(fwd, row, n - row).astype(F32)
    t01 = t / L
    bands = jnp.linspace(1e-4, HY_BANDS - 1, HY_BANDS, dtype=F32)
    ang = (2.0 * math.pi / L) * t[:, None] * bands[None, :]
    valid = (fwd | bwd).astype(F32)
    feats = jnp.concatenate([t01[:, None], jnp.cos(ang), -jnp.sin(ang),
                             jnp.zeros((n, LANE - 1 - HY_EMB), F32), valid[:, None]], axis=-1)
    w1p = jnp.pad(w1, ((0, LANE - HY_EMB), (0, 0)))
    tr = _pick(L, (512, 256))
    nb_half = n // 2 // tr
    ncol = HY_ORDER * HY_CH
    k, ssum = pl.pallas_call(
        _filter_kernel,
        grid=(n // tr,),
        in_specs=[pl.BlockSpec((tr, LANE), lambda i: (i, 0)),
                  pl.BlockSpec((LANE, HY_FFN), lambda i: (0, 0)),
                  pl.BlockSpec((1, HY_FFN), lambda i: (0, 0)),
                  pl.BlockSpec((HY_FFN, HY_FFN), lambda i: (0, 0)),
                  pl.BlockSpec((1, HY_FFN), lambda i: (0, 0)),
                  pl.BlockSpec((HY_FFN, ncol), lambda i: (0, jnp.where(i < nb_half, 0, 1))),
                  pl.BlockSpec((1, HY_FFN), lambda i: (0, 0)),
                  pl.BlockSpec((1, ncol), lambda i: (0, 0))],
        out_specs=[pl.BlockSpec((HY_ORDER, tr, HY_CH), lambda i: (0, i, 0)),
                   pl.BlockSpec((1, ncol), lambda i: (0, 0))],
        out_shape=[jax.ShapeDtypeStruct((HY_ORDER, n, HY_CH), F32),
                   jax.ShapeDtypeStruct((1, ncol), F32)],
        compiler_params=_cparams(("arbitrary",)),
    )(feats, w1p, b1.reshape(1, -1), w2, b2.reshape(1, -1), w3, freq.reshape(1, -1), decay.reshape(1, ncol))
    return k, (1.0 / ssum).reshape(HY_ORDER, 1, HY_CH)


def _dft_tables(n1, r_in):
    n = n1 * FFT_N2
    k1 = jnp.arange(n1)
    a1 = (2.0 * math.pi / n1) * ((k1[:, None] * jnp.arange(r_in)[None, :]) % n1).astype(F32)
    f1 = jnp.concatenate([jnp.cos(a1), -jnp.sin(a1)], axis=0)
    f3 = jnp.concatenate([jnp.cos(a1).T, -jnp.sin(a1).T], axis=1) / n
    n2 = jnp.arange(FFT_N2)
    kk = k1[:, None, None] + n1 * n2[None, :, None]
    ang = (2.0 * math.pi / n) * ((kk * n2[None, None, :]) % n).astype(F32)
    gr, gi = jnp.cos(ang), -jnp.sin(ang)
    g = jnp.concatenate([jnp.concatenate([gr, -gi], axis=2), jnp.concatenate([gi, gr], axis=2)], axis=1)
    return f1.astype(BF16), f3.astype(BF16), g.astype(BF16), jnp.swapaxes(g, 1, 2).astype(BF16)


def _fft1_kernel(f_ref, x_ref, o_ref):
    o_ref[...] = jnp.dot(f_ref[...], x_ref[...].astype(BF16), preferred_element_type=F32).astype(o_ref.dtype)


def _fft_stage1(f1, x):
    nb, r_in, cols = x.shape
    m = f1.shape[0]
    tn = _pick(cols, (8192, 4096, 2048))
    return pl.pallas_call(
        _fft1_kernel,
        grid=(nb, cols // tn),
        in_specs=[pl.BlockSpec((m, r_in), lambda b, j: (0, 0)),
                  pl.BlockSpec((None, r_in, tn), lambda b, j: (b, 0, j))],
        out_specs=pl.BlockSpec((None, m, tn), lambda b, j: (b, 0, j)),
        out_shape=jax.ShapeDtypeStruct((nb, m, cols), BF16),
        compiler_params=_cparams(("arbitrary", "arbitrary")),
    )(f1, x)


def _fft_filt_kernel(a_ref, g_ref, s_ref, o_ref):
    c = a_ref.shape[-1]
    a = a_ref[...].reshape(2 * FFT_N2, c)
    o_ref[...] = jnp.dot(g_ref[...], a, preferred_element_type=F32) * s_ref[...]


def _fft_filter_spectrum(a, g, inv):
    no, _, n1, _, c = a.shape
    return pl.pallas_call(
        _fft_filt_kernel,
        grid=(n1, no),
        in_specs=[pl.BlockSpec((None, 2, None, FFT_N2, c), lambda k, o: (o, 0, k, 0, 0)),
                  pl.BlockSpec((None, 2 * FFT_N2, 2 * FFT_N2), lambda k, o: (k, 0, 0)),
                  pl.BlockSpec((None, 1, c), lambda k, o: (o, 0, 0))],
        out_specs=pl.BlockSpec((None, None, 2 * FFT_N2, c), lambda k, o: (o, k, 0, 0)),
        out_shape=jax.ShapeDtypeStruct((no, n1, 2 * FFT_N2, c), F32),
        compiler_params=_cparams(("arbitrary", "arbitrary")),
    )(a, g, inv)


def _fft_mid_kernel(a_ref, g_ref, gt_ref, kh_ref, o_ref):
    c = a_ref.shape[-1]
    a = a_ref[...].reshape(2 * FFT_N2, c)
    x = jnp.dot(g_ref[...], a, preferred_element_type=F32)
    xr, xi = x[:FFT_N2], x[FFT_N2:]
    kr, ki = kh_ref[:FFT_N2, :], kh_ref[FFT_N2:, :]
    y = jnp.concatenate([xr * kr - xi * ki, xr * ki + xi * kr], axis=0).astype(BF16)
    bm = jnp.dot(gt_ref[...], y, preferred_element_type=F32)
    o_ref[...] = bm.reshape(2, FFT_N2, c).astype(o_ref.dtype)


def _fft_mid(a, g, gt, khat):
    nb, _, n1, _, c = a.shape
    return pl.pallas_call(
        _fft_mid_kernel,
        grid=(n1, nb),
        in_specs=[pl.BlockSpec((None, 2, None, FFT_N2, c), lambda k, b: (b, 0, k, 0, 0)),
                  pl.BlockSpec((None, 2 * FFT_N2, 2 * FFT_N2), lambda k, b: (k, 0, 0)),
                  pl.BlockSpec((None, 2 * FFT_N2, 2 * FFT_N2), lambda k, b: (k, 0, 0)),
                  pl.BlockSpec((None, 2 * FFT_N2, c), lambda k, b: (k, 0, 0))],
        out_specs=pl.BlockSpec((None, 2, None, FFT_N2, c), lambda k, b: (b, 0, k, 0, 0)),
        out_shape=jax.ShapeDtypeStruct(a.shape, BF16),
        compiler_params=_cparams(("arbitrary", "arbitrary")),
    )(a, g, gt, khat)


def _fft3_kernel(f_ref, bm_ref, y_ref, gate_ref, bias_ref, o_ref):
    conv = jnp.dot(f_ref[...], bm_ref[...], preferred_element_type=F32)
    o_ref[...] = gate_ref[...] * (conv + y_ref[...] * bias_ref[...])


def _fft_stage3(f3, bm, y, gate, bias_row):
    nb, m, cols = bm.shape
    r = f3.shape[0]
    tn = _pick(cols, (8192, 4096, 2048))
    return pl.pallas_call(
        _fft3_kernel,
        grid=(nb, cols // tn),
        in_specs=[pl.BlockSpec((r, m), lambda b, j: (0, 0)),
                  pl.BlockSpec((None, m, tn), lambda b, j: (b, 0, j)),
                  pl.BlockSpec((None, r, tn), lambda b, j: (b, 0, j)),
                  pl.BlockSpec((None, r, tn), lambda b, j: (b, 0, j)),
                  pl.BlockSpec((1, tn), lambda b, j: (0, j))],
        out_specs=pl.BlockSpec((None, r, tn), lambda b, j: (b, 0, j)),
        out_shape=jax.ShapeDtypeStruct((nb, r, cols), F32),
        compiler_params=_cparams(("arbitrary", "arbitrary")),
    )(f3, bm, y, gate, bias_row)


def _hyena_sequence(v, x1, x2, hy, bias):
    B, L, C = v.shape
    r_valid = L // FFT_N2
    r_in = max(r_valid, 16)
    n1 = max(2 * r_valid, r_in)
    n = n1 * FFT_N2
    f1, f3, g, gt = _dft_tables(n1, r_in)
    f1k = _dft_tables(n1, n1)[0]
    k, inv = _hyena_filters(L, n, *hy)
    ak = _fft_stage1(f1k, k.reshape(HY_ORDER, n1, FFT_N2 * C))
    khat = _fft_filter_spectrum(ak.reshape(HY_ORDER, 2, n1, FFT_N2, C), g, inv)
    cols = FFT_N2 * C

    def view(a):
        a = a.reshape(B, r_valid, cols)
        return a if r_in == r_valid else jnp.pad(a, ((0, 0), (0, r_in - r_valid), (0, 0)))

    y = view(v)
    for o, gate in enumerate((view(x1), view(x2))):
        a = _fft_stage1(f1, y)
        bm = _fft_mid(a.reshape(B, 2, n1, FFT_N2, C), g, gt, khat[o])
        y = _fft_stage3(f3, bm.reshape(B, 2 * n1, cols), y, gate, jnp.tile(bias[o].astype(F32), FFT_N2)[None, :])
    return y[:, :r_valid].reshape(B, L, C)


def _s5_tables(a_re, a_im, log_dt, b_re, b_im, c_re, c_im, jj):
    lam = lax.complex(jnp.minimum(a_re.astype(F32), -1e-4), a_im.astype(F32))
    dt = jnp.exp(log_dt.astype(F32))[..., None]
    abar = jnp.exp(lam * dt)
    bbar = ((abar - 1.0) / lam)[..., None] * lax.complex(b_re.astype(F32), b_im.astype(F32))
    cmat = lax.complex(c_re.astype(F32), c_im.astype(F32))
    T = S5_T

    def powers(m):
        m = jnp.asarray(m, F32)
        return jnp.exp(lam * dt * m.reshape(m.shape + (1, 1, 1)))

    pw = powers(jnp.arange(T + 1))
    kt = jnp.real(jnp.einsum('dgop,tdgp,dgpi->tdgoi', cmat, pw[:T], bbar, precision=HI))
    tt = jnp.arange(T)
    lag = tt[None, :] - tt[:, None]
    w_intra = jnp.where((lag >= 0)[:, :, None, None, None, None],
                        kt[jnp.clip(lag, 0, T - 1)], 0.0)
    w_intra = w_intra.transpose(2, 3, 0, 5, 1, 4).reshape(2, S5_NG, T * S5_GROUP, T * S5_GROUP)
    wb = pw[T - 1 - tt][..., None] * bbar[None]
    wb = wb.transpose(1, 2, 0, 4, 3).reshape(2, S5_NG, T * S5_GROUP, S5_P)
    w_cat = jnp.concatenate([w_intra, jnp.real(wb), jnp.imag(wb)], axis=-1)
    cp = cmat[None] * pw[1:, :, :, None, :]
    cp = cp.transpose(1, 2, 4, 0, 3).reshape(2, S5_NG, S5_P, T * S5_GROUP)
    c_cat = jnp.concatenate([jnp.real(cp), -jnp.imag(cp)], axis=2)

    def coef(z):
        zr, zi = jnp.real(z), jnp.imag(z)
        return jnp.stack([jnp.concatenate([zr, zr], -1), jnp.concatenate([-zi, zi], -1)], axis=-2)

    step = coef(powers(jnp.array(T)))
    seg = coef(powers(jnp.array(T * jj)))
    ptab = coef(powers(T * jnp.arange(jj))).transpose(1, 2, 0, 3, 4)
    return w_cat.astype(BF16), c_cat.astype(BF16), jnp.concatenate([step, seg], axis=2), ptab


def _cmul(coef_a, coef_b, s):
    return coef_a * s + coef_b * pltpu.roll(s, S5_P, axis=1)


def _s5_kernel(u_ref, w_ref, c_ref, cf_ref, p_ref, y_ref, ds_scr, sp_scr, *, nb, jj):
    J = jj * S5_SEG
    o = jnp.dot(u_ref[...], w_ref[...], preferred_element_type=F32)
    y_ref[...] = o[:, :S5_T * S5_GROUP]
    ds_scr[...] = o[:, S5_T * S5_GROUP:]
    a1, a2 = cf_ref[0:1, :], cf_ref[1:2, :]
    g1, g2 = cf_ref[2:3, :], cf_ref[3:4, :]

    def local_step(j, states):
        new = []
        for b in range(nb):
            r = pl.multiple_of(b * J + j * S5_SEG, S5_SEG)
            sp_scr[pl.ds(r, S5_SEG), :] = states[b]
            new.append(_cmul(a1, a2, states[b]) + ds_scr[pl.ds(r, S5_SEG), :])
        return tuple(new)

    zero = jnp.zeros((S5_SEG, 2 * S5_P), F32)
    ends = lax.fori_loop(0, jj, local_step, (zero,) * nb)
    rid = lax.broadcasted_iota(jnp.int32, (S5_SEG, 2 * S5_P), 0)
    carries = []
    for b in range(nb):
        c = zero
        for _ in range(S5_SEG - 1):
            c = jnp.where(rid == 0, 0.0, pltpu.roll(ends[b] + _cmul(g1, g2, c), 1, axis=0))
        carries.append((c, pltpu.roll(c, S5_P, axis=1)))

    def fix_step(j, carry):
        p = p_ref[j]
        for b in range(nb):
            r = pl.multiple_of(b * J + j * S5_SEG, S5_SEG)
            c, cs = carries[b]
            sp_scr[pl.ds(r, S5_SEG), :] += p[0:1, :] * c + p[1:2, :] * cs
        return carry

    lax.fori_loop(0, jj, fix_step, 0)
    y_ref[...] += jnp.dot(sp_scr[...].astype(BF16), c_ref[...], preferred_element_type=F32)


def _s5_scan(u, w_cat, c_cat, coefs, ptab, nb, jj):
    rows = u.shape[2]
    tc = S5_T * S5_GROUP
    kern = functools.partial(_s5_kernel, nb=nb, jj=jj)
    return pl.pallas_call(
        kern,
        grid=(2, S5_NG),
        in_specs=[pl.BlockSpec((None, None, rows, tc), lambda d, g: (d, g, 0, 0)),
                  pl.BlockSpec((None, None, tc, tc + 2 * S5_P), lambda d, g: (d, g, 0, 0)),
                  pl.BlockSpec((None, None, 2 * S5_P, tc), lambda d, g: (d, g, 0, 0)),
                  pl.BlockSpec((None, None, 4, 2 * S5_P), lambda d, g: (d, g, 0, 0)),
                  pl.BlockSpec((None, None, jj, 2, 2 * S5_P), lambda d, g: (d, g, 0, 0, 0))],
        out_specs=pl.BlockSpec((None, None, rows, tc), lambda d, g: (d, g, 0, 0)),
        out_shape=jax.ShapeDtypeStruct((2, S5_NG, rows, tc), F32),
        scratch_shapes=[pltpu.VMEM((rows, 2 * S5_P), F32), pltpu.VMEM((rows, 2 * S5_P), F32)],
        compiler_params=_cparams(("arbitrary", "arbitrary")),
    )(u, w_cat, c_cat, coefs, ptab)


def _s5_mixer(u_lat, u_ctx, s5):
    B, L, _ = u_lat.shape
    CTX = u_ctx.shape[1]
    S = L + CTX
    J = S // S5_T
    jj = J // S5_SEG
    w_cat, c_cat, coefs, ptab = _s5_tables(*s5, jj)
    fwd = jnp.concatenate([u_ctx, u_lat], axis=1)
    bwd = jnp.concatenate([u_ctx[:, ::-1], u_lat[:, ::-1]], axis=1)
    u = jnp.stack([fwd, bwd]).astype(BF16)
    u = u.reshape(2, B, S5_SEG, jj, S5_T, S5_NG, S5_GROUP).transpose(0, 5, 1, 3, 2, 4, 6)
    y = _s5_scan(u.reshape(2, S5_NG, B * J, S5_T * S5_GROUP), w_cat, c_cat, coefs, ptab, B, jj)
    y = y.reshape(2, S5_NG, B, jj, S5_SEG, S5_T, S5_GROUP).transpose(0, 2, 4, 3, 5, 1, 6).reshape(2, B, S, S5_CH)
    y_lat = y[0, :, CTX:] + y[1, :, CTX:][:, ::-1]
    y_ctx = y[0, :, :CTX] + y[1, :, :CTX][:, ::-1]
    return y_lat, y_ctx


def _gelu_tanh(x):
    return 0.5 * x * (1.0 + jnp.tanh(math.sqrt(2.0 / math.pi) * (x + 0.044715 * (x * x * x))))


def _even_out_kernel(hl_ref, ys_ref, u_ref, h_ref, mod_ref, dsk_ref, wg_ref, wo_ref, o_ref):
    y = _gelu_tanh(ys_ref[...] + dsk_ref[...] * u_ref[...])
    s = y * jax.nn.sigmoid(jnp.dot(y.astype(BF16), wg_ref[...], preferred_element_type=F32))
    ol = (jnp.dot(hl_ref[...].astype(BF16), wo_ref[:HY_CH, :], preferred_element_type=F32)
          + jnp.dot(s.astype(BF16), wo_ref[HY_CH:, :], preferred_element_type=F32))
    o_ref[...] = h_ref[...] + mod_ref[2:3, :] * ol


def _even_out(tok, hl, ys, z, h, mods_l, dsk, w_glu, w_out):
    tm = tok.tm
    return pl.pallas_call(
        _even_out_kernel,
        grid=(tok.n_all,),
        in_specs=[pl.BlockSpec((tm, HY_CH), lambda i: (i, 0)),
                  pl.BlockSpec((tm, S5_CH), lambda i: (i, 0)),
                  pl.BlockSpec((tm, S5_CH), lambda i: (i, HY_COLS // S5_CH)),
                  pl.BlockSpec((tm, D), lambda i: (i, 0)),
                  pl.BlockSpec((None, N_MOD, D), lambda i: (tok.mod_row(i), 0, 0)),
                  pl.BlockSpec((1, S5_CH), lambda i: (0, 0)),
                  pl.BlockSpec((S5_CH, S5_CH), lambda i: (0, 0)),
                  pl.BlockSpec((D, D), lambda i: (0, 0))],
        out_specs=pl.BlockSpec((tm, D), lambda i: (i, 0)),
        out_shape=jax.ShapeDtypeStruct((tok.T, D), F32),
        compiler_params=_cparams(("arbitrary",)),
    )(hl, ys, z, h, mods_l, dsk.reshape(1, S5_CH), w_glu, w_out)


def _ffn_kernel(h_ref, mod_ref, g_ref, wg_ref, wu_ref, wd_ref, o_ref, y_scr, acc_scr):
    j = pl.program_id(1)

    @pl.when(j == 0)
    def _():
        y_scr[...] = _ada_norm(h_ref[...], g_ref[...], mod_ref[...], 3, 4).astype(BF16)
        acc_scr[...] = jnp.zeros_like(acc_scr)

    y = y_scr[...]
    gate = jnp.dot(y, wg_ref[...], preferred_element_type=F32)
    up = jnp.dot(y, wu_ref[...], preferred_element_type=F32)
    act = (gate * jax.nn.sigmoid(gate) * up).astype(BF16)
    acc_scr[...] += jnp.dot(act, wd_ref[...], preferred_element_type=F32)

    @pl.when(j == pl.num_programs(1) - 1)
    def _():
        o_ref[...] = h_ref[...] + mod_ref[5:6, :] * acc_scr[...]


def _ffn(tok, h, mods_l, gain, wg, wu, wd):
    tm = tok.tm
    ff = wg.shape[1]
    tf = _pick(ff, (1408, 512, 256, 128))
    return pl.pallas_call(
        _ffn_kernel,
        grid=(tok.n_all, ff // tf),
        in_specs=[pl.BlockSpec((tm, D), lambda i, j: (i, 0)),
                  pl.BlockSpec((None, N_MOD, D), lambda i, j: (tok.mod_row(i), 0, 0)),
                  pl.BlockSpec((1, D), lambda i, j: (0, 0)),
                  pl.BlockSpec((D, tf), lambda i, j: (0, j)),
                  pl.BlockSpec((D, tf), lambda i, j: (0, j)),
                  pl.BlockSpec((tf, D), lambda i, j: (j, 0))],
        out_specs=pl.BlockSpec((tm, D), lambda i, j: (i, 0)),
        out_shape=jax.ShapeDtypeStruct((tok.T, D), F32),
        scratch_shapes=[pltpu.VMEM((tm, D), BF16), pltpu.VMEM((tm, D), F32)],
        compiler_params=_cparams(("arbitrary", "arbitrary")),
    )(h, mods_l, gain.reshape(1, D), wg, wu, wd)


def _even_layer(tok, h, mods_l, p):
    B, L, CTX = tok.B, tok.L, tok.CTX
    nl = B * L
    z = _norm_matmul(tok, h, mods_l, p['norm_mix'], p['w_in'].astype(BF16), EV_IN)
    v, x1, x2 = _short_conv(tok, z, p['conv_w'], p['conv_b'])
    lat = lambda a: a[:nl].reshape(B, L, -1)
    ctx = lambda a: a[nl:].reshape(B, CTX, -1)
    hl = _hyena_sequence(lat(v), lat(x1), lat(x2), p['hy'], p['hy_bias'])
    parts = [hl.reshape(nl, HY_CH)]
    u = z[:, HY_COLS:]
    ys_lat, ys_ctx = _s5_mixer(lat(u), ctx(u), p['s5'])
    if p['need_ctx']:
        hc = _hyena_sequence(ctx(v), ctx(x1), ctx(x2), p['hy'], p['hy_bias'])
        parts.append(hc.reshape(B * CTX, HY_CH))
    else:
        parts.append(jnp.zeros((B * CTX, HY_CH), F32))
    hy_all = jnp.concatenate(parts, axis=0)
    ys_all = jnp.concatenate([ys_lat.reshape(nl, S5_CH), ys_ctx.reshape(B * CTX, S5_CH)], axis=0)
    h = _even_out(tok, hy_all, ys_all, z, h, mods_l, p['s5_d'], p['s5_w_glu'].astype(BF16), p['w_out'].astype(BF16))
    return _ffn(tok, h, mods_l, p['norm_ffn'], p['ff_wg'].astype(BF16), p['ff_wu'].astype(BF16),
                p['ff_wd'].astype(BF16))


def _rope_tables(L, tm):
    t = jnp.arange(L)
    row = (t // GRID_W).astype(F32)[:, None]
    col = (t % GRID_W).astype(F32)[:, None]

    def pattern(dim):
        nf = dim // 4
        inv = ROPE_BASE ** (-jnp.arange(nf, dtype=F32) / nf)
        ar, ac = row * inv[None, :], col * inv[None, :]
        cos = jnp.concatenate([jnp.cos(ar)] * 2 + [jnp.cos(ac)] * 2, axis=1)
        z = jnp.zeros((L, nf), F32)
        s_up = jnp.concatenate([-jnp.sin(ar), z, -jnp.sin(ac), z], axis=1)
        s_dn = jnp.concatenate([z, jnp.sin(ar), z, jnp.sin(ac)], axis=1)
        return cos, s_up, s_dn

    def pad_mla(a, fill):
        return jnp.concatenate([jnp.full((L, MLA_NOPE), fill, F32), a,
                                jnp.full((L, HEAD_PAD - MLA_NOPE - MLA_ROPE), fill, F32)], axis=1)

    cm, um, dm = pattern(MLA_ROPE)
    cg, ug, dg = pattern(GQA_HD)
    mla = jnp.stack([pad_mla(cm, 1.0), pad_mla(um, 0.0), pad_mla(dm, 0.0)])
    gqa = jnp.stack([jnp.tile(cg, (1, 2)), jnp.tile(ug, (1, 2)), jnp.tile(dg, (1, 2))])
    ident = jnp.stack([jnp.ones((tm, LANE), F32), jnp.zeros((tm, LANE), F32), jnp.zeros((tm, LANE), F32)])
    return jnp.stack([jnp.concatenate([mla, ident], axis=1), jnp.concatenate([gqa, ident], axis=1)])


def _rope(x, tab, w):
    outs = []
    for h in range(x.shape[1] // LANE):
        xs = x[:, h * LANE:(h + 1) * LANE]
        outs.append(xs * tab[0] + pltpu.roll(xs, LANE - w, axis=1) * tab[1] + pltpu.roll(xs, w, axis=1) * tab[2])
    return outs[0] if len(outs) == 1 else jnp.concatenate(outs, axis=1)


def _rms(x, g):
    return x * lax.rsqrt(jnp.mean(x * x, axis=-1, keepdims=True) + EPS) * g


_O_CQ, _O_CKV, _O_GQ, _O_GK, _O_GV, _O_KR = 0, 256, 384, 896, 1024, 1152


def _odd_proj_kernel(z_ref, tab_ref, qn_ref, kvn_ref, wuq_ref, wuk_ref, wuv_ref, e_ref,
                     q_ref, k_ref, v_ref, gq_ref, gk_ref, gv_ref):
    z = z_ref[...]
    mt, gt = tab_ref[0], tab_ref[1]
    qn = _rms(z[:, _O_CQ:_O_CKV], qn_ref[...]).astype(BF16)
    q = jnp.dot(qn, wuq_ref[...], preferred_element_type=F32)
    q_ref[...] = (_rope(q, mt, MLA_ROPE // 4) * MLA_SCALE).astype(BF16)
    kvn = _rms(z[:, _O_CKV:_O_GQ], kvn_ref[...]).astype(BF16)
    k = (jnp.dot(kvn, wuk_ref[...], preferred_element_type=F32)
         + jnp.dot(z[:, _O_KR:], e_ref[...], precision=HI, preferred_element_type=F32))
    k_ref[...] = _rope(k, mt, MLA_ROPE // 4).astype(BF16)
    v_ref[...] = jnp.dot(kvn, wuv_ref[...], preferred_element_type=F32).astype(BF16)
    gq_ref[...] = (_rope(z[:, _O_GQ:_O_GK], gt, GQA_HD // 4) * GQA_SCALE).astype(BF16)
    gk_ref[...] = _rope(z[:, _O_GK:_O_GV], gt, GQA_HD // 4).astype(BF16)
    gv_ref[...] = z[:, _O_GV:_O_KR].astype(BF16)


def _odd_proj(tok, z, tabs, q_norm, kv_norm, w_uq, w_ukv):
    tm = tok.tm
    hq = MLA_HEADS * HEAD_PAD
    wq = jnp.pad(w_uq.reshape(Q_LORA, MLA_HEADS, MLA_NOPE + MLA_ROPE),
                 ((0, 0), (0, 0), (0, HEAD_PAD - MLA_NOPE - MLA_ROPE))).reshape(Q_LORA, hq).astype(BF16)
    wkv = w_ukv.reshape(KV_LORA, MLA_HEADS, MLA_NOPE + MLA_V)
    wk = jnp.pad(wkv[..., :MLA_NOPE], ((0, 0), (0, 0), (0, HEAD_PAD - MLA_NOPE))).reshape(KV_LORA, hq).astype(BF16)
    wv = wkv[..., MLA_NOPE:].reshape(KV_LORA, MLA_HEADS * MLA_V).astype(BF16)
    eye = jnp.eye(MLA_ROPE, dtype=F32)
    e_head = jnp.pad(eye, ((0, LANE - MLA_ROPE), (MLA_NOPE, HEAD_PAD - MLA_NOPE - MLA_ROPE)))
    e = jnp.tile(e_head, (1, MLA_HEADS))
    tab_blk = lambda i: (0, 0, jnp.where(i < tok.n_lat, i % tok.per_seq, tok.per_seq), 0)
    full = lambda shape: pl.BlockSpec(shape, lambda i: (0,) * len(shape))
    widths = (hq, hq, MLA_HEADS * MLA_V, GQA_HEADS * GQA_HD, GQA_KV * GQA_HD, GQA_KV * GQA_HD)
    return pl.pallas_call(
        _odd_proj_kernel,
        grid=(tok.n_all,),
        in_specs=[pl.BlockSpec((tm, OD_IN_PAD), lambda i: (i, 0)),
                  pl.BlockSpec((2, 3, tm, LANE), tab_blk),
                  full((1, Q_LORA)), full((1, KV_LORA)), full((Q_LORA, hq)), full((KV_LORA, hq)),
                  full((KV_LORA, MLA_HEADS * MLA_V)), full((LANE, hq))],
        out_specs=[pl.BlockSpec((tm, w), lambda i: (i, 0)) for w in widths],
        out_shape=[jax.ShapeDtypeStruct((tok.T, w), BF16) for w in widths],
        compiler_params=_cparams(("arbitrary",)),
    )(z, tabs, q_norm.reshape(1, -1), kv_norm.reshape(1, -1), wq, wk, wv, e)


def _mla_attn_kernel(q_ref, k_ref, v_ref, o_ref, *, tk, nk):
    tq = q_ref.shape[0]
    outs = []
    for hh in range(2):
        q = q_ref[:, hh * HEAD_PAD:(hh + 1) * HEAD_PAD]

        def body(c, carry, hh=hh, q=q):
            m, l, acc = carry
            r = pl.multiple_of(c * tk, tk)
            kc = k_ref[pl.ds(r, tk), hh * HEAD_PAD:(hh + 1) * HEAD_PAD]
            vc = v_ref[pl.ds(r, tk), hh * MLA_V:(hh + 1) * MLA_V]
            s = lax.dot_general(q, kc, (((1,), (1,)), ((), ())), preferred_element_type=F32)
            m_new = jnp.maximum(m, jnp.max(s, axis=-1, keepdims=True))
            alpha = jnp.exp(m - m_new)
            p = jnp.exp(s - m_new)
            l = alpha * l + jnp.sum(p, axis=-1, keepdims=True)
            acc = alpha * acc + jnp.dot(p.astype(BF16), vc, preferred_element_type=F32)
            return m_new, l, acc

        init = (jnp.full((tq, 1), NEG, F32), jnp.zeros((tq, 1), F32), jnp.zeros((tq, MLA_V), F32))
        _, l, acc = lax.fori_loop(0, nk, body, init)
        outs.append(acc / l)
    o_ref[...] = jnp.concatenate(outs, axis=1).astype(o_ref.dtype)


def _mla_attention(q, k, v):
    B, Lq, _ = q.shape
    Nk = k.shape[1]
    tq = _pick(Lq, (256, 128))
    tk = _pick(Nk, (768, 512, 256, 128))
    kern = functools.partial(_mla_attn_kernel, tk=tk, nk=Nk // tk)
    return pl.pallas_call(
        kern,
        grid=(B, MLA_HEADS // 2, Lq // tq),
        in_specs=[pl.BlockSpec((None, tq, 2 * HEAD_PAD), lambda b, h, i: (b, i, h)),
                  pl.BlockSpec((None, Nk, 2 * HEAD_PAD), lambda b, h, i: (b, 0, h)),
                  pl.BlockSpec((None, Nk, 2 * MLA_V), lambda b, h, i: (b, 0, h))],
        out_specs=pl.BlockSpec((None, tq, 2 * MLA_V), lambda b, h, i: (b, i, h)),
        out_shape=jax.ShapeDtypeStruct((B, Lq, MLA_HEADS * MLA_V), BF16),
        compiler_params=_cparams(("arbitrary", "arbitrary", "arbitrary")),
    )(q, k, v)


def _gqa_kernel(sink_ref, q_ref, kc_ref, vc_ref, *rest, L, has_band):
    if has_band:
        k_ref, v_ref, o_ref = rest
        qi = pl.program_id(1)
        start = pl.multiple_of(jnp.clip((qi - 1) * BLK, 0, L - 3 * BLK), BLK)
        qpos = qi * BLK + lax.broadcasted_iota(jnp.int32, (BLK, 3 * BLK), 0)
        kpos = start + lax.broadcasted_iota(jnp.int32, (BLK, 3 * BLK), 1)
        valid = jnp.abs(qpos - kpos) <= WINDOW
    else:
        (o_ref,) = rest
    dn = (((1,), (1,)), ((), ()))
    group = GQA_HEADS // GQA_KV
    outs = []
    for h in range(GQA_HEADS):
        ks = slice((h // group) * GQA_HD, (h // group + 1) * GQA_HD)
        q = q_ref[:, h * GQA_HD:(h + 1) * GQA_HD]
        sink = sink_ref[h]
        s_ctx = lax.dot_general(q, kc_ref[:, ks], dn, preferred_element_type=F32)
        m = jnp.maximum(jnp.max(s_ctx, axis=-1, keepdims=True), sink)
        if has_band:
            s_band = lax.dot_general(q, k_ref[pl.ds(start, 3 * BLK), ks], dn, preferred_element_type=F32)
            s_band = jnp.where(valid, s_band, NEG)
            m = jnp.maximum(m, jnp.max(s_band, axis=-1, keepdims=True))
        p_ctx = jnp.exp(s_ctx - m)
        den = jnp.sum(p_ctx, axis=-1, keepdims=True) + jnp.exp(sink - m)
        if has_band:
            p_band = jnp.exp(s_band - m)
            den = den + jnp.sum(p_band, axis=-1, keepdims=True)
        inv = 1.0 / den
        o = jnp.dot((p_ctx * inv).astype(BF16), vc_ref[:, ks], preferred_element_type=F32)
        if has_band:
            o = o + jnp.dot((p_band * inv).astype(BF16), v_ref[pl.ds(start, 3 * BLK), ks],
                            preferred_element_type=F32)
        outs.append(o)
    o_ref[...] = jnp.concatenate(outs, axis=1).astype(o_ref.dtype)


def _gqa_attention(sink, q, kc, vc, k=None, v=None):
    B, Lq, _ = q.shape
    CTX = kc.shape[1]
    has_band = k is not None
    kw = GQA_KV * GQA_HD
    in_specs = [pl.BlockSpec(memory_space=pltpu.SMEM),
                pl.BlockSpec((None, BLK, GQA_HEADS * GQA_HD), lambda b, i: (b, i, 0)),
                pl.BlockSpec((None, CTX, kw), lambda b, i: (b, 0, 0)),
                pl.BlockSpec((None, CTX, kw), lambda b, i: (b, 0, 0))]
    args = [sink.astype(F32), q, kc, vc]
    if has_band:
        in_specs += [pl.BlockSpec((None, Lq, kw), lambda b, i: (b, 0, 0))] * 2
        args += [k, v]
    kern = functools.partial(_gqa_kernel, L=Lq, has_band=has_band)
    return pl.pallas_call(
        kern,
        grid=(B, Lq // BLK),
        in_specs=in_specs,
        out_specs=pl.BlockSpec((None, BLK, GQA_HEADS * GQA_HD), lambda b, i: (b, i, 0)),
        out_shape=jax.ShapeDtypeStruct((B, Lq, GQA_HEADS * GQA_HD), BF16),
        compiler_params=_cparams(("arbitrary", "arbitrary")),
    )(*args)


def _odd_out_kernel(a_ref, g_ref, h_ref, mod_ref, wo_ref, o_ref):
    half = a_ref.shape[1]
    ol = (jnp.dot(a_ref[...], wo_ref[:half, :], preferred_element_type=F32)
          + jnp.dot(g_ref[...], wo_ref[half:, :], preferred_element_type=F32))
    o_ref[...] = h_ref[...] + mod_ref[2:3, :] * ol


def _odd_out(tok, mla, gqa, h, mods_l, w_out):
    tm = tok.tm
    half = mla.shape[1]
    return pl.pallas_call(
        _odd_out_kernel,
        grid=(tok.n_all,),
        in_specs=[pl.BlockSpec((tm, half), lambda i: (i, 0)),
                  pl.BlockSpec((tm, half), lambda i: (i, 0)),
                  pl.BlockSpec((tm, D), lambda i: (i, 0)),
                  pl.BlockSpec((None, N_MOD, D), lambda i: (tok.mod_row(i), 0, 0)),
                  pl.BlockSpec((D, D), lambda i: (0, 0))],
        out_specs=pl.BlockSpec((tm, D), lambda i: (i, 0)),
        out_shape=jax.ShapeDtypeStruct((tok.T, D), F32),
        compiler_params=_cparams(("arbitrary",)),
    )(mla, gqa, h, mods_l, w_out)


def _top2_gates(logits):
    lane = lax.broadcasted_iota(jnp.int32, logits.shape, 1)
    lg = jnp.where(lane < N_EXP, logits, -jnp.inf)
    m1 = jnp.max(lg, axis=-1, keepdims=True)
    i1 = jnp.min(jnp.where(lg == m1, lane, LANE), axis=-1, keepdims=True)
    lg2 = jnp.where(lane == i1, -jnp.inf, lg)
    m2 = jnp.max(lg2, axis=-1, keepdims=True)
    i2 = jnp.min(jnp.where(lg2 == m2, lane, LANE), axis=-1, keepdims=True)
    e = jnp.exp(m2 - m1)
    w1 = 1.0 / (1.0 + e)
    return jnp.where(lane == i1, w1, 0.0) + jnp.where(lane == i2, e * w1, 0.0)


def _moe_dense_kernel(h_ref, mod_ref, g_ref, r_ref, wg_ref, wu_ref, wd_ref, o_ref, y_scr, gate_scr, acc_scr):
    e, f = pl.program_id(1), pl.program_id(2)

    @pl.when((e == 0) & (f == 0))
    def _():
        y = _ada_norm(h_ref[...], g_ref[...], mod_ref[...], 3, 4)
        y_scr[...] = y.astype(BF16)
        gate_scr[...] = _top2_gates(jnp.dot(y, r_ref[...], precision=HI, preferred_element_type=F32))
        acc_scr[...] = jnp.zeros_like(acc_scr)

    lane = lax.broadcasted_iota(jnp.int32, gate_scr.shape, 1)
    gate_e = jnp.sum(jnp.where(lane == e, gate_scr[...], 0.0), axis=-1, keepdims=True)
    y = y_scr[...]
    gate = jnp.dot(y, wg_ref[...], preferred_element_type=F32)
    up = jnp.dot(y, wu_ref[...], preferred_element_type=F32)
    act = (gate * jax.nn.sigmoid(gate) * up).astype(BF16)
    acc_scr[...] += gate_e * jnp.dot(act, wd_ref[...], preferred_element_type=F32)

    @pl.when((e == pl.num_programs(1) - 1) & (f == pl.num_programs(2) - 1))
    def _():
        o_ref[...] = h_ref[...] + mod_ref[5:6, :] * acc_scr[...]


def _moe_dense(tok, h, mods_l, gain, router, wg, wu, wd):
    tm = tok.tm
    tf = EXP_FF // 2
    rp = jnp.pad(router, ((0, 0), (0, LANE - N_EXP)))
    return pl.pallas_call(
        _moe_dense_kernel,
        grid=(tok.n_all, N_EXP, EXP_FF // tf),
        in_specs=[pl.BlockSpec((tm, D), lambda i, e, f: (i, 0)),
                  pl.BlockSpec((None, N_MOD, D), lambda i, e, f: (tok.mod_row(i), 0, 0)),
                  pl.BlockSpec((1, D), lambda i, e, f: (0, 0)),
                  pl.BlockSpec((D, LANE), lambda i, e, f: (0, 0)),
                  pl.BlockSpec((None, D, tf), lambda i, e, f: (e, 0, f)),
                  pl.BlockSpec((None, D, tf), lambda i, e, f: (e, 0, f)),
                  pl.BlockSpec((None, tf, D), lambda i, e, f: (e, f, 0))],
        out_specs=pl.BlockSpec((tm, D), lambda i, e, f: (i, 0)),
        out_shape=jax.ShapeDtypeStruct((tok.T, D), F32),
        scratch_shapes=[pltpu.VMEM((tm, D), BF16), pltpu.VMEM((tm, LANE), F32), pltpu.VMEM((tm, D), F32)],
        compiler_params=_cparams(("arbitrary", "arbitrary", "arbitrary")),
    )(h, mods_l, gain.reshape(1, D), rp, wg, wu, wd)


def _odd_layer(tok, h, mods_l, tabs, p):
    B, L, CTX = tok.B, tok.L, tok.CTX
    nl = B * L
    w = p['w_in']
    w_in = jnp.concatenate([w[:, :Q_LORA + KV_LORA], w[:, 416:1184], w[:, 384:416],
                            jnp.zeros((D, OD_IN_PAD - 1184), w.dtype)], axis=1).astype(BF16)
    z = _norm_matmul(tok, h, mods_l, p['norm_mix'], w_in, OD_IN_PAD)
    q, k, v, gq, gk, gv = _odd_proj(tok, z, tabs, p['q_norm'], p['kv_norm'], p['w_uq'], p['w_ukv'])
    lat = lambda a: a[:nl].reshape(B, L, -1)
    ctx = lambda a: a[nl:].reshape(B, CTX, -1)
    cat = lambda a: jnp.concatenate([ctx(a), lat(a)], axis=1)
    mla_l = _mla_attention(lat(q), cat(k), cat(v))
    gqa_l = _gqa_attention(p['sink'], lat(gq), ctx(gk), ctx(gv), lat(gk), lat(gv))
    if p['need_ctx']:
        mla_c = _mla_attention(ctx(q), ctx(k), ctx(v))
        gqa_c = _gqa_attention(p['sink'], ctx(gq), ctx(gk), ctx(gv))
    else:
        mla_c = jnp.zeros((B, CTX, MLA_HEADS * MLA_V), BF16)
        gqa_c = jnp.zeros((B, CTX, GQA_HEADS * GQA_HD), BF16)
    flat = lambda a, c: jnp.concatenate([a.reshape(nl, -1), c.reshape(B * CTX, -1)], axis=0)
    h = _odd_out(tok, flat(mla_l, mla_c), flat(gqa_l, gqa_c), h, mods_l, p['w_out'].astype(BF16))
    return _moe_dense(tok, h, mods_l, p['norm_ffn'], p['router'], p['moe_wg'].astype(BF16),
                      p['moe_wu'].astype(BF16), p['moe_wd'].astype(BF16))


def _final_norm_kernel(h_ref, g_ref, o_ref):
    o_ref[...] = _rms(h_ref[...], g_ref[...])


def _final_norm(tok, h, gain):
    tm = tok.tm
    return pl.pallas_call(
        _final_norm_kernel,
        grid=(tok.n_lat,),
        in_specs=[pl.BlockSpec((tm, D), lambda i: (i, 0)), pl.BlockSpec((1, D), lambda i: (0, 0))],
        out_specs=pl.BlockSpec((tm, D), lambda i: (i, 0)),
        out_shape=jax.ShapeDtypeStruct((tok.B * tok.L, D), F32),
        compiler_params=_cparams(("arbitrary",)),
    )(h, gain.reshape(1, D))


def kernel(x, c, ctx, c_ctx, mod_w, mod_b, norm_mix, norm_ffn, final_norm,
           ev_w_in, ev_conv_w, ev_conv_b, hy_w1, hy_b1, hy_w2, hy_b2, hy_w3, hy_freq, hy_decay, hy_bias,
           s5_a_re, s5_a_im, s5_log_dt, s5_b_re, s5_b_im, s5_c_re, s5_c_im, s5_d, s5_w_glu, ev_w_out,
           ff_w_gate, ff_w_up, ff_w_down,
           od_w_in, mla_q_norm, mla_w_uq, mla_kv_norm, mla_w_ukv, gqa_sink, od_w_out,
           moe_router, moe_w_gate, moe_w_up, moe_w_down):
    B, L, _ = x.shape
    CTX = ctx.shape[1]
    tok = _Tok(B, L, CTX, _pick(math.gcd(L, B * CTX), (512, 256, 128)))
    cond_t = jnp.concatenate([c, c_ctx[None, :], jnp.zeros((8 - B - 1, D), F32)], axis=0).T
    mods = _modulations(cond_t, B + 1, mod_w, mod_b)
    tabs = _rope_tables(L, tok.tm)
    h = jnp.concatenate([x.reshape(B * L, D), ctx.reshape(B * CTX, D)], axis=0)
    for l in range(DEPTH):
        i = l // 2
        need_ctx = l < DEPTH - 1
        if l % 2 == 0:
            p = dict(norm_mix=norm_mix[l], norm_ffn=norm_ffn[l], w_in=ev_w_in[i], conv_w=ev_conv_w[i],
                     conv_b=ev_conv_b[i],
                     hy=(hy_w1[i], hy_b1[i], hy_w2[i], hy_b2[i], hy_w3[i], hy_freq[i], hy_decay[i]),
                     hy_bias=hy_bias[i],
                     s5=(s5_a_re[i], s5_a_im[i], s5_log_dt[i], s5_b_re[i], s5_b_im[i], s5_c_re[i], s5_c_im[i]),
                     s5_d=s5_d[i], s5_w_glu=s5_w_glu[i], w_out=ev_w_out[i],
                     ff_wg=ff_w_gate[i], ff_wu=ff_w_up[i], ff_wd=ff_w_down[i], need_ctx=need_ctx)
            h = _even_layer(tok, h, mods[l], p)
        else:
            p = dict(norm_mix=norm_mix[l], norm_ffn=norm_ffn[l], w_in=od_w_in[i], q_norm=mla_q_norm[i],
                     w_uq=mla_w_uq[i], kv_norm=mla_kv_norm[i], w_ukv=mla_w_ukv[i], sink=gqa_sink[i],
                     w_out=od_w_out[i], router=moe_router[i], moe_wg=moe_w_gate[i], moe_wu=moe_w_up[i],
                     moe_wd=moe_w_down[i], need_ctx=need_ctx)
            h = _odd_layer(tok, h, mods[l], tabs, p)
    return _final_norm(tok, h, final_norm).reshape(B, L, D)
```

```python
import functools
import math

import jax
import jax.numpy as jnp
from jax import lax
from jax.experimental import pallas as pl
from jax.experimental.pallas import tpu as pltpu

F32 = jnp.float32
BF16 = jnp.bfloat16
HI = lax.Precision.HIGHEST

D = 1024
DEPTH = 4
GRID_W = 64
EPS = 1e-6
NEG = -1e30
N_MOD = 6

HY_CH = 512
HY_ORDER = 2
HY_BANDS = 16
HY_EMB = 1 + 2 * HY_BANDS
HY_FFN = 64
HY_SHIFT = 0.05
HY_COLS = (HY_ORDER + 1) * HY_CH
S5_CH = 512
S5_GROUP = 16
S5_NG = S5_CH // S5_GROUP
S5_P = 64
S5_T = 16
S5_SEG = 8
EV_IN = HY_COLS + S5_CH

MLA_HEADS = 8
MLA_NOPE = 64
MLA_ROPE = 32
MLA_V = 64
Q_LORA = 256
KV_LORA = 128
GQA_HEADS = 8
GQA_KV = 2
GQA_HD = 64
WINDOW = 128
BLK = 128
ROPE_BASE = 10000.0
MLA_SCALE = (MLA_NOPE + MLA_ROPE) ** -0.5
GQA_SCALE = GQA_HD ** -0.5
LOG2E = math.log2(math.e)
HEAD_PAD = 128
OD_IN_PAD = 1280

D_FF = 2816
N_EXP = 8
EXP_FF = 3584

LANE = 128
FFT_N2 = 128
VMEM_LIMIT = 56 * 1024 * 1024


def _cparams(sem):
    return pltpu.CompilerParams(dimension_semantics=sem, vmem_limit_bytes=VMEM_LIMIT)


def _pick(n, cands):
    for c in cands:
        if n % c == 0:
            return c
    raise ValueError(f"no tile for {n} in {cands}")


def _mod_kernel(ct_ref, w_ref, b_ref, o_ref, *, nrows):
    c = ct_ref[...]
    s = c * jax.nn.sigmoid(c)
    w = w_ref[...]
    rows = [jnp.sum(w * s[:, r:r + 1], axis=0, keepdims=True) for r in range(nrows)]
    rows.append(jnp.zeros((8 - nrows, w.shape[1]), F32))
    o_ref[...] = jnp.concatenate(rows, axis=0) + b_ref[...]


def _modulations(cond_t, nrows, mod_w, mod_b):
    tn = 1536
    out = pl.pallas_call(
        functools.partial(_mod_kernel, nrows=nrows),
        grid=(DEPTH, N_MOD * D // tn),
        in_specs=[pl.BlockSpec((D, 8), lambda l, j: (0, 0)),
                  pl.BlockSpec((None, D, tn), lambda l, j: (l, 0, j)),
                  pl.BlockSpec((None, 1, tn), lambda l, j: (l, 0, j))],
        out_specs=pl.BlockSpec((None, 8, tn), lambda l, j: (l, 0, j)),
        out_shape=jax.ShapeDtypeStruct((DEPTH, 8, N_MOD * D), F32),
        compiler_params=_cparams(("arbitrary", "arbitrary")),
    )(cond_t, mod_w, mod_b.reshape(DEPTH, 1, N_MOD * D))
    return out.reshape(DEPTH, 8, N_MOD, D)


class _Tok:
    def __init__(self, B, L, CTX, tm):
        assert L % tm == 0 and (B * CTX) % tm == 0
        self.B, self.L, self.CTX, self.tm = B, L, CTX, tm
        self.n_lat = B * L // tm
        self.n_all = self.n_lat + B * CTX // tm
        self.T = B * (L + CTX)
        self.per_seq = L // tm

    def mod_row(self, i):
        return jnp.where(i < self.n_lat, i // self.per_seq, self.B)


def _ada_norm(x, gain, mod, shift_idx, scale_idx):
    y = x * lax.rsqrt(jnp.mean(x * x, axis=-1, keepdims=True) + EPS) * gain
    return y * (1.0 + mod[scale_idx:scale_idx + 1, :]) + mod[shift_idx:shift_idx + 1, :]


def _norm_mm_kernel(h_ref, mod_ref, g_ref, w_ref, o_ref, y_scr):
    @pl.when(pl.program_id(1) == 0)
    def _():
        y_scr[...] = _ada_norm(h_ref[...], g_ref[...], mod_ref[...], 0, 1).astype(BF16)

    o_ref[...] = jnp.dot(y_scr[...], w_ref[...], preferred_element_type=F32).astype(o_ref.dtype)


def _norm_matmul(tok, h, mods_l, gain, w, tn, out_dtype=F32):
    tm, n = tok.tm, w.shape[1]
    return pl.pallas_call(
        _norm_mm_kernel,
        grid=(tok.n_all, n // tn),
        in_specs=[pl.BlockSpec((tm, D), lambda i, j: (i, 0)),
                  pl.BlockSpec((None, N_MOD, D), lambda i, j: (tok.mod_row(i), 0, 0)),
                  pl.BlockSpec((1, D), lambda i, j: (0, 0)),
                  pl.BlockSpec((D, tn), lambda i, j: (0, j))],
        out_specs=pl.BlockSpec((tm, tn), lambda i, j: (i, j)),
        out_shape=jax.ShapeDtypeStruct((tok.T, n), out_dtype),
        scratch_shapes=[pltpu.VMEM((tm, D), BF16)],
        compiler_params=_cparams(("arbitrary", "arbitrary")),
    )(h, mods_l, gain.reshape(1, D), w)


def _short_conv_kernel(z_ref, zp_ref, zn_ref, w_ref, b_ref, v_ref, x1_ref, x2_ref, *, tm, n_lat, L, CTX):
    i = pl.program_id(0)
    is_lat = i < n_lat
    seqlen = jnp.where(is_lat, L, CTX)
    off = jnp.where(is_lat, i * tm, (i - n_lat) * tm)
    first = lax.rem(off, seqlen) == 0
    last = lax.rem(off + tm, seqlen) == 0
    z = z_ref[...]
    prev_row = jnp.where(first, 0.0, zp_ref[7:8, :])
    next_row = jnp.where(last, 0.0, zn_ref[0:1, :])
    rid = lax.broadcasted_iota(jnp.int32, z.shape, 0)
    zm1 = jnp.where(rid == 0, prev_row, pltpu.roll(z, 1, axis=0))
    zp1 = jnp.where(rid == tm - 1, next_row, pltpu.roll(z, tm - 1, axis=0))
    out = b_ref[...] + zm1 * w_ref[0:1, :] + z * w_ref[1:2, :] + zp1 * w_ref[2:3, :]
    v_ref[...] = out[:, :HY_CH]
    x1_ref[...] = out[:, HY_CH:2 * HY_CH]
    x2_ref[...] = out[:, 2 * HY_CH:]


def _short_conv(tok, z, conv_w, conv_b):
    tm = _pick(math.gcd(tok.L, tok.CTX), (256, 128))
    n_lat = tok.B * tok.L // tm
    n_all = tok.T // tm
    r8 = tm // 8
    kern = functools.partial(_short_conv_kernel, tm=tm, n_lat=n_lat, L=tok.L, CTX=tok.CTX)
    o = jax.ShapeDtypeStruct((tok.T, HY_CH), F32)
    return pl.pallas_call(
        kern,
        grid=(n_all,),
        in_specs=[pl.BlockSpec((tm, HY_COLS), lambda i: (i, 0)),
                  pl.BlockSpec((8, HY_COLS), lambda i: (jnp.maximum(i * r8 - 1, 0), 0)),
                  pl.BlockSpec((8, HY_COLS), lambda i: (jnp.minimum((i + 1) * r8, n_all * r8 - 1), 0)),
                  pl.BlockSpec((8, HY_COLS), lambda i: (0, 0)),
                  pl.BlockSpec((1, HY_COLS), lambda i: (0, 0))],
        out_specs=[pl.BlockSpec((tm, HY_CH), lambda i: (i, 0))] * 3,
        out_shape=[o, o, o],
        compiler_params=_cparams(("arbitrary",)),
    )(z, z, z, jnp.pad(conv_w, ((0, 8 - conv_w.shape[0]), (0, 0))), conv_b.reshape(1, HY_COLS))


def _filter_kernel(f_ref, w1_ref, b1_ref, w2_ref, b2_ref, w3_ref, fr_ref, dec_ref, k_ref, s_ref):
    @pl.when(pl.program_id(0) == 0)
    def _():
        s_ref[...] = jnp.zeros_like(s_ref)

    f = f_ref[...]
    fr = fr_ref[...]
    hid = jnp.sin(fr * (jnp.dot(f, w1_ref[...], precision=HI, preferred_element_type=F32) + b1_ref[...]))
    hid = jnp.sin(fr * (jnp.dot(hid, w2_ref[...], precision=HI, preferred_element_type=F32) + b2_ref[...]))
    h = jnp.dot(hid, w3_ref[...], precision=HI, preferred_element_type=F32)
    t01 = f[:, 0:1]
    valid = f[:, LANE - 1:LANE]
    k = h * (jnp.exp(-t01 * jnp.abs(dec_ref[...])) + HY_SHIFT) * valid
    k_ref[0] = k[:, :HY_CH]
    k_ref[1] = k[:, HY_CH:]
    s_ref[...] += jnp.sum(jnp.abs(k), axis=0, keepdims=True)


def _hyena_filters(L, n, w1, b1, w2, b2, w3, freq, decay):
    row = jnp.arange(n)
    fwd = row < L
    bwd = row > n - L
    t = jnp.where(fwd, row, n - row).astype(F32)
    t01 = t / L
    bands = jnp.linspace(1e-4, HY_BANDS - 1, HY_BANDS, dtype=F32)
    ang = (2.0 * math.pi / L) * t[:, None] * bands[None, :]
    valid = (fwd | bwd).astype(F32)
    feats = jnp.concatenate([t01[:, None], jnp.cos(ang), -jnp.sin(ang),
                             jnp.zeros((n, LANE - 1 - HY_EMB), F32), valid[:, None]], axis=-1)
    w1p = jnp.pad(w1, ((0, LANE - HY_EMB), (0, 0)))
    tr = _pick(L, (512, 256))
    nb_half = n // 2 // tr
    ncol = HY_ORDER * HY_CH
    k, ssum = pl.pallas_call(
        _filter_kernel,
        grid=(n // tr,),
        in_specs=[pl.BlockSpec((tr, LANE), lambda i: (i, 0)),
                  pl.BlockSpec((LANE, HY_FFN), lambda i: (0, 0)),
                  pl.BlockSpec((1, HY_FFN), lambda i: (0, 0)),
                  pl.BlockSpec((HY_FFN, HY_FFN), lambda i: (0, 0)),
                  pl.BlockSpec((1, HY_FFN), lambda i: (0, 0)),
                  pl.BlockSpec((HY_FFN, ncol), lambda i: (0, jnp.where(i < nb_half, 0, 1))),
                  pl.BlockSpec((1, HY_FFN), lambda i: (0, 0)),
                  pl.BlockSpec((1, ncol), lambda i: (0, 0))],
        out_specs=[pl.BlockSpec((HY_ORDER, tr, HY_CH), lambda i: (0, i, 0)),
                   pl.BlockSpec((1, ncol), lambda i: (0, 0))],
        out_shape=[jax.ShapeDtypeStruct((HY_ORDER, n, HY_CH), F32),
                   jax.ShapeDtypeStruct((1, ncol), F32)],
        compiler_params=_cparams(("arbitrary",)),
    )(feats, w1p, b1.reshape(1, -1), w2, b2.reshape(1, -1), w3, freq.reshape(1, -1), decay.reshape(1, ncol))
    return k, (1.0 / ssum).reshape(HY_ORDER, 1, HY_CH)


def _dft_tables(n1, r_in):
    n = n1 * FFT_N2
    k1 = jnp.arange(n1)
    a1 = (2.0 * math.pi / n1) * ((k1[:, None] * jnp.arange(r_in)[None, :]) % n1).astype(F32)
    f1 = jnp.concatenate([jnp.cos(a1), -jnp.sin(a1)], axis=0)
    f3 = jnp.concatenate([jnp.cos(a1).T, -jnp.sin(a1).T], axis=1) / n
    n2 = jnp.arange(FFT_N2)
    kk = k1[:, None, None] + n1 * n2[None, :, None]
    ang = (2.0 * math.pi / n) * ((kk * n2[None, None, :]) % n).astype(F32)
    gr, gi = jnp.cos(ang), -jnp.sin(ang)
    g = jnp.concatenate([jnp.concatenate([gr, -gi], axis=2), jnp.concatenate([gi, gr], axis=2)], axis=1)
    return f1.astype(BF16), f3.astype(BF16), g.astype(BF16), jnp.swapaxes(g, 1, 2).astype(BF16)


def _fft1_kernel(f_ref, x_ref, o_ref):
    o_ref[...] = jnp.dot(f_ref[...], x_ref[...].astype(BF16), preferred_element_type=F32).astype(o_ref.dtype)


def _fft_stage1(f1, x):
    nb, r_in, cols = x.shape
    m = f1.shape[0]
    tn = _pick(cols, (8192, 4096, 2048))
    return pl.pallas_call(
        _fft1_kernel,
        grid=(nb, cols // tn),
        in_specs=[pl.BlockSpec((m, r_in), lambda b, j: (0, 0)),
                  pl.BlockSpec((None, r_in, tn), lambda b, j: (b, 0, j))],
        out_specs=pl.BlockSpec((None, m, tn), lambda b, j: (b, 0, j)),
        out_shape=jax.ShapeDtypeStruct((nb, m, cols), BF16),
        compiler_params=_cparams(("arbitrary", "arbitrary")),
    )(f1, x)


def _fft_filt_kernel(a_ref, g_ref, s_ref, o_ref):
    c = a_ref.shape[-1]
    a = a_ref[...].reshape(2 * FFT_N2, c)
    o_ref[...] = jnp.dot(g_ref[...], a, preferred_element_type=F32) * s_ref[...]


def _fft_filter_spectrum(a, g, inv):
    no, _, n1, _, c = a.shape
    return pl.pallas_call(
        _fft_filt_kernel,
        grid=(n1, no),
        in_specs=[pl.BlockSpec((None, 2, None, FFT_N2, c), lambda k, o: (o, 0, k, 0, 0)),
                  pl.BlockSpec((None, 2 * FFT_N2, 2 * FFT_N2), lambda k, o: (k, 0, 0)),
                  pl.BlockSpec((None, 1, c), lambda k, o: (o, 0, 0))],
        out_specs=pl.BlockSpec((None, None, 2 * FFT_N2, c), lambda k, o: (o, k, 0, 0)),
        out_shape=jax.ShapeDtypeStruct((no, n1, 2 * FFT_N2, c), F32),
        compiler_params=_cparams(("arbitrary", "arbitrary")),
    )(a, g, inv)


def _fft_mid_kernel(a_ref, g_ref, gt_ref, kh_ref, o_ref):
    c = a_ref.shape[-1]
    a = a_ref[...].reshape(2 * FFT_N2, c)
    x = jnp.dot(g_ref[...], a, preferred_element_type=F32)
    xr, xi = x[:FFT_N2], x[FFT_N2:]
    kr, ki = kh_ref[:FFT_N2, :], kh_ref[FFT_N2:, :]
    y = jnp.concatenate([xr * kr - xi * ki, xr * ki + xi * kr], axis=0).astype(BF16)
    bm = jnp.dot(gt_ref[...], y, preferred_element_type=F32)
    o_ref[...] = bm.reshape(2, FFT_N2, c).astype(o_ref.dtype)


def _fft_mid(a, g, gt, khat):
    nb, _, n1, _, c = a.shape
    return pl.pallas_call(
        _fft_mid_kernel,
        grid=(n1, nb),
        in_specs=[pl.BlockSpec((None, 2, None, FFT_N2, c), lambda k, b: (b, 0, k, 0, 0)),
                  pl.BlockSpec((None, 2 * FFT_N2, 2 * FFT_N2), lambda k, b: (k, 0, 0)),
                  pl.BlockSpec((None, 2 * FFT_N2, 2 * FFT_N2), lambda k, b: (k, 0, 0)),
                  pl.BlockSpec((None, 2 * FFT_N2, c), lambda k, b: (k, 0, 0))],
        out_specs=pl.BlockSpec((None, 2, None, FFT_N2, c), lambda k, b: (b, 0, k, 0, 0)),
        out_shape=jax.ShapeDtypeStruct(a.shape, BF16),
        compiler_params=_cparams(("arbitrary", "arbitrary")),
    )(a, g, gt, khat)


def _fft3_kernel(f_ref, bm_ref, y_ref, gate_ref, bias_ref, o_ref):
    conv = jnp.dot(f_ref[...], bm_ref[...], preferred_element_type=F32)
    o_ref[...] = gate_ref[...] * (conv + y_ref[...] * bias_ref[...])


def _fft_stage3(f3, bm, y, gate, bias_row):
    nb, m, cols = bm.shape
    r = f3.shape[0]
    tn = _pick(cols, (8192, 4096, 2048))
    return pl.pallas_call(
        _fft3_kernel,
        grid=(nb, cols // tn),
        in_specs=[pl.BlockSpec((r, m), lambda b, j: (0, 0)),
                  pl.BlockSpec((None, m, tn), lambda b, j: (b, 0, j)),
                  pl.BlockSpec((None, r, tn), lambda b, j: (b, 0, j)),
                  pl.BlockSpec((None, r, tn), lambda b, j: (b, 0, j)),
                  pl.BlockSpec((1, tn), lambda b, j: (0, j))],
        out_specs=pl.BlockSpec((None, r, tn), lambda b, j: (b, 0, j)),
        out_shape=jax.ShapeDtypeStruct((nb, r, cols), F32),
        compiler_params=_cparams(("arbitrary", "arbitrary")),
    )(f3, bm, y, gate, bias_row)


def _hyena_sequence(v, x1, x2, hy, bias):
    B, L, C = v.shape
    r_valid = L // FFT_N2
    r_in = max(r_valid, 16)
    n1 = max(2 * r_valid, r_in)
    n = n1 * FFT_N2
    f1, f3, g, gt = _dft_tables(n1, r_in)
    f1k = _dft_tables(n1, n1)[0]
    k, inv = _hyena_filters(L, n, *hy)
    ak = _fft_stage1(f1k, k.reshape(HY_ORDER, n1, FFT_N2 * C))
    khat = _fft_filter_spectrum(ak.reshape(HY_ORDER, 2, n1, FFT_N2, C), g, inv)
    cols = FFT_N2 * C

    def view(a):
        a = a.reshape(B, r_valid, cols)
        return a if r_in == r_valid else jnp.pad(a, ((0, 0), (0, r_in - r_valid), (0, 0)))

    y = view(v)
    for o, gate in enumerate((view(x1), view(x2))):
        a = _fft_stage1(f1, y)
        bm = _fft_mid(a.reshape(B, 2, n1, FFT_N2, C), g, gt, khat[o])
        y = _fft_stage3(f3, bm.reshape(B, 2 * n1, cols), y, gate, jnp.tile(bias[o].astype(F32), FFT_N2)[None, :])
    return y[:, :r_valid].reshape(B, L, C)


def _s5_tables(a_re, a_im, log_dt, b_re, b_im, c_re, c_im, jj):
    lam = lax.complex(jnp.minimum(a_re.astype(F32), -1e-4), a_im.astype(F32))
    dt = jnp.exp(log_dt.astype(F32))[..., None]
    abar = jnp.exp(lam * dt)
    bbar = ((abar - 1.0) / lam)[..., None] * lax.complex(b_re.astype(F32), b_im.astype(F32))
    cmat = lax.complex(c_re.astype(F32), c_im.astype(F32))
    T = S5_T

    def powers(m):
        m = jnp.asarray(m, F32)
        return jnp.exp(lam * dt * m.reshape(m.shape + (1, 1, 1)))

    pw = powers(jnp.arange(T + 1))
    kt = jnp.real(jnp.einsum('dgop,tdgp,dgpi->tdgoi', cmat, pw[:T], bbar, precision=HI))
    tt = jnp.arange(T)
    lag = tt[None, :] - tt[:, None]
    w_intra = jnp.where((lag >= 0)[:, :, None, None, None, None],
                        kt[jnp.clip(lag, 0, T - 1)], 0.0)
    w_intra = jnp.stack([w_intra[:, :, 0], w_intra[::-1, ::-1, 1]], axis=2)
    w_intra = w_intra.transpose(2, 3, 0, 5, 1, 4).reshape(2, S5_NG, T * S5_GROUP, T * S5_GROUP)
    wb = pw[T - 1 - tt][..., None] * bbar[None]
    wb = jnp.stack([wb[:, 0], wb[::-1, 1]], axis=1)
    wb = wb.transpose(1, 2, 0, 4, 3).reshape(2, S5_NG, T * S5_GROUP, S5_P)
    w_cat = jnp.concatenate([w_intra, jnp.real(wb), jnp.imag(wb)], axis=-1)
    cp = cmat[None] * pw[1:, :, :, None, :]
    cp = jnp.stack([cp[:, 0], cp[::-1, 1]], axis=1)
    cp = cp.transpose(1, 2, 4, 0, 3).reshape(2, S5_NG, S5_P, T * S5_GROUP)
    c_cat = jnp.concatenate([jnp.real(cp), -jnp.imag(cp)], axis=2)

    def coef(z):
        zr, zi = jnp.real(z), jnp.imag(z)
        return jnp.stack([jnp.concatenate([zr, zr], -1), jnp.concatenate([-zi, zi], -1)], axis=-2)

    step = coef(powers(jnp.array(T)))
    seg = coef(powers(jnp.array(T * jj)))
    ptab = coef(powers(T * jnp.arange(jj))).transpose(1, 2, 0, 3, 4)
    return w_cat.astype(BF16), c_cat.astype(BF16), jnp.concatenate([step, seg], axis=2), ptab


def _cmul(coef_a, coef_b, s):
    return coef_a * s + coef_b * pltpu.roll(s, S5_P, axis=1)


def _s5_kernel(u_ref, w_ref, c_ref, cf_ref, p_ref, y_ref, ds_scr, sp_scr, *, nb, jj):
    J = jj * S5_SEG
    o = jnp.dot(u_ref[...], w_ref[...], preferred_element_type=F32)
    y_ref[...] = o[:, :S5_T * S5_GROUP]
    ds_scr[...] = o[:, S5_T * S5_GROUP:]
    a1, a2 = cf_ref[0:1, :], cf_ref[1:2, :]
    g1, g2 = cf_ref[2:3, :], cf_ref[3:4, :]
    rev = pl.program_id(0) == 1

    def chunk_row(j):
        return jnp.where(rev, jj - 1 - j, j) * S5_SEG

    def local_step(j, states):
        new = []
        for b in range(nb):
            r = pl.multiple_of(b * J + chunk_row(j), S5_SEG)
            sp_scr[pl.ds(r, S5_SEG), :] = states[b]
            new.append(_cmul(a1, a2, states[b]) + ds_scr[pl.ds(r, S5_SEG), :])
        return tuple(new)

    zero = jnp.zeros((S5_SEG, 2 * S5_P), F32)
    ends = lax.fori_loop(0, jj, local_step, (zero,) * nb)
    rid = lax.broadcasted_iota(jnp.int32, (S5_SEG, 2 * S5_P), 0)
    carries = []
    for b in range(nb):
        c = zero
        for _ in range(S5_SEG - 1):
            nxt = ends[b] + _cmul(g1, g2, c)
            c = jnp.where(rev,
                          jnp.where(rid == S5_SEG - 1, 0.0, pltpu.roll(nxt, S5_SEG - 1, axis=0)),
                          jnp.where(rid == 0, 0.0, pltpu.roll(nxt, 1, axis=0)))
        carries.append((c, pltpu.roll(c, S5_P, axis=1)))

    def fix_step(j, carry):
        p = p_ref[j]
        for b in range(nb):
            r = pl.multiple_of(b * J + chunk_row(j), S5_SEG)
            c, cs = carries[b]
            sp_scr[pl.ds(r, S5_SEG), :] += p[0:1, :] * c + p[1:2, :] * cs
        return carry

    lax.fori_loop(0, jj, fix_step, 0)
    y_ref[...] += jnp.dot(sp_scr[...].astype(BF16), c_ref[...], preferred_element_type=F32)


def _s5_scan(u, w_cat, c_cat, coefs, ptab, nb, jj):
    rows = u.shape[2]
    tc = S5_T * S5_GROUP
    kern = functools.partial(_s5_kernel, nb=nb, jj=jj)
    return pl.pallas_call(
        kern,
        grid=(2, S5_NG),
        in_specs=[pl.BlockSpec((None, None, rows, tc), lambda d, g: (d, g, 0, 0)),
                  pl.BlockSpec((None, None, tc, tc + 2 * S5_P), lambda d, g: (d, g, 0, 0)),
                  pl.BlockSpec((None, None, 2 * S5_P, tc), lambda d, g: (d, g, 0, 0)),
                  pl.BlockSpec((None, None, 4, 2 * S5_P), lambda d, g: (d, g, 0, 0)),
                  pl.BlockSpec((None, None, jj, 2, 2 * S5_P), lambda d, g: (d, g, 0, 0, 0))],
        out_specs=pl.BlockSpec((None, None, rows, tc), lambda d, g: (d, g, 0, 0)),
        out_shape=jax.ShapeDtypeStruct((2, S5_NG, rows, tc), F32),
        scratch_shapes=[pltpu.VMEM((rows, 2 * S5_P), F32), pltpu.VMEM((rows, 2 * S5_P), F32)],
        compiler_params=_cparams(("arbitrary", "arbitrary")),
    )(u, w_cat, c_cat, coefs, ptab)


def _s5_mixer(u_lat, u_ctx, s5):
    B, L, _ = u_lat.shape
    CTX = u_ctx.shape[1]
    S = L + CTX
    J = S // S5_T
    jj = J // S5_SEG
    w_cat, c_cat, coefs, ptab = _s5_tables(*s5, jj)
    u_lat, u_ctx = u_lat.astype(BF16), u_ctx.astype(BF16)
    u = jnp.stack([jnp.concatenate([u_ctx, u_lat], axis=1), jnp.concatenate([u_lat, u_ctx], axis=1)])
    u = u.reshape(2, B, S5_SEG, jj, S5_T, S5_NG, S5_GROUP).transpose(0, 5, 1, 3, 2, 4, 6)
    y = _s5_scan(u.reshape(2, S5_NG, B * J, S5_T * S5_GROUP), w_cat, c_cat, coefs, ptab, B, jj)
    y = y.reshape(2, S5_NG, B, jj, S5_SEG, S5_T, S5_GROUP).transpose(0, 2, 4, 3, 5, 1, 6).reshape(2, B, S, S5_CH)
    y_lat = y[0, :, CTX:] + y[1, :, :L]
    y_ctx = y[0, :, :CTX] + y[1, :, L:]
    return y_lat, y_ctx


def _gelu_tanh(x):
    return 0.5 * x * (1.0 + jnp.tanh(math.sqrt(2.0 / math.pi) * (x + 0.044715 * (x * x * x))))


def _even_out_kernel(hl_ref, ys_ref, u_ref, h_ref, mod_ref, dsk_ref, wg_ref, wo_ref, o_ref):
    y = _gelu_tanh(ys_ref[...] + dsk_ref[...] * u_ref[...])
    s = y * jax.nn.sigmoid(jnp.dot(y.astype(BF16), wg_ref[...], preferred_element_type=F32))
    ol = (jnp.dot(hl_ref[...].astype(BF16), wo_ref[:HY_CH, :], preferred_element_type=F32)
          + jnp.dot(s.astype(BF16), wo_ref[HY_CH:, :], preferred_element_type=F32))
    o_ref[...] = h_ref[...] + mod_ref[2:3, :] * ol


def _even_out(tok, hl, ys, z, h, mods_l, dsk, w_glu, w_out):
    tm = tok.tm
    return pl.pallas_call(
        _even_out_kernel,
        grid=(tok.n_all,),
        in_specs=[pl.BlockSpec((tm, HY_CH), lambda i: (i, 0)),
                  pl.BlockSpec((tm, S5_CH), lambda i: (i, 0)),
                  pl.BlockSpec((tm, S5_CH), lambda i: (i, HY_COLS // S5_CH)),
                  pl.BlockSpec((tm, D), lambda i: (i, 0)),
                  pl.BlockSpec((None, N_MOD, D), lambda i: (tok.mod_row(i), 0, 0)),
                  pl.BlockSpec((1, S5_CH), lambda i: (0, 0)),
                  pl.BlockSpec((S5_CH, S5_CH), lambda i: (0, 0)),
                  pl.BlockSpec((D, D), lambda i: (0, 0))],
        out_specs=pl.BlockSpec((tm, D), lambda i: (i, 0)),
        out_shape=jax.ShapeDtypeStruct((tok.T, D), F32),
        compiler_params=_cparams(("arbitrary",)),
    )(hl, ys, z, h, mods_l, dsk.reshape(1, S5_CH), w_glu, w_out)


def _ffn_kernel(h_ref, mod_ref, g_ref, wg_ref, wu_ref, wd_ref, o_ref, y_scr, acc_scr):
    j = pl.program_id(1)

    @pl.when(j == 0)
    def _():
        y_scr[...] = _ada_norm(h_ref[...], g_ref[...], mod_ref[...], 3, 4).astype(BF16)
        acc_scr[...] = jnp.zeros_like(acc_scr)

    y = y_scr[...]
    gate = jnp.dot(y, wg_ref[...], preferred_element_type=F32)
    up = jnp.dot(y, wu_ref[...], preferred_element_type=F32)
    act = (gate * jax.nn.sigmoid(gate) * up).astype(BF16)
    acc_scr[...] += jnp.dot(act, wd_ref[...], preferred_element_type=F32)

    @pl.when(j == pl.num_programs(1) - 1)
    def _():
        o_ref[...] = h_ref[...] + mod_ref[5:6, :] * acc_scr[...]


def _ffn(tok, h, mods_l, gain, wg, wu, wd):
    tm = tok.tm
    ff = wg.shape[1]
    tf = _pick(ff, (1408, 512, 256, 128))
    return pl.pallas_call(
        _ffn_kernel,
        grid=(tok.n_all, ff // tf),
        in_specs=[pl.BlockSpec((tm, D), lambda i, j: (i, 0)),
                  pl.BlockSpec((None, N_MOD, D), lambda i, j: (tok.mod_row(i), 0, 0)),
                  pl.BlockSpec((1, D), lambda i, j: (0, 0)),
                  pl.BlockSpec((D, tf), lambda i, j: (0, j)),
                  pl.BlockSpec((D, tf), lambda i, j: (0, j)),
                  pl.BlockSpec((tf, D), lambda i, j: (j, 0))],
        out_specs=pl.BlockSpec((tm, D), lambda i, j: (i, 0)),
        out_shape=jax.ShapeDtypeStruct((tok.T, D), F32),
        scratch_shapes=[pltpu.VMEM((tm, D), BF16), pltpu.VMEM((tm, D), F32)],
        compiler_params=_cparams(("arbitrary", "arbitrary")),
    )(h, mods_l, gain.reshape(1, D), wg, wu, wd)


def _even_layer(tok, h, mods_l, p):
    B, L, CTX = tok.B, tok.L, tok.CTX
    nl = B * L
    z = _norm_matmul(tok, h, mods_l, p['norm_mix'], p['w_in'].astype(BF16), EV_IN)
    v, x1, x2 = _short_conv(tok, z, p['conv_w'], p['conv_b'])
    lat = lambda a: a[:nl].reshape(B, L, -1)
    ctx = lambda a: a[nl:].reshape(B, CTX, -1)
    hl = _hyena_sequence(lat(v), lat(x1), lat(x2), p['hy'], p['hy_bias'])
    parts = [hl.reshape(nl, HY_CH)]
    u = z[:, HY_COLS:]
    ys_lat, ys_ctx = _s5_mixer(lat(u), ctx(u), p['s5'])
    if p['need_ctx']:
        hc = _hyena_sequence(ctx(v), ctx(x1), ctx(x2), p['hy'], p['hy_bias'])
        parts.append(hc.reshape(B * CTX, HY_CH))
    else:
        parts.append(jnp.zeros((B * CTX, HY_CH), F32))
    hy_all = jnp.concatenate(parts, axis=0)
    ys_all = jnp.concatenate([ys_lat.reshape(nl, S5_CH), ys_ctx.reshape(B * CTX, S5_CH)], axis=0)
    h = _even_out(tok, hy_all, ys_all, z, h, mods_l, p['s5_d'], p['s5_w_glu'].astype(BF16), p['w_out'].astype(BF16))
    return _ffn(tok, h, mods_l, p['norm_ffn'], p['ff_wg'].astype(BF16), p['ff_wu'].astype(BF16),
                p['ff_wd'].astype(BF16))


def _rope_tables(L, tm):
    t = jnp.arange(L)
    row = (t // GRID_W).astype(F32)[:, None]
    col = (t % GRID_W).astype(F32)[:, None]

    def pattern(dim):
        nf = dim // 4
        inv = ROPE_BASE ** (-jnp.arange(nf, dtype=F32) / nf)
        ar, ac = row * inv[None, :], col * inv[None, :]
        cos = jnp.concatenate([jnp.cos(ar)] * 2 + [jnp.cos(ac)] * 2, axis=1)
        z = jnp.zeros((L, nf), F32)
        s_up = jnp.concatenate([-jnp.sin(ar), z, -jnp.sin(ac), z], axis=1)
        s_dn = jnp.concatenate([z, jnp.sin(ar), z, jnp.sin(ac)], axis=1)
        return cos, s_up, s_dn

    def pad_mla(a, fill):
        return jnp.concatenate([jnp.full((L, MLA_NOPE), fill, F32), a,
                                jnp.full((L, HEAD_PAD - MLA_NOPE - MLA_ROPE), fill, F32)], axis=1)

    cm, um, dm = pattern(MLA_ROPE)
    cg, ug, dg = pattern(GQA_HD)
    mla = jnp.stack([pad_mla(cm, 1.0), pad_mla(um, 0.0), pad_mla(dm, 0.0)])
    gqa = jnp.stack([jnp.tile(cg, (1, 2)), jnp.tile(ug, (1, 2)), jnp.tile(dg, (1, 2))])
    ident = jnp.stack([jnp.ones((tm, LANE), F32), jnp.zeros((tm, LANE), F32), jnp.zeros((tm, LANE), F32)])
    return jnp.stack([jnp.concatenate([mla, ident], axis=1), jnp.concatenate([gqa, ident], axis=1)])


def _rope(x, tab, w):
    outs = []
    for h in range(x.shape[1] // LANE):
        xs = x[:, h * LANE:(h + 1) * LANE]
        outs.append(xs * tab[0] + pltpu.roll(xs, LANE - w, axis=1) * tab[1] + pltpu.roll(xs, w, axis=1) * tab[2])
    return outs[0] if len(outs) == 1 else jnp.concatenate(outs, axis=1)


def _rms(x, g):
    return x * lax.rsqrt(jnp.mean(x * x, axis=-1, keepdims=True) + EPS) * g


_O_CQ, _O_CKV, _O_GQ, _O_GK, _O_GV, _O_KR = 0, 256, 384, 896, 1024, 1152


def _odd_proj_kernel(z_ref, tab_ref, qn_ref, kvn_ref, wuq_ref, wuk_ref, wuv_ref, e_ref,
                     q_ref, k_ref, v_ref, gq_ref, gk_ref, gv_ref):
    z = z_ref[...]
    mt, gt = tab_ref[0], tab_ref[1]
    qn = _rms(z[:, _O_CQ:_O_CKV], qn_ref[...]).astype(BF16)
    q = jnp.dot(qn, wuq_ref[...], preferred_element_type=F32)
    q_ref[...] = (_rope(q, mt, MLA_ROPE // 4) * (MLA_SCALE * LOG2E)).astype(BF16)
    kvn = _rms(z[:, _O_CKV:_O_GQ], kvn_ref[...]).astype(BF16)
    k = (jnp.dot(kvn, wuk_ref[...], preferred_element_type=F32)
         + jnp.dot(z[:, _O_KR:], e_ref[...], precision=HI, preferred_element_type=F32))
    k_ref[...] = _rope(k, mt, MLA_ROPE // 4).astype(BF16)
    v_ref[...] = jnp.dot(kvn, wuv_ref[...], preferred_element_type=F32).astype(BF16)
    gq_ref[...] = (_rope(z[:, _O_GQ:_O_GK], gt, GQA_HD // 4) * GQA_SCALE).astype(BF16)
    gk_ref[...] = _rope(z[:, _O_GK:_O_GV], gt, GQA_HD // 4).astype(BF16)
    gv_ref[...] = z[:, _O_GV:_O_KR].astype(BF16)


def _odd_proj(tok, z, tabs, q_norm, kv_norm, w_uq, w_ukv):
    tm = tok.tm
    hq = MLA_HEADS * HEAD_PAD
    wq = jnp.pad(w_uq.reshape(Q_LORA, MLA_HEADS, MLA_NOPE + MLA_ROPE),
                 ((0, 0), (0, 0), (0, HEAD_PAD - MLA_NOPE - MLA_ROPE))).reshape(Q_LORA, hq).astype(BF16)
    wkv = w_ukv.reshape(KV_LORA, MLA_HEADS, MLA_NOPE + MLA_V)
    wk = jnp.pad(wkv[..., :MLA_NOPE], ((0, 0), (0, 0), (0, HEAD_PAD - MLA_NOPE))).reshape(KV_LORA, hq).astype(BF16)
    wv = wkv[..., MLA_NOPE:].reshape(KV_LORA, MLA_HEADS * MLA_V).astype(BF16)
    eye = jnp.eye(MLA_ROPE, dtype=F32)
    e_head = jnp.pad(eye, ((0, LANE - MLA_ROPE), (MLA_NOPE, HEAD_PAD - MLA_NOPE - MLA_ROPE)))
    e = jnp.tile(e_head, (1, MLA_HEADS))
    tab_blk = lambda i: (0, 0, jnp.where(i < tok.n_lat, i % tok.per_seq, tok.per_seq), 0)
    full = lambda shape: pl.BlockSpec(shape, lambda i: (0,) * len(shape))
    widths = (hq, hq, MLA_HEADS * MLA_V, GQA_HEADS * GQA_HD, GQA_KV * GQA_HD, GQA_KV * GQA_HD)
    return pl.pallas_call(
        _odd_proj_kernel,
        grid=(tok.n_all,),
        in_specs=[pl.BlockSpec((tm, OD_IN_PAD), lambda i: (i, 0)),
                  pl.BlockSpec((2, 3, tm, LANE), tab_blk),
                  full((1, Q_LORA)), full((1, KV_LORA)), full((Q_LORA, hq)), full((KV_LORA, hq)),
                  full((KV_LORA, MLA_HEADS * MLA_V)), full((LANE, hq))],
        out_specs=[pl.BlockSpec((tm, w), lambda i: (i, 0)) for w in widths],
        out_shape=[jax.ShapeDtypeStruct((tok.T, w), BF16) for w in widths],
        compiler_params=_cparams(("arbitrary",)),
    )(z, tabs, q_norm.reshape(1, -1), kv_norm.reshape(1, -1), wq, wk, wv, e)


def _mla_attn_kernel(q_ref, k_ref, vt_ref, o_ref, s_scr, p_scr, *, tk, nk, ks):
    tq = q_ref.shape[0]
    nslab = tk // ks
    dn = (((1,), (1,)), ((), ()))
    qs = [q_ref[:, h * HEAD_PAD:(h + 1) * HEAD_PAD] for h in range(2)]

    def qk_slab(h, c, j, mx):
        r = pl.multiple_of(c * tk + j * ks, ks)
        s = lax.dot_general(k_ref[pl.ds(r, ks), h * HEAD_PAD:(h + 1) * HEAD_PAD], qs[h], dn,
                            preferred_element_type=F32)
        s_scr[h, j * ks:(j + 1) * ks, :] = s
        return jnp.maximum(mx, jnp.max(s, axis=0, keepdims=True))

    def pv_slab(h, c, j):
        return jnp.dot(vt_ref[c, h * MLA_V:(h + 1) * MLA_V, j * ks:(j + 1) * ks], p_scr[h, j * ks:(j + 1) * ks, :],
                       preferred_element_type=F32)

    def step(x, c_sm, c_pv, c_qk, st):
        y = 1 - x
        m, l, acc, mc = st[x]
        m_new = jnp.maximum(m, mc)
        alpha = jnp.exp2(m - m_new)
        lsum = jnp.zeros((1, tq), F32)
        acc_y = st[y][2]
        mx_y = jnp.full((1, tq), NEG, F32)
        for j in range(nslab):
            acc_y = acc_y + pv_slab(y, c_pv, j)
            mx_y = qk_slab(y, c_qk, j, mx_y)
            p = jnp.exp2(s_scr[x, j * ks:(j + 1) * ks, :] - m_new)
            lsum = lsum + jnp.sum(p, axis=0, keepdims=True)
            p_scr[x, j * ks:(j + 1) * ks, :] = p.astype(BF16)
        new = [None, None]
        new[x] = (m_new, alpha * l + lsum, alpha * acc, mc)
        new[y] = (st[y][0], st[y][1], acc_y, mx_y)
        return tuple(new)

    def body(c, st):
        st = step(0, c, jnp.maximum(c - 1, 0), c, st)
        return step(1, c, c, jnp.minimum(c + 1, nk - 1), st)

    neg = jnp.full((1, tq), NEG, F32)
    zero = jnp.zeros((1, tq), F32)
    acc0 = jnp.zeros((MLA_V, tq), F32)
    p_scr[1] = jnp.zeros(p_scr.shape[1:], BF16)
    mx0 = neg
    for j in range(nslab):
        mx0 = qk_slab(0, 0, j, mx0)
    st = lax.fori_loop(0, nk, body, ((neg, zero, acc0, mx0), (neg, zero, acc0, neg)))
    acc1 = st[1][2]
    for j in range(nslab):
        acc1 = acc1 + pv_slab(1, nk - 1, j)
    out_t = jnp.concatenate([st[0][2] / st[0][1], acc1 / st[1][1]], axis=0)
    o_ref[...] = out_t.T.astype(o_ref.dtype)


MLA_TQ = (256, 128)
MLA_TK = (768, 512, 256, 128)
MLA_KS = 128


def _mla_attention(q, k, v):
    B, Lq, _ = q.shape
    Nk = k.shape[1]
    tq = _pick(Lq, MLA_TQ)
    tk = _pick(Nk, MLA_TK)
    nk = Nk // tk
    hp = MLA_HEADS // 2
    vt = v.reshape(B, nk, tk, hp, 2 * MLA_V).transpose(0, 3, 1, 4, 2)
    kern = functools.partial(_mla_attn_kernel, tk=tk, nk=nk, ks=math.gcd(tk, MLA_KS))
    return pl.pallas_call(
        kern,
        grid=(B, hp, Lq // tq),
        in_specs=[pl.BlockSpec((None, tq, 2 * HEAD_PAD), lambda b, h, i: (b, i, h)),
                  pl.BlockSpec((None, Nk, 2 * HEAD_PAD), lambda b, h, i: (b, 0, h)),
                  pl.BlockSpec((None, None, nk, 2 * MLA_V, tk), lambda b, h, i: (b, h, 0, 0, 0))],
        out_specs=pl.BlockSpec((None, tq, 2 * MLA_V), lambda b, h, i: (b, i, h)),
        out_shape=jax.ShapeDtypeStruct((B, Lq, MLA_HEADS * MLA_V), BF16),
        scratch_shapes=[pltpu.VMEM((2, tk, tq), F32), pltpu.VMEM((2, tk, tq), BF16)],
        compiler_params=_cparams(("arbitrary", "arbitrary", "arbitrary")),
    )(q, k, vt)


def _gqa_kernel(sink_ref, q_ref, kc_ref, vc_ref, *rest, L, has_band):
    if has_band:
        k_ref, v_ref, o_ref = rest
        qi = pl.program_id(1)
        start = pl.multiple_of(jnp.clip((qi - 1) * BLK, 0, L - 3 * BLK), BLK)
        qpos = qi * BLK + lax.broadcasted_iota(jnp.int32, (BLK, 3 * BLK), 0)
        kpos = start + lax.broadcasted_iota(jnp.int32, (BLK, 3 * BLK), 1)
        valid = jnp.abs(qpos - kpos) <= WINDOW
    else:
        (o_ref,) = rest
    dn = (((1,), (1,)), ((), ()))
    group = GQA_HEADS // GQA_KV
    outs = []
    for h in range(GQA_HEADS):
        ks = slice((h // group) * GQA_HD, (h // group + 1) * GQA_HD)
        q = q_ref[:, h * GQA_HD:(h + 1) * GQA_HD]
        sink = sink_ref[h]
        s_ctx = lax.dot_general(q, kc_ref[:, ks], dn, preferred_element_type=F32)
        m = jnp.maximum(jnp.max(s_ctx, axis=-1, keepdims=True), sink)
        if has_band:
            s_band = lax.dot_general(q, k_ref[pl.ds(start, 3 * BLK), ks], dn, preferred_element_type=F32)
            s_band = jnp.where(valid, s_band, NEG)
            m = jnp.maximum(m, jnp.max(s_band, axis=-1, keepdims=True))
        p_ctx = jnp.exp(s_ctx - m)
        den = jnp.sum(p_ctx, axis=-1, keepdims=True) + jnp.exp(sink - m)
        if has_band:
            p_band = jnp.exp(s_band - m)
            den = den + jnp.sum(p_band, axis=-1, keepdims=True)
        inv = 1.0 / den
        o = jnp.dot((p_ctx * inv).astype(BF16), vc_ref[:, ks], preferred_element_type=F32)
        if has_band:
            o = o + jnp.dot((p_band * inv).astype(BF16), v_ref[pl.ds(start, 3 * BLK), ks],
                            preferred_element_type=F32)
        outs.append(o)
    o_ref[...] = jnp.concatenate(outs, axis=1).astype(o_ref.dtype)


def _gqa_attention(sink, q, kc, vc, k=None, v=None):
    B, Lq, _ = q.shape
    CTX = kc.shape[1]
    has_band = k is not None
    kw = GQA_KV * GQA_HD
    in_specs = [pl.BlockSpec(memory_space=pltpu.SMEM),
                pl.BlockSpec((None, BLK, GQA_HEADS * GQA_HD), lambda b, i: (b, i, 0)),
                pl.BlockSpec((None, CTX, kw), lambda b, i: (b, 0, 0)),
                pl.BlockSpec((None, CTX, kw), lambda b, i: (b, 0, 0))]
    args = [sink.astype(F32), q, kc, vc]
    if has_band:
        in_specs += [pl.BlockSpec((None, Lq, kw), lambda b, i: (b, 0, 0))] * 2
        args += [k, v]
    kern = functools.partial(_gqa_kernel, L=Lq, has_band=has_band)
    return pl.pallas_call(
        kern,
        grid=(B, Lq // BLK),
        in_specs=in_specs,
        out_specs=pl.BlockSpec((None, BLK, GQA_HEADS * GQA_HD), lambda b, i: (b, i, 0)),
        out_shape=jax.ShapeDtypeStruct((B, Lq, GQA_HEADS * GQA_HD), BF16),
        compiler_params=_cparams(("arbitrary", "arbitrary")),
    )(*args)


def _odd_out_kernel(a_ref, g_ref, h_ref, mod_ref, wo_ref, o_ref):
    half = a_ref.shape[1]
    ol = (jnp.dot(a_ref[...], wo_ref[:half, :], preferred_element_type=F32)
          + jnp.dot(g_ref[...], wo_ref[half:, :], preferred_element_type=F32))
    o_ref[...] = h_ref[...] + mod_ref[2:3, :] * ol


def _odd_out(tok, mla, gqa, h, mods_l, w_out):
    tm = tok.tm
    half = mla.shape[1]
    return pl.pallas_call(
        _odd_out_kernel,
        grid=(tok.n_all,),
        in_specs=[pl.BlockSpec((tm, half), lambda i: (i, 0)),
                  pl.BlockSpec((tm, half), lambda i: (i, 0)),
                  pl.BlockSpec((tm, D), lambda i: (i, 0)),
                  pl.BlockSpec((None, N_MOD, D), lambda i: (tok.mod_row(i), 0, 0)),
                  pl.BlockSpec((D, D), lambda i: (0, 0))],
        out_specs=pl.BlockSpec((tm, D), lambda i: (i, 0)),
        out_shape=jax.ShapeDtypeStruct((tok.T, D), F32),
        compiler_params=_cparams(("arbitrary",)),
    )(mla, gqa, h, mods_l, w_out)


MOE_TR = 1024
MOE_TF = 896
ROUTE_W = 8


def _router_kernel(h_ref, mod_ref, g_ref, r_ref, y_ref, route_ref):
    y = _ada_norm(h_ref[...], g_ref[...], mod_ref[...], 3, 4)
    y_ref[...] = y
    logits = jnp.dot(y, r_ref[...], precision=HI, preferred_element_type=F32)
    lane = lax.broadcasted_iota(jnp.int32, logits.shape, 1)
    lg = jnp.where(lane < N_EXP, logits, -jnp.inf)
    m1 = jnp.max(lg, axis=-1, keepdims=True)
    i1 = jnp.min(jnp.where(lg == m1, lane, LANE), axis=-1, keepdims=True)
    lg2 = jnp.where(lane == i1, -jnp.inf, lg)
    m2 = jnp.max(lg2, axis=-1, keepdims=True)
    i2 = jnp.min(jnp.where(lg2 == m2, lane, LANE), axis=-1, keepdims=True)
    e = jnp.exp(m2 - m1)
    w1 = 1.0 / (1.0 + e)
    route = (jnp.where(lane == 0, w1, 0.0) + jnp.where(lane == 1, e * w1, 0.0)
             + jnp.where(lane == 2, i1.astype(F32), 0.0) + jnp.where(lane == 3, i2.astype(F32), 0.0))
    route_ref[...] = route[:, :ROUTE_W]


def _moe_router(tok, h, mods_l, gain, router):
    tm = tok.tm
    rp = jnp.pad(router, ((0, 0), (0, LANE - N_EXP)))
    return pl.pallas_call(
        _router_kernel,
        grid=(tok.n_all,),
        in_specs=[pl.BlockSpec((tm, D), lambda i: (i, 0)),
                  pl.BlockSpec((None, N_MOD, D), lambda i: (tok.mod_row(i), 0, 0)),
                  pl.BlockSpec((1, D), lambda i: (0, 0)),
                  pl.BlockSpec((D, LANE), lambda i: (0, 0))],
        out_specs=[pl.BlockSpec((tm, D), lambda i: (i, 0)), pl.BlockSpec((tm, ROUTE_W), lambda i: (i, 0))],
        out_shape=[jax.ShapeDtypeStruct((tok.T, D), F32), jax.ShapeDtypeStruct((tok.T, ROUTE_W), F32)],
        compiler_params=_cparams(("arbitrary",)),
    )(h, mods_l, gain.reshape(1, D), rp)


def _moe_plan(route, tr):
    T = route.shape[0]
    flat = route[:, 2:4].astype(jnp.int32).reshape(-1)
    onehot = (flat[:, None] == jnp.arange(N_EXP, dtype=jnp.int32)[None, :]).astype(jnp.int32)
    csum = jnp.cumsum(onehot, axis=0)
    rank = jnp.sum((csum - onehot) * onehot, axis=1)
    padded = (csum[-1] + tr - 1) // tr * tr
    ends = jnp.cumsum(padded)
    pos = (ends - padded)[flat] + rank
    n_tiles = (2 * T + N_EXP * (tr - 1)) // tr
    src = jnp.zeros((n_tiles * tr,), jnp.int32).at[pos].set(jnp.arange(2 * T, dtype=jnp.int32) // 2,
                                                            unique_indices=True)
    starts = jnp.arange(n_tiles, dtype=jnp.int32) * tr
    tile_expert = jnp.minimum(jnp.sum((starts[:, None] >= ends[None, :]).astype(jnp.int32), axis=1), N_EXP - 1)
    n_valid = (ends[-1] // tr).astype(jnp.int32).reshape(1)
    return src, tile_expert.astype(jnp.int32), n_valid, pos.reshape(T, 2)


def _gather_rows(idx_ref, src_hbm, dst, sem, n):
    def issue(r, carry):
        pltpu.make_async_copy(src_hbm.at[pl.ds(idx_ref[r], 1), :], dst.at[pl.ds(r, 1), :], sem).start()
        return carry

    lax.fori_loop(0, n, issue, 0, unroll=8)


def _wait_rows(src_hbm, dst, sem, n):
    pltpu.make_async_copy(src_hbm.at[pl.ds(0, n), :], dst, sem).wait()


def _moe_expert_kernel(te_ref, nv_ref, idx_ref, idxn_ref, y_hbm, wg_ref, wu_ref, wd_ref, o_ref,
                       xbuf, y_scr, acc_scr, sem, *, tr):
    t, f = pl.program_id(0), pl.program_id(1)
    nt, nf = pl.num_programs(0), pl.num_programs(1)
    valid = t < nv_ref[0]
    slot = lax.rem(t, 2)

    @pl.when((f == 0) & (t == 0))
    def _():
        _gather_rows(idx_ref, y_hbm, xbuf.at[0], sem.at[0], tr)

    @pl.when((f == 0) & (t + 1 < nv_ref[0]))
    def _():
        _gather_rows(idxn_ref, y_hbm, xbuf.at[1 - slot], sem.at[1 - slot], tr)

    @pl.when((f == 0) & valid)
    def _():
        _wait_rows(y_hbm, xbuf.at[slot], sem.at[slot], tr)
        y_scr[...] = xbuf[slot].astype(BF16)
        acc_scr[...] = jnp.zeros_like(acc_scr)

    @pl.when(valid)
    def _():
        y = y_scr[...]
        gate = jnp.dot(y, wg_ref[...], preferred_element_type=F32)
        up = jnp.dot(y, wu_ref[...], preferred_element_type=F32)
        act = (gate * jax.nn.sigmoid(gate) * up).astype(BF16)
        acc_scr[...] += jnp.dot(act, wd_ref[...], preferred_element_type=F32)

    @pl.when(f == nf - 1)
    def _():
        o_ref[...] = jnp.where(valid, acc_scr[...], 0.0)


def _moe_experts(y, src, tile_expert, n_valid, wg, wu, wd, tr):
    n_tiles = src.shape[0] // tr
    tf = MOE_TF
    kern = functools.partial(_moe_expert_kernel, tr=tr)
    smem = functools.partial(pl.BlockSpec, memory_space=pltpu.SMEM)
    grid_spec = pltpu.PrefetchScalarGridSpec(
        num_scalar_prefetch=2,
        grid=(n_tiles, EXP_FF // tf),
        in_specs=[smem((tr,), lambda t, f, te, nv: (t,)),
                  smem((tr,), lambda t, f, te, nv: (jnp.minimum(t + 1, n_tiles - 1),)),
                  pl.BlockSpec(memory_space=pl.ANY),
                  pl.BlockSpec((None, D, tf), lambda t, f, te, nv: (te[t], 0, f)),
                  pl.BlockSpec((None, D, tf), lambda t, f, te, nv: (te[t], 0, f)),
                  pl.BlockSpec((None, tf, D), lambda t, f, te, nv: (te[t], f, 0))],
        out_specs=pl.BlockSpec((tr, D), lambda t, f, te, nv: (t, 0)),
        scratch_shapes=[pltpu.VMEM((2, tr, D), F32), pltpu.VMEM((tr, D), BF16), pltpu.VMEM((tr, D), F32),
                        pltpu.SemaphoreType.DMA((2,))])
    return pl.pallas_call(
        kern,
        grid_spec=grid_spec,
        out_shape=jax.ShapeDtypeStruct((n_tiles * tr, D), F32),
        compiler_params=_cparams(("arbitrary", "arbitrary")),
    )(tile_expert, n_valid, src, src, y, wg, wu, wd)


def _moe_combine_kernel(idx_ref, idxn_ref, o_hbm, h_ref, route_ref, mod_ref, out_ref, buf, sem, *, tm):
    i = pl.program_id(0)
    n = pl.num_programs(0)
    slot = lax.rem(i, 2)

    @pl.when(i == 0)
    def _():
        _gather_rows(idx_ref, o_hbm, buf.at[0], sem.at[0], 2 * tm)

    @pl.when(i + 1 < n)
    def _():
        _gather_rows(idxn_ref, o_hbm, buf.at[1 - slot], sem.at[1 - slot], 2 * tm)

    _wait_rows(o_hbm, buf.at[slot], sem.at[slot], 2 * tm)
    r = route_ref[...]
    mix = r[:, 0:1] * buf[slot, :tm, :] + r[:, 1:2] * buf[slot, tm:, :]
    out_ref[...] = h_ref[...] + mod_ref[5:6, :] * mix


def _moe_combine(tok, o_sorted, pos, h, route, mods_l):
    tm = tok.tm
    n = tok.n_all
    idx = pos.reshape(n, tm, 2).transpose(0, 2, 1).reshape(-1)
    kern = functools.partial(_moe_combine_kernel, tm=tm)
    smem = functools.partial(pl.BlockSpec, memory_space=pltpu.SMEM)
    return pl.pallas_call(
        kern,
        grid=(n,),
        in_specs=[smem((2 * tm,), lambda i: (i,)),
                  smem((2 * tm,), lambda i: (jnp.minimum(i + 1, n - 1),)),
                  pl.BlockSpec(memory_space=pl.ANY),
                  pl.BlockSpec((tm, D), lambda i: (i, 0)),
                  pl.BlockSpec((tm, ROUTE_W), lambda i: (i, 0)),
                  pl.BlockSpec((None, N_MOD, D), lambda i: (tok.mod_row(i), 0, 0))],
        out_specs=pl.BlockSpec((tm, D), lambda i: (i, 0)),
        out_shape=jax.ShapeDtypeStruct((tok.T, D), F32),
        scratch_shapes=[pltpu.VMEM((2, 2 * tm, D), F32), pltpu.SemaphoreType.DMA((2,))],
        compiler_params=_cparams(("arbitrary",)),
    )(idx, idx, o_sorted, h, route, mods_l)


def _moe(tok, h, mods_l, gain, router, wg, wu, wd):
    y, route = _moe_router(tok, h, mods_l, gain, router)
    src, tile_expert, n_valid, pos = _moe_plan(route, MOE_TR)
    o_sorted = _moe_experts(y, src, tile_expert, n_valid, wg, wu, wd, MOE_TR)
    return _moe_combine(tok, o_sorted, pos, h, route, mods_l)


def _odd_layer(tok, h, mods_l, tabs, p):
    B, L, CTX = tok.B, tok.L, tok.CTX
    nl = B * L
    w = p['w_in']
    w_in = jnp.concatenate([w[:, :Q_LORA + KV_LORA], w[:, 416:1184], w[:, 384:416],
                            jnp.zeros((D, OD_IN_PAD - 1184), w.dtype)], axis=1).astype(BF16)
    z = _norm_matmul(tok, h, mods_l, p['norm_mix'], w_in, OD_IN_PAD)
    q, k, v, gq, gk, gv = _odd_proj(tok, z, tabs, p['q_norm'], p['kv_norm'], p['w_uq'], p['w_ukv'])
    lat = lambda a: a[:nl].reshape(B, L, -1)
    ctx = lambda a: a[nl:].reshape(B, CTX, -1)
    cat = lambda a: jnp.concatenate([ctx(a), lat(a)], axis=1)
    mla_l = _mla_attention(lat(q), cat(k), cat(v))
    gqa_l = _gqa_attention(p['sink'], lat(gq), ctx(gk), ctx(gv), lat(gk), lat(gv))
    if p['need_ctx']:
        mla_c = _mla_attention(ctx(q), ctx(k), ctx(v))
        gqa_c = _gqa_attention(p['sink'], ctx(gq), ctx(gk), ctx(gv))
    else:
        mla_c = jnp.zeros((B, CTX, MLA_HEADS * MLA_V), BF16)
        gqa_c = jnp.zeros((B, CTX, GQA_HEADS * GQA_HD), BF16)
    flat = lambda a, c: jnp.concatenate([a.reshape(nl, -1), c.reshape(B * CTX, -1)], axis=0)
    h = _odd_out(tok, flat(mla_l, mla_c), flat(gqa_l, gqa_c), h, mods_l, p['w_out'].astype(BF16))
    return _moe(tok, h, mods_l, p['norm_ffn'], p['router'], p['moe_wg'].astype(BF16),
                p['moe_wu'].astype(BF16), p['moe_wd'].astype(BF16))


def _final_norm_kernel(h_ref, g_ref, o_ref):
    o_ref[...] = _rms(h_ref[...], g_ref[...])


def _final_norm(tok, h, gain):
    tm = tok.tm
    return pl.pallas_call(
        _final_norm_kernel,
        grid=(tok.n_lat,),
        in_specs=[pl.BlockSpec((tm, D), lambda i: (i, 0)), pl.BlockSpec((1, D), lambda i: (0, 0))],
        out_specs=pl.BlockSpec((tm, D), lambda i: (i, 0)),
        out_shape=jax.ShapeDtypeStruct((tok.B * tok.L, D), F32),
        compiler_params=_cparams(("arbitrary",)),
    )(h, gain.reshape(1, D))


def kernel(x, c, ctx, c_ctx, mod_w, mod_b, norm_mix, norm_ffn, final_norm,
           ev_w_in, ev_conv_w, ev_conv_b, hy_w1, hy_b1, hy_w2, hy_b2, hy_w3, hy_freq, hy_decay, hy_bias,
           s5_a_re, s5_a_im, s5_log_dt, s5_b_re, s5_b_im, s5_c_re, s5_c_im, s5_d, s5_w_glu, ev_w_out,
           ff_w_gate, ff_w_up, ff_w_down,
           od_w_in, mla_q_norm, mla_w_uq, mla_kv_norm, mla_w_ukv, gqa_sink, od_w_out,
           moe_router, moe_w_gate, moe_w_up, moe_w_down):
    B, L, _ = x.shape
    CTX = ctx.shape[1]
    tok = _Tok(B, L, CTX, _pick(math.gcd(L, B * CTX), (512, 256, 128)))
    cond_t = jnp.concatenate([c, c_ctx[None, :], jnp.zeros((8 - B - 1, D), F32)], axis=0).T
    mods = _modulations(cond_t, B + 1, mod_w, mod_b)
    tabs = _rope_tables(L, tok.tm)
    h = jnp.concatenate([x.reshape(B * L, D), ctx.reshape(B * CTX, D)], axis=0)
    for l in range(DEPTH):
        i = l // 2
        need_ctx = l < DEPTH - 1
        if l % 2 == 0:
            p = dict(norm_mix=norm_mix[l], norm_ffn=norm_ffn[l], w_in=ev_w_in[i], conv_w=ev_conv_w[i],
                     conv_b=ev_conv_b[i],
                     hy=(hy_w1[i], hy_b1[i], hy_w2[i], hy_b2[i], hy_w3[i], hy_freq[i], hy_decay[i]),
                     hy_bias=hy_bias[i],
                     s5=(s5_a_re[i], s5_a_im[i], s5_log_dt[i], s5_b_re[i], s5_b_im[i], s5_c_re[i], s5_c_im[i]),
                     s5_d=s5_d[i], s5_w_glu=s5_w_glu[i], w_out=ev_w_out[i],
                     ff_wg=ff_w_gate[i], ff_wu=ff_w_up[i], ff_wd=ff_w_down[i], need_ctx=need_ctx)
            h = _even_layer(tok, h, mods[l], p)
        else:
            p = dict(norm_mix=norm_mix[l], norm_ffn=norm_ffn[l], w_in=od_w_in[i], q_norm=mla_q_norm[i],
                     w_uq=mla_w_uq[i], kv_norm=mla_kv_norm[i], w_ukv=mla_w_ukv[i], sink=gqa_sink[i],
                     w_out=od_w_out[i], router=moe_router[i], moe_wg=moe_w_gate[i], moe_wu=moe_w_up[i],
                     moe_wd=moe_w_down[i], need_ctx=need_ctx)
            h = _odd_layer(tok, h, mods[l], tabs, p)
    return _final_norm(tok, h, final_norm).reshape(B, L, D)
```

```python
import functools
import math

import jax
import jax.numpy as jnp
from jax import lax
from jax.experimental import pallas as pl
from jax.experimental.pallas import tpu as pltpu

F32 = jnp.float32
BF16 = jnp.bfloat16
HI = lax.Precision.HIGHEST

D = 1024
DEPTH = 4
GRID_W = 64
EPS = 1e-6
NEG = -1e30
N_MOD = 6

HY_CH = 512
HY_ORDER = 2
HY_BANDS = 16
HY_EMB = 1 + 2 * HY_BANDS
HY_FFN = 64
HY_SHIFT = 0.05
HY_COLS = (HY_ORDER + 1) * HY_CH
S5_CH = 512
S5_GROUP = 16
S5_NG = S5_CH // S5_GROUP
S5_P = 64
S5_T = 16
S5_SEG = 8
EV_IN = HY_COLS + S5_CH

MLA_HEADS = 8
MLA_NOPE = 64
MLA_ROPE = 32
MLA_V = 64
Q_LORA = 256
KV_LORA = 128
GQA_HEADS = 8
GQA_KV = 2
GQA_HD = 64
WINDOW = 128
BLK = 128
ROPE_BASE = 10000.0
MLA_SCALE = (MLA_NOPE + MLA_ROPE) ** -0.5
GQA_SCALE = GQA_HD ** -0.5
LOG2E = math.log2(math.e)
HEAD_PAD = 128
OD_IN_PAD = 1280

D_FF = 2816
N_EXP = 8
EXP_FF = 3584

LANE = 128
FFT_N2 = 128
VMEM_LIMIT = 56 * 1024 * 1024


def _cparams(sem):
    return pltpu.CompilerParams(dimension_semantics=sem, vmem_limit_bytes=VMEM_LIMIT)


def _pick(n, cands):
    for c in cands:
        if n % c == 0:
            return c
    raise ValueError(f"no tile for {n} in {cands}")


def _mod_kernel(ct_ref, w_ref, b_ref, o_ref, *, nrows):
    c = ct_ref[...]
    s = c * jax.nn.sigmoid(c)
    w = w_ref[...]
    rows = [jnp.sum(w * s[:, r:r + 1], axis=0, keepdims=True) for r in range(nrows)]
    rows.append(jnp.zeros((8 - nrows, w.shape[1]), F32))
    o_ref[...] = jnp.concatenate(rows, axis=0) + b_ref[...]


def _modulations(cond_t, nrows, mod_w, mod_b):
    tn = 1536
    out = pl.pallas_call(
        functools.partial(_mod_kernel, nrows=nrows),
        grid=(DEPTH, N_MOD * D // tn),
        in_specs=[pl.BlockSpec((D, 8), lambda l, j: (0, 0)),
                  pl.BlockSpec((None, D, tn), lambda l, j: (l, 0, j)),
                  pl.BlockSpec((None, 1, tn), lambda l, j: (l, 0, j))],
        out_specs=pl.BlockSpec((None, 8, tn), lambda l, j: (l, 0, j)),
        out_shape=jax.ShapeDtypeStruct((DEPTH, 8, N_MOD * D), F32),
        compiler_params=_cparams(("arbitrary", "arbitrary")),
    )(cond_t, mod_w, mod_b.reshape(DEPTH, 1, N_MOD * D))
    return out.reshape(DEPTH, 8, N_MOD, D)


class _Tok:
    def __init__(self, B, L, CTX, tm):
        assert L % tm == 0 and (B * CTX) % tm == 0
        self.B, self.L, self.CTX, self.tm = B, L, CTX, tm
        self.n_lat = B * L // tm
        self.n_all = self.n_lat + B * CTX // tm
        self.T = B * (L + CTX)
        self.per_seq = L // tm

    def mod_row(self, i):
        return jnp.where(i < self.n_lat, i // self.per_seq, self.B)


def _ada_norm(x, gain, mod, shift_idx, scale_idx):
    y = x * lax.rsqrt(jnp.mean(x * x, axis=-1, keepdims=True) + EPS) * gain
    return y * (1.0 + mod[scale_idx:scale_idx + 1, :]) + mod[shift_idx:shift_idx + 1, :]


def _norm_mm_kernel(h_ref, mod_ref, g_ref, w_ref, o_ref, y_scr):
    @pl.when(pl.program_id(1) == 0)
    def _():
        y_scr[...] = _ada_norm(h_ref[...], g_ref[...], mod_ref[...], 0, 1).astype(BF16)

    o_ref[...] = jnp.dot(y_scr[...], w_ref[...], preferred_element_type=F32).astype(o_ref.dtype)


def _norm_matmul(tok, h, mods_l, gain, w, tn, out_dtype=F32):
    tm, n = tok.tm, w.shape[1]
    return pl.pallas_call(
        _norm_mm_kernel,
        grid=(tok.n_all, n // tn),
        in_specs=[pl.BlockSpec((tm, D), lambda i, j: (i, 0)),
                  pl.BlockSpec((None, N_MOD, D), lambda i, j: (tok.mod_row(i), 0, 0)),
                  pl.BlockSpec((1, D), lambda i, j: (0, 0)),
                  pl.BlockSpec((D, tn), lambda i, j: (0, j))],
        out_specs=pl.BlockSpec((tm, tn), lambda i, j: (i, j)),
        out_shape=jax.ShapeDtypeStruct((tok.T, n), out_dtype),
        scratch_shapes=[pltpu.VMEM((tm, D), BF16)],
        compiler_params=_cparams(("arbitrary", "arbitrary")),
    )(h, mods_l, gain.reshape(1, D), w)


def _short_conv_kernel(z_ref, zp_ref, zn_ref, w_ref, b_ref, v_ref, x1_ref, x2_ref, *, tm, n_lat, L, CTX):
    i = pl.program_id(0)
    is_lat = i < n_lat
    seqlen = jnp.where(is_lat, L, CTX)
    off = jnp.where(is_lat, i * tm, (i - n_lat) * tm)
    first = lax.rem(off, seqlen) == 0
    last = lax.rem(off + tm, seqlen) == 0
    z = z_ref[...]
    prev_row = jnp.where(first, 0.0, zp_ref[7:8, :])
    next_row = jnp.where(last, 0.0, zn_ref[0:1, :])
    rid = lax.broadcasted_iota(jnp.int32, z.shape, 0)
    zm1 = jnp.where(rid == 0, prev_row, pltpu.roll(z, 1, axis=0))
    zp1 = jnp.where(rid == tm - 1, next_row, pltpu.roll(z, tm - 1, axis=0))
    out = b_ref[...] + zm1 * w_ref[0:1, :] + z * w_ref[1:2, :] + zp1 * w_ref[2:3, :]
    v_ref[...] = out[:, :HY_CH]
    x1_ref[...] = out[:, HY_CH:2 * HY_CH]
    x2_ref[...] = out[:, 2 * HY_CH:]


def _short_conv(tok, z, conv_w, conv_b):
    tm = _pick(math.gcd(tok.L, tok.CTX), (256, 128))
    n_lat = tok.B * tok.L // tm
    n_all = tok.T // tm
    r8 = tm // 8
    kern = functools.partial(_short_conv_kernel, tm=tm, n_lat=n_lat, L=tok.L, CTX=tok.CTX)
    o = jax.ShapeDtypeStruct((tok.T, HY_CH), F32)
    return pl.pallas_call(
        kern,
        grid=(n_all,),
        in_specs=[pl.BlockSpec((tm, HY_COLS), lambda i: (i, 0)),
                  pl.BlockSpec((8, HY_COLS), lambda i: (jnp.maximum(i * r8 - 1, 0), 0)),
                  pl.BlockSpec((8, HY_COLS), lambda i: (jnp.minimum((i + 1) * r8, n_all * r8 - 1), 0)),
                  pl.BlockSpec((8, HY_COLS), lambda i: (0, 0)),
                  pl.BlockSpec((1, HY_COLS), lambda i: (0, 0))],
        out_specs=[pl.BlockSpec((tm, HY_CH), lambda i: (i, 0))] * 3,
        out_shape=[o, o, o],
        compiler_params=_cparams(("arbitrary",)),
    )(z, z, z, jnp.pad(conv_w, ((0, 8 - conv_w.shape[0]), (0, 0))), conv_b.reshape(1, HY_COLS))


def _filter_kernel(f_ref, w1_ref, b1_ref, w2_ref, b2_ref, w3_ref, fr_ref, dec_ref, k_ref, s_ref):
    @pl.when(pl.program_id(0) == 0)
    def _():
        s_ref[...] = jnp.zeros_like(s_ref)

    f = f_ref[...]
    fr = fr_ref[...]
    hid = jnp.sin(fr * (jnp.dot(f, w1_ref[...], precision=HI, preferred_element_type=F32) + b1_ref[...]))
    hid = jnp.sin(fr * (jnp.dot(hid, w2_ref[...], precision=HI, preferred_element_type=F32) + b2_ref[...]))
    h = jnp.dot(hid, w3_ref[...], precision=HI, preferred_element_type=F32)
    t01 = f[:, 0:1]
    valid = f[:, LANE - 1:LANE]
    k = h * (jnp.exp(-t01 * jnp.abs(dec_ref[...])) + HY_SHIFT) * valid
    k_ref[0] = k[:, :HY_CH]
    k_ref[1] = k[:, HY_CH:]
    s_ref[...] += jnp.sum(jnp.abs(k), axis=0, keepdims=True)


def _hyena_filters(L, n, w1, b1, w2, b2, w3, freq, decay):
    row = jnp.arange(n)
    fwd = row < L
    bwd = row > n - L
    t = jnp.where(fwd, row, n - row).astype(F32)
    t01 = t / L
    bands = jnp.linspace(1e-4, HY_BANDS - 1, HY_BANDS, dtype=F32)
    ang = (2.0 * math.pi / L) * t[:, None] * bands[None, :]
    valid = (fwd | bwd).astype(F32)
    feats = jnp.concatenate([t01[:, None], jnp.cos(ang), -jnp.sin(ang),
                             jnp.zeros((n, LANE - 1 - HY_EMB), F32), valid[:, None]], axis=-1)
    w1p = jnp.pad(w1, ((0, LANE - HY_EMB), (0, 0)))
    tr = _pick(L, (512, 256))
    nb_half = n // 2 // tr
    ncol = HY_ORDER * HY_CH
    k, ssum = pl.pallas_call(
        _filter_kernel,
        grid=(n // tr,),
        in_specs=[pl.BlockSpec((tr, LANE), lambda i: (i, 0)),
                  pl.BlockSpec((LANE, HY_FFN), lambda i: (0, 0)),
                  pl.BlockSpec((1, HY_FFN), lambda i: (0, 0)),
                  pl.BlockSpec((HY_FFN, HY_FFN), lambda i: (0, 0)),
                  pl.BlockSpec((1, HY_FFN), lambda i: (0, 0)),
                  pl.BlockSpec((HY_FFN, ncol), lambda i: (0, jnp.where(i < nb_half, 0, 1))),
                  pl.BlockSpec((1, HY_FFN), lambda i: (0, 0)),
                  pl.BlockSpec((1, ncol), lambda i: (0, 0))],
        out_specs=[pl.BlockSpec((HY_ORDER, tr, HY_CH), lambda i: (0, i, 0)),
                   pl.BlockSpec((1, ncol), lambda i: (0, 0))],
        out_shape=[jax.ShapeDtypeStruct((HY_ORDER, n, HY_CH), F32),
                   jax.ShapeDtypeStruct((1, ncol), F32)],
        compiler_params=_cparams(("arbitrary",)),
    )(feats, w1p, b1.reshape(1, -1), w2, b2.reshape(1, -1), w3, freq.reshape(1, -1), decay.reshape(1, ncol))
    return k, (1.0 / ssum).reshape(HY_ORDER, 1, HY_CH)


def _dft_tables(n1, r_in):
    n = n1 * FFT_N2
    k1 = jnp.arange(n1)
    a1 = (2.0 * math.pi / n1) * ((k1[:, None] * jnp.arange(r_in)[None, :]) % n1).astype(F32)
    f1 = jnp.concatenate([jnp.cos(a1), -jnp.sin(a1)], axis=0)
    f3 = jnp.concatenate([jnp.cos(a1).T, -jnp.sin(a1).T], axis=1) / n
    n2 = jnp.arange(FFT_N2)
    kk = k1[:, None, None] + n1 * n2[None, :, None]
    ang = (2.0 * math.pi / n) * ((kk * n2[None, None, :]) % n).astype(F32)
    gr, gi = jnp.cos(ang), -jnp.sin(ang)
    g = jnp.concatenate([jnp.concatenate([gr, -gi], axis=2), jnp.concatenate([gi, gr], axis=2)], axis=1)
    return f1.astype(BF16), f3.astype(BF16), g.astype(BF16), jnp.swapaxes(g, 1, 2).astype(BF16)


def _fft1_kernel(f_ref, x_ref, o_ref):
    o_ref[...] = jnp.dot(f_ref[...], x_ref[...].astype(BF16), preferred_element_type=F32).astype(o_ref.dtype)


def _fft_stage1(f1, x):
    nb, r_in, cols = x.shape
    m = f1.shape[0]
    tn = _pick(cols, (8192, 4096, 2048))
    return pl.pallas_call(
        _fft1_kernel,
        grid=(nb, cols // tn),
        in_specs=[pl.BlockSpec((m, r_in), lambda b, j: (0, 0)),
                  pl.BlockSpec((None, r_in, tn), lambda b, j: (b, 0, j))],
        out_specs=pl.BlockSpec((None, m, tn), lambda b, j: (b, 0, j)),
        out_shape=jax.ShapeDtypeStruct((nb, m, cols), BF16),
        compiler_params=_cparams(("arbitrary", "arbitrary")),
    )(f1, x)


def _fft_filt_kernel(a_ref, g_ref, s_ref, o_ref):
    c = a_ref.shape[-1]
    a = a_ref[...].reshape(2 * FFT_N2, c)
    o_ref[...] = jnp.dot(g_ref[...], a, preferred_element_type=F32) * s_ref[...]


def _fft_filter_spectrum(a, g, inv):
    no, _, n1, _, c = a.shape
    return pl.pallas_call(
        _fft_filt_kernel,
        grid=(n1, no),
        in_specs=[pl.BlockSpec((None, 2, None, FFT_N2, c), lambda k, o: (o, 0, k, 0, 0)),
                  pl.BlockSpec((None, 2 * FFT_N2, 2 * FFT_N2), lambda k, o: (k, 0, 0)),
                  pl.BlockSpec((None, 1, c), lambda k, o: (o, 0, 0))],
        out_specs=pl.BlockSpec((None, None, 2 * FFT_N2, c), lambda k, o: (o, k, 0, 0)),
        out_shape=jax.ShapeDtypeStruct((no, n1, 2 * FFT_N2, c), F32),
        compiler_params=_cparams(("arbitrary", "arbitrary")),
    )(a, g, inv)


def _fft_mid_kernel(a_ref, g_ref, gt_ref, kh_ref, o_ref):
    c = a_ref.shape[-1]
    a = a_ref[...].reshape(2 * FFT_N2, c)
    x = jnp.dot(g_ref[...], a, preferred_element_type=F32)
    xr, xi = x[:FFT_N2], x[FFT_N2:]
    kr, ki = kh_ref[:FFT_N2, :], kh_ref[FFT_N2:, :]
    y = jnp.concatenate([xr * kr - xi * ki, xr * ki + xi * kr], axis=0).astype(BF16)
    bm = jnp.dot(gt_ref[...], y, preferred_element_type=F32)
    o_ref[...] = bm.reshape(2, FFT_N2, c).astype(o_ref.dtype)


def _fft_mid(a, g, gt, khat):
    nb, _, n1, _, c = a.shape
    return pl.pallas_call(
        _fft_mid_kernel,
        grid=(n1, nb),
        in_specs=[pl.BlockSpec((None, 2, None, FFT_N2, c), lambda k, b: (b, 0, k, 0, 0)),
                  pl.BlockSpec((None, 2 * FFT_N2, 2 * FFT_N2), lambda k, b: (k, 0, 0)),
                  pl.BlockSpec((None, 2 * FFT_N2, 2 * FFT_N2), lambda k, b: (k, 0, 0)),
                  pl.BlockSpec((None, 2 * FFT_N2, c), lambda k, b: (k, 0, 0))],
        out_specs=pl.BlockSpec((None, 2, None, FFT_N2, c), lambda k, b: (b, 0, k, 0, 0)),
        out_shape=jax.ShapeDtypeStruct(a.shape, BF16),
        compiler_params=_cparams(("arbitrary", "arbitrary")),
    )(a, g, gt, khat)


def _fft3_kernel(f_ref, bm_ref, y_ref, gate_ref, bias_ref, o_ref):
    conv = jnp.dot(f_ref[...], bm_ref[...], preferred_element_type=F32)
    o_ref[...] = gate_ref[...] * (conv + y_ref[...] * bias_ref[...])


def _fft_stage3(f3, bm, y, gate, bias_row):
    nb, m, cols = bm.shape
    r = f3.shape[0]
    tn = _pick(cols, (8192, 4096, 2048))
    return pl.pallas_call(
        _fft3_kernel,
        grid=(nb, cols // tn),
        in_specs=[pl.BlockSpec((r, m), lambda b, j: (0, 0)),
                  pl.BlockSpec((None, m, tn), lambda b, j: (b, 0, j)),
                  pl.BlockSpec((None, r, tn), lambda b, j: (b, 0, j)),
                  pl.BlockSpec((None, r, tn), lambda b, j: (b, 0, j)),
                  pl.BlockSpec((1, tn), lambda b, j: (0, j))],
        out_specs=pl.BlockSpec((None, r, tn), lambda b, j: (b, 0, j)),
        out_shape=jax.ShapeDtypeStruct((nb, r, cols), F32),
        compiler_params=_cparams(("arbitrary", "arbitrary")),
    )(f3, bm, y, gate, bias_row)


def _hyena_sequence(v, x1, x2, hy, bias):
    B, L, C = v.shape
    r_valid = L // FFT_N2
    r_in = max(r_valid, 16)
    n1 = max(2 * r_valid, r_in)
    n = n1 * FFT_N2
    f1, f3, g, gt = _dft_tables(n1, r_in)
    f1k = _dft_tables(n1, n1)[0]
    k, inv = _hyena_filters(L, n, *hy)
    ak = _fft_stage1(f1k, k.reshape(HY_ORDER, n1, FFT_N2 * C))
    khat = _fft_filter_spectrum(ak.reshape(HY_ORDER, 2, n1, FFT_N2, C), g, inv)
    cols = FFT_N2 * C

    def view(a):
        a = a.reshape(B, r_valid, cols)
        return a if r_in == r_valid else jnp.pad(a, ((0, 0), (0, r_in - r_valid), (0, 0)))

    y = view(v)
    for o, gate in enumerate((view(x1), view(x2))):
        a = _fft_stage1(f1, y)
        bm = _fft_mid(a.reshape(B, 2, n1, FFT_N2, C), g, gt, khat[o])
        y = _fft_stage3(f3, bm.reshape(B, 2 * n1, cols), y, gate, jnp.tile(bias[o].astype(F32), FFT_N2)[None, :])
    return y[:, :r_valid].reshape(B, L, C)


def _s5_tables(a_re, a_im, log_dt, b_re, b_im, c_re, c_im, jj_ctx, jj_lat):
    lam = lax.complex(jnp.minimum(a_re.astype(F32), -1e-4), a_im.astype(F32))
    dt = jnp.exp(log_dt.astype(F32))[..., None]
    abar = jnp.exp(lam * dt)
    bbar = ((abar - 1.0) / lam)[..., None] * lax.complex(b_re.astype(F32), b_im.astype(F32))
    cmat = lax.complex(c_re.astype(F32), c_im.astype(F32))
    T = S5_T

    def powers(m):
        m = jnp.asarray(m, F32)
        return jnp.exp(lam * dt * m.reshape(m.shape + (1, 1, 1)))

    pw = powers(jnp.arange(T + 1))
    kt = jnp.real(jnp.einsum('dgop,tdgp,dgpi->tdgoi', cmat, pw[:T], bbar, precision=HI))
    tt = jnp.arange(T)
    lag = tt[None, :] - tt[:, None]
    w_intra = jnp.where((lag >= 0)[:, :, None, None, None, None],
                        kt[jnp.clip(lag, 0, T - 1)], 0.0)
    w_intra = jnp.stack([w_intra[:, :, 0], w_intra[::-1, ::-1, 1]], axis=2)
    w_intra = w_intra.transpose(2, 3, 0, 5, 1, 4).reshape(2, S5_NG, T * S5_GROUP, T * S5_GROUP)
    wb = pw[T - 1 - tt][..., None] * bbar[None]
    wb = jnp.stack([wb[:, 0], wb[::-1, 1]], axis=1)
    wb = wb.transpose(1, 2, 0, 4, 3).reshape(2, S5_NG, T * S5_GROUP, S5_P)
    w_cat = jnp.concatenate([w_intra, jnp.real(wb), jnp.imag(wb)], axis=-1)
    cp = cmat[None] * pw[1:, :, :, None, :]
    cp = jnp.stack([cp[:, 0], cp[::-1, 1]], axis=1)
    cp = cp.transpose(1, 2, 4, 0, 3).reshape(2, S5_NG, S5_P, T * S5_GROUP)
    c_cat = jnp.concatenate([jnp.real(cp), -jnp.imag(cp)], axis=2)

    def coef(z):
        zr, zi = jnp.real(z), jnp.imag(z)
        return jnp.stack([jnp.concatenate([zr, zr], -1), jnp.concatenate([-zi, zi], -1)], axis=-2)

    step = coef(powers(jnp.array(T)))
    seg_c, seg_l = (coef(powers(jnp.array(T * n))) for n in (jj_ctx, jj_lat))
    coefs = jnp.pad(jnp.concatenate([step, seg_c, seg_l], axis=2), ((0, 0), (0, 0), (0, 2), (0, 0)))
    ptab = coef(powers(T * jnp.arange(max(jj_ctx, jj_lat)))).transpose(1, 2, 0, 3, 4)
    return w_cat.astype(BF16), c_cat.astype(BF16), coefs, ptab


def _cmul(coef_a, coef_b, s):
    return coef_a * s + coef_b * pltpu.roll(s, S5_P, axis=1)


S5_GPB = LANE // S5_GROUP
S5_TC = S5_T * S5_GROUP


def _s5_in_kernel(z_ref, w_ref, yi_ref, ds_ref):
    nj = z_ref.shape[0] // S5_T
    ws = [z_ref[pl.ds(t, nj, stride=S5_T), :].T for t in range(S5_T)]
    for g in range(S5_GPB):
        vt = jnp.concatenate([w[S5_GROUP * g:S5_GROUP * (g + 1), :] for w in ws], axis=0)
        v = vt.T.astype(BF16)
        for d in range(2):
            o = jnp.dot(v, w_ref[d, g], preferred_element_type=F32)
            yi_ref[d, g] = o[:, :S5_TC]
            ds_ref[d, g] = o[:, S5_TC:]


def _s5_in(z, w_cat, nj):
    T = z.shape[0]
    R = T // S5_T
    col0 = HY_COLS // LANE
    return pl.pallas_call(
        _s5_in_kernel,
        grid=(R // nj, S5_CH // LANE),
        in_specs=[pl.BlockSpec((nj * S5_T, LANE), lambda i, c: (i, col0 + c)),
                  pl.BlockSpec((2, S5_GPB, S5_TC, S5_TC + 2 * S5_P), lambda i, c: (0, c, 0, 0))],
        out_specs=[pl.BlockSpec((2, S5_GPB, nj, S5_TC), lambda i, c: (0, c, i, 0)),
                   pl.BlockSpec((2, S5_GPB, nj, 2 * S5_P), lambda i, c: (0, c, i, 0))],
        out_shape=[jax.ShapeDtypeStruct((2, S5_NG, R, S5_TC), F32),
                   jax.ShapeDtypeStruct((2, S5_NG, R, 2 * S5_P), F32)],
        compiler_params=_cparams(("arbitrary", "arbitrary")),
    )(z, w_cat)


def _s5_scan_kernel(ds_ref, yi_ref, c_ref, cf_ref, p_ref, y_ref, sp_scr, *, parts, reverse):
    a1, a2 = cf_ref[0:1, :], cf_ref[1:2, :]
    rid = lax.broadcasted_iota(jnp.int32, (S5_SEG, 2 * S5_P), 0)
    first, last = (S5_SEG - 1, 0) if reverse else (0, S5_SEG - 1)
    shift = S5_SEG - 1 if reverse else 1
    nb = len(parts[0][0])
    zero = jnp.zeros((S5_SEG, 2 * S5_P), F32)
    s0 = [zero] * nb
    for pi, (bases, jj) in enumerate(parts):
        g1, g2 = cf_ref[2 + 2 * pi:3 + 2 * pi, :], cf_ref[3 + 2 * pi:4 + 2 * pi, :]

        def rows(b, k, bases=bases, jj=jj):
            return pl.ds(bases[b] + (jj - 1 - k if reverse else k), S5_SEG, stride=jj)

        def local_step(k, states, rows=rows):
            new = []
            for b in range(nb):
                sp_scr[rows(b, k), :] = states[b]
                new.append(_cmul(a1, a2, states[b]) + ds_ref[rows(b, k), :])
            return tuple(new)

        ends = lax.fori_loop(0, jj, local_step, (zero,) * nb)
        carries, nxt_s0 = [], []
        for b in range(nb):
            c = jnp.where(rid == first, s0[b], 0.0)
            for _ in range(S5_SEG - 1):
                c = jnp.where(rid == first, s0[b], pltpu.roll(ends[b] + _cmul(g1, g2, c), shift, axis=0))
            fin = ends[b] + _cmul(g1, g2, c)
            nxt_s0.append(jnp.broadcast_to(fin[last:last + 1, :], fin.shape))
            carries.append((c, pltpu.roll(c, S5_P, axis=1)))

        def fix_step(k, carry, rows=rows, carries=carries):
            p = p_ref[k]
            for b in range(nb):
                c, cs = carries[b]
                sp_scr[rows(b, k), :] += p[0:1, :] * c + p[1:2, :] * cs
            return carry

        lax.fori_loop(0, jj, fix_step, 0)
        s0 = nxt_s0
    y_ref[...] = yi_ref[...] + jnp.dot(sp_scr[...].astype(BF16), c_ref[...], preferred_element_type=F32)


def _s5_scan(d, ds, yi, c_cat, coefs, ptab, parts):
    R = ds.shape[2]
    jjm = ptab.shape[2]
    kern = functools.partial(_s5_scan_kernel, parts=parts, reverse=(d == 1))
    return pl.pallas_call(
        kern,
        grid=(S5_NG,),
        in_specs=[pl.BlockSpec((None, None, R, 2 * S5_P), lambda g: (d, g, 0, 0)),
                  pl.BlockSpec((None, None, R, S5_TC), lambda g: (d, g, 0, 0)),
                  pl.BlockSpec((None, None, 2 * S5_P, S5_TC), lambda g: (d, g, 0, 0)),
                  pl.BlockSpec((None, None, 8, 2 * S5_P), lambda g: (d, g, 0, 0)),
                  pl.BlockSpec((None, None, jjm, 2, 2 * S5_P), lambda g: (d, g, 0, 0, 0))],
        out_specs=pl.BlockSpec((None, R, S5_TC), lambda g: (g, 0, 0)),
        out_shape=jax.ShapeDtypeStruct((S5_NG, R, S5_TC), F32),
        scratch_shapes=[pltpu.VMEM((R, 2 * S5_P), F32)],
        compiler_params=_cparams(("arbitrary",)),
    )(ds, yi, c_cat, coefs, ptab)


def _s5_out_kernel(yf_ref, yb_ref, o_ref):
    nj = yf_ref.shape[1]
    yts = [(yf_ref[g] + yb_ref[g]).T for g in range(S5_GPB)]
    for t in range(S5_T):
        zt = jnp.concatenate([y[S5_GROUP * t:S5_GROUP * (t + 1), :] for y in yts], axis=0)
        o_ref[pl.ds(t, nj, stride=S5_T), :] = zt.T


def _s5_out(yf, yb, nj):
    R = yf.shape[1]
    spec = pl.BlockSpec((S5_GPB, nj, S5_TC), lambda i, c: (c, i, 0))
    return pl.pallas_call(
        _s5_out_kernel,
        grid=(R // nj, S5_CH // LANE),
        in_specs=[spec, spec],
        out_specs=pl.BlockSpec((nj * S5_T, LANE), lambda i, c: (i, c)),
        out_shape=jax.ShapeDtypeStruct((R * S5_T, S5_CH), F32),
        compiler_params=_cparams(("arbitrary", "arbitrary")),
    )(yf, yb)


def _s5_mixer(tok, z, s5):
    B = tok.B
    cl, cc = tok.L // S5_T, tok.CTX // S5_T
    jl, jc = cl // S5_SEG, cc // S5_SEG
    w_cat, c_cat, coefs, ptab = _s5_tables(*s5, jc, jl)
    yi, ds = _s5_in(z, w_cat, math.gcd(B * cl, B * cc, 64))
    lat = (tuple(b * cl for b in range(B)), jl)
    ctx = (tuple(B * cl + b * cc for b in range(B)), jc)
    yf = _s5_scan(0, ds, yi, c_cat, coefs, ptab, (ctx, lat))
    yb = _s5_scan(1, ds, yi, c_cat, coefs, ptab, (ctx, lat))
    return _s5_out(yf, yb, math.gcd(B * cl, B * cc, 64))


def _gelu_tanh(x):
    return 0.5 * x * (1.0 + jnp.tanh(math.sqrt(2.0 / math.pi) * (x + 0.044715 * (x * x * x))))


def _even_out_kernel(hl_ref, ys_ref, u_ref, h_ref, mod_ref, dsk_ref, wg_ref, wo_ref, o_ref):
    y = _gelu_tanh(ys_ref[...] + dsk_ref[...] * u_ref[...])
    s = y * jax.nn.sigmoid(jnp.dot(y.astype(BF16), wg_ref[...], preferred_element_type=F32))
    ol = (jnp.dot(hl_ref[...].astype(BF16), wo_ref[:HY_CH, :], preferred_element_type=F32)
          + jnp.dot(s.astype(BF16), wo_ref[HY_CH:, :], preferred_element_type=F32))
    o_ref[...] = h_ref[...] + mod_ref[2:3, :] * ol


def _even_out(tok, hl, ys, z, h, mods_l, dsk, w_glu, w_out):
    tm = tok.tm
    return pl.pallas_call(
        _even_out_kernel,
        grid=(tok.n_all,),
        in_specs=[pl.BlockSpec((tm, HY_CH), lambda i: (i, 0)),
                  pl.BlockSpec((tm, S5_CH), lambda i: (i, 0)),
                  pl.BlockSpec((tm, S5_CH), lambda i: (i, HY_COLS // S5_CH)),
                  pl.BlockSpec((tm, D), lambda i: (i, 0)),
                  pl.BlockSpec((None, N_MOD, D), lambda i: (tok.mod_row(i), 0, 0)),
                  pl.BlockSpec((1, S5_CH), lambda i: (0, 0)),
                  pl.BlockSpec((S5_CH, S5_CH), lambda i: (0, 0)),
                  pl.BlockSpec((D, D), lambda i: (0, 0))],
        out_specs=pl.BlockSpec((tm, D), lambda i: (i, 0)),
        out_shape=jax.ShapeDtypeStruct((tok.T, D), F32),
        compiler_params=_cparams(("arbitrary",)),
    )(hl, ys, z, h, mods_l, dsk.reshape(1, S5_CH), w_glu, w_out)


def _ffn_kernel(h_ref, mod_ref, g_ref, wg_ref, wu_ref, wd_ref, o_ref, y_scr, acc_scr):
    j = pl.program_id(1)

    @pl.when(j == 0)
    def _():
        y_scr[...] = _ada_norm(h_ref[...], g_ref[...], mod_ref[...], 3, 4).astype(BF16)
        acc_scr[...] = jnp.zeros_like(acc_scr)

    y = y_scr[...]
    gate = jnp.dot(y, wg_ref[...], preferred_element_type=F32)
    up = jnp.dot(y, wu_ref[...], preferred_element_type=F32)
    act = (gate * jax.nn.sigmoid(gate) * up).astype(BF16)
    acc_scr[...] += jnp.dot(act, wd_ref[...], preferred_element_type=F32)

    @pl.when(j == pl.num_programs(1) - 1)
    def _():
        o_ref[...] = h_ref[...] + mod_ref[5:6, :] * acc_scr[...]


def _ffn(tok, h, mods_l, gain, wg, wu, wd):
    tm = tok.tm
    ff = wg.shape[1]
    tf = _pick(ff, (1408, 512, 256, 128))
    return pl.pallas_call(
        _ffn_kernel,
        grid=(tok.n_all, ff // tf),
        in_specs=[pl.BlockSpec((tm, D), lambda i, j: (i, 0)),
                  pl.BlockSpec((None, N_MOD, D), lambda i, j: (tok.mod_row(i), 0, 0)),
                  pl.BlockSpec((1, D), lambda i, j: (0, 0)),
                  pl.BlockSpec((D, tf), lambda i, j: (0, j)),
                  pl.BlockSpec((D, tf), lambda i, j: (0, j)),
                  pl.BlockSpec((tf, D), lambda i, j: (j, 0))],
        out_specs=pl.BlockSpec((tm, D), lambda i, j: (i, 0)),
        out_shape=jax.ShapeDtypeStruct((tok.T, D), F32),
        scratch_shapes=[pltpu.VMEM((tm, D), BF16), pltpu.VMEM((tm, D), F32)],
        compiler_params=_cparams(("arbitrary", "arbitrary")),
    )(h, mods_l, gain.reshape(1, D), wg, wu, wd)


def _even_layer(tok, h, mods_l, p):
    B, L, CTX = tok.B, tok.L, tok.CTX
    nl = B * L
    z = _norm_matmul(tok, h, mods_l, p['norm_mix'], p['w_in'].astype(BF16), EV_IN)
    v, x1, x2 = _short_conv(tok, z, p['conv_w'], p['conv_b'])
    lat = lambda a: a[:nl].reshape(B, L, -1)
    ctx = lambda a: a[nl:].reshape(B, CTX, -1)
    hl = _hyena_sequence(lat(v), lat(x1), lat(x2), p['hy'], p['hy_bias'])
    parts = [hl.reshape(nl, HY_CH)]
    ys_all = _s5_mixer(tok, z, p['s5'])
    if p['need_ctx']:
        hc = _hyena_sequence(ctx(v), ctx(x1), ctx(x2), p['hy'], p['hy_bias'])
        parts.append(hc.reshape(B * CTX, HY_CH))
    else:
        parts.append(jnp.zeros((B * CTX, HY_CH), F32))
    hy_all = jnp.concatenate(parts, axis=0)
    h = _even_out(tok, hy_all, ys_all, z, h, mods_l, p['s5_d'], p['s5_w_glu'].astype(BF16), p['w_out'].astype(BF16))
    return _ffn(tok, h, mods_l, p['norm_ffn'], p['ff_wg'].astype(BF16), p['ff_wu'].astype(BF16),
                p['ff_wd'].astype(BF16))


def _rope_tables(L, tm):
    t = jnp.arange(L)
    row = (t // GRID_W).astype(F32)[:, None]
    col = (t % GRID_W).astype(F32)[:, None]

    def pattern(dim):
        nf = dim // 4
        inv = ROPE_BASE ** (-jnp.arange(nf, dtype=F32) / nf)
        ar, ac = row * inv[None, :], col * inv[None, :]
        cos = jnp.concatenate([jnp.cos(ar)] * 2 + [jnp.cos(ac)] * 2, axis=1)
        z = jnp.zeros((L, nf), F32)
        s_up = jnp.concatenate([-jnp.sin(ar), z, -jnp.sin(ac), z], axis=1)
        s_dn = jnp.concatenate([z, jnp.sin(ar), z, jnp.sin(ac)], axis=1)
        return cos, s_up, s_dn

    def pad_mla(a, fill):
        return jnp.concatenate([jnp.full((L, MLA_NOPE), fill, F32), a,
                                jnp.full((L, HEAD_PAD - MLA_NOPE - MLA_ROPE), fill, F32)], axis=1)

    cm, um, dm = pattern(MLA_ROPE)
    cg, ug, dg = pattern(GQA_HD)
    mla = jnp.stack([pad_mla(cm, 1.0), pad_mla(um, 0.0), pad_mla(dm, 0.0)])
    gqa = jnp.stack([jnp.tile(cg, (1, 2)), jnp.tile(ug, (1, 2)), jnp.tile(dg, (1, 2))])
    ident = jnp.stack([jnp.ones((tm, LANE), F32), jnp.zeros((tm, LANE), F32), jnp.zeros((tm, LANE), F32)])
    return jnp.stack([jnp.concatenate([mla, ident], axis=1), jnp.concatenate([gqa, ident], axis=1)])


def _rope(x, tab, w):
    outs = []
    for h in range(x.shape[1] // LANE):
        xs = x[:, h * LANE:(h + 1) * LANE]
        outs.append(xs * tab[0] + pltpu.roll(xs, LANE - w, axis=1) * tab[1] + pltpu.roll(xs, w, axis=1) * tab[2])
    return outs[0] if len(outs) == 1 else jnp.concatenate(outs, axis=1)


def _rms(x, g):
    return x * lax.rsqrt(jnp.mean(x * x, axis=-1, keepdims=True) + EPS) * g


_O_CQ, _O_CKV, _O_GQ, _O_GK, _O_GV, _O_KR = 0, 256, 384, 896, 1024, 1152


def _odd_proj_kernel(z_ref, tab_ref, qn_ref, kvn_ref, wuq_ref, wuk_ref, wuv_ref, e_ref,
                     q_ref, k_ref, v_ref, gq_ref, gk_ref, gv_ref):
    z = z_ref[...]
    mt, gt = tab_ref[0], tab_ref[1]
    qn = _rms(z[:, _O_CQ:_O_CKV], qn_ref[...]).astype(BF16)
    q = jnp.dot(qn, wuq_ref[...], preferred_element_type=F32)
    q_ref[...] = (_rope(q, mt, MLA_ROPE // 4) * (MLA_SCALE * LOG2E)).astype(BF16)
    kvn = _rms(z[:, _O_CKV:_O_GQ], kvn_ref[...]).astype(BF16)
    k = (jnp.dot(kvn, wuk_ref[...], preferred_element_type=F32)
         + jnp.dot(z[:, _O_KR:], e_ref[...], precision=HI, preferred_element_type=F32))
    k_ref[...] = _rope(k, mt, MLA_ROPE // 4).astype(BF16)
    v_ref[...] = jnp.dot(kvn, wuv_ref[...], preferred_element_type=F32).astype(BF16)
    gq_ref[...] = (_rope(z[:, _O_GQ:_O_GK], gt, GQA_HD // 4) * (GQA_SCALE * LOG2E)).astype(BF16)
    gk_ref[...] = _rope(z[:, _O_GK:_O_GV], gt, GQA_HD // 4).astype(BF16)
    gv_ref[...] = z[:, _O_GV:_O_KR].astype(BF16)


def _odd_proj(tok, z, tabs, q_norm, kv_norm, w_uq, w_ukv):
    tm = tok.tm
    hq = MLA_HEADS * HEAD_PAD
    wq = jnp.pad(w_uq.reshape(Q_LORA, MLA_HEADS, MLA_NOPE + MLA_ROPE),
                 ((0, 0), (0, 0), (0, HEAD_PAD - MLA_NOPE - MLA_ROPE))).reshape(Q_LORA, hq).astype(BF16)
    wkv = w_ukv.reshape(KV_LORA, MLA_HEADS, MLA_NOPE + MLA_V)
    wk = jnp.pad(wkv[..., :MLA_NOPE], ((0, 0), (0, 0), (0, HEAD_PAD - MLA_NOPE))).reshape(KV_LORA, hq).astype(BF16)
    wv = wkv[..., MLA_NOPE:].reshape(KV_LORA, MLA_HEADS * MLA_V).astype(BF16)
    eye = jnp.eye(MLA_ROPE, dtype=F32)
    e_head = jnp.pad(eye, ((0, LANE - MLA_ROPE), (MLA_NOPE, HEAD_PAD - MLA_NOPE - MLA_ROPE)))
    e = jnp.tile(e_head, (1, MLA_HEADS))
    tab_blk = lambda i: (0, 0, jnp.where(i < tok.n_lat, i % tok.per_seq, tok.per_seq), 0)
    full = lambda shape: pl.BlockSpec(shape, lambda i: (0,) * len(shape))
    widths = (hq, hq, MLA_HEADS * MLA_V, GQA_HEADS * GQA_HD, GQA_KV * GQA_HD, GQA_KV * GQA_HD)
    return pl.pallas_call(
        _odd_proj_kernel,
        grid=(tok.n_all,),
        in_specs=[pl.BlockSpec((tm, OD_IN_PAD), lambda i: (i, 0)),
                  pl.BlockSpec((2, 3, tm, LANE), tab_blk),
                  full((1, Q_LORA)), full((1, KV_LORA)), full((Q_LORA, hq)), full((KV_LORA, hq)),
                  full((KV_LORA, MLA_HEADS * MLA_V)), full((LANE, hq))],
        out_specs=[pl.BlockSpec((tm, w), lambda i: (i, 0)) for w in widths],
        out_shape=[jax.ShapeDtypeStruct((tok.T, w), BF16) for w in widths],
        compiler_params=_cparams(("arbitrary",)),
    )(z, tabs, q_norm.reshape(1, -1), kv_norm.reshape(1, -1), wq, wk, wv, e)


def _mla_attn_kernel(q_ref, k_ref, vt_ref, o_ref, s_scr, p_scr, *, tk, nk, ks):
    tq = q_ref.shape[0]
    nslab = tk // ks
    dn = (((1,), (1,)), ((), ()))
    qs = [q_ref[:, h * HEAD_PAD:(h + 1) * HEAD_PAD] for h in range(2)]

    def qk_slab(h, c, j, mx):
        r = pl.multiple_of(c * tk + j * ks, ks)
        s = lax.dot_general(k_ref[pl.ds(r, ks), h * HEAD_PAD:(h + 1) * HEAD_PAD], qs[h], dn,
                            preferred_element_type=F32)
        s_scr[h, j * ks:(j + 1) * ks, :] = s
        return jnp.maximum(mx, jnp.max(s, axis=0, keepdims=True))

    def pv_slab(h, c, j):
        return jnp.dot(vt_ref[c, h * MLA_V:(h + 1) * MLA_V, j * ks:(j + 1) * ks], p_scr[h, j * ks:(j + 1) * ks, :],
                       preferred_element_type=F32)

    def step(x, c_sm, c_pv, c_qk, st):
        y = 1 - x
        m, l, acc, mc = st[x]
        m_new = jnp.maximum(m, mc)
        alpha = jnp.exp2(m - m_new)
        lsum = jnp.zeros((1, tq), F32)
        acc_y = st[y][2]
        mx_y = jnp.full((1, tq), NEG, F32)
        for j in range(nslab):
            acc_y = acc_y + pv_slab(y, c_pv, j)
            mx_y = qk_slab(y, c_qk, j, mx_y)
            p = jnp.exp2(s_scr[x, j * ks:(j + 1) * ks, :] - m_new)
            lsum = lsum + jnp.sum(p, axis=0, keepdims=True)
            p_scr[x, j * ks:(j + 1) * ks, :] = p.astype(BF16)
        new = [None, None]
        new[x] = (m_new, alpha * l + lsum, alpha * acc, mc)
        new[y] = (st[y][0], st[y][1], acc_y, mx_y)
        return tuple(new)

    def body(c, st):
        st = step(0, c, jnp.maximum(c - 1, 0), c, st)
        return step(1, c, c, jnp.minimum(c + 1, nk - 1), st)

    neg = jnp.full((1, tq), NEG, F32)
    zero = jnp.zeros((1, tq), F32)
    acc0 = jnp.zeros((MLA_V, tq), F32)
    p_scr[1] = jnp.zeros(p_scr.shape[1:], BF16)
    mx0 = neg
    for j in range(nslab):
        mx0 = qk_slab(0, 0, j, mx0)
    st = lax.fori_loop(0, nk, body, ((neg, zero, acc0, mx0), (neg, zero, acc0, neg)))
    acc1 = st[1][2]
    for j in range(nslab):
        acc1 = acc1 + pv_slab(1, nk - 1, j)
    out_t = jnp.concatenate([st[0][2] / st[0][1], acc1 / st[1][1]], axis=0)
    o_ref[...] = out_t.T.astype(o_ref.dtype)


MLA_TQ = (256, 128)
MLA_TK = (768, 512, 256, 128)
MLA_KS = 128


def _mla_attention(q, k, v):
    B, Lq, _ = q.shape
    Nk = k.shape[1]
    tq = _pick(Lq, MLA_TQ)
    tk = _pick(Nk, MLA_TK)
    nk = Nk // tk
    hp = MLA_HEADS // 2
    vt = v.reshape(B, nk, tk, hp, 2 * MLA_V).transpose(0, 3, 1, 4, 2)
    kern = functools.partial(_mla_attn_kernel, tk=tk, nk=nk, ks=math.gcd(tk, MLA_KS))
    return pl.pallas_call(
        kern,
        grid=(B, hp, Lq // tq),
        in_specs=[pl.BlockSpec((None, tq, 2 * HEAD_PAD), lambda b, h, i: (b, i, h)),
                  pl.BlockSpec((None, Nk, 2 * HEAD_PAD), lambda b, h, i: (b, 0, h)),
                  pl.BlockSpec((None, None, nk, 2 * MLA_V, tk), lambda b, h, i: (b, h, 0, 0, 0))],
        out_specs=pl.BlockSpec((None, tq, 2 * MLA_V), lambda b, h, i: (b, i, h)),
        out_shape=jax.ShapeDtypeStruct((B, Lq, MLA_HEADS * MLA_V), BF16),
        scratch_shapes=[pltpu.VMEM((2, tk, tq), F32), pltpu.VMEM((2, tk, tq), BF16)],
        compiler_params=_cparams(("arbitrary", "arbitrary", "arbitrary")),
    )(q, k, vt)


def _gqa_kernel(sink_ref, q_ref, kc_ref, vct_ref, *rest, L, has_band):
    group = GQA_HEADS // GQA_KV
    gw = group * BLK
    if has_band:
        kp_ref, k_ref, kn_ref, vtp_ref, vt_ref, vtn_ref, o_ref = rest
        qi = pl.program_id(1)
        keys = jnp.concatenate([kp_ref[...], k_ref[...], kn_ref[...], kc_ref[...]], axis=0)
        vals_t = jnp.concatenate([vtp_ref[...], vt_ref[...], vtn_ref[...], vct_ref[...]], axis=1)
        nkey = keys.shape[0]
        kpos = (qi - 1) * BLK + lax.broadcasted_iota(jnp.int32, (nkey, gw), 0)
        qpos = qi * BLK + (lax.broadcasted_iota(jnp.int32, (nkey, gw), 1) & (BLK - 1))
        is_ctx = lax.broadcasted_iota(jnp.int32, (nkey, gw), 0) >= 3 * BLK
        valid = is_ctx | ((jnp.abs(qpos - kpos) <= WINDOW) & (kpos >= 0) & (kpos < L))
    else:
        (o_ref,) = rest
        keys, vals_t = kc_ref[...], vct_ref[...]
    dn = (((1,), (1,)), ((), ()))
    G = range(GQA_KV)
    qs = [jnp.concatenate([q_ref[:, (kh * group + g) * GQA_HD:(kh * group + g + 1) * GQA_HD] for g in range(group)],
                          axis=0) for kh in G]
    s = [lax.dot_general(keys[:, kh * GQA_HD:(kh + 1) * GQA_HD], qs[kh], dn, preferred_element_type=F32) for kh in G]
    if has_band:
        s = [jnp.where(valid, x, NEG) for x in s]
    sink = [sink_ref[:, kh * gw:(kh + 1) * gw] for kh in G]
    m = [jnp.maximum(jnp.max(s[kh], axis=0, keepdims=True), sink[kh]) for kh in G]
    p = [jnp.exp2(s[kh] - m[kh]) for kh in G]
    den = [jnp.sum(p[kh], axis=0, keepdims=True) + jnp.exp2(sink[kh] - m[kh]) for kh in G]
    ot = [jnp.dot(vals_t[kh * GQA_HD:(kh + 1) * GQA_HD, :], p[kh].astype(BF16), preferred_element_type=F32)
          * (1.0 / den[kh]) for kh in G]
    o = jnp.concatenate(ot, axis=0).T
    o_ref[...] = jnp.concatenate([o[g * BLK:(g + 1) * BLK, kh * GQA_HD:(kh + 1) * GQA_HD]
                                  for kh in G for g in range(group)], axis=1).astype(o_ref.dtype)


def _gqa_attention(sink, q, kc, vc, k=None, v=None):
    B, Lq, _ = q.shape
    CTX = kc.shape[1]
    has_band = k is not None
    kw = GQA_KV * GQA_HD
    nb = Lq // BLK
    sink_row = jnp.repeat(sink.astype(F32) * LOG2E, BLK)[None, :]
    in_specs = [pl.BlockSpec((1, GQA_HEADS * BLK), lambda b, i: (0, 0)),
                pl.BlockSpec((None, BLK, GQA_HEADS * GQA_HD), lambda b, i: (b, i, 0)),
                pl.BlockSpec((None, CTX, kw), lambda b, i: (b, 0, 0)),
                pl.BlockSpec((None, kw, CTX), lambda b, i: (b, 0, 0))]
    args = [sink_row, q, kc, jnp.swapaxes(vc, 1, 2)]
    if has_band:
        prev = lambda i: jnp.maximum(i - 1, 0)
        nxt = lambda i: jnp.minimum(i + 1, nb - 1)
        in_specs += [pl.BlockSpec((None, BLK, kw), lambda b, i: (b, prev(i), 0)),
                     pl.BlockSpec((None, BLK, kw), lambda b, i: (b, i, 0)),
                     pl.BlockSpec((None, BLK, kw), lambda b, i: (b, nxt(i), 0)),
                     pl.BlockSpec((None, kw, BLK), lambda b, i: (b, 0, prev(i))),
                     pl.BlockSpec((None, kw, BLK), lambda b, i: (b, 0, i)),
                     pl.BlockSpec((None, kw, BLK), lambda b, i: (b, 0, nxt(i)))]
        vt = jnp.swapaxes(v, 1, 2)
        args += [k, k, k, vt, vt, vt]
    kern = functools.partial(_gqa_kernel, L=Lq, has_band=has_band)
    return pl.pallas_call(
        kern,
        grid=(B, nb),
        in_specs=in_specs,
        out_specs=pl.BlockSpec((None, BLK, GQA_HEADS * GQA_HD), lambda b, i: (b, i, 0)),
        out_shape=jax.ShapeDtypeStruct((B, Lq, GQA_HEADS * GQA_HD), BF16),
        compiler_params=_cparams(("arbitrary", "arbitrary")),
    )(*args)


def _odd_out_kernel(a_ref, g_ref, h_ref, mod_ref, wo_ref, o_ref):
    half = a_ref.shape[1]
    ol = (jnp.dot(a_ref[...], wo_ref[:half, :], preferred_element_type=F32)
          + jnp.dot(g_ref[...], wo_ref[half:, :], preferred_element_type=F32))
    o_ref[...] = h_ref[...] + mod_ref[2:3, :] * ol


def _odd_out(tok, mla, gqa, h, mods_l, w_out):
    tm = tok.tm
    half = mla.shape[1]
    return pl.pallas_call(
        _odd_out_kernel,
        grid=(tok.n_all,),
        in_specs=[pl.BlockSpec((tm, half), lambda i: (i, 0)),
                  pl.BlockSpec((tm, half), lambda i: (i, 0)),
                  pl.BlockSpec((tm, D), lambda i: (i, 0)),
                  pl.BlockSpec((None, N_MOD, D), lambda i: (tok.mod_row(i), 0, 0)),
                  pl.BlockSpec((D, D), lambda i: (0, 0))],
        out_specs=pl.BlockSpec((tm, D), lambda i: (i, 0)),
        out_shape=jax.ShapeDtypeStruct((tok.T, D), F32),
        compiler_params=_cparams(("arbitrary",)),
    )(mla, gqa, h, mods_l, w_out)


MOE_TR = 1024
MOE_TF = 896
ROUTE_W = 8


def _router_kernel(h_ref, mod_ref, g_ref, r_ref, y_ref, route_ref):
    y = _ada_norm(h_ref[...], g_ref[...], mod_ref[...], 3, 4)
    y_ref[...] = y
    logits = jnp.dot(y, r_ref[...], precision=HI, preferred_element_type=F32)
    lane = lax.broadcasted_iota(jnp.int32, logits.shape, 1)
    lg = jnp.where(lane < N_EXP, logits, -jnp.inf)
    m1 = jnp.max(lg, axis=-1, keepdims=True)
    i1 = jnp.min(jnp.where(lg == m1, lane, LANE), axis=-1, keepdims=True)
    lg2 = jnp.where(lane == i1, -jnp.inf, lg)
    m2 = jnp.max(lg2, axis=-1, keepdims=True)
    i2 = jnp.min(jnp.where(lg2 == m2, lane, LANE), axis=-1, keepdims=True)
    e = jnp.exp(m2 - m1)
    w1 = 1.0 / (1.0 + e)
    route = (jnp.where(lane == 0, w1, 0.0) + jnp.where(lane == 1, e * w1, 0.0)
             + jnp.where(lane == 2, i1.astype(F32), 0.0) + jnp.where(lane == 3, i2.astype(F32), 0.0))
    route_ref[...] = route[:, :ROUTE_W]


def _moe_router(tok, h, mods_l, gain, router):
    tm = tok.tm
    rp = jnp.pad(router, ((0, 0), (0, LANE - N_EXP)))
    return pl.pallas_call(
        _router_kernel,
        grid=(tok.n_all,),
        in_specs=[pl.BlockSpec((tm, D), lambda i: (i, 0)),
                  pl.BlockSpec((None, N_MOD, D), lambda i: (tok.mod_row(i), 0, 0)),
                  pl.BlockSpec((1, D), lambda i: (0, 0)),
                  pl.BlockSpec((D, LANE), lambda i: (0, 0))],
        out_specs=[pl.BlockSpec((tm, D), lambda i: (i, 0)), pl.BlockSpec((tm, ROUTE_W), lambda i: (i, 0))],
        out_shape=[jax.ShapeDtypeStruct((tok.T, D), F32), jax.ShapeDtypeStruct((tok.T, ROUTE_W), F32)],
        compiler_params=_cparams(("arbitrary",)),
    )(h, mods_l, gain.reshape(1, D), rp)


def _moe_plan(route, tr):
    T = route.shape[0]
    flat = route[:, 2:4].astype(jnp.int32).reshape(-1)
    onehot = (flat[:, None] == jnp.arange(N_EXP, dtype=jnp.int32)[None, :]).astype(jnp.int32)
    csum = jnp.cumsum(onehot, axis=0)
    rank = jnp.sum((csum - onehot) * onehot, axis=1)
    padded = (csum[-1] + tr - 1) // tr * tr
    ends = jnp.cumsum(padded)
    pos = (ends - padded)[flat] + rank
    n_tiles = (2 * T + N_EXP * (tr - 1)) // tr
    src = jnp.zeros((n_tiles * tr,), jnp.int32).at[pos].set(jnp.arange(2 * T, dtype=jnp.int32) // 2,
                                                            unique_indices=True)
    starts = jnp.arange(n_tiles, dtype=jnp.int32) * tr
    tile_expert = jnp.minimum(jnp.sum((starts[:, None] >= ends[None, :]).astype(jnp.int32), axis=1), N_EXP - 1)
    n_valid = (ends[-1] // tr).astype(jnp.int32).reshape(1)
    return src, tile_expert.astype(jnp.int32), n_valid, pos.reshape(T, 2)


def _gather_rows(idx_ref, src_hbm, dst, sem, n):
    def issue(r, carry):
        pltpu.make_async_copy(src_hbm.at[pl.ds(idx_ref[r], 1), :], dst.at[pl.ds(r, 1), :], sem).start()
        return carry

    lax.fori_loop(0, n, issue, 0, unroll=8)


def _wait_rows(src_hbm, dst, sem, n):
    pltpu.make_async_copy(src_hbm.at[pl.ds(0, n), :], dst, sem).wait()


def _moe_expert_kernel(te_ref, nv_ref, idx_ref, idxn_ref, y_hbm, wg_ref, wu_ref, wd_ref, o_ref,
                       xbuf, y_scr, acc_scr, sem, *, tr):
    t, f = pl.program_id(0), pl.program_id(1)
    nt, nf = pl.num_programs(0), pl.num_programs(1)
    valid = t < nv_ref[0]
    slot = lax.rem(t, 2)

    @pl.when((f == 0) & (t == 0))
    def _():
        _gather_rows(idx_ref, y_hbm, xbuf.at[0], sem.at[0], tr)

    @pl.when((f == 0) & (t + 1 < nv_ref[0]))
    def _():
        _gather_rows(idxn_ref, y_hbm, xbuf.at[1 - slot], sem.at[1 - slot], tr)

    @pl.when((f == 0) & valid)
    def _():
        _wait_rows(y_hbm, xbuf.at[slot], sem.at[slot], tr)
        y_scr[...] = xbuf[slot].astype(BF16)
        acc_scr[...] = jnp.zeros_like(acc_scr)

    @pl.when(valid)
    def _():
        y = y_scr[...]
        gate = jnp.dot(y, wg_ref[...], preferred_element_type=F32)
        up = jnp.dot(y, wu_ref[...], preferred_element_type=F32)
        act = (gate * jax.nn.sigmoid(gate) * up).astype(BF16)
        acc_scr[...] += jnp.dot(act, wd_ref[...], preferred_element_type=F32)

    @pl.when(f == nf - 1)
    def _():
        o_ref[...] = jnp.where(valid, acc_scr[...], 0.0)


def _moe_experts(y, src, tile_expert, n_valid, wg, wu, wd, tr):
    n_tiles = src.shape[0] // tr
    tf = MOE_TF
    kern = functools.partial(_moe_expert_kernel, tr=tr)
    smem = functools.partial(pl.BlockSpec, memory_space=pltpu.SMEM)
    grid_spec = pltpu.PrefetchScalarGridSpec(
        num_scalar_prefetch=2,
        grid=(n_tiles, EXP_FF // tf),
        in_specs=[smem((tr,), lambda t, f, te, nv: (t,)),
                  smem((tr,), lambda t, f, te, nv: (jnp.minimum(t + 1, n_tiles - 1),)),
                  pl.BlockSpec(memory_space=pl.ANY),
                  pl.BlockSpec((None, D, tf), lambda t, f, te, nv: (te[t], 0, f)),
                  pl.BlockSpec((None, D, tf), lambda t, f, te, nv: (te[t], 0, f)),
                  pl.BlockSpec((None, tf, D), lambda t, f, te, nv: (te[t], f, 0))],
        out_specs=pl.BlockSpec((tr, D), lambda t, f, te, nv: (t, 0)),
        scratch_shapes=[pltpu.VMEM((2, tr, D), F32), pltpu.VMEM((tr, D), BF16), pltpu.VMEM((tr, D), F32),
                        pltpu.SemaphoreType.DMA((2,))])
    return pl.pallas_call(
        kern,
        grid_spec=grid_spec,
        out_shape=jax.ShapeDtypeStruct((n_tiles * tr, D), F32),
        compiler_params=_cparams(("arbitrary", "arbitrary")),
    )(tile_expert, n_valid, src, src, y, wg, wu, wd)


def _moe_combine_kernel(idx_ref, idxn_ref, o_hbm, h_ref, route_ref, mod_ref, out_ref, buf, sem, *, tm):
    i = pl.program_id(0)
    n = pl.num_programs(0)
    slot = lax.rem(i, 2)

    @pl.when(i == 0)
    def _():
        _gather_rows(idx_ref, o_hbm, buf.at[0], sem.at[0], 2 * tm)

    @pl.when(i + 1 < n)
    def _():
        _gather_rows(idxn_ref, o_hbm, buf.at[1 - slot], sem.at[1 - slot], 2 * tm)

    _wait_rows(o_hbm, buf.at[slot], sem.at[slot], 2 * tm)
    r = route_ref[...]
    mix = r[:, 0:1] * buf[slot, :tm, :] + r[:, 1:2] * buf[slot, tm:, :]
    out_ref[...] = h_ref[...] + mod_ref[5:6, :] * mix


def _moe_combine(tok, o_sorted, pos, h, route, mods_l):
    tm = tok.tm
    n = tok.n_all
    idx = pos.reshape(n, tm, 2).transpose(0, 2, 1).reshape(-1)
    kern = functools.partial(_moe_combine_kernel, tm=tm)
    smem = functools.partial(pl.BlockSpec, memory_space=pltpu.SMEM)
    return pl.pallas_call(
        kern,
        grid=(n,),
        in_specs=[smem((2 * tm,), lambda i: (i,)),
                  smem((2 * tm,), lambda i: (jnp.minimum(i + 1, n - 1),)),
                  pl.BlockSpec(memory_space=pl.ANY),
                  pl.BlockSpec((tm, D), lambda i: (i, 0)),
                  pl.BlockSpec((tm, ROUTE_W), lambda i: (i, 0)),
                  pl.BlockSpec((None, N_MOD, D), lambda i: (tok.mod_row(i), 0, 0))],
        out_specs=pl.BlockSpec((tm, D), lambda i: (i, 0)),
        out_shape=jax.ShapeDtypeStruct((tok.T, D), F32),
        scratch_shapes=[pltpu.VMEM((2, 2 * tm, D), F32), pltpu.SemaphoreType.DMA((2,))],
        compiler_params=_cparams(("arbitrary",)),
    )(idx, idx, o_sorted, h, route, mods_l)


def _moe(tok, h, mods_l, gain, router, wg, wu, wd):
    y, route = _moe_router(tok, h, mods_l, gain, router)
    src, tile_expert, n_valid, pos = _moe_plan(route, MOE_TR)
    o_sorted = _moe_experts(y, src, tile_expert, n_valid, wg, wu, wd, MOE_TR)
    return _moe_combine(tok, o_sorted, pos, h, route, mods_l)


def _odd_layer(tok, h, mods_l, tabs, p):
    B, L, CTX = tok.B, tok.L, tok.CTX
    nl = B * L
    w = p['w_in']
    w_in = jnp.concatenate([w[:, :Q_LORA + KV_LORA], w[:, 416:1184], w[:, 384:416],
                            jnp.zeros((D, OD_IN_PAD - 1184), w.dtype)], axis=1).astype(BF16)
    z = _norm_matmul(tok, h, mods_l, p['norm_mix'], w_in, OD_IN_PAD)
    q, k, v, gq, gk, gv = _odd_proj(tok, z, tabs, p['q_norm'], p['kv_norm'], p['w_uq'], p['w_ukv'])
    lat = lambda a: a[:nl].reshape(B, L, -1)
    ctx = lambda a: a[nl:].reshape(B, CTX, -1)
    cat = lambda a: jnp.concatenate([ctx(a), lat(a)], axis=1)
    mla_l = _mla_attention(lat(q), cat(k), cat(v))
    gqa_l = _gqa_attention(p['sink'], lat(gq), ctx(gk), ctx(gv), lat(gk), lat(gv))
    if p['need_ctx']:
        mla_c = _mla_attention(ctx(q), ctx(k), ctx(v))
        gqa_c = _gqa_attention(p['sink'], ctx(gq), ctx(gk), ctx(gv))
    else:
        mla_c = jnp.zeros((B, CTX, MLA_HEADS * MLA_V), BF16)
        gqa_c = jnp.zeros((B, CTX, GQA_HEADS * GQA_HD), BF16)
    flat = lambda a, c: jnp.concatenate([a.reshape(nl, -1), c.reshape(B * CTX, -1)], axis=0)
    h = _odd_out(tok, flat(mla_l, mla_c), flat(gqa_l, gqa_c), h, mods_l, p['w_out'].astype(BF16))
    return _moe(tok, h, mods_l, p['norm_ffn'], p['router'], p['moe_wg'].astype(BF16),
                p['moe_wu'].astype(BF16), p['moe_wd'].astype(BF16))


def _final_norm_kernel(h_ref, g_ref, o_ref):
    o_ref[...] = _rms(h_ref[...], g_ref[...])


def _final_norm(tok, h, gain):
    tm = tok.tm
    return pl.pallas_call(
        _final_norm_kernel,
        grid=(tok.n_lat,),
        in_specs=[pl.BlockSpec((tm, D), lambda i: (i, 0)), pl.BlockSpec((1, D), lambda i: (0, 0))],
        out_specs=pl.BlockSpec((tm, D), lambda i: (i, 0)),
        out_shape=jax.ShapeDtypeStruct((tok.B * tok.L, D), F32),
        compiler_params=_cparams(("arbitrary",)),
    )(h, gain.reshape(1, D))


def kernel(x, c, ctx, c_ctx, mod_w, mod_b, norm_mix, norm_ffn, final_norm,
           ev_w_in, ev_conv_w, ev_conv_b, hy_w1, hy_b1, hy_w2, hy_b2, hy_w3, hy_freq, hy_decay, hy_bias,
           s5_a_re, s5_a_im, s5_log_dt, s5_b_re, s5_b_im, s5_c_re, s5_c_im, s5_d, s5_w_glu, ev_w_out,
           ff_w_gate, ff_w_up, ff_w_down,
           od_w_in, mla_q_norm, mla_w_uq, mla_kv_norm, mla_w_ukv, gqa_sink, od_w_out,
           moe_router, moe_w_gate, moe_w_up, moe_w_down):
    B, L, _ = x.shape
    CTX = ctx.shape[1]
    tok = _Tok(B, L, CTX, _pick(math.gcd(L, B * CTX), (512, 256, 128)))
    cond_t = jnp.concatenate([c, c_ctx[None, :], jnp.zeros((8 - B - 1, D), F32)], axis=0).T
    mods = _modulations(cond_t, B + 1, mod_w, mod_b)
    tabs = _rope_tables(L, tok.tm)
    h = jnp.concatenate([x.reshape(B * L, D), ctx.reshape(B * CTX, D)], axis=0)
    for l in range(DEPTH):
        i = l // 2
        need_ctx = l < DEPTH - 1
        if l % 2 == 0:
            p = dict(norm_mix=norm_mix[l], norm_ffn=norm_ffn[l], w_in=ev_w_in[i], conv_w=ev_conv_w[i],
                     conv_b=ev_conv_b[i],
                     hy=(hy_w1[i], hy_b1[i], hy_w2[i], hy_b2[i], hy_w3[i], hy_freq[i], hy_decay[i]),
                     hy_bias=hy_bias[i],
                     s5=(s5_a_re[i], s5_a_im[i], s5_log_dt[i], s5_b_re[i], s5_b_im[i], s5_c_re[i], s5_c_im[i]),
                     s5_d=s5_d[i], s5_w_glu=s5_w_glu[i], w_out=ev_w_out[i],
                     ff_wg=ff_w_gate[i], ff_wu=ff_w_up[i], ff_wd=ff_w_down[i], need_ctx=need_ctx)
            h = _even_layer(tok, h, mods[l], p)
        else:
            p = dict(norm_mix=norm_mix[l], norm_ffn=norm_ffn[l], w_in=od_w_in[i], q_norm=mla_q_norm[i],
                     w_uq=mla_w_uq[i], kv_norm=mla_kv_norm[i], w_ukv=mla_w_ukv[i], sink=gqa_sink[i],
                     w_out=od_w_out[i], router=moe_router[i], moe_wg=moe_w_gate[i], moe_wu=moe_w_up[i],
                     moe_wd=moe_w_down[i], need_ctx=need_ctx)
            h = _odd_layer(tok, h, mods[l], tabs, p)
    return _final_norm(tok, h, final_norm).reshape(B, L, D)
```

```python
import functools
import math

import jax
import jax.numpy as jnp
from jax import lax
from jax.experimental import pallas as pl
from jax.experimental.pallas import tpu as pltpu

F32 = jnp.float32
BF16 = jnp.bfloat16
HI = lax.Precision.HIGHEST

D = 1024
DEPTH = 4
GRID_W = 64
EPS = 1e-6
NEG = -1e30
N_MOD = 6

HY_CH = 512
HY_ORDER = 2
HY_BANDS = 16
HY_EMB = 1 + 2 * HY_BANDS
HY_FFN = 64
HY_SHIFT = 0.05
HY_COLS = (HY_ORDER + 1) * HY_CH
S5_CH = 512
S5_GROUP = 16
S5_NG = S5_CH // S5_GROUP
S5_P = 64
S5_T = 16
S5_SEG = 8
EV_IN = HY_COLS + S5_CH

MLA_HEADS = 8
MLA_NOPE = 64
MLA_ROPE = 32
MLA_V = 64
Q_LORA = 256
KV_LORA = 128
GQA_HEADS = 8
GQA_KV = 2
GQA_HD = 64
WINDOW = 128
BLK = 128
ROPE_BASE = 10000.0
MLA_SCALE = (MLA_NOPE + MLA_ROPE) ** -0.5
GQA_SCALE = GQA_HD ** -0.5
LOG2E = math.log2(math.e)
HEAD_PAD = 128
OD_IN_PAD = 1280

D_FF = 2816
N_EXP = 8
EXP_FF = 3584

LANE = 128
FFT_N2 = 128
FFT_PAD = 8
VMEM_LIMIT = 56 * 1024 * 1024


def _cparams(sem):
    return pltpu.CompilerParams(dimension_semantics=sem, vmem_limit_bytes=VMEM_LIMIT)


def _pick(n, cands):
    for c in cands:
        if n % c == 0:
            return c
    raise ValueError(f"no tile for {n} in {cands}")


def _mod_kernel(ct_ref, w_ref, b_ref, o_ref, *, nrows):
    c = ct_ref[...]
    s = c * jax.nn.sigmoid(c)
    w = w_ref[...]
    rows = [jnp.sum(w * s[:, r:r + 1], axis=0, keepdims=True) for r in range(nrows)]
    rows.append(jnp.zeros((8 - nrows, w.shape[1]), F32))
    o_ref[...] = jnp.concatenate(rows, axis=0) + b_ref[...]


def _modulations(cond_t, nrows, mod_w, mod_b):
    tn = 1536
    out = pl.pallas_call(
        functools.partial(_mod_kernel, nrows=nrows),
        grid=(DEPTH, N_MOD * D // tn),
        in_specs=[pl.BlockSpec((D, 8), lambda l, j: (0, 0)),
                  pl.BlockSpec((None, D, tn), lambda l, j: (l, 0, j)),
                  pl.BlockSpec((None, 1, tn), lambda l, j: (l, 0, j))],
        out_specs=pl.BlockSpec((None, 8, tn), lambda l, j: (l, 0, j)),
        out_shape=jax.ShapeDtypeStruct((DEPTH, 8, N_MOD * D), F32),
        compiler_params=_cparams(("arbitrary", "arbitrary")),
    )(cond_t, mod_w, mod_b.reshape(DEPTH, 1, N_MOD * D))
    return out.reshape(DEPTH, 8, N_MOD, D)


class _Tok:
    def __init__(self, B, L, CTX, tm):
        assert L % tm == 0 and (B * CTX) % tm == 0
        self.B, self.L, self.CTX, self.tm = B, L, CTX, tm
        self.n_lat = B * L // tm
        self.n_all = self.n_lat + B * CTX // tm
        self.T = B * (L + CTX)
        self.per_seq = L // tm

    def mod_row(self, i):
        return jnp.where(i < self.n_lat, i // self.per_seq, self.B)


def _ada_norm(x, gain, mod, shift_idx, scale_idx):
    y = x * lax.rsqrt(jnp.mean(x * x, axis=-1, keepdims=True) + EPS) * gain
    return y * (1.0 + mod[scale_idx:scale_idx + 1, :]) + mod[shift_idx:shift_idx + 1, :]


def _norm_mm_kernel(h_ref, mod_ref, g_ref, w_ref, o_ref, y_scr):
    @pl.when(pl.program_id(1) == 0)
    def _():
        y_scr[...] = _ada_norm(h_ref[...], g_ref[...], mod_ref[...], 0, 1).astype(BF16)

    o_ref[...] = jnp.dot(y_scr[...], w_ref[...], preferred_element_type=F32).astype(o_ref.dtype)


def _norm_matmul(tok, h, mods_l, gain, w, tn, out_dtype=F32):
    tm, n = tok.tm, w.shape[1]
    return pl.pallas_call(
        _norm_mm_kernel,
        grid=(tok.n_all, n // tn),
        in_specs=[pl.BlockSpec((tm, D), lambda i, j: (i, 0)),
                  pl.BlockSpec((None, N_MOD, D), lambda i, j: (tok.mod_row(i), 0, 0)),
                  pl.BlockSpec((1, D), lambda i, j: (0, 0)),
                  pl.BlockSpec((D, tn), lambda i, j: (0, j))],
        out_specs=pl.BlockSpec((tm, tn), lambda i, j: (i, j)),
        out_shape=jax.ShapeDtypeStruct((tok.T, n), out_dtype),
        scratch_shapes=[pltpu.VMEM((tm, D), BF16)],
        compiler_params=_cparams(("arbitrary", "arbitrary")),
    )(h, mods_l, gain.reshape(1, D), w)


def _short_conv_kernel(z_ref, zp_ref, zn_ref, w_ref, b_ref, v_ref, x1_ref, x2_ref, *, tm, n_lat, L, CTX):
    i = pl.program_id(0)
    is_lat = i < n_lat
    seqlen = jnp.where(is_lat, L, CTX)
    off = jnp.where(is_lat, i * tm, (i - n_lat) * tm)
    first = lax.rem(off, seqlen) == 0
    last = lax.rem(off + tm, seqlen) == 0
    z = z_ref[...]
    prev_row = jnp.where(first, 0.0, zp_ref[7:8, :])
    next_row = jnp.where(last, 0.0, zn_ref[0:1, :])
    rid = lax.broadcasted_iota(jnp.int32, z.shape, 0)
    zm1 = jnp.where(rid == 0, prev_row, pltpu.roll(z, 1, axis=0))
    zp1 = jnp.where(rid == tm - 1, next_row, pltpu.roll(z, tm - 1, axis=0))
    out = b_ref[...] + zm1 * w_ref[0:1, :] + z * w_ref[1:2, :] + zp1 * w_ref[2:3, :]
    v_ref[...] = out[:, :HY_CH]
    x1_ref[...] = out[:, HY_CH:2 * HY_CH]
    x2_ref[...] = out[:, 2 * HY_CH:]


def _short_conv(tok, z, conv_w, conv_b):
    tm = _pick(math.gcd(tok.L, tok.CTX), (256, 128))
    n_lat = tok.B * tok.L // tm
    n_all = tok.T // tm
    r8 = tm // 8
    kern = functools.partial(_short_conv_kernel, tm=tm, n_lat=n_lat, L=tok.L, CTX=tok.CTX)
    o = jax.ShapeDtypeStruct((tok.T, HY_CH), F32)
    return pl.pallas_call(
        kern,
        grid=(n_all,),
        in_specs=[pl.BlockSpec((tm, HY_COLS), lambda i: (i, 0)),
                  pl.BlockSpec((8, HY_COLS), lambda i: (jnp.maximum(i * r8 - 1, 0), 0)),
                  pl.BlockSpec((8, HY_COLS), lambda i: (jnp.minimum((i + 1) * r8, n_all * r8 - 1), 0)),
                  pl.BlockSpec((8, HY_COLS), lambda i: (0, 0)),
                  pl.BlockSpec((1, HY_COLS), lambda i: (0, 0))],
        out_specs=[pl.BlockSpec((tm, HY_CH), lambda i: (i, 0))] * 3,
        out_shape=[o, o, o],
        compiler_params=_cparams(("arbitrary",)),
    )(z, z, z, jnp.pad(conv_w, ((0, 8 - conv_w.shape[0]), (0, 0))), conv_b.reshape(1, HY_COLS))


def _filter_kernel(f_ref, w1_ref, b1_ref, w2_ref, b2_ref, w3_ref, fr_ref, dec_ref, k_ref, s_ref):
    @pl.when(pl.program_id(0) == 0)
    def _():
        s_ref[...] = jnp.zeros_like(s_ref)

    f = f_ref[...]
    fr = fr_ref[...]
    hid = jnp.sin(fr * (jnp.dot(f, w1_ref[...], precision=HI, preferred_element_type=F32) + b1_ref[...]))
    hid = jnp.sin(fr * (jnp.dot(hid, w2_ref[...], precision=HI, preferred_element_type=F32) + b2_ref[...]))
    h = jnp.dot(hid, w3_ref[...], precision=HI, preferred_element_type=F32)
    t01 = f[:, 0:1]
    valid = f[:, LANE - 1:LANE]
    k = h * (jnp.exp(-t01 * jnp.abs(dec_ref[...])) + HY_SHIFT) * valid
    k_ref[0] = k[:, :HY_CH]
    k_ref[1] = k[:, HY_CH:]
    s_ref[...] += jnp.sum(jnp.abs(k), axis=0, keepdims=True)


def _hyena_filters(L, n, w1, b1, w2, b2, w3, freq, decay):
    row = jnp.arange(n)
    fwd = row < L
    bwd = row > n - L
    t = jnp.where(fwd, row, n - row).astype(F32)
    t01 = t / L
    bands = jnp.linspace(1e-4, HY_BANDS - 1, HY_BANDS, dtype=F32)
    ang = (2.0 * math.pi / L) * t[:, None] * bands[None, :]
    valid = (fwd | bwd).astype(F32)
    feats = jnp.concatenate([t01[:, None], jnp.cos(ang), -jnp.sin(ang),
                             jnp.zeros((n, LANE - 1 - HY_EMB), F32), valid[:, None]], axis=-1)
    w1p = jnp.pad(w1, ((0, LANE - HY_EMB), (0, 0)))
    tr = _pick(L, (512, 256))
    nb_half = n // 2 // tr
    ncol = HY_ORDER * HY_CH
    k, ssum = pl.pallas_call(
        _filter_kernel,
        grid=(n // tr,),
        in_specs=[pl.BlockSpec((tr, LANE), lambda i: (i, 0)),
                  pl.BlockSpec((LANE, HY_FFN), lambda i: (0, 0)),
                  pl.BlockSpec((1, HY_FFN), lambda i: (0, 0)),
                  pl.BlockSpec((HY_FFN, HY_FFN), lambda i: (0, 0)),
                  pl.BlockSpec((1, HY_FFN), lambda i: (0, 0)),
                  pl.BlockSpec((HY_FFN, ncol), lambda i: (0, jnp.where(i < nb_half, 0, 1))),
                  pl.BlockSpec((1, HY_FFN), lambda i: (0, 0)),
                  pl.BlockSpec((1, ncol), lambda i: (0, 0))],
        out_specs=[pl.BlockSpec((HY_ORDER, tr, HY_CH), lambda i: (0, i, 0)),
                   pl.BlockSpec((1, ncol), lambda i: (0, 0))],
        out_shape=[jax.ShapeDtypeStruct((HY_ORDER, n, HY_CH), F32),
                   jax.ShapeDtypeStruct((1, ncol), F32)],
        compiler_params=_cparams(("arbitrary",)),
    )(feats, w1p, b1.reshape(1, -1), w2, b2.reshape(1, -1), w3, freq.reshape(1, -1), decay.reshape(1, ncol))
    return k, (1.0 / ssum).reshape(HY_ORDER, 1, HY_CH)


def _dft_tables(n1, r_in):
    n = n1 * FFT_N2
    k1 = jnp.arange(n1)
    a1 = (2.0 * math.pi / n1) * ((k1[:, None] * jnp.arange(r_in)[None, :]) % n1).astype(F32)
    f1 = jnp.concatenate([jnp.cos(a1), -jnp.sin(a1)], axis=0)
    f3 = jnp.concatenate([jnp.cos(a1).T, -jnp.sin(a1).T], axis=1) / n
    n2 = jnp.arange(FFT_N2)
    kk = k1[:, None, None] + n1 * n2[None, :, None]
    ang = (2.0 * math.pi / n) * ((kk * n2[None, None, :]) % n).astype(F32)
    gr, gi = jnp.cos(ang), -jnp.sin(ang)
    g = jnp.concatenate([jnp.concatenate([gr, -gi], axis=2), jnp.concatenate([gi, gr], axis=2)], axis=1)
    return f1.astype(BF16), f3.astype(BF16), g.astype(BF16), jnp.swapaxes(g, 1, 2).astype(BF16)


def _fft1_kernel(f_ref, x_ref, o_ref):
    o_ref[...] = jnp.dot(f_ref[...], x_ref[...].astype(BF16), preferred_element_type=F32).astype(o_ref.dtype)


def _fft_stage1(f1, x):
    nb, r_in, cols = x.shape
    m = f1.shape[0]
    tn = _pick(cols, (8192, 4096, 2048))
    return pl.pallas_call(
        _fft1_kernel,
        grid=(nb, cols // tn),
        in_specs=[pl.BlockSpec((m, r_in), lambda b, j: (0, 0)),
                  pl.BlockSpec((None, r_in, tn), lambda b, j: (b, 0, j))],
        out_specs=pl.BlockSpec((None, m, tn), lambda b, j: (b, 0, j)),
        out_shape=jax.ShapeDtypeStruct((nb, m, cols), BF16),
        compiler_params=_cparams(("arbitrary", "arbitrary")),
    )(f1, x)


def _fft_filt_kernel(a_ref, g_ref, s_ref, o_ref):
    c = a_ref.shape[-1]
    a = a_ref[...].reshape(2 * FFT_N2, c)
    o_ref[...] = jnp.dot(g_ref[...], a, preferred_element_type=F32) * s_ref[...]


def _fft_filter_spectrum(a, g, inv):
    no, _, n1, _, c = a.shape
    return pl.pallas_call(
        _fft_filt_kernel,
        grid=(n1, no),
        in_specs=[pl.BlockSpec((None, 2, None, FFT_N2, c), lambda k, o: (o, 0, k, 0, 0)),
                  pl.BlockSpec((None, 2 * FFT_N2, 2 * FFT_N2), lambda k, o: (k, 0, 0)),
                  pl.BlockSpec((None, 1, c), lambda k, o: (o, 0, 0))],
        out_specs=pl.BlockSpec((None, None, 2 * FFT_N2, c), lambda k, o: (o, k, 0, 0)),
        out_shape=jax.ShapeDtypeStruct((no, n1, 2 * FFT_N2, c), F32),
        compiler_params=_cparams(("arbitrary", "arbitrary")),
    )(a, g, inv)


def _fft_mid_kernel(a_ref, g_ref, gt_ref, kh_ref, o_ref):
    c = a_ref.shape[-1]
    a = a_ref[...].reshape(2 * FFT_N2, c)
    x = jnp.dot(g_ref[...], a, preferred_element_type=F32)
    xr, xi = x[:FFT_N2], x[FFT_N2:]
    kr, ki = kh_ref[:FFT_N2, :], kh_ref[FFT_N2:, :]
    y = jnp.concatenate([xr * kr - xi * ki, xr * ki + xi * kr], axis=0).astype(BF16)
    bm = jnp.dot(gt_ref[...], y, preferred_element_type=F32)
    o_ref[...] = bm.reshape(2, FFT_N2, c).astype(o_ref.dtype)


def _fft_mid(a, g, gt, khat):
    nb, _, n1, _, c = a.shape
    return pl.pallas_call(
        _fft_mid_kernel,
        grid=(n1, nb),
        in_specs=[pl.BlockSpec((None, 2, None, FFT_N2, c), lambda k, b: (b, 0, k, 0, 0)),
                  pl.BlockSpec((None, 2 * FFT_N2, 2 * FFT_N2), lambda k, b: (k, 0, 0)),
                  pl.BlockSpec((None, 2 * FFT_N2, 2 * FFT_N2), lambda k, b: (k, 0, 0)),
                  pl.BlockSpec((None, 2 * FFT_N2, c), lambda k, b: (k, 0, 0))],
        out_specs=pl.BlockSpec((None, 2, None, FFT_N2, c), lambda k, b: (b, 0, k, 0, 0)),
        out_shape=jax.ShapeDtypeStruct(a.shape, BF16),
        compiler_params=_cparams(("arbitrary", "arbitrary")),
    )(a, g, gt, khat)


def _fft3_kernel(f_ref, bm_ref, y_ref, gate_ref, bias_ref, o_ref):
    conv = jnp.dot(f_ref[...], bm_ref[...], preferred_element_type=F32)
    o_ref[...] = gate_ref[...] * (conv + y_ref[...] * bias_ref[...])


def _fft_stage3(f3, bm, y, gate, bias_row):
    nb, m, cols = bm.shape
    r = f3.shape[0]
    tn = _pick(cols, (8192, 4096, 2048))
    return pl.pallas_call(
        _fft3_kernel,
        grid=(nb, cols // tn),
        in_specs=[pl.BlockSpec((r, m), lambda b, j: (0, 0)),
                  pl.BlockSpec((None, m, tn), lambda b, j: (b, 0, j)),
                  pl.BlockSpec((None, r, tn), lambda b, j: (b, 0, j)),
                  pl.BlockSpec((None, r, tn), lambda b, j: (b, 0, j)),
                  pl.BlockSpec((1, tn), lambda b, j: (0, j))],
        out_specs=pl.BlockSpec((None, r, tn), lambda b, j: (b, 0, j)),
        out_shape=jax.ShapeDtypeStruct((nb, r, cols), F32),
        compiler_params=_cparams(("arbitrary", "arbitrary")),
    )(f3, bm, y, gate, bias_row)


def _fft1_tok_kernel(f_ref, x_ref, o_ref, s_scr, *, rows):
    n1 = f_ref.shape[0]
    f = f_ref[...]
    pitch = n1 + FFT_PAD

    def body(n2, carry):
        xs = x_ref[pl.ds(n2, rows, stride=FFT_N2), :].astype(BF16)
        s_scr[pl.ds(pl.multiple_of(n2 * pitch, 8), n1), :] = jnp.dot(f, xs, preferred_element_type=F32)
        return carry

    lax.fori_loop(0, FFT_N2, body, 0, unroll=8)

    def emit(k1, carry):
        o_ref[k1] = s_scr[pl.ds(k1, FFT_N2, stride=pitch), :].astype(o_ref.dtype)
        return carry

    lax.fori_loop(0, n1, emit, 0, unroll=4)


def _fft_stage1_tok(f1, x):
    nb, n, c = x.shape
    rows = n // FFT_N2
    n1 = f1.shape[0] // 2
    return pl.pallas_call(
        functools.partial(_fft1_tok_kernel, rows=rows),
        grid=(nb, c // LANE, 2),
        in_specs=[pl.BlockSpec((None, n1, rows), lambda b, j, r: (r, 0, 0)),
                  pl.BlockSpec((None, n, LANE), lambda b, j, r: (b, 0, j))],
        out_specs=pl.BlockSpec((None, None, n1, FFT_N2, LANE), lambda b, j, r: (b, r, 0, 0, j)),
        out_shape=jax.ShapeDtypeStruct((nb, 2, n1, FFT_N2, c), BF16),
        scratch_shapes=[pltpu.VMEM((FFT_N2 * (n1 + FFT_PAD), LANE), F32)],
        compiler_params=_cparams(("arbitrary", "arbitrary", "arbitrary")),
    )(f1.reshape(2, n1, rows), x)


def _fft3_tok_kernel(f_ref, bm_ref, y_ref, gate_ref, bias_ref, o_ref, s_scr, t_scr, *, rows):
    half = pl.program_id(2)
    n1 = bm_ref.shape[0]
    sp, tp = FFT_N2 + FFT_PAD, rows + FFT_PAD

    def stage(k1, carry):
        s_scr[pl.ds(pl.multiple_of(k1 * sp, 8), FFT_N2), :] = bm_ref[k1].astype(F32)
        return carry

    lax.fori_loop(0, n1, stage, 0, unroll=4)
    f = f_ref[...]

    def part(n2):
        return jnp.dot(f, s_scr[pl.ds(n2, n1, stride=sp), :].astype(BF16), preferred_element_type=F32)

    def dst(n2):
        return pl.ds(pl.multiple_of(n2 * tp, 8), rows)

    @pl.when(half == 0)
    def _():
        def body(n2, carry):
            t_scr[dst(n2), :] = part(n2)
            return carry
        lax.fori_loop(0, FFT_N2, body, 0, unroll=8)

    @pl.when(half == 1)
    def _():
        def body(n2, carry):
            t_scr[dst(n2), :] += part(n2)
            return carry
        lax.fori_loop(0, FFT_N2, body, 0, unroll=8)

        def emit(r, carry):
            tok = pl.ds(pl.multiple_of(r * FFT_N2, FFT_N2), FFT_N2)
            conv = t_scr[pl.ds(r, FFT_N2, stride=tp), :]
            o_ref[tok, :] = gate_ref[tok, :] * (conv + y_ref[tok, :] * bias_ref[...])
            return carry
        lax.fori_loop(0, rows, emit, 0, unroll=2)


def _fft_stage3_tok(f3, bm, y, gate, bias):
    nb, _, n1, _, c = bm.shape
    n = y.shape[1]
    rows = n // FFT_N2
    tok = pl.BlockSpec((None, n, LANE), lambda b, j, r: (b, 0, j))
    return pl.pallas_call(
        functools.partial(_fft3_tok_kernel, rows=rows),
        grid=(nb, c // LANE, 2),
        in_specs=[pl.BlockSpec((None, rows, n1), lambda b, j, r: (r, 0, 0)),
                  pl.BlockSpec((None, None, n1, FFT_N2, LANE), lambda b, j, r: (b, r, 0, 0, j)),
                  tok, tok,
                  pl.BlockSpec((1, LANE), lambda b, j, r: (0, j))],
        out_specs=tok,
        out_shape=jax.ShapeDtypeStruct((nb, n, c), F32),
        scratch_shapes=[pltpu.VMEM((n1 * (FFT_N2 + FFT_PAD), LANE), F32),
                        pltpu.VMEM((FFT_N2 * (rows + FFT_PAD), LANE), F32)],
        compiler_params=_cparams(("arbitrary", "arbitrary", "arbitrary")),
    )(f3.reshape(rows, 2, n1).transpose(1, 0, 2), bm, y, gate, bias.astype(F32).reshape(1, c))


def _hyena_sequence(v, x1, x2, hy, bias):
    B, L, C = v.shape
    r_valid = L // FFT_N2
    r_in = max(r_valid, 16)
    n1 = max(2 * r_valid, r_in)
    n = n1 * FFT_N2
    f1, f3, g, gt = _dft_tables(n1, r_in)
    f1k = _dft_tables(n1, n1)[0]
    k, inv = _hyena_filters(L, n, *hy)
    if r_valid == r_in:
        khat = _fft_filter_spectrum(_fft_stage1_tok(f1k, k), g, inv)
        y = v
        for o, gate in enumerate((x1, x2)):
            bm = _fft_mid(_fft_stage1_tok(f1, y), g, gt, khat[o])
            y = _fft_stage3_tok(f3, bm, y, gate, bias[o])
        return y
    ak = _fft_stage1(f1k, k.reshape(HY_ORDER, n1, FFT_N2 * C))
    khat = _fft_filter_spectrum(ak.reshape(HY_ORDER, 2, n1, FFT_N2, C), g, inv)
    cols = FFT_N2 * C

    def view(a):
        a = a.reshape(B, r_valid, cols)
        return a if r_in == r_valid else jnp.pad(a, ((0, 0), (0, r_in - r_valid), (0, 0)))

    y = view(v)
    for o, gate in enumerate((view(x1), view(x2))):
        a = _fft_stage1(f1, y)
        bm = _fft_mid(a.reshape(B, 2, n1, FFT_N2, C), g, gt, khat[o])
        y = _fft_stage3(f3, bm.reshape(B, 2 * n1, cols), y, gate, jnp.tile(bias[o].astype(F32), FFT_N2)[None, :])
    return y[:, :r_valid].reshape(B, L, C)


def _s5_tables(a_re, a_im, log_dt, b_re, b_im, c_re, c_im, jj_ctx, jj_lat):
    lam = lax.complex(jnp.minimum(a_re.astype(F32), -1e-4), a_im.astype(F32))
    dt = jnp.exp(log_dt.astype(F32))[..., None]
    abar = jnp.exp(lam * dt)
    bbar = ((abar - 1.0) / lam)[..., None] * lax.complex(b_re.astype(F32), b_im.astype(F32))
    cmat = lax.complex(c_re.astype(F32), c_im.astype(F32))
    T = S5_T

    def powers(m):
        m = jnp.asarray(m, F32)
        return jnp.exp(lam * dt * m.reshape(m.shape + (1, 1, 1)))

    pw = powers(jnp.arange(T + 1))
    kt = jnp.real(jnp.einsum('dgop,tdgp,dgpi->tdgoi', cmat, pw[:T], bbar, precision=HI))
    tt = jnp.arange(T)
    lag = tt[None, :] - tt[:, None]
    w_intra = jnp.where((lag >= 0)[:, :, None, None, None, None],
                        kt[jnp.clip(lag, 0, T - 1)], 0.0)
    w_intra = jnp.stack([w_intra[:, :, 0], w_intra[::-1, ::-1, 1]], axis=2)
    w_intra = w_intra.transpose(2, 3, 0, 5, 1, 4).reshape(2, S5_NG, T * S5_GROUP, T * S5_GROUP)
    wb = pw[T - 1 - tt][..., None] * bbar[None]
    wb = jnp.stack([wb[:, 0], wb[::-1, 1]], axis=1)
    wb = wb.transpose(1, 2, 0, 4, 3).reshape(2, S5_NG, T * S5_GROUP, S5_P)
    w_cat = jnp.concatenate([w_intra, jnp.real(wb), jnp.imag(wb)], axis=-1)
    cp = cmat[None] * pw[1:, :, :, None, :]
    cp = jnp.stack([cp[:, 0], cp[::-1, 1]], axis=1)
    cp = cp.transpose(1, 2, 4, 0, 3).reshape(2, S5_NG, S5_P, T * S5_GROUP)
    c_cat = jnp.concatenate([jnp.real(cp), -jnp.imag(cp)], axis=2)

    def coef(z):
        zr, zi = jnp.real(z), jnp.imag(z)
        return jnp.stack([jnp.concatenate([zr, zr], -1), jnp.concatenate([-zi, zi], -1)], axis=-2)

    step = coef(powers(jnp.array(T)))
    seg_c, seg_l = (coef(powers(jnp.array(T * n))) for n in (jj_ctx, jj_lat))
    coefs = jnp.pad(jnp.concatenate([step, seg_c, seg_l], axis=2), ((0, 0), (0, 0), (0, 2), (0, 0)))
    ptab = coef(powers(T * jnp.arange(max(jj_ctx, jj_lat)))).transpose(1, 2, 0, 3, 4)
    return w_cat.astype(BF16), c_cat.astype(BF16), coefs, ptab


def _cmul(coef_a, coef_b, s):
    return coef_a * s + coef_b * pltpu.roll(s, S5_P, axis=1)


S5_GPB = LANE // S5_GROUP
S5_TC = S5_T * S5_GROUP


def _s5_in_kernel(z_ref, w_ref, yi_ref, ds_ref):
    nj = z_ref.shape[0] // S5_T
    ws = [z_ref[pl.ds(t, nj, stride=S5_T), :].T for t in range(S5_T)]
    for g in range(S5_GPB):
        vt = jnp.concatenate([w[S5_GROUP * g:S5_GROUP * (g + 1), :] for w in ws], axis=0)
        v = vt.T.astype(BF16)
        for d in range(2):
            o = jnp.dot(v, w_ref[d, g], preferred_element_type=F32)
            yi_ref[d, g] = o[:, :S5_TC]
            ds_ref[d, g] = o[:, S5_TC:]


def _s5_in(z, w_cat, nj):
    T = z.shape[0]
    R = T // S5_T
    col0 = HY_COLS // LANE
    return pl.pallas_call(
        _s5_in_kernel,
        grid=(R // nj, S5_CH // LANE),
        in_specs=[pl.BlockSpec((nj * S5_T, LANE), lambda i, c: (i, col0 + c)),
                  pl.BlockSpec((2, S5_GPB, S5_TC, S5_TC + 2 * S5_P), lambda i, c: (0, c, 0, 0))],
        out_specs=[pl.BlockSpec((2, S5_GPB, nj, S5_TC), lambda i, c: (0, c, i, 0)),
                   pl.BlockSpec((2, S5_GPB, nj, 2 * S5_P), lambda i, c: (0, c, i, 0))],
        out_shape=[jax.ShapeDtypeStruct((2, S5_NG, R, S5_TC), F32),
                   jax.ShapeDtypeStruct((2, S5_NG, R, 2 * S5_P), F32)],
        compiler_params=_cparams(("arbitrary", "arbitrary")),
    )(z, w_cat)


def _s5_scan_kernel(ds_ref, yi_ref, c_ref, cf_ref, p_ref, y_ref, sp_scr, *, parts, reverse):
    a1, a2 = cf_ref[0:1, :], cf_ref[1:2, :]
    rid = lax.broadcasted_iota(jnp.int32, (S5_SEG, 2 * S5_P), 0)
    first, last = (S5_SEG - 1, 0) if reverse else (0, S5_SEG - 1)
    shift = S5_SEG - 1 if reverse else 1
    nb = len(parts[0][0])
    zero = jnp.zeros((S5_SEG, 2 * S5_P), F32)
    s0 = [zero] * nb
    for pi, (bases, jj) in enumerate(parts):
        g1, g2 = cf_ref[2 + 2 * pi:3 + 2 * pi, :], cf_ref[3 + 2 * pi:4 + 2 * pi, :]

        def rows(b, k, bases=bases, jj=jj):
            return pl.ds(bases[b] + (jj - 1 - k if reverse else k), S5_SEG, stride=jj)

        def local_step(k, states, rows=rows):
            new = []
            for b in range(nb):
                sp_scr[rows(b, k), :] = states[b]
                new.append(_cmul(a1, a2, states[b]) + ds_ref[rows(b, k), :])
            return tuple(new)

        ends = lax.fori_loop(0, jj, local_step, (zero,) * nb)
        carries, nxt_s0 = [], []
        for b in range(nb):
            c = jnp.where(rid == first, s0[b], 0.0)
            for _ in range(S5_SEG - 1):
                c = jnp.where(rid == first, s0[b], pltpu.roll(ends[b] + _cmul(g1, g2, c), shift, axis=0))
            fin = ends[b] + _cmul(g1, g2, c)
            nxt_s0.append(jnp.broadcast_to(fin[last:last + 1, :], fin.shape))
            carries.append((c, pltpu.roll(c, S5_P, axis=1)))

        def fix_step(k, carry, rows=rows, carries=carries):
            p = p_ref[k]
            for b in range(nb):
                c, cs = carries[b]
                sp_scr[rows(b, k), :] += p[0:1, :] * c + p[1:2, :] * cs
            return carry

        lax.fori_loop(0, jj, fix_step, 0)
        s0 = nxt_s0
    y_ref[...] = yi_ref[...] + jnp.dot(sp_scr[...].astype(BF16), c_ref[...], preferred_element_type=F32)


def _s5_scan(d, ds, yi, c_cat, coefs, ptab, parts):
    R = ds.shape[2]
    jjm = ptab.shape[2]
    kern = functools.partial(_s5_scan_kernel, parts=parts, reverse=(d == 1))
    return pl.pallas_call(
        kern,
        grid=(S5_NG,),
        in_specs=[pl.BlockSpec((None, None, R, 2 * S5_P), lambda g: (d, g, 0, 0)),
                  pl.BlockSpec((None, None, R, S5_TC), lambda g: (d, g, 0, 0)),
                  pl.BlockSpec((None, None, 2 * S5_P, S5_TC), lambda g: (d, g, 0, 0)),
                  pl.BlockSpec((None, None, 8, 2 * S5_P), lambda g: (d, g, 0, 0)),
                  pl.BlockSpec((None, None, jjm, 2, 2 * S5_P), lambda g: (d, g, 0, 0, 0))],
        out_specs=pl.BlockSpec((None, R, S5_TC), lambda g: (g, 0, 0)),
        out_shape=jax.ShapeDtypeStruct((S5_NG, R, S5_TC), F32),
        scratch_shapes=[pltpu.VMEM((R, 2 * S5_P), F32)],
        compiler_params=_cparams(("arbitrary",)),
    )(ds, yi, c_cat, coefs, ptab)


def _s5_out_kernel(yf_ref, yb_ref, o_ref):
    nj = yf_ref.shape[1]
    yts = [(yf_ref[g] + yb_ref[g]).T for g in range(S5_GPB)]
    for t in range(S5_T):
        zt = jnp.concatenate([y[S5_GROUP * t:S5_GROUP * (t + 1), :] for y in yts], axis=0)
        o_ref[pl.ds(t, nj, stride=S5_T), :] = zt.T


def _s5_out(yf, yb, nj):
    R = yf.shape[1]
    spec = pl.BlockSpec((S5_GPB, nj, S5_TC), lambda i, c: (c, i, 0))
    return pl.pallas_call(
        _s5_out_kernel,
        grid=(R // nj, S5_CH // LANE),
        in_specs=[spec, spec],
        out_specs=pl.BlockSpec((nj * S5_T, LANE), lambda i, c: (i, c)),
        out_shape=jax.ShapeDtypeStruct((R * S5_T, S5_CH), F32),
        compiler_params=_cparams(("arbitrary", "arbitrary")),
    )(yf, yb)


def _s5_mixer(tok, z, s5):
    B = tok.B
    cl, cc = tok.L // S5_T, tok.CTX // S5_T
    jl, jc = cl // S5_SEG, cc // S5_SEG
    w_cat, c_cat, coefs, ptab = _s5_tables(*s5, jc, jl)
    yi, ds = _s5_in(z, w_cat, math.gcd(B * cl, B * cc, 64))
    lat = (tuple(b * cl for b in range(B)), jl)
    ctx = (tuple(B * cl + b * cc for b in range(B)), jc)
    yf = _s5_scan(0, ds, yi, c_cat, coefs, ptab, (ctx, lat))
    yb = _s5_scan(1, ds, yi, c_cat, coefs, ptab, (ctx, lat))
    return _s5_out(yf, yb, math.gcd(B * cl, B * cc, 64))


def _gelu_tanh(x):
    return 0.5 * x * (1.0 + jnp.tanh(math.sqrt(2.0 / math.pi) * (x + 0.044715 * (x * x * x))))


def _even_out_kernel(hl_ref, ys_ref, u_ref, h_ref, mod_ref, dsk_ref, wg_ref, wo_ref, o_ref):
    y = _gelu_tanh(ys_ref[...] + dsk_ref[...] * u_ref[...])
    s = y * jax.nn.sigmoid(jnp.dot(y.astype(BF16), wg_ref[...], preferred_element_type=F32))
    ol = (jnp.dot(hl_ref[...].astype(BF16), wo_ref[:HY_CH, :], preferred_element_type=F32)
          + jnp.dot(s.astype(BF16), wo_ref[HY_CH:, :], preferred_element_type=F32))
    o_ref[...] = h_ref[...] + mod_ref[2:3, :] * ol


def _even_out(tok, hl, ys, z, h, mods_l, dsk, w_glu, w_out):
    tm = tok.tm
    return pl.pallas_call(
        _even_out_kernel,
        grid=(tok.n_all,),
        in_specs=[pl.BlockSpec((tm, HY_CH), lambda i: (i, 0)),
                  pl.BlockSpec((tm, S5_CH), lambda i: (i, 0)),
                  pl.BlockSpec((tm, S5_CH), lambda i: (i, HY_COLS // S5_CH)),
                  pl.BlockSpec((tm, D), lambda i: (i, 0)),
                  pl.BlockSpec((None, N_MOD, D), lambda i: (tok.mod_row(i), 0, 0)),
                  pl.BlockSpec((1, S5_CH), lambda i: (0, 0)),
                  pl.BlockSpec((S5_CH, S5_CH), lambda i: (0, 0)),
                  pl.BlockSpec((D, D), lambda i: (0, 0))],
        out_specs=pl.BlockSpec((tm, D), lambda i: (i, 0)),
        out_shape=jax.ShapeDtypeStruct((tok.T, D), F32),
        compiler_params=_cparams(("arbitrary",)),
    )(hl, ys, z, h, mods_l, dsk.reshape(1, S5_CH), w_glu, w_out)


def _ffn_kernel(h_ref, mod_ref, g_ref, wg_ref, wu_ref, wd_ref, o_ref, y_scr, acc_scr):
    j = pl.program_id(1)

    @pl.when(j == 0)
    def _():
        y_scr[...] = _ada_norm(h_ref[...], g_ref[...], mod_ref[...], 3, 4).astype(BF16)
        acc_scr[...] = jnp.zeros_like(acc_scr)

    y = y_scr[...]
    gate = jnp.dot(y, wg_ref[...], preferred_element_type=F32)
    up = jnp.dot(y, wu_ref[...], preferred_element_type=F32)
    act = (gate * jax.nn.sigmoid(gate) * up).astype(BF16)
    acc_scr[...] += jnp.dot(act, wd_ref[...], preferred_element_type=F32)

    @pl.when(j == pl.num_programs(1) - 1)
    def _():
        o_ref[...] = h_ref[...] + mod_ref[5:6, :] * acc_scr[...]


def _ffn(tok, h, mods_l, gain, wg, wu, wd):
    tm = tok.tm
    ff = wg.shape[1]
    tf = _pick(ff, (1408, 512, 256, 128))
    return pl.pallas_call(
        _ffn_kernel,
        grid=(tok.n_all, ff // tf),
        in_specs=[pl.BlockSpec((tm, D), lambda i, j: (i, 0)),
                  pl.BlockSpec((None, N_MOD, D), lambda i, j: (tok.mod_row(i), 0, 0)),
                  pl.BlockSpec((1, D), lambda i, j: (0, 0)),
                  pl.BlockSpec((D, tf), lambda i, j: (0, j)),
                  pl.BlockSpec((D, tf), lambda i, j: (0, j)),
                  pl.BlockSpec((tf, D), lambda i, j: (j, 0))],
        out_specs=pl.BlockSpec((tm, D), lambda i, j: (i, 0)),
        out_shape=jax.ShapeDtypeStruct((tok.T, D), F32),
        scratch_shapes=[pltpu.VMEM((tm, D), BF16), pltpu.VMEM((tm, D), F32)],
        compiler_params=_cparams(("arbitrary", "arbitrary")),
    )(h, mods_l, gain.reshape(1, D), wg, wu, wd)


def _even_layer(tok, h, mods_l, p):
    B, L, CTX = tok.B, tok.L, tok.CTX
    nl = B * L
    z = _norm_matmul(tok, h, mods_l, p['norm_mix'], p['w_in'].astype(BF16), EV_IN)
    v, x1, x2 = _short_conv(tok, z, p['conv_w'], p['conv_b'])
    lat = lambda a: a[:nl].reshape(B, L, -1)
    ctx = lambda a: a[nl:].reshape(B, CTX, -1)
    hl = _hyena_sequence(lat(v), lat(x1), lat(x2), p['hy'], p['hy_bias'])
    parts = [hl.reshape(nl, HY_CH)]
    ys_all = _s5_mixer(tok, z, p['s5'])
    if p['need_ctx']:
        hc = _hyena_sequence(ctx(v), ctx(x1), ctx(x2), p['hy'], p['hy_bias'])
        parts.append(hc.reshape(B * CTX, HY_CH))
    else:
        parts.append(jnp.zeros((B * CTX, HY_CH), F32))
    hy_all = jnp.concatenate(parts, axis=0)
    h = _even_out(tok, hy_all, ys_all, z, h, mods_l, p['s5_d'], p['s5_w_glu'].astype(BF16), p['w_out'].astype(BF16))
    return _ffn(tok, h, mods_l, p['norm_ffn'], p['ff_wg'].astype(BF16), p['ff_wu'].astype(BF16),
                p['ff_wd'].astype(BF16))


def _rope_tables(L, tm):
    t = jnp.arange(L)
    row = (t // GRID_W).astype(F32)[:, None]
    col = (t % GRID_W).astype(F32)[:, None]

    def pattern(dim):
        nf = dim // 4
        inv = ROPE_BASE ** (-jnp.arange(nf, dtype=F32) / nf)
        ar, ac = row * inv[None, :], col * inv[None, :]
        cos = jnp.concatenate([jnp.cos(ar)] * 2 + [jnp.cos(ac)] * 2, axis=1)
        z = jnp.zeros((L, nf), F32)
        s_up = jnp.concatenate([-jnp.sin(ar), z, -jnp.sin(ac), z], axis=1)
        s_dn = jnp.concatenate([z, jnp.sin(ar), z, jnp.sin(ac)], axis=1)
        return cos, s_up, s_dn

    def pad_mla(a, fill):
        return jnp.concatenate([jnp.full((L, MLA_NOPE), fill, F32), a,
                                jnp.full((L, HEAD_PAD - MLA_NOPE - MLA_ROPE), fill, F32)], axis=1)

    cm, um, dm = pattern(MLA_ROPE)
    cg, ug, dg = pattern(GQA_HD)
    mla = jnp.stack([pad_mla(cm, 1.0), pad_mla(um, 0.0), pad_mla(dm, 0.0)])
    gqa = jnp.stack([jnp.tile(cg, (1, 2)), jnp.tile(ug, (1, 2)), jnp.tile(dg, (1, 2))])
    ident = jnp.stack([jnp.ones((tm, LANE), F32), jnp.zeros((tm, LANE), F32), jnp.zeros((tm, LANE), F32)])
    return jnp.stack([jnp.concatenate([mla, ident], axis=1), jnp.concatenate([gqa, ident], axis=1)])


def _rope(x, tab, w):
    outs = []
    for h in range(x.shape[1] // LANE):
        xs = x[:, h * LANE:(h + 1) * LANE]
        outs.append(xs * tab[0] + pltpu.roll(xs, LANE - w, axis=1) * tab[1] + pltpu.roll(xs, w, axis=1) * tab[2])
    return outs[0] if len(outs) == 1 else jnp.concatenate(outs, axis=1)


def _rms(x, g):
    return x * lax.rsqrt(jnp.mean(x * x, axis=-1, keepdims=True) + EPS) * g


_O_CQ, _O_CKV, _O_GQ, _O_GK, _O_GV, _O_KR = 0, 256, 384, 896, 1024, 1152


def _odd_proj_kernel(z_ref, tab_ref, qn_ref, kvn_ref, wuq_ref, wuk_ref, wuv_ref, e_ref,
                     q_ref, k_ref, v_ref, gq_ref, gk_ref, gv_ref):
    z = z_ref[...]
    mt, gt = tab_ref[0], tab_ref[1]
    qn = _rms(z[:, _O_CQ:_O_CKV], qn_ref[...]).astype(BF16)
    q = jnp.dot(qn, wuq_ref[...], preferred_element_type=F32)
    q_ref[...] = (_rope(q, mt, MLA_ROPE // 4) * (MLA_SCALE * LOG2E)).astype(BF16)
    kvn = _rms(z[:, _O_CKV:_O_GQ], kvn_ref[...]).astype(BF16)
    k = (jnp.dot(kvn, wuk_ref[...], preferred_element_type=F32)
         + jnp.dot(z[:, _O_KR:], e_ref[...], precision=HI, preferred_element_type=F32))
    k_ref[...] = _rope(k, mt, MLA_ROPE // 4).astype(BF16)
    v_ref[...] = jnp.dot(kvn, wuv_ref[...], preferred_element_type=F32).astype(BF16)
    gq_ref[...] = (_rope(z[:, _O_GQ:_O_GK], gt, GQA_HD // 4) * (GQA_SCALE * LOG2E)).astype(BF16)
    gk_ref[...] = _rope(z[:, _O_GK:_O_GV], gt, GQA_HD // 4).astype(BF16)
    gv_ref[...] = z[:, _O_GV:_O_KR].astype(BF16)


def _odd_proj(tok, z, tabs, q_norm, kv_norm, w_uq, w_ukv):
    tm = tok.tm
    hq = MLA_HEADS * HEAD_PAD
    wq = jnp.pad(w_uq.reshape(Q_LORA, MLA_HEADS, MLA_NOPE + MLA_ROPE),
                 ((0, 0), (0, 0), (0, HEAD_PAD - MLA_NOPE - MLA_ROPE))).reshape(Q_LORA, hq).astype(BF16)
    wkv = w_ukv.reshape(KV_LORA, MLA_HEADS, MLA_NOPE + MLA_V)
    wk = jnp.pad(wkv[..., :MLA_NOPE], ((0, 0), (0, 0), (0, HEAD_PAD - MLA_NOPE))).reshape(KV_LORA, hq).astype(BF16)
    wv = wkv[..., MLA_NOPE:].reshape(KV_LORA, MLA_HEADS * MLA_V).astype(BF16)
    eye = jnp.eye(MLA_ROPE, dtype=F32)
    e_head = jnp.pad(eye, ((0, LANE - MLA_ROPE), (MLA_NOPE, HEAD_PAD - MLA_NOPE - MLA_ROPE)))
    e = jnp.tile(e_head, (1, MLA_HEADS))
    tab_blk = lambda i: (0, 0, jnp.where(i < tok.n_lat, i % tok.per_seq, tok.per_seq), 0)
    full = lambda shape: pl.BlockSpec(shape, lambda i: (0,) * len(shape))
    widths = (hq, hq, MLA_HEADS * MLA_V, GQA_HEADS * GQA_HD, GQA_KV * GQA_HD, GQA_KV * GQA_HD)
    return pl.pallas_call(
        _odd_proj_kernel,
        grid=(tok.n_all,),
        in_specs=[pl.BlockSpec((tm, OD_IN_PAD), lambda i: (i, 0)),
                  pl.BlockSpec((2, 3, tm, LANE), tab_blk),
                  full((1, Q_LORA)), full((1, KV_LORA)), full((Q_LORA, hq)), full((KV_LORA, hq)),
                  full((KV_LORA, MLA_HEADS * MLA_V)), full((LANE, hq))],
        out_specs=[pl.BlockSpec((tm, w), lambda i: (i, 0)) for w in widths],
        out_shape=[jax.ShapeDtypeStruct((tok.T, w), BF16) for w in widths],
        compiler_params=_cparams(("arbitrary",)),
    )(z, tabs, q_norm.reshape(1, -1), kv_norm.reshape(1, -1), wq, wk, wv, e)


def _mla_attn_kernel(q_ref, k_ref, vt_ref, o_ref, s_scr, p_scr, *, tk, nk, ks):
    tq = q_ref.shape[0]
    nslab = tk // ks
    dn = (((1,), (1,)), ((), ()))
    qs = [q_ref[:, h * HEAD_PAD:(h + 1) * HEAD_PAD] for h in range(2)]

    def qk_slab(h, c, j, mx):
        r = pl.multiple_of(c * tk + j * ks, ks)
        s = lax.dot_general(k_ref[pl.ds(r, ks), h * HEAD_PAD:(h + 1) * HEAD_PAD], qs[h], dn,
                            preferred_element_type=F32)
        s_scr[h, j * ks:(j + 1) * ks, :] = s
        return jnp.maximum(mx, jnp.max(s, axis=0, keepdims=True))

    def pv_slab(h, c, j):
        return jnp.dot(vt_ref[c, h * MLA_V:(h + 1) * MLA_V, j * ks:(j + 1) * ks], p_scr[h, j * ks:(j + 1) * ks, :],
                       preferred_element_type=F32)

    def step(x, c_sm, c_pv, c_qk, st):
        y = 1 - x
        m, l, acc, mc = st[x]
        m_new = jnp.maximum(m, mc)
        alpha = jnp.exp2(m - m_new)
        lsum = jnp.zeros((1, tq), F32)
        acc_y = st[y][2]
        mx_y = jnp.full((1, tq), NEG, F32)
        for j in range(nslab):
            acc_y = acc_y + pv_slab(y, c_pv, j)
            mx_y = qk_slab(y, c_qk, j, mx_y)
            p = jnp.exp2(s_scr[x, j * ks:(j + 1) * ks, :] - m_new)
            lsum = lsum + jnp.sum(p, axis=0, keepdims=True)
            p_scr[x, j * ks:(j + 1) * ks, :] = p.astype(BF16)
        new = [None, None]
        new[x] = (m_new, alpha * l + lsum, alpha * acc, mc)
        new[y] = (st[y][0], st[y][1], acc_y, mx_y)
        return tuple(new)

    def body(c, st):
        st = step(0, c, jnp.maximum(c - 1, 0), c, st)
        return step(1, c, c, jnp.minimum(c + 1, nk - 1), st)

    neg = jnp.full((1, tq), NEG, F32)
    zero = jnp.zeros((1, tq), F32)
    acc0 = jnp.zeros((MLA_V, tq), F32)
    p_scr[1] = jnp.zeros(p_scr.shape[1:], BF16)
    mx0 = neg
    for j in range(nslab):
        mx0 = qk_slab(0, 0, j, mx0)
    st = lax.fori_loop(0, nk, body, ((neg, zero, acc0, mx0), (neg, zero, acc0, neg)))
    acc1 = st[1][2]
    for j in range(nslab):
        acc1 = acc1 + pv_slab(1, nk - 1, j)
    out_t = jnp.concatenate([st[0][2] / st[0][1], acc1 / st[1][1]], axis=0)
    o_ref[...] = out_t.T.astype(o_ref.dtype)


MLA_TQ = (256, 128)
MLA_TK = (768, 512, 256, 128)
MLA_KS = 128


def _mla_attention(q, k, v):
    B, Lq, _ = q.shape
    Nk = k.shape[1]
    tq = _pick(Lq, MLA_TQ)
    tk = _pick(Nk, MLA_TK)
    nk = Nk // tk
    hp = MLA_HEADS // 2
    vt = v.reshape(B, nk, tk, hp, 2 * MLA_V).transpose(0, 3, 1, 4, 2)
    kern = functools.partial(_mla_attn_kernel, tk=tk, nk=nk, ks=math.gcd(tk, MLA_KS))
    return pl.pallas_call(
        kern,
        grid=(B, hp, Lq // tq),
        in_specs=[pl.BlockSpec((None, tq, 2 * HEAD_PAD), lambda b, h, i: (b, i, h)),
                  pl.BlockSpec((None, Nk, 2 * HEAD_PAD), lambda b, h, i: (b, 0, h)),
                  pl.BlockSpec((None, None, nk, 2 * MLA_V, tk), lambda b, h, i: (b, h, 0, 0, 0))],
        out_specs=pl.BlockSpec((None, tq, 2 * MLA_V), lambda b, h, i: (b, i, h)),
        out_shape=jax.ShapeDtypeStruct((B, Lq, MLA_HEADS * MLA_V), BF16),
        scratch_shapes=[pltpu.VMEM((2, tk, tq), F32), pltpu.VMEM((2, tk, tq), BF16)],
        compiler_params=_cparams(("arbitrary", "arbitrary", "arbitrary")),
    )(q, k, vt)


def _gqa_kernel(sink_ref, q_ref, kc_ref, vct_ref, *rest, L, has_band):
    group = GQA_HEADS // GQA_KV
    gw = group * BLK
    if has_band:
        kp_ref, k_ref, kn_ref, vtp_ref, vt_ref, vtn_ref, o_ref = rest
        qi = pl.program_id(1)
        keys = jnp.concatenate([kp_ref[...], k_ref[...], kn_ref[...], kc_ref[...]], axis=0)
        vals_t = jnp.concatenate([vtp_ref[...], vt_ref[...], vtn_ref[...], vct_ref[...]], axis=1)
        nkey = keys.shape[0]
        kpos = (qi - 1) * BLK + lax.broadcasted_iota(jnp.int32, (nkey, gw), 0)
        qpos = qi * BLK + (lax.broadcasted_iota(jnp.int32, (nkey, gw), 1) & (BLK - 1))
        is_ctx = lax.broadcasted_iota(jnp.int32, (nkey, gw), 0) >= 3 * BLK
        valid = is_ctx | ((jnp.abs(qpos - kpos) <= WINDOW) & (kpos >= 0) & (kpos < L))
    else:
        (o_ref,) = rest
        keys, vals_t = kc_ref[...], vct_ref[...]
    dn = (((1,), (1,)), ((), ()))
    G = range(GQA_KV)
    qs = [jnp.concatenate([q_ref[:, (kh * group + g) * GQA_HD:(kh * group + g + 1) * GQA_HD] for g in range(group)],
                          axis=0) for kh in G]
    s = [lax.dot_general(keys[:, kh * GQA_HD:(kh + 1) * GQA_HD], qs[kh], dn, preferred_element_type=F32) for kh in G]
    if has_band:
        s = [jnp.where(valid, x, NEG) for x in s]
    sink = [sink_ref[:, kh * gw:(kh + 1) * gw] for kh in G]
    m = [jnp.maximum(jnp.max(s[kh], axis=0, keepdims=True), sink[kh]) for kh in G]
    p = [jnp.exp2(s[kh] - m[kh]) for kh in G]
    den = [jnp.sum(p[kh], axis=0, keepdims=True) + jnp.exp2(sink[kh] - m[kh]) for kh in G]
    ot = [jnp.dot(vals_t[kh * GQA_HD:(kh + 1) * GQA_HD, :], p[kh].astype(BF16), preferred_element_type=F32)
          * (1.0 / den[kh]) for kh in G]
    o = jnp.concatenate(ot, axis=0).T
    o_ref[...] = jnp.concatenate([o[g * BLK:(g + 1) * BLK, kh * GQA_HD:(kh + 1) * GQA_HD]
                                  for kh in G for g in range(group)], axis=1).astype(o_ref.dtype)


def _gqa_attention(sink, q, kc, vc, k=None, v=None):
    B, Lq, _ = q.shape
    CTX = kc.shape[1]
    has_band = k is not None
    kw = GQA_KV * GQA_HD
    nb = Lq // BLK
    sink_row = jnp.repeat(sink.astype(F32) * LOG2E, BLK)[None, :]
    in_specs = [pl.BlockSpec((1, GQA_HEADS * BLK), lambda b, i: (0, 0)),
                pl.BlockSpec((None, BLK, GQA_HEADS * GQA_HD), lambda b, i: (b, i, 0)),
                pl.BlockSpec((None, CTX, kw), lambda b, i: (b, 0, 0)),
                pl.BlockSpec((None, kw, CTX), lambda b, i: (b, 0, 0))]
    args = [sink_row, q, kc, jnp.swapaxes(vc, 1, 2)]
    if has_band:
        prev = lambda i: jnp.maximum(i - 1, 0)
        nxt = lambda i: jnp.minimum(i + 1, nb - 1)
        in_specs += [pl.BlockSpec((None, BLK, kw), lambda b, i: (b, prev(i), 0)),
                     pl.BlockSpec((None, BLK, kw), lambda b, i: (b, i, 0)),
                     pl.BlockSpec((None, BLK, kw), lambda b, i: (b, nxt(i), 0)),
                     pl.BlockSpec((None, kw, BLK), lambda b, i: (b, 0, prev(i))),
                     pl.BlockSpec((None, kw, BLK), lambda b, i: (b, 0, i)),
                     pl.BlockSpec((None, kw, BLK), lambda b, i: (b, 0, nxt(i)))]
        vt = jnp.swapaxes(v, 1, 2)
        args += [k, k, k, vt, vt, vt]
    kern = functools.partial(_gqa_kernel, L=Lq, has_band=has_band)
    return pl.pallas_call(
        kern,
        grid=(B, nb),
        in_specs=in_specs,
        out_specs=pl.BlockSpec((None, BLK, GQA_HEADS * GQA_HD), lambda b, i: (b, i, 0)),
        out_shape=jax.ShapeDtypeStruct((B, Lq, GQA_HEADS * GQA_HD), BF16),
        compiler_params=_cparams(("arbitrary", "arbitrary")),
    )(*args)


def _odd_out_kernel(a_ref, g_ref, h_ref, mod_ref, wo_ref, o_ref):
    half = a_ref.shape[1]
    ol = (jnp.dot(a_ref[...], wo_ref[:half, :], preferred_element_type=F32)
          + jnp.dot(g_ref[...], wo_ref[half:, :], preferred_element_type=F32))
    o_ref[...] = h_ref[...] + mod_ref[2:3, :] * ol


def _odd_out(tok, mla, gqa, h, mods_l, w_out):
    tm = tok.tm
    half = mla.shape[1]
    return pl.pallas_call(
        _odd_out_kernel,
        grid=(tok.n_all,),
        in_specs=[pl.BlockSpec((tm, half), lambda i: (i, 0)),
                  pl.BlockSpec((tm, half), lambda i: (i, 0)),
                  pl.BlockSpec((tm, D), lambda i: (i, 0)),
                  pl.BlockSpec((None, N_MOD, D), lambda i: (tok.mod_row(i), 0, 0)),
                  pl.BlockSpec((D, D), lambda i: (0, 0))],
        out_specs=pl.BlockSpec((tm, D), lambda i: (i, 0)),
        out_shape=jax.ShapeDtypeStruct((tok.T, D), F32),
        compiler_params=_cparams(("arbitrary",)),
    )(mla, gqa, h, mods_l, w_out)


MOE_TR = 1024
MOE_TF = 896
MOE_NF = EXP_FF // MOE_TF
ROUTE_W = 8


def _router_kernel(h_ref, mod_ref, g_ref, r_ref, y_ref, route_ref):
    y = _ada_norm(h_ref[...], g_ref[...], mod_ref[...], 3, 4)
    y_ref[...] = y
    logits = jnp.dot(y, r_ref[...], precision=HI, preferred_element_type=F32)
    lane = lax.broadcasted_iota(jnp.int32, logits.shape, 1)
    lg = jnp.where(lane < N_EXP, logits, -jnp.inf)
    m1 = jnp.max(lg, axis=-1, keepdims=True)
    i1 = jnp.min(jnp.where(lg == m1, lane, LANE), axis=-1, keepdims=True)
    lg2 = jnp.where(lane == i1, -jnp.inf, lg)
    m2 = jnp.max(lg2, axis=-1, keepdims=True)
    i2 = jnp.min(jnp.where(lg2 == m2, lane, LANE), axis=-1, keepdims=True)
    e = jnp.exp(m2 - m1)
    w1 = 1.0 / (1.0 + e)
    route = (jnp.where(lane == 0, w1, 0.0) + jnp.where(lane == 1, e * w1, 0.0)
             + jnp.where(lane == 2, i1.astype(F32), 0.0) + jnp.where(lane == 3, i2.astype(F32), 0.0))
    route_ref[...] = route[:, :ROUTE_W]


def _moe_router(tok, h, mods_l, gain, router):
    tm = tok.tm
    rp = jnp.pad(router, ((0, 0), (0, LANE - N_EXP)))
    return pl.pallas_call(
        _router_kernel,
        grid=(tok.n_all,),
        in_specs=[pl.BlockSpec((tm, D), lambda i: (i, 0)),
                  pl.BlockSpec((None, N_MOD, D), lambda i: (tok.mod_row(i), 0, 0)),
                  pl.BlockSpec((1, D), lambda i: (0, 0)),
                  pl.BlockSpec((D, LANE), lambda i: (0, 0))],
        out_specs=[pl.BlockSpec((tm, D), lambda i: (i, 0)), pl.BlockSpec((tm, ROUTE_W), lambda i: (i, 0))],
        out_shape=[jax.ShapeDtypeStruct((tok.T, D), F32), jax.ShapeDtypeStruct((tok.T, ROUTE_W), F32)],
        compiler_params=_cparams(("arbitrary",)),
    )(h, mods_l, gain.reshape(1, D), rp)


def _moe_plan(route, tr):
    T = route.shape[0]
    flat = route[:, 2:4].astype(jnp.int32).reshape(-1)
    onehot = (flat[:, None] == jnp.arange(N_EXP, dtype=jnp.int32)[None, :]).astype(jnp.int32)
    csum = jnp.cumsum(onehot, axis=0)
    rank = jnp.sum((csum - onehot) * onehot, axis=1)
    padded = (csum[-1] + tr - 1) // tr * tr
    ends = jnp.cumsum(padded)
    pos = (ends - padded)[flat] + rank
    n_tiles = (2 * T + N_EXP * (tr - 1)) // tr
    src = jnp.zeros((n_tiles * tr,), jnp.int32).at[pos].set(jnp.arange(2 * T, dtype=jnp.int32) // 2,
                                                            unique_indices=True)
    starts = jnp.arange(n_tiles, dtype=jnp.int32) * tr
    tile_expert = jnp.minimum(jnp.sum((starts[:, None] >= ends[None, :]).astype(jnp.int32), axis=1), N_EXP - 1)
    n_valid = (ends[-1] // tr).astype(jnp.int32).reshape(1)
    return src, tile_expert.astype(jnp.int32), n_valid, pos.reshape(T, 2)


def _gather_rows(idx_ref, src_hbm, dst, sem, n):
    def issue(r, carry):
        pltpu.make_async_copy(src_hbm.at[pl.ds(idx_ref[r], 1), :], dst.at[pl.ds(r, 1), :], sem).start()
        return carry

    lax.fori_loop(0, n, issue, 0, unroll=8)


def _wait_rows(src_hbm, dst, sem, n):
    pltpu.make_async_copy(src_hbm.at[pl.ds(0, n), :], dst, sem).wait()


def _moe_expert_kernel(te_ref, nv_ref, idx_ref, idxn_ref, y_hbm, wg_ref, wu_ref, wd_ref, o_ref,
                       xbuf, y_scr, acc_scr, sem, *, tr):
    t, f = pl.program_id(0), pl.program_id(1)
    nf = pl.num_programs(1)
    nv = nv_ref[0]
    valid = t < nv
    slot = lax.rem(t, 2)
    per_f = tr // MOE_NF

    @pl.when((f == 0) & (t == 0))
    def _():
        _gather_rows(idx_ref, y_hbm, xbuf.at[0], sem.at[0], tr)

    @pl.when((f == 0) & valid)
    def _():
        _wait_rows(y_hbm, xbuf.at[slot], sem.at[slot], tr)
        y_scr[...] = xbuf[slot].astype(BF16)
        acc_scr[...] = jnp.zeros_like(acc_scr)

    @pl.when(valid)
    def _():
        base = f * per_f
        nxt = xbuf.at[1 - slot]
        for r in range(per_f):
            pltpu.make_async_copy(y_hbm.at[pl.ds(idxn_ref[base + r], 1), :], nxt.at[pl.ds(base + r, 1), :],
                                  sem.at[1 - slot]).start()
        y = y_scr[...]
        gate = jnp.dot(y, wg_ref[...], preferred_element_type=F32)
        up = jnp.dot(y, wu_ref[...], preferred_element_type=F32)
        act = (gate * jax.nn.sigmoid(gate) * up).astype(BF16)
        acc_scr[...] += jnp.dot(act, wd_ref[...], preferred_element_type=F32)

    @pl.when((f == nf - 1) & (t == nv - 1))
    def _():
        _wait_rows(y_hbm, xbuf.at[1 - slot], sem.at[1 - slot], tr)

    @pl.when(f == nf - 1)
    def _():
        o_ref[...] = jnp.where(valid, acc_scr[...], 0.0)


def _moe_experts(y, src, tile_expert, n_valid, wg, wu, wd, tr):
    n_tiles = src.shape[0] // tr
    tf = MOE_TF
    kern = functools.partial(_moe_expert_kernel, tr=tr)
    smem = functools.partial(pl.BlockSpec, memory_space=pltpu.SMEM)
    grid_spec = pltpu.PrefetchScalarGridSpec(
        num_scalar_prefetch=2,
        grid=(n_tiles, EXP_FF // tf),
        in_specs=[smem((tr,), lambda t, f, te, nv: (t,)),
                  smem((tr,), lambda t, f, te, nv: (jnp.minimum(t + 1, jnp.maximum(nv[0] - 1, 0)),)),
                  pl.BlockSpec(memory_space=pl.ANY),
                  pl.BlockSpec((None, D, tf), lambda t, f, te, nv: (te[t], 0, f)),
                  pl.BlockSpec((None, D, tf), lambda t, f, te, nv: (te[t], 0, f)),
                  pl.BlockSpec((None, tf, D), lambda t, f, te, nv: (te[t], f, 0))],
        out_specs=pl.BlockSpec((tr, D), lambda t, f, te, nv: (t, 0)),
        scratch_shapes=[pltpu.VMEM((2, tr, D), F32), pltpu.VMEM((tr, D), BF16), pltpu.VMEM((tr, D), F32),
                        pltpu.SemaphoreType.DMA((2,))])
    return pl.pallas_call(
        kern,
        grid_spec=grid_spec,
        out_shape=jax.ShapeDtypeStruct((n_tiles * tr, D), F32),
        compiler_params=_cparams(("arbitrary", "arbitrary")),
    )(tile_expert, n_valid, src, src, y, wg, wu, wd)


def _moe_combine_kernel(idx_ref, idxn_ref, o_hbm, h_ref, route_ref, mod_ref, out_ref, buf, sem, *, tm):
    i = pl.program_id(0)
    n = pl.num_programs(0)
    slot = lax.rem(i, 2)

    @pl.when(i == 0)
    def _():
        _gather_rows(idx_ref, o_hbm, buf.at[0], sem.at[0], 2 * tm)

    nxt = buf.at[1 - slot]
    for r in range(2 * tm):
        pltpu.make_async_copy(o_hbm.at[pl.ds(idxn_ref[r], 1), :], nxt.at[pl.ds(r, 1), :], sem.at[1 - slot]).start()
    _wait_rows(o_hbm, buf.at[slot], sem.at[slot], 2 * tm)
    r = route_ref[...]
    mix = r[:, 0:1] * buf[slot, :tm, :] + r[:, 1:2] * buf[slot, tm:, :]
    out_ref[...] = h_ref[...] + mod_ref[5:6, :] * mix

    @pl.when(i == n - 1)
    def _():
        _wait_rows(o_hbm, nxt, sem.at[1 - slot], 2 * tm)


def _moe_combine(tok, o_sorted, pos, h, route, mods_l):
    tm = tok.tm
    n = tok.n_all
    idx = pos.reshape(n, tm, 2).transpose(0, 2, 1).reshape(-1)
    kern = functools.partial(_moe_combine_kernel, tm=tm)
    smem = functools.partial(pl.BlockSpec, memory_space=pltpu.SMEM)
    return pl.pallas_call(
        kern,
        grid=(n,),
        in_specs=[smem((2 * tm,), lambda i: (i,)),
                  smem((2 * tm,), lambda i: (jnp.minimum(i + 1, n - 1),)),
                  pl.BlockSpec(memory_space=pl.ANY),
                  pl.BlockSpec((tm, D), lambda i: (i, 0)),
                  pl.BlockSpec((tm, ROUTE_W), lambda i: (i, 0)),
                  pl.BlockSpec((None, N_MOD, D), lambda i: (tok.mod_row(i), 0, 0))],
        out_specs=pl.BlockSpec((tm, D), lambda i: (i, 0)),
        out_shape=jax.ShapeDtypeStruct((tok.T, D), F32),
        scratch_shapes=[pltpu.VMEM((2, 2 * tm, D), F32), pltpu.SemaphoreType.DMA((2,))],
        compiler_params=_cparams(("arbitrary",)),
    )(idx, idx, o_sorted, h, route, mods_l)


def _moe(tok, h, mods_l, gain, router, wg, wu, wd):
    y, route = _moe_router(tok, h, mods_l, gain, router)
    src, tile_expert, n_valid, pos = _moe_plan(route, MOE_TR)
    o_sorted = _moe_experts(y, src, tile_expert, n_valid, wg, wu, wd, MOE_TR)
    return _moe_combine(tok, o_sorted, pos, h, route, mods_l)


def _odd_layer(tok, h, mods_l, tabs, p):
    B, L, CTX = tok.B, tok.L, tok.CTX
    nl = B * L
    w = p['w_in']
    w_in = jnp.concatenate([w[:, :Q_LORA + KV_LORA], w[:, 416:1184], w[:, 384:416],
                            jnp.zeros((D, OD_IN_PAD - 1184), w.dtype)], axis=1).astype(BF16)
    z = _norm_matmul(tok, h, mods_l, p['norm_mix'], w_in, OD_IN_PAD)
    q, k, v, gq, gk, gv = _odd_proj(tok, z, tabs, p['q_norm'], p['kv_norm'], p['w_uq'], p['w_ukv'])
    lat = lambda a: a[:nl].reshape(B, L, -1)
    ctx = lambda a: a[nl:].reshape(B, CTX, -1)
    cat = lambda a: jnp.concatenate([ctx(a), lat(a)], axis=1)
    mla_l = _mla_attention(lat(q), cat(k), cat(v))
    gqa_l = _gqa_attention(p['sink'], lat(gq), ctx(gk), ctx(gv), lat(gk), lat(gv))
    if p['need_ctx']:
        mla_c = _mla_attention(ctx(q), ctx(k), ctx(v))
        gqa_c = _gqa_attention(p['sink'], ctx(gq), ctx(gk), ctx(gv))
    else:
        mla_c = jnp.zeros((B, CTX, MLA_HEADS * MLA_V), BF16)
        gqa_c = jnp.zeros((B, CTX, GQA_HEADS * GQA_HD), BF16)
    flat = lambda a, c: jnp.concatenate([a.reshape(nl, -1), c.reshape(B * CTX, -1)], axis=0)
    h = _odd_out(tok, flat(mla_l, mla_c), flat(gqa_l, gqa_c), h, mods_l, p['w_out'].astype(BF16))
    return _moe(tok, h, mods_l, p['norm_ffn'], p['router'], p['moe_wg'].astype(BF16),
                p['moe_wu'].astype(BF16), p['moe_wd'].astype(BF16))


def _final_norm_kernel(h_ref, g_ref, o_ref):
    o_ref[...] = _rms(h_ref[...], g_ref[...])


def _final_norm(tok, h, gain):
    tm = tok.tm
    return pl.pallas_call(
        _final_norm_kernel,
        grid=(tok.n_lat,),
        in_specs=[pl.BlockSpec((tm, D), lambda i: (i, 0)), pl.BlockSpec((1, D), lambda i: (0, 0))],
        out_specs=pl.BlockSpec((tm, D), lambda i: (i, 0)),
        out_shape=jax.ShapeDtypeStruct((tok.B * tok.L, D), F32),
        compiler_params=_cparams(("arbitrary",)),
    )(h, gain.reshape(1, D))


def kernel(x, c, ctx, c_ctx, mod_w, mod_b, norm_mix, norm_ffn, final_norm,
           ev_w_in, ev_conv_w, ev_conv_b, hy_w1, hy_b1, hy_w2, hy_b2, hy_w3, hy_freq, hy_decay, hy_bias,
           s5_a_re, s5_a_im, s5_log_dt, s5_b_re, s5_b_im, s5_c_re, s5_c_im, s5_d, s5_w_glu, ev_w_out,
           ff_w_gate, ff_w_up, ff_w_down,
           od_w_in, mla_q_norm, mla_w_uq, mla_kv_norm, mla_w_ukv, gqa_sink, od_w_out,
           moe_router, moe_w_gate, moe_w_up, moe_w_down):
    B, L, _ = x.shape
    CTX = ctx.shape[1]
    tok = _Tok(B, L, CTX, _pick(math.gcd(L, B * CTX), (512, 256, 128)))
    cond_t = jnp.concatenate([c, c_ctx[None, :], jnp.zeros((8 - B - 1, D), F32)], axis=0).T
    mods = _modulations(cond_t, B + 1, mod_w, mod_b)
    tabs = _rope_tables(L, tok.tm)
    h = jnp.concatenate([x.reshape(B * L, D), ctx.reshape(B * CTX, D)], axis=0)
    for l in range(DEPTH):
        i = l // 2
        need_ctx = l < DEPTH - 1
        if l % 2 == 0:
            p = dict(norm_mix=norm_mix[l], norm_ffn=norm_ffn[l], w_in=ev_w_in[i], conv_w=ev_conv_w[i],
                     conv_b=ev_conv_b[i],
                     hy=(hy_w1[i], hy_b1[i], hy_w2[i], hy_b2[i], hy_w3[i], hy_freq[i], hy_decay[i]),
                     hy_bias=hy_bias[i],
                     s5=(s5_a_re[i], s5_a_im[i], s5_log_dt[i], s5_b_re[i], s5_b_im[i], s5_c_re[i], s5_c_im[i]),
                     s5_d=s5_d[i], s5_w_glu=s5_w_glu[i], w_out=ev_w_out[i],
                     ff_wg=ff_w_gate[i], ff_wu=ff_w_up[i], ff_wd=ff_w_down[i], need_ctx=need_ctx)
            h = _even_layer(tok, h, mods[l], p)
        else:
            p = dict(norm_mix=norm_mix[l], norm_ffn=norm_ffn[l], w_in=od_w_in[i], q_norm=mla_q_norm[i],
                     w_uq=mla_w_uq[i], kv_norm=mla_kv_norm[i], w_ukv=mla_w_ukv[i], sink=gqa_sink[i],
                     w_out=od_w_out[i], router=moe_router[i], moe_wg=moe_w_gate[i], moe_wu=moe_w_up[i],
                     moe_wd=moe_w_down[i], need_ctx=need_ctx)
            h = _odd_layer(tok, h, mods[l], tabs, p)
    return _final_norm(tok, h, final_norm).reshape(B, L, D)
```

```python
import functools
import math

import jax
import jax.numpy as jnp
from jax import lax
from jax.experimental import pallas as pl
from jax.experimental.pallas import tpu as pltpu

F32 = jnp.float32
BF16 = jnp.bfloat16
HI = lax.Precision.HIGHEST

D = 1024
DEPTH = 4
GRID_W = 64
EPS = 1e-6
NEG = -1e30
N_MOD = 6

HY_CH = 512
HY_ORDER = 2
HY_BANDS = 16
HY_EMB = 1 + 2 * HY_BANDS
HY_FFN = 64
HY_SHIFT = 0.05
HY_COLS = (HY_ORDER + 1) * HY_CH
S5_CH = 512
S5_GROUP = 16
S5_NG = S5_CH // S5_GROUP
S5_P = 64
S5_T = 16
S5_SEG = 8
EV_IN = HY_COLS + S5_CH

MLA_HEADS = 8
MLA_NOPE = 64
MLA_ROPE = 32
MLA_V = 64
Q_LORA = 256
KV_LORA = 128
GQA_HEADS = 8
GQA_KV = 2
GQA_HD = 64
WINDOW = 128
BLK = 128
ROPE_BASE = 10000.0
MLA_SCALE = (MLA_NOPE + MLA_ROPE) ** -0.5
GQA_SCALE = GQA_HD ** -0.5
LOG2E = math.log2(math.e)
HEAD_PAD = 128
MLA_VP = MLA_V + 16
OD_IN_PAD = 1280

D_FF = 2816
N_EXP = 8
EXP_FF = 3584

LANE = 128
FFT_N2 = 128
FFT_PAD = 8
VMEM_LIMIT = 56 * 1024 * 1024


def _cparams(sem):
    return pltpu.CompilerParams(dimension_semantics=sem, vmem_limit_bytes=VMEM_LIMIT)


def _pick(n, cands):
    for c in cands:
        if n % c == 0:
            return c
    raise ValueError(f"no tile for {n} in {cands}")


def _mod_kernel(ct_ref, w_ref, b_ref, o_ref, *, nrows):
    c = ct_ref[...]
    s = c * jax.nn.sigmoid(c)
    w = w_ref[...]
    rows = [jnp.sum(w * s[:, r:r + 1], axis=0, keepdims=True) for r in range(nrows)]
    rows.append(jnp.zeros((8 - nrows, w.shape[1]), F32))
    o_ref[...] = jnp.concatenate(rows, axis=0) + b_ref[...]


def _modulations(cond_t, nrows, mod_w, mod_b):
    tn = 1536
    out = pl.pallas_call(
        functools.partial(_mod_kernel, nrows=nrows),
        grid=(DEPTH, N_MOD * D // tn),
        in_specs=[pl.BlockSpec((D, 8), lambda l, j: (0, 0)),
                  pl.BlockSpec((None, D, tn), lambda l, j: (l, 0, j)),
                  pl.BlockSpec((None, 1, tn), lambda l, j: (l, 0, j))],
        out_specs=pl.BlockSpec((None, 8, tn), lambda l, j: (l, 0, j)),
        out_shape=jax.ShapeDtypeStruct((DEPTH, 8, N_MOD * D), F32),
        compiler_params=_cparams(("arbitrary", "arbitrary")),
    )(cond_t, mod_w, mod_b.reshape(DEPTH, 1, N_MOD * D))
    return out.reshape(DEPTH, 8, N_MOD, D)


class _Tok:
    def __init__(self, B, L, CTX, tm):
        assert L % tm == 0 and (B * CTX) % tm == 0
        self.B, self.L, self.CTX, self.tm = B, L, CTX, tm
        self.n_lat = B * L // tm
        self.n_all = self.n_lat + B * CTX // tm
        self.T = B * (L + CTX)
        self.per_seq = L // tm

    def mod_row(self, i):
        return jnp.where(i < self.n_lat, i // self.per_seq, self.B)


def _ada_norm(x, gain, mod, shift_idx, scale_idx):
    y = x * lax.rsqrt(jnp.mean(x * x, axis=-1, keepdims=True) + EPS) * gain
    return y * (1.0 + mod[scale_idx:scale_idx + 1, :]) + mod[shift_idx:shift_idx + 1, :]


def _norm_mm_kernel(h_ref, mod_ref, g_ref, w_ref, o_ref, y_scr):
    @pl.when(pl.program_id(1) == 0)
    def _():
        y_scr[...] = _ada_norm(h_ref[...], g_ref[...], mod_ref[...], 0, 1).astype(BF16)

    o_ref[...] = jnp.dot(y_scr[...], w_ref[...], preferred_element_type=F32).astype(o_ref.dtype)


def _norm_matmul(tok, h, mods_l, gain, w, tn, out_dtype=F32):
    tm, n = tok.tm, w.shape[1]
    return pl.pallas_call(
        _norm_mm_kernel,
        grid=(tok.n_all, n // tn),
        in_specs=[pl.BlockSpec((tm, D), lambda i, j: (i, 0)),
                  pl.BlockSpec((None, N_MOD, D), lambda i, j: (tok.mod_row(i), 0, 0)),
                  pl.BlockSpec((1, D), lambda i, j: (0, 0)),
                  pl.BlockSpec((D, tn), lambda i, j: (0, j))],
        out_specs=pl.BlockSpec((tm, tn), lambda i, j: (i, j)),
        out_shape=jax.ShapeDtypeStruct((tok.T, n), out_dtype),
        scratch_shapes=[pltpu.VMEM((tm, D), BF16)],
        compiler_params=_cparams(("arbitrary", "arbitrary")),
    )(h, mods_l, gain.reshape(1, D), w)


def _short_conv_kernel(z_ref, zp_ref, zn_ref, w_ref, b_ref, v_ref, x1_ref, x2_ref, *, tm, n_lat, L, CTX):
    i = pl.program_id(0)
    is_lat = i < n_lat
    seqlen = jnp.where(is_lat, L, CTX)
    off = jnp.where(is_lat, i * tm, (i - n_lat) * tm)
    first = lax.rem(off, seqlen) == 0
    last = lax.rem(off + tm, seqlen) == 0
    z = z_ref[...]
    prev_row = jnp.where(first, 0.0, zp_ref[7:8, :])
    next_row = jnp.where(last, 0.0, zn_ref[0:1, :])
    rid = lax.broadcasted_iota(jnp.int32, z.shape, 0)
    zm1 = jnp.where(rid == 0, prev_row, pltpu.roll(z, 1, axis=0))
    zp1 = jnp.where(rid == tm - 1, next_row, pltpu.roll(z, tm - 1, axis=0))
    out = b_ref[...] + zm1 * w_ref[0:1, :] + z * w_ref[1:2, :] + zp1 * w_ref[2:3, :]
    v_ref[...] = out[:, :HY_CH]
    x1_ref[...] = out[:, HY_CH:2 * HY_CH]
    x2_ref[...] = out[:, 2 * HY_CH:]


def _short_conv(tok, z, conv_w, conv_b):
    tm = _pick(math.gcd(tok.L, tok.CTX), (256, 128))
    n_lat = tok.B * tok.L // tm
    n_all = tok.T // tm
    r8 = tm // 8
    kern = functools.partial(_short_conv_kernel, tm=tm, n_lat=n_lat, L=tok.L, CTX=tok.CTX)
    o = jax.ShapeDtypeStruct((tok.T, HY_CH), F32)
    return pl.pallas_call(
        kern,
        grid=(n_all,),
        in_specs=[pl.BlockSpec((tm, HY_COLS), lambda i: (i, 0)),
                  pl.BlockSpec((8, HY_COLS), lambda i: (jnp.maximum(i * r8 - 1, 0), 0)),
                  pl.BlockSpec((8, HY_COLS), lambda i: (jnp.minimum((i + 1) * r8, n_all * r8 - 1), 0)),
                  pl.BlockSpec((8, HY_COLS), lambda i: (0, 0)),
                  pl.BlockSpec((1, HY_COLS), lambda i: (0, 0))],
        out_specs=[pl.BlockSpec((tm, HY_CH), lambda i: (i, 0))] * 3,
        out_shape=[o, o, o],
        compiler_params=_cparams(("arbitrary",)),
    )(z, z, z, jnp.pad(conv_w, ((0, 8 - conv_w.shape[0]), (0, 0))), conv_b.reshape(1, HY_COLS))


def _filter_kernel(f_ref, w1_ref, b1_ref, w2_ref, b2_ref, w3_ref, fr_ref, dec_ref, k_ref, s_ref):
    @pl.when(pl.program_id(0) == 0)
    def _():
        s_ref[...] = jnp.zeros_like(s_ref)

    f = f_ref[...]
    fr = fr_ref[...]
    hid = jnp.sin(fr * (jnp.dot(f, w1_ref[...], precision=HI, preferred_element_type=F32) + b1_ref[...]))
    hid = jnp.sin(fr * (jnp.dot(hid, w2_ref[...], precision=HI, preferred_element_type=F32) + b2_ref[...]))
    h = jnp.dot(hid, w3_ref[...], precision=HI, preferred_element_type=F32)
    t01 = f[:, 0:1]
    valid = f[:, LANE - 1:LANE]
    k = h * (jnp.exp(-t01 * jnp.abs(dec_ref[...])) + HY_SHIFT) * valid
    k_ref[0] = k[:, :HY_CH]
    k_ref[1] = k[:, HY_CH:]
    s_ref[...] += jnp.sum(jnp.abs(k), axis=0, keepdims=True)


def _hyena_filters(L, n, w1, b1, w2, b2, w3, freq, decay):
    row = jnp.arange(n)
    fwd = row < L
    bwd = row > n - L
    t = jnp.where(fwd, row, n - row).astype(F32)
    t01 = t / L
    bands = jnp.linspace(1e-4, HY_BANDS - 1, HY_BANDS, dtype=F32)
    ang = (2.0 * math.pi / L) * t[:, None] * bands[None, :]
    valid = (fwd | bwd).astype(F32)
    feats = jnp.concatenate([t01[:, None], jnp.cos(ang), -jnp.sin(ang),
                             jnp.zeros((n, LANE - 1 - HY_EMB), F32), valid[:, None]], axis=-1)
    w1p = jnp.pad(w1, ((0, LANE - HY_EMB), (0, 0)))
    tr = _pick(L, (512, 256))
    nb_half = n // 2 // tr
    ncol = HY_ORDER * HY_CH
    k, ssum = pl.pallas_call(
        _filter_kernel,
        grid=(n // tr,),
        in_specs=[pl.BlockSpec((tr, LANE), lambda i: (i, 0)),
                  pl.BlockSpec((LANE, HY_FFN), lambda i: (0, 0)),
                  pl.BlockSpec((1, HY_FFN), lambda i: (0, 0)),
                  pl.BlockSpec((HY_FFN, HY_FFN), lambda i: (0, 0)),
                  pl.BlockSpec((1, HY_FFN), lambda i: (0, 0)),
                  pl.BlockSpec((HY_FFN, ncol), lambda i: (0, jnp.where(i < nb_half, 0, 1))),
                  pl.BlockSpec((1, HY_FFN), lambda i: (0, 0)),
                  pl.BlockSpec((1, ncol), lambda i: (0, 0))],
        out_specs=[pl.BlockSpec((HY_ORDER, tr, HY_CH), lambda i: (0, i, 0)),
                   pl.BlockSpec((1, ncol), lambda i: (0, 0))],
        out_shape=[jax.ShapeDtypeStruct((HY_ORDER, n, HY_CH), F32),
                   jax.ShapeDtypeStruct((1, ncol), F32)],
        compiler_params=_cparams(("arbitrary",)),
    )(feats, w1p, b1.reshape(1, -1), w2, b2.reshape(1, -1), w3, freq.reshape(1, -1), decay.reshape(1, ncol))
    return k, (1.0 / ssum).reshape(HY_ORDER, 1, HY_CH)


def _dft_tables(n1, r_in):
    n = n1 * FFT_N2
    k1 = jnp.arange(n1)
    a1 = (2.0 * math.pi / n1) * ((k1[:, None] * jnp.arange(r_in)[None, :]) % n1).astype(F32)
    f1 = jnp.concatenate([jnp.cos(a1), -jnp.sin(a1)], axis=0)
    f3 = jnp.concatenate([jnp.cos(a1).T, -jnp.sin(a1).T], axis=1) / n
    n2 = jnp.arange(FFT_N2)
    kk = k1[:, None, None] + n1 * n2[None, :, None]
    ang = (2.0 * math.pi / n) * ((kk * n2[None, None, :]) % n).astype(F32)
    gr, gi = jnp.cos(ang), -jnp.sin(ang)
    g = jnp.concatenate([jnp.concatenate([gr, -gi], axis=2), jnp.concatenate([gi, gr], axis=2)], axis=1)
    return f1.astype(BF16), f3.astype(BF16), g.astype(BF16), jnp.swapaxes(g, 1, 2).astype(BF16)


def _fft1_kernel(f_ref, x_ref, o_ref):
    o_ref[...] = jnp.dot(f_ref[...], x_ref[...].astype(BF16), preferred_element_type=F32).astype(o_ref.dtype)


def _fft_stage1(f1, x):
    nb, r_in, cols = x.shape
    m = f1.shape[0]
    tn = _pick(cols, (8192, 4096, 2048))
    return pl.pallas_call(
        _fft1_kernel,
        grid=(nb, cols // tn),
        in_specs=[pl.BlockSpec((m, r_in), lambda b, j: (0, 0)),
                  pl.BlockSpec((None, r_in, tn), lambda b, j: (b, 0, j))],
        out_specs=pl.BlockSpec((None, m, tn), lambda b, j: (b, 0, j)),
        out_shape=jax.ShapeDtypeStruct((nb, m, cols), BF16),
        compiler_params=_cparams(("arbitrary", "arbitrary")),
    )(f1, x)


def _fft_filt_kernel(a_ref, g_ref, s_ref, o_ref):
    c = a_ref.shape[-1]
    a = a_ref[...].reshape(2 * FFT_N2, c)
    o_ref[...] = jnp.dot(g_ref[...], a, preferred_element_type=F32) * s_ref[...]


def _fft_filter_spectrum(a, g, inv):
    no, _, n1, _, c = a.shape
    return pl.pallas_call(
        _fft_filt_kernel,
        grid=(n1, no),
        in_specs=[pl.BlockSpec((None, 2, None, FFT_N2, c), lambda k, o: (o, 0, k, 0, 0)),
                  pl.BlockSpec((None, 2 * FFT_N2, 2 * FFT_N2), lambda k, o: (k, 0, 0)),
                  pl.BlockSpec((None, 1, c), lambda k, o: (o, 0, 0))],
        out_specs=pl.BlockSpec((None, None, 2 * FFT_N2, c), lambda k, o: (o, k, 0, 0)),
        out_shape=jax.ShapeDtypeStruct((no, n1, 2 * FFT_N2, c), F32),
        compiler_params=_cparams(("arbitrary", "arbitrary")),
    )(a, g, inv)


def _fft_mid_kernel(a_ref, g_ref, gt_ref, kh_ref, o_ref):
    c = a_ref.shape[-1]
    a = a_ref[...].reshape(2 * FFT_N2, c)
    x = jnp.dot(g_ref[...], a, preferred_element_type=F32)
    xr, xi = x[:FFT_N2], x[FFT_N2:]
    kr, ki = kh_ref[:FFT_N2, :], kh_ref[FFT_N2:, :]
    y = jnp.concatenate([xr * kr - xi * ki, xr * ki + xi * kr], axis=0).astype(BF16)
    bm = jnp.dot(gt_ref[...], y, preferred_element_type=F32)
    o_ref[...] = bm.reshape(2, FFT_N2, c).astype(o_ref.dtype)


def _fft_mid(a, g, gt, khat):
    nb, _, n1, _, c = a.shape
    return pl.pallas_call(
        _fft_mid_kernel,
        grid=(n1, nb),
        in_specs=[pl.BlockSpec((None, 2, None, FFT_N2, c), lambda k, b: (b, 0, k, 0, 0)),
                  pl.BlockSpec((None, 2 * FFT_N2, 2 * FFT_N2), lambda k, b: (k, 0, 0)),
                  pl.BlockSpec((None, 2 * FFT_N2, 2 * FFT_N2), lambda k, b: (k, 0, 0)),
                  pl.BlockSpec((None, 2 * FFT_N2, c), lambda k, b: (k, 0, 0))],
        out_specs=pl.BlockSpec((None, 2, None, FFT_N2, c), lambda k, b: (b, 0, k, 0, 0)),
        out_shape=jax.ShapeDtypeStruct(a.shape, BF16),
        compiler_params=_cparams(("arbitrary", "arbitrary")),
    )(a, g, gt, khat)


def _fft3_kernel(f_ref, bm_ref, y_ref, gate_ref, bias_ref, o_ref):
    conv = jnp.dot(f_ref[...], bm_ref[...], preferred_element_type=F32)
    o_ref[...] = gate_ref[...] * (conv + y_ref[...] * bias_ref[...])


def _fft_stage3(f3, bm, y, gate, bias_row):
    nb, m, cols = bm.shape
    r = f3.shape[0]
    tn = _pick(cols, (8192, 4096, 2048))
    return pl.pallas_call(
        _fft3_kernel,
        grid=(nb, cols // tn),
        in_specs=[pl.BlockSpec((r, m), lambda b, j: (0, 0)),
                  pl.BlockSpec((None, m, tn), lambda b, j: (b, 0, j)),
                  pl.BlockSpec((None, r, tn), lambda b, j: (b, 0, j)),
                  pl.BlockSpec((None, r, tn), lambda b, j: (b, 0, j)),
                  pl.BlockSpec((1, tn), lambda b, j: (0, j))],
        out_specs=pl.BlockSpec((None, r, tn), lambda b, j: (b, 0, j)),
        out_shape=jax.ShapeDtypeStruct((nb, r, cols), F32),
        compiler_params=_cparams(("arbitrary", "arbitrary")),
    )(f3, bm, y, gate, bias_row)


def _fft1_tok_kernel(f_ref, x_ref, o_ref, s_scr, *, rows):
    n1 = f_ref.shape[0]
    f = f_ref[...]
    pitch = n1 + FFT_PAD

    def body(n2, carry):
        xs = x_ref[pl.ds(n2, rows, stride=FFT_N2), :].astype(BF16)
        s_scr[pl.ds(pl.multiple_of(n2 * pitch, 8), n1), :] = jnp.dot(f, xs, preferred_element_type=F32)
        return carry

    lax.fori_loop(0, FFT_N2, body, 0, unroll=8)

    def emit(k1, carry):
        o_ref[k1] = s_scr[pl.ds(k1, FFT_N2, stride=pitch), :].astype(o_ref.dtype)
        return carry

    lax.fori_loop(0, n1, emit, 0, unroll=4)


def _fft_stage1_tok(f1, x):
    nb, n, c = x.shape
    rows = n // FFT_N2
    n1 = f1.shape[0] // 2
    return pl.pallas_call(
        functools.partial(_fft1_tok_kernel, rows=rows),
        grid=(nb, c // LANE, 2),
        in_specs=[pl.BlockSpec((None, n1, rows), lambda b, j, r: (r, 0, 0)),
                  pl.BlockSpec((None, n, LANE), lambda b, j, r: (b, 0, j))],
        out_specs=pl.BlockSpec((None, None, n1, FFT_N2, LANE), lambda b, j, r: (b, r, 0, 0, j)),
        out_shape=jax.ShapeDtypeStruct((nb, 2, n1, FFT_N2, c), BF16),
        scratch_shapes=[pltpu.VMEM((FFT_N2 * (n1 + FFT_PAD), LANE), F32)],
        compiler_params=_cparams(("arbitrary", "arbitrary", "arbitrary")),
    )(f1.reshape(2, n1, rows), x)


def _fft3_tok_kernel(f_ref, bm_ref, y_ref, gate_ref, bias_ref, o_ref, s_scr, t_scr, *, rows):
    half = pl.program_id(2)
    n1 = bm_ref.shape[0]
    sp, tp = FFT_N2 + FFT_PAD, rows + FFT_PAD

    def stage(k1, carry):
        s_scr[pl.ds(pl.multiple_of(k1 * sp, 8), FFT_N2), :] = bm_ref[k1].astype(F32)
        return carry

    lax.fori_loop(0, n1, stage, 0, unroll=4)
    f = f_ref[...]

    def part(n2):
        return jnp.dot(f, s_scr[pl.ds(n2, n1, stride=sp), :].astype(BF16), preferred_element_type=F32)

    def dst(n2):
        return pl.ds(pl.multiple_of(n2 * tp, 8), rows)

    @pl.when(half == 0)
    def _():
        def body(n2, carry):
            t_scr[dst(n2), :] = part(n2)
            return carry
        lax.fori_loop(0, FFT_N2, body, 0, unroll=8)

    @pl.when(half == 1)
    def _():
        def body(n2, carry):
            t_scr[dst(n2), :] += part(n2)
            return carry
        lax.fori_loop(0, FFT_N2, body, 0, unroll=8)

        def emit(r, carry):
            tok = pl.ds(pl.multiple_of(r * FFT_N2, FFT_N2), FFT_N2)
            conv = t_scr[pl.ds(r, FFT_N2, stride=tp), :]
            o_ref[tok, :] = gate_ref[tok, :] * (conv + y_ref[tok, :] * bias_ref[...])
            return carry
        lax.fori_loop(0, rows, emit, 0, unroll=2)


def _fft_stage3_tok(f3, bm, y, gate, bias):
    nb, _, n1, _, c = bm.shape
    n = y.shape[1]
    rows = n // FFT_N2
    tok = pl.BlockSpec((None, n, LANE), lambda b, j, r: (b, 0, j))
    return pl.pallas_call(
        functools.partial(_fft3_tok_kernel, rows=rows),
        grid=(nb, c // LANE, 2),
        in_specs=[pl.BlockSpec((None, rows, n1), lambda b, j, r: (r, 0, 0)),
                  pl.BlockSpec((None, None, n1, FFT_N2, LANE), lambda b, j, r: (b, r, 0, 0, j)),
                  tok, tok,
                  pl.BlockSpec((1, LANE), lambda b, j, r: (0, j))],
        out_specs=tok,
        out_shape=jax.ShapeDtypeStruct((nb, n, c), F32),
        scratch_shapes=[pltpu.VMEM((n1 * (FFT_N2 + FFT_PAD), LANE), F32),
                        pltpu.VMEM((FFT_N2 * (rows + FFT_PAD), LANE), F32)],
        compiler_params=_cparams(("arbitrary", "arbitrary", "arbitrary")),
    )(f3.reshape(rows, 2, n1).transpose(1, 0, 2), bm, y, gate, bias.astype(F32).reshape(1, c))


def _hyena_sequence(v, x1, x2, hy, bias):
    B, L, C = v.shape
    r_valid = L // FFT_N2
    r_in = max(r_valid, 16)
    n1 = max(2 * r_valid, r_in)
    n = n1 * FFT_N2
    f1, f3, g, gt = _dft_tables(n1, r_in)
    f1k = _dft_tables(n1, n1)[0]
    k, inv = _hyena_filters(L, n, *hy)
    if r_valid == r_in:
        khat = _fft_filter_spectrum(_fft_stage1_tok(f1k, k), g, inv)
        y = v
        for o, gate in enumerate((x1, x2)):
            bm = _fft_mid(_fft_stage1_tok(f1, y), g, gt, khat[o])
            y = _fft_stage3_tok(f3, bm, y, gate, bias[o])
        return y
    ak = _fft_stage1(f1k, k.reshape(HY_ORDER, n1, FFT_N2 * C))
    khat = _fft_filter_spectrum(ak.reshape(HY_ORDER, 2, n1, FFT_N2, C), g, inv)
    cols = FFT_N2 * C

    def view(a):
        a = a.reshape(B, r_valid, cols)
        return a if r_in == r_valid else jnp.pad(a, ((0, 0), (0, r_in - r_valid), (0, 0)))

    y = view(v)
    for o, gate in enumerate((view(x1), view(x2))):
        a = _fft_stage1(f1, y)
        bm = _fft_mid(a.reshape(B, 2, n1, FFT_N2, C), g, gt, khat[o])
        y = _fft_stage3(f3, bm.reshape(B, 2 * n1, cols), y, gate, jnp.tile(bias[o].astype(F32), FFT_N2)[None, :])
    return y[:, :r_valid].reshape(B, L, C)


def _s5_tables(a_re, a_im, log_dt, b_re, b_im, c_re, c_im, jj_ctx, jj_lat):
    lam = lax.complex(jnp.minimum(a_re.astype(F32), -1e-4), a_im.astype(F32))
    dt = jnp.exp(log_dt.astype(F32))[..., None]
    abar = jnp.exp(lam * dt)
    bbar = ((abar - 1.0) / lam)[..., None] * lax.complex(b_re.astype(F32), b_im.astype(F32))
    cmat = lax.complex(c_re.astype(F32), c_im.astype(F32))
    T = S5_T

    def powers(m):
        m = jnp.asarray(m, F32)
        return jnp.exp(lam * dt * m.reshape(m.shape + (1, 1, 1)))

    pw = powers(jnp.arange(T + 1))
    kt = jnp.real(jnp.einsum('dgop,tdgp,dgpi->tdgoi', cmat, pw[:T], bbar, precision=HI))
    tt = jnp.arange(T)
    lag = tt[None, :] - tt[:, None]
    w_intra = jnp.where((lag >= 0)[:, :, None, None, None, None],
                        kt[jnp.clip(lag, 0, T - 1)], 0.0)
    w_intra = jnp.stack([w_intra[:, :, 0], w_intra[::-1, ::-1, 1]], axis=2)
    w_intra = w_intra.transpose(2, 3, 0, 5, 1, 4).reshape(2, S5_NG, T * S5_GROUP, T * S5_GROUP)
    wb = pw[T - 1 - tt][..., None] * bbar[None]
    wb = jnp.stack([wb[:, 0], wb[::-1, 1]], axis=1)
    wb = wb.transpose(1, 2, 0, 4, 3).reshape(2, S5_NG, T * S5_GROUP, S5_P)
    w_cat = jnp.concatenate([w_intra, jnp.real(wb), jnp.imag(wb)], axis=-1)
    cp = cmat[None] * pw[1:, :, :, None, :]
    cp = jnp.stack([cp[:, 0], cp[::-1, 1]], axis=1)
    cp = cp.transpose(1, 2, 4, 0, 3).reshape(2, S5_NG, S5_P, T * S5_GROUP)
    c_cat = jnp.concatenate([jnp.real(cp), -jnp.imag(cp)], axis=2)

    def coef(z):
        zr, zi = jnp.real(z), jnp.imag(z)
        return jnp.stack([jnp.concatenate([zr, zr], -1), jnp.concatenate([-zi, zi], -1)], axis=-2)

    step = coef(powers(jnp.array(T)))
    seg_c, seg_l = (coef(powers(jnp.array(T * n))) for n in (jj_ctx, jj_lat))
    coefs = jnp.pad(jnp.concatenate([step, seg_c, seg_l], axis=2), ((0, 0), (0, 0), (0, 2), (0, 0)))
    ptab = coef(powers(T * jnp.arange(max(jj_ctx, jj_lat)))).transpose(1, 2, 0, 3, 4)
    return w_cat.astype(BF16), c_cat.astype(BF16), coefs, ptab


def _cmul(coef_a, coef_b, s):
    return coef_a * s + coef_b * pltpu.roll(s, S5_P, axis=1)


S5_GPB = LANE // S5_GROUP
S5_TC = S5_T * S5_GROUP


def _s5_in_kernel(z_ref, w_ref, yi_ref, ds_ref):
    nj = z_ref.shape[0] // S5_T
    ws = [z_ref[pl.ds(t, nj, stride=S5_T), :].T for t in range(S5_T)]
    for g in range(S5_GPB):
        vt = jnp.concatenate([w[S5_GROUP * g:S5_GROUP * (g + 1), :] for w in ws], axis=0)
        v = vt.T.astype(BF16)
        for d in range(2):
            o = jnp.dot(v, w_ref[d, g], preferred_element_type=F32)
            yi_ref[d, g] = o[:, :S5_TC]
            ds_ref[d, g] = o[:, S5_TC:]


def _s5_in(z, w_cat, nj):
    T = z.shape[0]
    R = T // S5_T
    col0 = HY_COLS // LANE
    return pl.pallas_call(
        _s5_in_kernel,
        grid=(R // nj, S5_CH // LANE),
        in_specs=[pl.BlockSpec((nj * S5_T, LANE), lambda i, c: (i, col0 + c)),
                  pl.BlockSpec((2, S5_GPB, S5_TC, S5_TC + 2 * S5_P), lambda i, c: (0, c, 0, 0))],
        out_specs=[pl.BlockSpec((2, S5_GPB, nj, S5_TC), lambda i, c: (0, c, i, 0)),
                   pl.BlockSpec((2, S5_GPB, nj, 2 * S5_P), lambda i, c: (0, c, i, 0))],
        out_shape=[jax.ShapeDtypeStruct((2, S5_NG, R, S5_TC), F32),
                   jax.ShapeDtypeStruct((2, S5_NG, R, 2 * S5_P), F32)],
        compiler_params=_cparams(("arbitrary", "arbitrary")),
    )(z, w_cat)


def _s5_scan_kernel(ds_ref, yi_ref, c_ref, cf_ref, p_ref, y_ref, sp_scr, *, parts, reverse):
    a1, a2 = cf_ref[0:1, :], cf_ref[1:2, :]
    rid = lax.broadcasted_iota(jnp.int32, (S5_SEG, 2 * S5_P), 0)
    first, last = (S5_SEG - 1, 0) if reverse else (0, S5_SEG - 1)
    shift = S5_SEG - 1 if reverse else 1
    nb = len(parts[0][0])
    zero = jnp.zeros((S5_SEG, 2 * S5_P), F32)
    s0 = [zero] * nb
    for pi, (bases, jj) in enumerate(parts):
        g1, g2 = cf_ref[2 + 2 * pi:3 + 2 * pi, :], cf_ref[3 + 2 * pi:4 + 2 * pi, :]

        def rows(b, k, bases=bases, jj=jj):
            return pl.ds(bases[b] + (jj - 1 - k if reverse else k), S5_SEG, stride=jj)

        def local_step(k, states, rows=rows):
            new = []
            for b in range(nb):
                sp_scr[rows(b, k), :] = states[b]
                new.append(_cmul(a1, a2, states[b]) + ds_ref[rows(b, k), :])
            return tuple(new)

        ends = lax.fori_loop(0, jj, local_step, (zero,) * nb)
        carries, nxt_s0 = [], []
        for b in range(nb):
            c = jnp.where(rid == first, s0[b], 0.0)
            for _ in range(S5_SEG - 1):
                c = jnp.where(rid == first, s0[b], pltpu.roll(ends[b] + _cmul(g1, g2, c), shift, axis=0))
            fin = ends[b] + _cmul(g1, g2, c)
            nxt_s0.append(jnp.broadcast_to(fin[last:last + 1, :], fin.shape))
            carries.append((c, pltpu.roll(c, S5_P, axis=1)))

        def fix_step(k, carry, rows=rows, carries=carries):
            p = p_ref[k]
            for b in range(nb):
                c, cs = carries[b]
                sp_scr[rows(b, k), :] += p[0:1, :] * c + p[1:2, :] * cs
            return carry

        lax.fori_loop(0, jj, fix_step, 0)
        s0 = nxt_s0
    y_ref[...] = yi_ref[...] + jnp.dot(sp_scr[...].astype(BF16), c_ref[...], preferred_element_type=F32)


def _s5_scan(d, ds, yi, c_cat, coefs, ptab, parts):
    R = ds.shape[2]
    jjm = ptab.shape[2]
    kern = functools.partial(_s5_scan_kernel, parts=parts, reverse=(d == 1))
    return pl.pallas_call(
        kern,
        grid=(S5_NG,),
        in_specs=[pl.BlockSpec((None, None, R, 2 * S5_P), lambda g: (d, g, 0, 0)),
                  pl.BlockSpec((None, None, R, S5_TC), lambda g: (d, g, 0, 0)),
                  pl.BlockSpec((None, None, 2 * S5_P, S5_TC), lambda g: (d, g, 0, 0)),
                  pl.BlockSpec((None, None, 8, 2 * S5_P), lambda g: (d, g, 0, 0)),
                  pl.BlockSpec((None, None, jjm, 2, 2 * S5_P), lambda g: (d, g, 0, 0, 0))],
        out_specs=pl.BlockSpec((None, R, S5_TC), lambda g: (g, 0, 0)),
        out_shape=jax.ShapeDtypeStruct((S5_NG, R, S5_TC), F32),
        scratch_shapes=[pltpu.VMEM((R, 2 * S5_P), F32)],
        compiler_params=_cparams(("arbitrary",)),
    )(ds, yi, c_cat, coefs, ptab)


def _s5_out_kernel(yf_ref, yb_ref, o_ref):
    nj = yf_ref.shape[1]
    yts = [(yf_ref[g] + yb_ref[g]).T for g in range(S5_GPB)]
    for t in range(S5_T):
        zt = jnp.concatenate([y[S5_GROUP * t:S5_GROUP * (t + 1), :] for y in yts], axis=0)
        o_ref[pl.ds(t, nj, stride=S5_T), :] = zt.T


def _s5_out(yf, yb, nj):
    R = yf.shape[1]
    spec = pl.BlockSpec((S5_GPB, nj, S5_TC), lambda i, c: (c, i, 0))
    return pl.pallas_call(
        _s5_out_kernel,
        grid=(R // nj, S5_CH // LANE),
        in_specs=[spec, spec],
        out_specs=pl.BlockSpec((nj * S5_T, LANE), lambda i, c: (i, c)),
        out_shape=jax.ShapeDtypeStruct((R * S5_T, S5_CH), F32),
        compiler_params=_cparams(("arbitrary", "arbitrary")),
    )(yf, yb)


def _s5_mixer(tok, z, s5):
    B = tok.B
    cl, cc = tok.L // S5_T, tok.CTX // S5_T
    jl, jc = cl // S5_SEG, cc // S5_SEG
    w_cat, c_cat, coefs, ptab = _s5_tables(*s5, jc, jl)
    yi, ds = _s5_in(z, w_cat, math.gcd(B * cl, B * cc, 64))
    lat = (tuple(b * cl for b in range(B)), jl)
    ctx = (tuple(B * cl + b * cc for b in range(B)), jc)
    yf = _s5_scan(0, ds, yi, c_cat, coefs, ptab, (ctx, lat))
    yb = _s5_scan(1, ds, yi, c_cat, coefs, ptab, (ctx, lat))
    return _s5_out(yf, yb, math.gcd(B * cl, B * cc, 64))


def _gelu_tanh(x):
    return 0.5 * x * (1.0 + jnp.tanh(math.sqrt(2.0 / math.pi) * (x + 0.044715 * (x * x * x))))


def _even_out_kernel(hl_ref, ys_ref, u_ref, h_ref, mod_ref, dsk_ref, wg_ref, wo_ref, o_ref):
    y = _gelu_tanh(ys_ref[...] + dsk_ref[...] * u_ref[...])
    s = y * jax.nn.sigmoid(jnp.dot(y.astype(BF16), wg_ref[...], preferred_element_type=F32))
    ol = (jnp.dot(hl_ref[...].astype(BF16), wo_ref[:HY_CH, :], preferred_element_type=F32)
          + jnp.dot(s.astype(BF16), wo_ref[HY_CH:, :], preferred_element_type=F32))
    o_ref[...] = h_ref[...] + mod_ref[2:3, :] * ol


def _even_out(tok, hl, ys, z, h, mods_l, dsk, w_glu, w_out):
    tm = tok.tm
    return pl.pallas_call(
        _even_out_kernel,
        grid=(tok.n_all,),
        in_specs=[pl.BlockSpec((tm, HY_CH), lambda i: (i, 0)),
                  pl.BlockSpec((tm, S5_CH), lambda i: (i, 0)),
                  pl.BlockSpec((tm, S5_CH), lambda i: (i, HY_COLS // S5_CH)),
                  pl.BlockSpec((tm, D), lambda i: (i, 0)),
                  pl.BlockSpec((None, N_MOD, D), lambda i: (tok.mod_row(i), 0, 0)),
                  pl.BlockSpec((1, S5_CH), lambda i: (0, 0)),
                  pl.BlockSpec((S5_CH, S5_CH), lambda i: (0, 0)),
                  pl.BlockSpec((D, D), lambda i: (0, 0))],
        out_specs=pl.BlockSpec((tm, D), lambda i: (i, 0)),
        out_shape=jax.ShapeDtypeStruct((tok.T, D), F32),
        compiler_params=_cparams(("arbitrary",)),
    )(hl, ys, z, h, mods_l, dsk.reshape(1, S5_CH), w_glu, w_out)


def _ffn_kernel(h_ref, mod_ref, g_ref, wg_ref, wu_ref, wd_ref, o_ref, y_scr, acc_scr):
    j = pl.program_id(1)

    @pl.when(j == 0)
    def _():
        y_scr[...] = _ada_norm(h_ref[...], g_ref[...], mod_ref[...], 3, 4).astype(BF16)
        acc_scr[...] = jnp.zeros_like(acc_scr)

    y = y_scr[...]
    gate = jnp.dot(y, wg_ref[...], preferred_element_type=F32)
    up = jnp.dot(y, wu_ref[...], preferred_element_type=F32)
    act = (gate * jax.nn.sigmoid(gate) * up).astype(BF16)
    acc_scr[...] += jnp.dot(act, wd_ref[...], preferred_element_type=F32)

    @pl.when(j == pl.num_programs(1) - 1)
    def _():
        o_ref[...] = h_ref[...] + mod_ref[5:6, :] * acc_scr[...]


def _ffn(tok, h, mods_l, gain, wg, wu, wd):
    tm = tok.tm
    ff = wg.shape[1]
    tf = _pick(ff, (1408, 512, 256, 128))
    return pl.pallas_call(
        _ffn_kernel,
        grid=(tok.n_all, ff // tf),
        in_specs=[pl.BlockSpec((tm, D), lambda i, j: (i, 0)),
                  pl.BlockSpec((None, N_MOD, D), lambda i, j: (tok.mod_row(i), 0, 0)),
                  pl.BlockSpec((1, D), lambda i, j: (0, 0)),
                  pl.BlockSpec((D, tf), lambda i, j: (0, j)),
                  pl.BlockSpec((D, tf), lambda i, j: (0, j)),
                  pl.BlockSpec((tf, D), lambda i, j: (j, 0))],
        out_specs=pl.BlockSpec((tm, D), lambda i, j: (i, 0)),
        out_shape=jax.ShapeDtypeStruct((tok.T, D), F32),
        scratch_shapes=[pltpu.VMEM((tm, D), BF16), pltpu.VMEM((tm, D), F32)],
        compiler_params=_cparams(("arbitrary", "arbitrary")),
    )(h, mods_l, gain.reshape(1, D), wg, wu, wd)


def _even_layer(tok, h, mods_l, p):
    B, L, CTX = tok.B, tok.L, tok.CTX
    nl = B * L
    z = _norm_matmul(tok, h, mods_l, p['norm_mix'], p['w_in'].astype(BF16), EV_IN)
    v, x1, x2 = _short_conv(tok, z, p['conv_w'], p['conv_b'])
    lat = lambda a: a[:nl].reshape(B, L, -1)
    ctx = lambda a: a[nl:].reshape(B, CTX, -1)
    hl = _hyena_sequence(lat(v), lat(x1), lat(x2), p['hy'], p['hy_bias'])
    parts = [hl.reshape(nl, HY_CH)]
    ys_all = _s5_mixer(tok, z, p['s5'])
    if p['need_ctx']:
        hc = _hyena_sequence(ctx(v), ctx(x1), ctx(x2), p['hy'], p['hy_bias'])
        parts.append(hc.reshape(B * CTX, HY_CH))
    else:
        parts.append(jnp.zeros((B * CTX, HY_CH), F32))
    hy_all = jnp.concatenate(parts, axis=0)
    h = _even_out(tok, hy_all, ys_all, z, h, mods_l, p['s5_d'], p['s5_w_glu'].astype(BF16), p['w_out'].astype(BF16))
    return _ffn(tok, h, mods_l, p['norm_ffn'], p['ff_wg'].astype(BF16), p['ff_wu'].astype(BF16),
                p['ff_wd'].astype(BF16))


def _rope_tables(L, tm):
    t = jnp.arange(L)
    row = (t // GRID_W).astype(F32)[:, None]
    col = (t % GRID_W).astype(F32)[:, None]

    def pattern(dim):
        nf = dim // 4
        inv = ROPE_BASE ** (-jnp.arange(nf, dtype=F32) / nf)
        ar, ac = row * inv[None, :], col * inv[None, :]
        cos = jnp.concatenate([jnp.cos(ar)] * 2 + [jnp.cos(ac)] * 2, axis=1)
        z = jnp.zeros((L, nf), F32)
        s_up = jnp.concatenate([-jnp.sin(ar), z, -jnp.sin(ac), z], axis=1)
        s_dn = jnp.concatenate([z, jnp.sin(ar), z, jnp.sin(ac)], axis=1)
        return cos, s_up, s_dn

    def pad_mla(a, fill):
        return jnp.concatenate([jnp.full((L, MLA_NOPE), fill, F32), a,
                                jnp.full((L, HEAD_PAD - MLA_NOPE - MLA_ROPE), fill, F32)], axis=1)

    cm, um, dm = pattern(MLA_ROPE)
    cg, ug, dg = pattern(GQA_HD)
    mla = jnp.stack([pad_mla(cm, 1.0), pad_mla(um, 0.0), pad_mla(dm, 0.0)])
    gqa = jnp.stack([jnp.tile(cg, (1, 2)), jnp.tile(ug, (1, 2)), jnp.tile(dg, (1, 2))])
    ident = jnp.stack([jnp.ones((tm, LANE), F32), jnp.zeros((tm, LANE), F32), jnp.zeros((tm, LANE), F32)])
    return jnp.stack([jnp.concatenate([mla, ident], axis=1), jnp.concatenate([gqa, ident], axis=1)])


def _rope(x, tab, w):
    outs = []
    for h in range(x.shape[1] // LANE):
        xs = x[:, h * LANE:(h + 1) * LANE]
        outs.append(xs * tab[0] + pltpu.roll(xs, LANE - w, axis=1) * tab[1] + pltpu.roll(xs, w, axis=1) * tab[2])
    return outs[0] if len(outs) == 1 else jnp.concatenate(outs, axis=1)


def _rms(x, g):
    return x * lax.rsqrt(jnp.mean(x * x, axis=-1, keepdims=True) + EPS) * g


_O_CQ, _O_CKV, _O_GQ, _O_GK, _O_GV, _O_KR = 0, 256, 384, 896, 1024, 1152


def _odd_proj_kernel(z_ref, tab_ref, qn_ref, kvn_ref, wuq_ref, wuk_ref, wuv_ref, e_ref,
                     q_ref, k_ref, v_ref, gq_ref, gk_ref, gv_ref):
    z = z_ref[...]
    mt, gt = tab_ref[0], tab_ref[1]
    qn = _rms(z[:, _O_CQ:_O_CKV], qn_ref[...]).astype(BF16)
    q = jnp.dot(qn, wuq_ref[...], preferred_element_type=F32)
    q_ref[...] = (_rope(q, mt, MLA_ROPE // 4) * (MLA_SCALE * LOG2E)).astype(BF16)
    kvn = _rms(z[:, _O_CKV:_O_GQ], kvn_ref[...]).astype(BF16)
    k = (jnp.dot(kvn, wuk_ref[...], preferred_element_type=F32)
         + jnp.dot(z[:, _O_KR:], e_ref[...], precision=HI, preferred_element_type=F32))
    k_ref[...] = _rope(k, mt, MLA_ROPE // 4).astype(BF16)
    v_ref[...] = jnp.dot(kvn, wuv_ref[...], preferred_element_type=F32).astype(BF16)
    gq_ref[...] = (_rope(z[:, _O_GQ:_O_GK], gt, GQA_HD // 4) * (GQA_SCALE * LOG2E)).astype(BF16)
    gk_ref[...] = _rope(z[:, _O_GK:_O_GV], gt, GQA_HD // 4).astype(BF16)
    gv_ref[...] = z[:, _O_GV:_O_KR].astype(BF16)


def _odd_proj(tok, z, tabs, q_norm, kv_norm, w_uq, w_ukv):
    tm = tok.tm
    hq = MLA_HEADS * HEAD_PAD
    wq = jnp.pad(w_uq.reshape(Q_LORA, MLA_HEADS, MLA_NOPE + MLA_ROPE),
                 ((0, 0), (0, 0), (0, HEAD_PAD - MLA_NOPE - MLA_ROPE))).reshape(Q_LORA, hq).astype(BF16)
    wkv = w_ukv.reshape(KV_LORA, MLA_HEADS, MLA_NOPE + MLA_V)
    wk = jnp.pad(wkv[..., :MLA_NOPE], ((0, 0), (0, 0), (0, HEAD_PAD - MLA_NOPE))).reshape(KV_LORA, hq).astype(BF16)
    wv = wkv[..., MLA_NOPE:].reshape(KV_LORA, MLA_HEADS * MLA_V).astype(BF16)
    eye = jnp.eye(MLA_ROPE, dtype=F32)
    e_head = jnp.pad(eye, ((0, LANE - MLA_ROPE), (MLA_NOPE, HEAD_PAD - MLA_NOPE - MLA_ROPE)))
    e = jnp.tile(e_head, (1, MLA_HEADS))
    tab_blk = lambda i: (0, 0, jnp.where(i < tok.n_lat, i % tok.per_seq, tok.per_seq), 0)
    full = lambda shape: pl.BlockSpec(shape, lambda i: (0,) * len(shape))
    widths = (hq, hq, MLA_HEADS * MLA_V, GQA_HEADS * GQA_HD, GQA_KV * GQA_HD, GQA_KV * GQA_HD)
    return pl.pallas_call(
        _odd_proj_kernel,
        grid=(tok.n_all,),
        in_specs=[pl.BlockSpec((tm, OD_IN_PAD), lambda i: (i, 0)),
                  pl.BlockSpec((2, 3, tm, LANE), tab_blk),
                  full((1, Q_LORA)), full((1, KV_LORA)), full((Q_LORA, hq)), full((KV_LORA, hq)),
                  full((KV_LORA, MLA_HEADS * MLA_V)), full((LANE, hq))],
        out_specs=[pl.BlockSpec((tm, w), lambda i: (i, 0)) for w in widths],
        out_shape=[jax.ShapeDtypeStruct((tok.T, w), BF16) for w in widths],
        compiler_params=_cparams(("arbitrary",)),
    )(z, tabs, q_norm.reshape(1, -1), kv_norm.reshape(1, -1), wq, wk, wv, e)


def _mla_attn_kernel(q_ref, k_ref, vt_ref, o_ref, s_scr, p_scr, *, tk, nk, ks):
    tq = q_ref.shape[0]
    nslab = tk // ks
    kq, kv = math.gcd(tk, MLA_KS_QK), math.gcd(tk, MLA_KS_PV)
    dn = (((1,), (1,)), ((), ()))
    qs = [q_ref[:, h * HEAD_PAD:(h + 1) * HEAD_PAD] for h in range(2)]

    def qk_slab(h, c, j, mx):
        if (j * ks) % kq:
            return mx
        r = pl.multiple_of(c * tk + j * ks, ks)
        s = lax.dot_general(k_ref[pl.ds(r, kq), h * HEAD_PAD:(h + 1) * HEAD_PAD], qs[h], dn,
                            preferred_element_type=F32)
        s_scr[h, j * ks:j * ks + kq, :] = s
        return jnp.maximum(mx, jnp.max(s, axis=0, keepdims=True))

    def pv_slab(h, c, j):
        if (j * ks) % kv:
            return 0.0
        return jnp.dot(vt_ref[c, h * MLA_VP:(h + 1) * MLA_VP, j * ks:j * ks + kv], p_scr[h, j * ks:j * ks + kv, :],
                       preferred_element_type=F32)

    def step(x, c_sm, c_pv, c_qk, st):
        y = 1 - x
        m, acc, mc = st[x]
        m_new = jnp.maximum(m, mc)
        alpha = jnp.exp2(m - m_new)
        acc_y = st[y][1]
        mx_y = jnp.full((1, tq), NEG, F32)
        for j in range(nslab):
            acc_y = acc_y + pv_slab(y, c_pv, j)
            mx_y = qk_slab(y, c_qk, j, mx_y)
            p_scr[x, j * ks:(j + 1) * ks, :] = jnp.exp2(s_scr[x, j * ks:(j + 1) * ks, :] - m_new).astype(BF16)
        new = [None, None]
        new[x] = (m_new, alpha * acc, mc)
        new[y] = (st[y][0], acc_y, mx_y)
        return tuple(new)

    def body(c, st):
        st = step(0, c, jnp.maximum(c - 1, 0), c, st)
        return step(1, c, c, jnp.minimum(c + 1, nk - 1), st)

    neg = jnp.full((1, tq), NEG, F32)
    acc0 = jnp.zeros((MLA_VP, tq), F32)
    p_scr[1] = jnp.zeros(p_scr.shape[1:], BF16)
    mx0 = neg
    for j in range(nslab):
        mx0 = qk_slab(0, 0, j, mx0)
    def trip(i, st):
        for u in range(MLA_UNROLL):
            st = body(i * MLA_UNROLL + u, st)
        return st

    st = lax.fori_loop(0, nk // MLA_UNROLL, trip, ((neg, acc0, mx0), (neg, acc0, neg)))
    for c in range(nk - nk % MLA_UNROLL, nk):
        st = body(jnp.int32(c), st)
    acc1 = st[1][1]
    for j in range(nslab):
        acc1 = acc1 + pv_slab(1, nk - 1, j)
    out_t = jnp.concatenate([a[:MLA_V] / a[MLA_V:MLA_V + 1] for a in (st[0][1], acc1)], axis=0)
    o_ref[...] = out_t.T.astype(o_ref.dtype)


MLA_TQ = (256, 128)
MLA_TK = (768, 512, 256, 128)
MLA_KS = 128
MLA_KS_QK = 384
MLA_KS_PV = 256
MLA_UNROLL = 5


def _mla_attention(q, k, v):
    B, Lq, _ = q.shape
    Nk = k.shape[1]
    tq = _pick(Lq, MLA_TQ)
    tk = _pick(Nk, MLA_TK)
    nk = Nk // tk
    hp = MLA_HEADS // 2
    vt = v.reshape(B, nk, tk, hp, 2, MLA_V).transpose(0, 3, 1, 4, 5, 2)
    vt = jnp.concatenate([vt, jnp.ones((B, hp, nk, 2, MLA_VP - MLA_V, tk), v.dtype)], axis=4)
    vt = vt.reshape(B, hp, nk, 2 * MLA_VP, tk)
    kern = functools.partial(_mla_attn_kernel, tk=tk, nk=nk, ks=math.gcd(tk, MLA_KS))
    return pl.pallas_call(
        kern,
        grid=(B, hp, Lq // tq),
        in_specs=[pl.BlockSpec((None, tq, 2 * HEAD_PAD), lambda b, h, i: (b, i, h)),
                  pl.BlockSpec((None, Nk, 2 * HEAD_PAD), lambda b, h, i: (b, 0, h)),
                  pl.BlockSpec((None, None, nk, 2 * MLA_VP, tk), lambda b, h, i: (b, h, 0, 0, 0))],
        out_specs=pl.BlockSpec((None, tq, 2 * MLA_V), lambda b, h, i: (b, i, h)),
        out_shape=jax.ShapeDtypeStruct((B, Lq, MLA_HEADS * MLA_V), BF16),
        scratch_shapes=[pltpu.VMEM((2, tk, tq), F32), pltpu.VMEM((2, tk, tq), BF16)],
        compiler_params=_cparams(("arbitrary", "arbitrary", "arbitrary")),
    )(q, k, vt)


def _gqa_kernel(sink_ref, q_ref, kc_ref, vct_ref, *rest, L, has_band):
    group = GQA_HEADS // GQA_KV
    gw = group * BLK
    if has_band:
        kp_ref, k_ref, kn_ref, vtp_ref, vt_ref, vtn_ref, o_ref = rest
        qi = pl.program_id(1)
        keys = jnp.concatenate([kp_ref[...], k_ref[...], kn_ref[...], kc_ref[...]], axis=0)
        vals_t = jnp.concatenate([vtp_ref[...], vt_ref[...], vtn_ref[...], vct_ref[...]], axis=1)
        nkey = keys.shape[0]
        kpos = (qi - 1) * BLK + lax.broadcasted_iota(jnp.int32, (nkey, gw), 0)
        qpos = qi * BLK + (lax.broadcasted_iota(jnp.int32, (nkey, gw), 1) & (BLK - 1))
        is_ctx = lax.broadcasted_iota(jnp.int32, (nkey, gw), 0) >= 3 * BLK
        valid = is_ctx | ((jnp.abs(qpos - kpos) <= WINDOW) & (kpos >= 0) & (kpos < L))
    else:
        (o_ref,) = rest
        keys, vals_t = kc_ref[...], vct_ref[...]
    dn = (((1,), (1,)), ((), ()))
    G = range(GQA_KV)
    qs = [jnp.concatenate([q_ref[:, (kh * group + g) * GQA_HD:(kh * group + g + 1) * GQA_HD] for g in range(group)],
                          axis=0) for kh in G]
    s = [lax.dot_general(keys[:, kh * GQA_HD:(kh + 1) * GQA_HD], qs[kh], dn, preferred_element_type=F32) for kh in G]
    if has_band:
        s = [jnp.where(valid, x, NEG) for x in s]
    sink = [sink_ref[:, kh * gw:(kh + 1) * gw] for kh in G]
    m = [jnp.maximum(jnp.max(s[kh], axis=0, keepdims=True), sink[kh]) for kh in G]
    p = [jnp.exp2(s[kh] - m[kh]) for kh in G]
    den = [jnp.sum(p[kh], axis=0, keepdims=True) + jnp.exp2(sink[kh] - m[kh]) for kh in G]
    ot = [jnp.dot(vals_t[kh * GQA_HD:(kh + 1) * GQA_HD, :], p[kh].astype(BF16), preferred_element_type=F32)
          * (1.0 / den[kh]) for kh in G]
    o = jnp.concatenate(ot, axis=0).T
    o_ref[...] = jnp.concatenate([o[g * BLK:(g + 1) * BLK, kh * GQA_HD:(kh + 1) * GQA_HD]
                                  for kh in G for g in range(group)], axis=1).astype(o_ref.dtype)


def _gqa_attention(sink, q, kc, vc, k=None, v=None):
    B, Lq, _ = q.shape
    CTX = kc.shape[1]
    has_band = k is not None
    kw = GQA_KV * GQA_HD
    nb = Lq // BLK
    sink_row = jnp.repeat(sink.astype(F32) * LOG2E, BLK)[None, :]
    in_specs = [pl.BlockSpec((1, GQA_HEADS * BLK), lambda b, i: (0, 0)),
                pl.BlockSpec((None, BLK, GQA_HEADS * GQA_HD), lambda b, i: (b, i, 0)),
                pl.BlockSpec((None, CTX, kw), lambda b, i: (b, 0, 0)),
                pl.BlockSpec((None, kw, CTX), lambda b, i: (b, 0, 0))]
    args = [sink_row, q, kc, jnp.swapaxes(vc, 1, 2)]
    if has_band:
        prev = lambda i: jnp.maximum(i - 1, 0)
        nxt = lambda i: jnp.minimum(i + 1, nb - 1)
        in_specs += [pl.BlockSpec((None, BLK, kw), lambda b, i: (b, prev(i), 0)),
                     pl.BlockSpec((None, BLK, kw), lambda b, i: (b, i, 0)),
                     pl.BlockSpec((None, BLK, kw), lambda b, i: (b, nxt(i), 0)),
                     pl.BlockSpec((None, kw, BLK), lambda b, i: (b, 0, prev(i))),
                     pl.BlockSpec((None, kw, BLK), lambda b, i: (b, 0, i)),
                     pl.BlockSpec((None, kw, BLK), lambda b, i: (b, 0, nxt(i)))]
        vt = jnp.swapaxes(v, 1, 2)
        args += [k, k, k, vt, vt, vt]
    kern = functools.partial(_gqa_kernel, L=Lq, has_band=has_band)
    return pl.pallas_call(
        kern,
        grid=(B, nb),
        in_specs=in_specs,
        out_specs=pl.BlockSpec((None, BLK, GQA_HEADS * GQA_HD), lambda b, i: (b, i, 0)),
        out_shape=jax.ShapeDtypeStruct((B, Lq, GQA_HEADS * GQA_HD), BF16),
        compiler_params=_cparams(("arbitrary", "arbitrary")),
    )(*args)


def _odd_out_kernel(a_ref, g_ref, h_ref, mod_ref, wo_ref, o_ref):
    half = a_ref.shape[1]
    ol = (jnp.dot(a_ref[...], wo_ref[:half, :], preferred_element_type=F32)
          + jnp.dot(g_ref[...], wo_ref[half:, :], preferred_element_type=F32))
    o_ref[...] = h_ref[...] + mod_ref[2:3, :] * ol


def _odd_out(tok, mla, gqa, h, mods_l, w_out):
    tm = tok.tm
    half = mla.shape[1]
    return pl.pallas_call(
        _odd_out_kernel,
        grid=(tok.n_all,),
        in_specs=[pl.BlockSpec((tm, half), lambda i: (i, 0)),
                  pl.BlockSpec((tm, half), lambda i: (i, 0)),
                  pl.BlockSpec((tm, D), lambda i: (i, 0)),
                  pl.BlockSpec((None, N_MOD, D), lambda i: (tok.mod_row(i), 0, 0)),
                  pl.BlockSpec((D, D), lambda i: (0, 0))],
        out_specs=pl.BlockSpec((tm, D), lambda i: (i, 0)),
        out_shape=jax.ShapeDtypeStruct((tok.T, D), F32),
        compiler_params=_cparams(("arbitrary",)),
    )(mla, gqa, h, mods_l, w_out)


MOE_TR = 512
MOE_TF = 1792
MOE_IDX_BLK = 1024
MOE_NF = EXP_FF // MOE_TF
ROUTE_W = 8


def _router_kernel(h_ref, mod_ref, g_ref, r_ref, y_ref, route_ref):
    y = _ada_norm(h_ref[...], g_ref[...], mod_ref[...], 3, 4)
    y_ref[...] = y
    logits = jnp.dot(y, r_ref[...], precision=HI, preferred_element_type=F32)
    lane = lax.broadcasted_iota(jnp.int32, logits.shape, 1)
    lg = jnp.where(lane < N_EXP, logits, -jnp.inf)
    m1 = jnp.max(lg, axis=-1, keepdims=True)
    i1 = jnp.min(jnp.where(lg == m1, lane, LANE), axis=-1, keepdims=True)
    lg2 = jnp.where(lane == i1, -jnp.inf, lg)
    m2 = jnp.max(lg2, axis=-1, keepdims=True)
    i2 = jnp.min(jnp.where(lg2 == m2, lane, LANE), axis=-1, keepdims=True)
    e = jnp.exp(m2 - m1)
    w1 = 1.0 / (1.0 + e)
    route = (jnp.where(lane == 0, w1, 0.0) + jnp.where(lane == 1, e * w1, 0.0)
             + jnp.where(lane == 2, i1.astype(F32), 0.0) + jnp.where(lane == 3, i2.astype(F32), 0.0))
    route_ref[...] = route[:, :ROUTE_W]


def _moe_router(tok, h, mods_l, gain, router):
    tm = tok.tm
    rp = jnp.pad(router, ((0, 0), (0, LANE - N_EXP)))
    return pl.pallas_call(
        _router_kernel,
        grid=(tok.n_all,),
        in_specs=[pl.BlockSpec((tm, D), lambda i: (i, 0)),
                  pl.BlockSpec((None, N_MOD, D), lambda i: (tok.mod_row(i), 0, 0)),
                  pl.BlockSpec((1, D), lambda i: (0, 0)),
                  pl.BlockSpec((D, LANE), lambda i: (0, 0))],
        out_specs=[pl.BlockSpec((tm, D), lambda i: (i, 0)), pl.BlockSpec((tm, ROUTE_W), lambda i: (i, 0))],
        out_shape=[jax.ShapeDtypeStruct((tok.T, D), F32), jax.ShapeDtypeStruct((tok.T, ROUTE_W), F32)],
        compiler_params=_cparams(("arbitrary",)),
    )(h, mods_l, gain.reshape(1, D), rp)


def _moe_plan(route, tr):
    T = route.shape[0]
    flat = route[:, 2:4].astype(jnp.int32).reshape(-1)
    onehot = (flat[:, None] == jnp.arange(N_EXP, dtype=jnp.int32)[None, :]).astype(jnp.int32)
    csum = jnp.cumsum(onehot, axis=0)
    rank = jnp.sum((csum - onehot) * onehot, axis=1)
    padded = (csum[-1] + tr - 1) // tr * tr
    ends = jnp.cumsum(padded)
    pos = (ends - padded)[flat] + rank
    tpb = MOE_IDX_BLK // tr
    n_tiles = -(-((2 * T + N_EXP * (tr - 1)) // tr) // tpb) * tpb
    src = jnp.zeros((n_tiles * tr,), jnp.int32).at[pos].set(jnp.arange(2 * T, dtype=jnp.int32) // 2,
                                                            unique_indices=True)
    starts = jnp.arange(n_tiles, dtype=jnp.int32) * tr
    tile_expert = jnp.minimum(jnp.sum((starts[:, None] >= ends[None, :]).astype(jnp.int32), axis=1), N_EXP - 1)
    n_valid = (ends[-1] // tr).astype(jnp.int32).reshape(1)
    return src, tile_expert.astype(jnp.int32), n_valid, pos.reshape(T, 2)


def _gather_rows(idx_ref, src_hbm, dst, sem, n):
    def issue(r, carry):
        pltpu.make_async_copy(src_hbm.at[pl.ds(idx_ref[r], 1), :], dst.at[pl.ds(r, 1), :], sem).start()
        return carry

    lax.fori_loop(0, n, issue, 0, unroll=8)


def _wait_rows(src_hbm, dst, sem, n):
    pltpu.make_async_copy(src_hbm.at[pl.ds(0, n), :], dst, sem).wait()


def _moe_expert_kernel(te_ref, nv_ref, idx_ref, idxn_ref, y_hbm, wg_ref, wu_ref, wd_ref, o_ref,
                       xbuf, y_scr, acc_scr, sem, *, tr):
    t, f = pl.program_id(0), pl.program_id(1)
    nf = pl.num_programs(1)
    nv = nv_ref[0]
    valid = t < nv
    slot = lax.rem(t, 2)
    per_f = tr // MOE_NF
    tpb = MOE_IDX_BLK // tr
    nxt_tile = jnp.minimum(t + 1, jnp.maximum(nv - 1, 0))

    @pl.when((f == 0) & (t == 0))
    def _():
        _gather_rows(idx_ref, y_hbm, xbuf.at[0], sem.at[0], tr)

    @pl.when((f == 0) & valid)
    def _():
        _wait_rows(y_hbm, xbuf.at[slot], sem.at[slot], tr)
        y_scr[...] = xbuf[slot].astype(BF16)
        acc_scr[...] = jnp.zeros_like(acc_scr)

    @pl.when(valid)
    def _():
        base = f * per_f
        ibase = lax.rem(nxt_tile, tpb) * tr + base
        nxt = xbuf.at[1 - slot]
        for r in range(per_f):
            pltpu.make_async_copy(y_hbm.at[pl.ds(idxn_ref[ibase + r], 1), :], nxt.at[pl.ds(base + r, 1), :],
                                  sem.at[1 - slot]).start()
        y = y_scr[...]
        gate = jnp.dot(y, wg_ref[...], preferred_element_type=F32)
        up = jnp.dot(y, wu_ref[...], preferred_element_type=F32)
        act = (gate * jax.nn.sigmoid(gate) * up).astype(BF16)
        acc_scr[...] += jnp.dot(act, wd_ref[...], preferred_element_type=F32)

    @pl.when((f == nf - 1) & (t == nv - 1))
    def _():
        _wait_rows(y_hbm, xbuf.at[1 - slot], sem.at[1 - slot], tr)

    @pl.when(f == nf - 1)
    def _():
        o_ref[...] = jnp.where(valid, acc_scr[...], 0.0)


def _moe_experts(y, src, tile_expert, n_valid, wg, wu, wd, tr):
    n_tiles = src.shape[0] // tr
    tf = MOE_TF
    tpb = MOE_IDX_BLK // tr
    kern = functools.partial(_moe_expert_kernel, tr=tr)
    smem = functools.partial(pl.BlockSpec, memory_space=pltpu.SMEM)
    grid_spec = pltpu.PrefetchScalarGridSpec(
        num_scalar_prefetch=2,
        grid=(n_tiles, EXP_FF // tf),
        in_specs=[smem((MOE_IDX_BLK,), lambda t, f, te, nv: (t // tpb,)),
                  smem((MOE_IDX_BLK,), lambda t, f, te, nv: (jnp.minimum(t + 1, jnp.maximum(nv[0] - 1, 0)) // tpb,)),
                  pl.BlockSpec(memory_space=pl.ANY),
                  pl.BlockSpec((None, D, tf), lambda t, f, te, nv: (te[t], 0, f)),
                  pl.BlockSpec((None, D, tf), lambda t, f, te, nv: (te[t], 0, f)),
                  pl.BlockSpec((None, tf, D), lambda t, f, te, nv: (te[t], f, 0))],
        out_specs=pl.BlockSpec((tr, D), lambda t, f, te, nv: (t, 0)),
        scratch_shapes=[pltpu.VMEM((2, tr, D), F32), pltpu.VMEM((tr, D), BF16), pltpu.VMEM((tr, D), F32),
                        pltpu.SemaphoreType.DMA((2,))])
    return pl.pallas_call(
        kern,
        grid_spec=grid_spec,
        out_shape=jax.ShapeDtypeStruct((n_tiles * tr, D), F32),
        compiler_params=_cparams(("arbitrary", "arbitrary")),
    )(tile_expert, n_valid, src, src, y, wg, wu, wd)


def _moe_combine_kernel(idx_ref, idxn_ref, o_hbm, h_ref, route_ref, mod_ref, out_ref, buf, sem, *, tm):
    i = pl.program_id(0)
    n = pl.num_programs(0)
    slot = lax.rem(i, 2)

    @pl.when(i == 0)
    def _():
        _gather_rows(idx_ref, o_hbm, buf.at[0], sem.at[0], 2 * tm)

    nxt = buf.at[1 - slot]
    for r in range(2 * tm):
        pltpu.make_async_copy(o_hbm.at[pl.ds(idxn_ref[r], 1), :], nxt.at[pl.ds(r, 1), :], sem.at[1 - slot]).start()
    _wait_rows(o_hbm, buf.at[slot], sem.at[slot], 2 * tm)
    r = route_ref[...]
    mix = r[:, 0:1] * buf[slot, :tm, :] + r[:, 1:2] * buf[slot, tm:, :]
    out_ref[...] = h_ref[...] + mod_ref[5:6, :] * mix

    @pl.when(i == n - 1)
    def _():
        _wait_rows(o_hbm, nxt, sem.at[1 - slot], 2 * tm)


def _moe_combine(tok, o_sorted, pos, h, route, mods_l):
    tm = tok.tm
    n = tok.n_all
    idx = pos.reshape(n, tm, 2).transpose(0, 2, 1).reshape(-1)
    kern = functools.partial(_moe_combine_kernel, tm=tm)
    smem = functools.partial(pl.BlockSpec, memory_space=pltpu.SMEM)
    return pl.pallas_call(
        kern,
        grid=(n,),
        in_specs=[smem((2 * tm,), lambda i: (i,)),
                  smem((2 * tm,), lambda i: (jnp.minimum(i + 1, n - 1),)),
                  pl.BlockSpec(memory_space=pl.ANY),
                  pl.BlockSpec((tm, D), lambda i: (i, 0)),
                  pl.BlockSpec((tm, ROUTE_W), lambda i: (i, 0)),
                  pl.BlockSpec((None, N_MOD, D), lambda i: (tok.mod_row(i), 0, 0))],
        out_specs=pl.BlockSpec((tm, D), lambda i: (i, 0)),
        out_shape=jax.ShapeDtypeStruct((tok.T, D), F32),
        scratch_shapes=[pltpu.VMEM((2, 2 * tm, D), F32), pltpu.SemaphoreType.DMA((2,))],
        compiler_params=_cparams(("arbitrary",)),
    )(idx, idx, o_sorted, h, route, mods_l)


def _moe(tok, h, mods_l, gain, router, wg, wu, wd):
    y, route = _moe_router(tok, h, mods_l, gain, router)
    src, tile_expert, n_valid, pos = _moe_plan(route, MOE_TR)
    o_sorted = _moe_experts(y, src, tile_expert, n_valid, wg, wu, wd, MOE_TR)
    return _moe_combine(tok, o_sorted, pos, h, route, mods_l)


def _odd_layer(tok, h, mods_l, tabs, p):
    B, L, CTX = tok.B, tok.L, tok.CTX
    nl = B * L
    w = p['w_in']
    w_in = jnp.concatenate([w[:, :Q_LORA + KV_LORA], w[:, 416:1184], w[:, 384:416],
                            jnp.zeros((D, OD_IN_PAD - 1184), w.dtype)], axis=1).astype(BF16)
    z = _norm_matmul(tok, h, mods_l, p['norm_mix'], w_in, OD_IN_PAD)
    q, k, v, gq, gk, gv = _odd_proj(tok, z, tabs, p['q_norm'], p['kv_norm'], p['w_uq'], p['w_ukv'])
    lat = lambda a: a[:nl].reshape(B, L, -1)
    ctx = lambda a: a[nl:].reshape(B, CTX, -1)
    cat = lambda a: jnp.concatenate([ctx(a), lat(a)], axis=1)
    mla_l = _mla_attention(lat(q), cat(k), cat(v))
    gqa_l = _gqa_attention(p['sink'], lat(gq), ctx(gk), ctx(gv), lat(gk), lat(gv))
    if p['need_ctx']:
        mla_c = _mla_attention(ctx(q), ctx(k), ctx(v))
        gqa_c = _gqa_attention(p['sink'], ctx(gq), ctx(gk), ctx(gv))
    else:
        mla_c = jnp.zeros((B, CTX, MLA_HEADS * MLA_V), BF16)
        gqa_c = jnp.zeros((B, CTX, GQA_HEADS * GQA_HD), BF16)
    flat = lambda a, c: jnp.concatenate([a.reshape(nl, -1), c.reshape(B * CTX, -1)], axis=0)
    h = _odd_out(tok, flat(mla_l, mla_c), flat(gqa_l, gqa_c), h, mods_l, p['w_out'].astype(BF16))
    return _moe(tok, h, mods_l, p['norm_ffn'], p['router'], p['moe_wg'].astype(BF16),
                p['moe_wu'].astype(BF16), p['moe_wd'].astype(BF16))


def _final_norm_kernel(h_ref, g_ref, o_ref):
    o_ref[...] = _rms(h_ref[...], g_ref[...])


def _final_norm(tok, h, gain):
    tm = tok.tm
    return pl.pallas_call(
        _final_norm_kernel,
        grid=(tok.n_lat,),
        in_specs=[pl.BlockSpec((tm, D), lambda i: (i, 0)), pl.BlockSpec((1, D), lambda i: (0, 0))],
        out_specs=pl.BlockSpec((tm, D), lambda i: (i, 0)),
        out_shape=jax.ShapeDtypeStruct((tok.B * tok.L, D), F32),
        compiler_params=_cparams(("arbitrary",)),
    )(h, gain.reshape(1, D))


def kernel(x, c, ctx, c_ctx, mod_w, mod_b, norm_mix, norm_ffn, final_norm,
           ev_w_in, ev_conv_w, ev_conv_b, hy_w1, hy_b1, hy_w2, hy_b2, hy_w3, hy_freq, hy_decay, hy_bias,
           s5_a_re, s5_a_im, s5_log_dt, s5_b_re, s5_b_im, s5_c_re, s5_c_im, s5_d, s5_w_glu, ev_w_out,
           ff_w_gate, ff_w_up, ff_w_down,
           od_w_in, mla_q_norm, mla_w_uq, mla_kv_norm, mla_w_ukv, gqa_sink, od_w_out,
           moe_router, moe_w_gate, moe_w_up, moe_w_down):
    B, L, _ = x.shape
    CTX = ctx.shape[1]
    tok = _Tok(B, L, CTX, _pick(math.gcd(L, B * CTX), (512, 256, 128)))
    cond_t = jnp.concatenate([c, c_ctx[None, :], jnp.zeros((8 - B - 1, D), F32)], axis=0).T
    mods = _modulations(cond_t, B + 1, mod_w, mod_b)
    tabs = _rope_tables(L, tok.tm)
    h = jnp.concatenate([x.reshape(B * L, D), ctx.reshape(B * CTX, D)], axis=0)
    for l in range(DEPTH):
        i = l // 2
        need_ctx = l < DEPTH - 1
        if l % 2 == 0:
            p = dict(norm_mix=norm_mix[l], norm_ffn=norm_ffn[l], w_in=ev_w_in[i], conv_w=ev_conv_w[i],
                     conv_b=ev_conv_b[i],
                     hy=(hy_w1[i], hy_b1[i], hy_w2[i], hy_b2[i], hy_w3[i], hy_freq[i], hy_decay[i]),
                     hy_bias=hy_bias[i],
                     s5=(s5_a_re[i], s5_a_im[i], s5_log_dt[i], s5_b_re[i], s5_b_im[i], s5_c_re[i], s5_c_im[i]),
                     s5_d=s5_d[i], s5_w_glu=s5_w_glu[i], w_out=ev_w_out[i],
                     ff_wg=ff_w_gate[i], ff_wu=ff_w_up[i], ff_wd=ff_w_down[i], need_ctx=need_ctx)
            h = _even_layer(tok, h, mods[l], p)
        else:
            p = dict(norm_mix=norm_mix[l], norm_ffn=norm_ffn[l], w_in=od_w_in[i], q_norm=mla_q_norm[i],
                     w_uq=mla_w_uq[i], kv_norm=mla_kv_norm[i], w_ukv=mla_w_ukv[i], sink=gqa_sink[i],
                     w_out=od_w_out[i], router=moe_router[i], moe_wg=moe_w_gate[i], moe_wu=moe_w_up[i],
                     moe_wd=moe_w_down[i], need_ctx=need_ctx)
            h = _odd_layer(tok, h, mods[l], tabs, p)
    return _final_norm(tok, h, final_norm).reshape(B, L, D)
```

```python
import functools
import math

import jax
import jax.numpy as jnp
from jax import lax
from jax.experimental import pallas as pl
from jax.experimental.pallas import tpu as pltpu

F32 = jnp.float32
BF16 = jnp.bfloat16
HI = lax.Precision.HIGHEST

D = 1024
DEPTH = 4
GRID_W = 64
EPS = 1e-6
NEG = -1e30
N_MOD = 6

HY_CH = 512
HY_ORDER = 2
HY_BANDS = 16
HY_EMB = 1 + 2 * HY_BANDS
HY_FFN = 64
HY_SHIFT = 0.05
HY_COLS = (HY_ORDER + 1) * HY_CH
S5_CH = 512
S5_GROUP = 16
S5_NG = S5_CH // S5_GROUP
S5_P = 64
S5_T = 16
S5_SEG = 8
EV_IN = HY_COLS + S5_CH

MLA_HEADS = 8
MLA_NOPE = 64
MLA_ROPE = 32
MLA_V = 64
Q_LORA = 256
KV_LORA = 128
GQA_HEADS = 8
GQA_KV = 2
GQA_HD = 64
WINDOW = 128
BLK = 128
ROPE_BASE = 10000.0
MLA_SCALE = (MLA_NOPE + MLA_ROPE) ** -0.5
GQA_SCALE = GQA_HD ** -0.5
LOG2E = math.log2(math.e)
HEAD_PAD = 128
MLA_VP = MLA_V + 16
OD_IN_PAD = 1280

D_FF = 2816
N_EXP = 8
EXP_FF = 3584

LANE = 128
FFT_N2 = 128
FFT_PAD = 8
VMEM_LIMIT = 56 * 1024 * 1024


def _cparams(sem):
    return pltpu.CompilerParams(dimension_semantics=sem, vmem_limit_bytes=VMEM_LIMIT)


def _pick(n, cands):
    for c in cands:
        if n % c == 0:
            return c
    raise ValueError(f"no tile for {n} in {cands}")


def _mod_kernel(ct_ref, w_ref, b_ref, o_ref, *, nrows):
    c = ct_ref[...]
    s = c * jax.nn.sigmoid(c)
    w = w_ref[...]
    rows = [jnp.sum(w * s[:, r:r + 1], axis=0, keepdims=True) for r in range(nrows)]
    rows.append(jnp.zeros((8 - nrows, w.shape[1]), F32))
    o_ref[...] = jnp.concatenate(rows, axis=0) + b_ref[...]


def _modulations(cond_t, nrows, mod_w, mod_b):
    tn = 1536
    out = pl.pallas_call(
        functools.partial(_mod_kernel, nrows=nrows),
        grid=(DEPTH, N_MOD * D // tn),
        in_specs=[pl.BlockSpec((D, 8), lambda l, j: (0, 0)),
                  pl.BlockSpec((None, D, tn), lambda l, j: (l, 0, j)),
                  pl.BlockSpec((None, 1, tn), lambda l, j: (l, 0, j))],
        out_specs=pl.BlockSpec((None, 8, tn), lambda l, j: (l, 0, j)),
        out_shape=jax.ShapeDtypeStruct((DEPTH, 8, N_MOD * D), F32),
        compiler_params=_cparams(("arbitrary", "arbitrary")),
    )(cond_t, mod_w, mod_b.reshape(DEPTH, 1, N_MOD * D))
    return out.reshape(DEPTH, 8, N_MOD, D)


class _Tok:
    def __init__(self, B, L, CTX, tm):
        assert L % tm == 0 and (B * CTX) % tm == 0
        self.B, self.L, self.CTX, self.tm = B, L, CTX, tm
        self.n_lat = B * L // tm
        self.n_all = self.n_lat + B * CTX // tm
        self.T = B * (L + CTX)
        self.per_seq = L // tm

    def mod_row(self, i):
        return jnp.where(i < self.n_lat, i // self.per_seq, self.B)


def _ada_norm(x, gain, mod, shift_idx, scale_idx):
    y = x * lax.rsqrt(jnp.mean(x * x, axis=-1, keepdims=True) + EPS) * gain
    return y * (1.0 + mod[scale_idx:scale_idx + 1, :]) + mod[shift_idx:shift_idx + 1, :]


def _norm_mm_kernel(h_ref, mod_ref, g_ref, w_ref, o_ref, y_scr):
    @pl.when(pl.program_id(1) == 0)
    def _():
        y_scr[...] = _ada_norm(h_ref[...], g_ref[...], mod_ref[...], 0, 1).astype(BF16)

    o_ref[...] = jnp.dot(y_scr[...], w_ref[...], preferred_element_type=F32).astype(o_ref.dtype)


def _norm_matmul(tok, h, mods_l, gain, w, tn, out_dtype=F32):
    tm, n = tok.tm, w.shape[1]
    return pl.pallas_call(
        _norm_mm_kernel,
        grid=(tok.n_all, n // tn),
        in_specs=[pl.BlockSpec((tm, D), lambda i, j: (i, 0)),
                  pl.BlockSpec((None, N_MOD, D), lambda i, j: (tok.mod_row(i), 0, 0)),
                  pl.BlockSpec((1, D), lambda i, j: (0, 0)),
                  pl.BlockSpec((D, tn), lambda i, j: (0, j))],
        out_specs=pl.BlockSpec((tm, tn), lambda i, j: (i, j)),
        out_shape=jax.ShapeDtypeStruct((tok.T, n), out_dtype),
        scratch_shapes=[pltpu.VMEM((tm, D), BF16)],
        compiler_params=_cparams(("arbitrary", "arbitrary")),
    )(h, mods_l, gain.reshape(1, D), w)


def _short_conv_kernel(z_ref, zp_ref, zn_ref, w_ref, b_ref, v_ref, x1_ref, x2_ref, *, tm, n_lat, L, CTX):
    i = pl.program_id(0)
    is_lat = i < n_lat
    seqlen = jnp.where(is_lat, L, CTX)
    off = jnp.where(is_lat, i * tm, (i - n_lat) * tm)
    first = lax.rem(off, seqlen) == 0
    last = lax.rem(off + tm, seqlen) == 0
    z = z_ref[...]
    prev_row = jnp.where(first, 0.0, zp_ref[7:8, :])
    next_row = jnp.where(last, 0.0, zn_ref[0:1, :])
    rid = lax.broadcasted_iota(jnp.int32, z.shape, 0)
    zm1 = jnp.where(rid == 0, prev_row, pltpu.roll(z, 1, axis=0))
    zp1 = jnp.where(rid == tm - 1, next_row, pltpu.roll(z, tm - 1, axis=0))
    out = b_ref[...] + zm1 * w_ref[0:1, :] + z * w_ref[1:2, :] + zp1 * w_ref[2:3, :]
    v_ref[...] = out[:, :HY_CH]
    x1_ref[...] = out[:, HY_CH:2 * HY_CH]
    x2_ref[...] = out[:, 2 * HY_CH:]


def _short_conv(tok, z, conv_w, conv_b):
    tm = _pick(math.gcd(tok.L, tok.CTX), (256, 128))
    n_lat = tok.B * tok.L // tm
    n_all = tok.T // tm
    r8 = tm // 8
    kern = functools.partial(_short_conv_kernel, tm=tm, n_lat=n_lat, L=tok.L, CTX=tok.CTX)
    o = jax.ShapeDtypeStruct((tok.T, HY_CH), F32)
    return pl.pallas_call(
        kern,
        grid=(n_all,),
        in_specs=[pl.BlockSpec((tm, HY_COLS), lambda i: (i, 0)),
                  pl.BlockSpec((8, HY_COLS), lambda i: (jnp.maximum(i * r8 - 1, 0), 0)),
                  pl.BlockSpec((8, HY_COLS), lambda i: (jnp.minimum((i + 1) * r8, n_all * r8 - 1), 0)),
                  pl.BlockSpec((8, HY_COLS), lambda i: (0, 0)),
                  pl.BlockSpec((1, HY_COLS), lambda i: (0, 0))],
        out_specs=[pl.BlockSpec((tm, HY_CH), lambda i: (i, 0))] * 3,
        out_shape=[o, o, o],
        compiler_params=_cparams(("arbitrary",)),
    )(z, z, z, jnp.pad(conv_w, ((0, 8 - conv_w.shape[0]), (0, 0))), conv_b.reshape(1, HY_COLS))


def _filter_kernel(f_ref, w1_ref, b1_ref, w2_ref, b2_ref, w3_ref, fr_ref, dec_ref, k_ref, s_ref):
    @pl.when(pl.program_id(0) == 0)
    def _():
        s_ref[...] = jnp.zeros_like(s_ref)

    f = f_ref[...]
    fr = fr_ref[...]
    hid = jnp.sin(fr * (jnp.dot(f, w1_ref[...], precision=HI, preferred_element_type=F32) + b1_ref[...]))
    hid = jnp.sin(fr * (jnp.dot(hid, w2_ref[...], precision=HI, preferred_element_type=F32) + b2_ref[...]))
    h = jnp.dot(hid, w3_ref[...], precision=HI, preferred_element_type=F32)
    t01 = f[:, 0:1]
    valid = f[:, LANE - 1:LANE]
    k = h * (jnp.exp(-t01 * jnp.abs(dec_ref[...])) + HY_SHIFT) * valid
    k_ref[0] = k[:, :HY_CH]
    k_ref[1] = k[:, HY_CH:]
    s_ref[...] += jnp.sum(jnp.abs(k), axis=0, keepdims=True)


def _hyena_filters(L, n, w1, b1, w2, b2, w3, freq, decay):
    row = jnp.arange(n)
    fwd = row < L
    bwd = row > n - L
    t = jnp.where(fwd, row, n - row).astype(F32)
    t01 = t / L
    bands = jnp.linspace(1e-4, HY_BANDS - 1, HY_BANDS, dtype=F32)
    ang = (2.0 * math.pi / L) * t[:, None] * bands[None, :]
    valid = (fwd | bwd).astype(F32)
    feats = jnp.concatenate([t01[:, None], jnp.cos(ang), -jnp.sin(ang),
                             jnp.zeros((n, LANE - 1 - HY_EMB), F32), valid[:, None]], axis=-1)
    w1p = jnp.pad(w1, ((0, LANE - HY_EMB), (0, 0)))
    tr = _pick(L, (512, 256))
    nb_half = n // 2 // tr
    ncol = HY_ORDER * HY_CH
    k, ssum = pl.pallas_call(
        _filter_kernel,
        grid=(n // tr,),
        in_specs=[pl.BlockSpec((tr, LANE), lambda i: (i, 0)),
                  pl.BlockSpec((LANE, HY_FFN), lambda i: (0, 0)),
                  pl.BlockSpec((1, HY_FFN), lambda i: (0, 0)),
                  pl.BlockSpec((HY_FFN, HY_FFN), lambda i: (0, 0)),
                  pl.BlockSpec((1, HY_FFN), lambda i: (0, 0)),
                  pl.BlockSpec((HY_FFN, ncol), lambda i: (0, jnp.where(i < nb_half, 0, 1))),
                  pl.BlockSpec((1, HY_FFN), lambda i: (0, 0)),
                  pl.BlockSpec((1, ncol), lambda i: (0, 0))],
        out_specs=[pl.BlockSpec((HY_ORDER, tr, HY_CH), lambda i: (0, i, 0)),
                   pl.BlockSpec((1, ncol), lambda i: (0, 0))],
        out_shape=[jax.ShapeDtypeStruct((HY_ORDER, n, HY_CH), F32),
                   jax.ShapeDtypeStruct((1, ncol), F32)],
        compiler_params=_cparams(("arbitrary",)),
    )(feats, w1p, b1.reshape(1, -1), w2, b2.reshape(1, -1), w3, freq.reshape(1, -1), decay.reshape(1, ncol))
    return k, (1.0 / ssum).reshape(HY_ORDER, 1, HY_CH)


def _dft_tables(n1, r_in):
    n = n1 * FFT_N2
    k1 = jnp.arange(n1)
    a1 = (2.0 * math.pi / n1) * ((k1[:, None] * jnp.arange(r_in)[None, :]) % n1).astype(F32)
    f1 = jnp.concatenate([jnp.cos(a1), -jnp.sin(a1)], axis=0)
    f3 = jnp.concatenate([jnp.cos(a1).T, -jnp.sin(a1).T], axis=1) / n
    n2 = jnp.arange(FFT_N2)
    kk = k1[:, None, None] + n1 * n2[None, :, None]
    ang = (2.0 * math.pi / n) * ((kk * n2[None, None, :]) % n).astype(F32)
    gr, gi = jnp.cos(ang), -jnp.sin(ang)
    g = jnp.concatenate([jnp.concatenate([gr, -gi], axis=2), jnp.concatenate([gi, gr], axis=2)], axis=1)
    return f1.astype(BF16), f3.astype(BF16), g.astype(BF16), jnp.swapaxes(g, 1, 2).astype(BF16)


def _fft1_kernel(f_ref, x_ref, o_ref):
    o_ref[...] = jnp.dot(f_ref[...], x_ref[...].astype(BF16), preferred_element_type=F32).astype(o_ref.dtype)


def _fft_stage1(f1, x):
    nb, r_in, cols = x.shape
    m = f1.shape[0]
    tn = _pick(cols, (8192, 4096, 2048))
    return pl.pallas_call(
        _fft1_kernel,
        grid=(nb, cols // tn),
        in_specs=[pl.BlockSpec((m, r_in), lambda b, j: (0, 0)),
                  pl.BlockSpec((None, r_in, tn), lambda b, j: (b, 0, j))],
        out_specs=pl.BlockSpec((None, m, tn), lambda b, j: (b, 0, j)),
        out_shape=jax.ShapeDtypeStruct((nb, m, cols), BF16),
        compiler_params=_cparams(("arbitrary", "arbitrary")),
    )(f1, x)


def _fft_filt_kernel(a_ref, g_ref, s_ref, o_ref):
    c = a_ref.shape[-1]
    a = a_ref[...].reshape(2 * FFT_N2, c)
    o_ref[...] = jnp.dot(g_ref[...], a, preferred_element_type=F32) * s_ref[...]


def _fft_filter_spectrum(a, g, inv):
    no, _, n1, _, c = a.shape
    return pl.pallas_call(
        _fft_filt_kernel,
        grid=(n1, no),
        in_specs=[pl.BlockSpec((None, 2, None, FFT_N2, c), lambda k, o: (o, 0, k, 0, 0)),
                  pl.BlockSpec((None, 2 * FFT_N2, 2 * FFT_N2), lambda k, o: (k, 0, 0)),
                  pl.BlockSpec((None, 1, c), lambda k, o: (o, 0, 0))],
        out_specs=pl.BlockSpec((None, None, 2 * FFT_N2, c), lambda k, o: (o, k, 0, 0)),
        out_shape=jax.ShapeDtypeStruct((no, n1, 2 * FFT_N2, c), F32),
        compiler_params=_cparams(("arbitrary", "arbitrary")),
    )(a, g, inv)


FFT_KB = 4


def _fft_mid_kernel(a_ref, g_ref, gt_ref, kh_ref, o_ref):
    kb, c = a_ref.shape[1], a_ref.shape[-1]
    K = range(kb)
    a = [jnp.concatenate([a_ref[0, k], a_ref[1, k]], axis=0) for k in K]
    x = [jnp.dot(g_ref[k], a[k], preferred_element_type=F32) for k in K]
    y = []
    for k in K:
        xr, xi = x[k][:FFT_N2], x[k][FFT_N2:]
        kr, ki = kh_ref[k, :FFT_N2, :], kh_ref[k, FFT_N2:, :]
        y.append(jnp.concatenate([xr * kr - xi * ki, xr * ki + xi * kr], axis=0).astype(BF16))
    bm = [jnp.dot(gt_ref[k], y[k], preferred_element_type=F32) for k in K]
    for k in K:
        o_ref[0, k] = bm[k][:FFT_N2].astype(o_ref.dtype)
        o_ref[1, k] = bm[k][FFT_N2:].astype(o_ref.dtype)


def _fft_mid(a, g, gt, khat):
    nb, _, n1, _, c = a.shape
    kb = math.gcd(n1, FFT_KB)
    return pl.pallas_call(
        _fft_mid_kernel,
        grid=(n1 // kb, nb),
        in_specs=[pl.BlockSpec((None, 2, kb, FFT_N2, c), lambda k, b: (b, 0, k, 0, 0)),
                  pl.BlockSpec((kb, 2 * FFT_N2, 2 * FFT_N2), lambda k, b: (k, 0, 0)),
                  pl.BlockSpec((kb, 2 * FFT_N2, 2 * FFT_N2), lambda k, b: (k, 0, 0)),
                  pl.BlockSpec((kb, 2 * FFT_N2, c), lambda k, b: (k, 0, 0))],
        out_specs=pl.BlockSpec((None, 2, kb, FFT_N2, c), lambda k, b: (b, 0, k, 0, 0)),
        out_shape=jax.ShapeDtypeStruct(a.shape, BF16),
        compiler_params=_cparams(("arbitrary", "arbitrary")),
    )(a, g, gt, khat)


def _fft3_kernel(f_ref, bm_ref, y_ref, gate_ref, bias_ref, o_ref):
    conv = jnp.dot(f_ref[...], bm_ref[...], preferred_element_type=F32)
    o_ref[...] = gate_ref[...] * (conv + y_ref[...] * bias_ref[...])


def _fft_stage3(f3, bm, y, gate, bias_row):
    nb, m, cols = bm.shape
    r = f3.shape[0]
    tn = _pick(cols, (8192, 4096, 2048))
    return pl.pallas_call(
        _fft3_kernel,
        grid=(nb, cols // tn),
        in_specs=[pl.BlockSpec((r, m), lambda b, j: (0, 0)),
                  pl.BlockSpec((None, m, tn), lambda b, j: (b, 0, j)),
                  pl.BlockSpec((None, r, tn), lambda b, j: (b, 0, j)),
                  pl.BlockSpec((None, r, tn), lambda b, j: (b, 0, j)),
                  pl.BlockSpec((1, tn), lambda b, j: (0, j))],
        out_specs=pl.BlockSpec((None, r, tn), lambda b, j: (b, 0, j)),
        out_shape=jax.ShapeDtypeStruct((nb, r, cols), F32),
        compiler_params=_cparams(("arbitrary", "arbitrary")),
    )(f3, bm, y, gate, bias_row)


def _fft1_tok_kernel(f_ref, x_ref, o_ref, s_scr, *, rows):
    n1 = f_ref.shape[0]
    f = f_ref[...]
    pitch = n1 + FFT_PAD

    def body(n2, carry):
        xs = x_ref[pl.ds(n2, rows, stride=FFT_N2), :].astype(BF16)
        s_scr[pl.ds(pl.multiple_of(n2 * pitch, 8), n1), :] = jnp.dot(f, xs, preferred_element_type=F32)
        return carry

    lax.fori_loop(0, FFT_N2, body, 0, unroll=8)

    def emit(k1, carry):
        o_ref[k1] = s_scr[pl.ds(k1, FFT_N2, stride=pitch), :].astype(o_ref.dtype)
        return carry

    lax.fori_loop(0, n1, emit, 0, unroll=4)


def _fft_stage1_tok(f1, x, nb):
    c = x.shape[1]
    rows = f1.shape[1]
    n = rows * FFT_N2
    n1 = f1.shape[0] // 2
    return pl.pallas_call(
        functools.partial(_fft1_tok_kernel, rows=rows),
        grid=(nb, c // LANE, 2),
        in_specs=[pl.BlockSpec((None, n1, rows), lambda b, j, r: (r, 0, 0)),
                  pl.BlockSpec((n, LANE), lambda b, j, r: (b, j))],
        out_specs=pl.BlockSpec((None, None, n1, FFT_N2, LANE), lambda b, j, r: (b, r, 0, 0, j)),
        out_shape=jax.ShapeDtypeStruct((nb, 2, n1, FFT_N2, c), BF16),
        scratch_shapes=[pltpu.VMEM((FFT_N2 * (n1 + FFT_PAD), LANE), F32)],
        compiler_params=_cparams(("arbitrary", "arbitrary", "arbitrary")),
    )(f1.reshape(2, n1, rows), x)


def _fft3_tok_kernel(f_ref, bm_ref, y_ref, gate_ref, bias_ref, o_ref, s_scr, t_scr, *, rows):
    half = pl.program_id(2)
    n1 = bm_ref.shape[0]
    sp, tp = FFT_N2 + FFT_PAD, rows + FFT_PAD

    def stage(k1, carry):
        s_scr[pl.ds(pl.multiple_of(k1 * sp, 8), FFT_N2), :] = bm_ref[k1].astype(F32)
        return carry

    lax.fori_loop(0, n1, stage, 0, unroll=4)
    f = f_ref[...]

    def part(n2):
        return jnp.dot(f, s_scr[pl.ds(n2, n1, stride=sp), :].astype(BF16), preferred_element_type=F32)

    def dst(n2):
        return pl.ds(pl.multiple_of(n2 * tp, 8), rows)

    @pl.when(half == 0)
    def _():
        def body(n2, carry):
            t_scr[dst(n2), :] = part(n2)
            return carry
        lax.fori_loop(0, FFT_N2, body, 0, unroll=8)

    @pl.when(half == 1)
    def _():
        def body(n2, carry):
            t_scr[dst(n2), :] += part(n2)
            return carry
        lax.fori_loop(0, FFT_N2, body, 0, unroll=8)

        def emit(r, carry):
            tok = pl.ds(pl.multiple_of(r * FFT_N2, FFT_N2), FFT_N2)
            conv = t_scr[pl.ds(r, FFT_N2, stride=tp), :]
            o_ref[tok, :] = gate_ref[tok, :] * (conv + y_ref[tok, :] * bias_ref[...])
            return carry
        lax.fori_loop(0, rows, emit, 0, unroll=2)


def _fft_stage3_tok(f3, bm, y, gate, bias):
    nb, _, n1, _, c = bm.shape
    rows = f3.shape[0]
    n = rows * FFT_N2
    tok = pl.BlockSpec((n, LANE), lambda b, j, r: (b, j))
    return pl.pallas_call(
        functools.partial(_fft3_tok_kernel, rows=rows),
        grid=(nb, c // LANE, 2),
        in_specs=[pl.BlockSpec((None, rows, n1), lambda b, j, r: (r, 0, 0)),
                  pl.BlockSpec((None, None, n1, FFT_N2, LANE), lambda b, j, r: (b, r, 0, 0, j)),
                  tok, tok,
                  pl.BlockSpec((1, LANE), lambda b, j, r: (0, j))],
        out_specs=tok,
        out_shape=jax.ShapeDtypeStruct((nb * n, c), F32),
        scratch_shapes=[pltpu.VMEM((n1 * (FFT_N2 + FFT_PAD), LANE), F32),
                        pltpu.VMEM((FFT_N2 * (rows + FFT_PAD), LANE), F32)],
        compiler_params=_cparams(("arbitrary", "arbitrary", "arbitrary")),
    )(f3.reshape(rows, 2, n1).transpose(1, 0, 2), bm, y, gate, bias.astype(F32).reshape(1, c))


def _hyena_long(v, x1, x2, B, L, hy, bias):
    r_in = L // FFT_N2
    n1 = 2 * r_in
    f1, f3, g, gt = _dft_tables(n1, r_in)
    f1k = _dft_tables(n1, n1)[0]
    k, inv = _hyena_filters(L, n1 * FFT_N2, *hy)
    khat = _fft_filter_spectrum(_fft_stage1_tok(f1k, k.reshape(HY_ORDER * n1 * FFT_N2, HY_CH), HY_ORDER), g, inv)
    y = v
    for o, gate in enumerate((x1, x2)):
        bm = _fft_mid(_fft_stage1_tok(f1, y, B), g, gt, khat[o])
        y = _fft_stage3_tok(f3, bm, y, gate, bias[o])
    return y


def _hyena_sequence(v, x1, x2, hy, bias):
    B, L, C = v.shape
    r_valid = L // FFT_N2
    r_in = max(r_valid, 16)
    n1 = max(2 * r_valid, r_in)
    n = n1 * FFT_N2
    if r_valid == r_in:
        return _hyena_long(v.reshape(B * L, C), x1.reshape(B * L, C), x2.reshape(B * L, C), B, L, hy,
                           bias).reshape(B, L, C)
    f1, f3, g, gt = _dft_tables(n1, r_in)
    f1k = _dft_tables(n1, n1)[0]
    k, inv = _hyena_filters(L, n, *hy)
    ak = _fft_stage1(f1k, k.reshape(HY_ORDER, n1, FFT_N2 * C))
    khat = _fft_filter_spectrum(ak.reshape(HY_ORDER, 2, n1, FFT_N2, C), g, inv)
    cols = FFT_N2 * C

    def view(a):
        a = a.reshape(B, r_valid, cols)
        return a if r_in == r_valid else jnp.pad(a, ((0, 0), (0, r_in - r_valid), (0, 0)))

    y = view(v)
    for o, gate in enumerate((view(x1), view(x2))):
        a = _fft_stage1(f1, y)
        bm = _fft_mid(a.reshape(B, 2, n1, FFT_N2, C), g, gt, khat[o])
        y = _fft_stage3(f3, bm.reshape(B, 2 * n1, cols), y, gate, jnp.tile(bias[o].astype(F32), FFT_N2)[None, :])
    return y[:, :r_valid].reshape(B, L, C)


def _s5_tables(a_re, a_im, log_dt, b_re, b_im, c_re, c_im, jj_ctx, jj_lat):
    lam = lax.complex(jnp.minimum(a_re.astype(F32), -1e-4), a_im.astype(F32))
    dt = jnp.exp(log_dt.astype(F32))[..., None]
    abar = jnp.exp(lam * dt)
    bbar = ((abar - 1.0) / lam)[..., None] * lax.complex(b_re.astype(F32), b_im.astype(F32))
    cmat = lax.complex(c_re.astype(F32), c_im.astype(F32))
    T = S5_T

    def powers(m):
        m = jnp.asarray(m, F32)
        return jnp.exp(lam * dt * m.reshape(m.shape + (1, 1, 1)))

    pw = powers(jnp.arange(T + 1))
    kt = jnp.real(jnp.einsum('dgop,tdgp,dgpi->tdgoi', cmat, pw[:T], bbar, precision=HI))
    tt = jnp.arange(T)
    lag = tt[None, :] - tt[:, None]
    w_intra = jnp.where((lag >= 0)[:, :, None, None, None, None],
                        kt[jnp.clip(lag, 0, T - 1)], 0.0)
    w_intra = jnp.stack([w_intra[:, :, 0], w_intra[::-1, ::-1, 1]], axis=2)
    w_intra = w_intra.transpose(2, 3, 0, 5, 1, 4).reshape(2, S5_NG, T * S5_GROUP, T * S5_GROUP)
    wb = pw[T - 1 - tt][..., None] * bbar[None]
    wb = jnp.stack([wb[:, 0], wb[::-1, 1]], axis=1)
    wb = wb.transpose(1, 2, 0, 4, 3).reshape(2, S5_NG, T * S5_GROUP, S5_P)
    w_cat = jnp.concatenate([w_intra, jnp.real(wb), jnp.imag(wb)], axis=-1)
    cp = cmat[None] * pw[1:, :, :, None, :]
    cp = jnp.stack([cp[:, 0], cp[::-1, 1]], axis=1)
    cp = cp.transpose(1, 2, 4, 0, 3).reshape(2, S5_NG, S5_P, T * S5_GROUP)
    c_cat = jnp.concatenate([jnp.real(cp), -jnp.imag(cp)], axis=2)

    def coef(z):
        zr, zi = jnp.real(z), jnp.imag(z)
        return jnp.stack([jnp.concatenate([zr, zr], -1), jnp.concatenate([-zi, zi], -1)], axis=-2)

    step = coef(powers(jnp.array(T)))
    seg_c, seg_l = (coef(powers(jnp.array(T * n))) for n in (jj_ctx, jj_lat))
    coefs = jnp.pad(jnp.concatenate([step, seg_c, seg_l], axis=2), ((0, 0), (0, 0), (0, 2), (0, 0)))
    ptab = coef(powers(T * jnp.arange(max(jj_ctx, jj_lat)))).transpose(1, 2, 0, 3, 4)
    return w_cat.astype(BF16), c_cat.astype(BF16), coefs, ptab


def _cmul(coef_a, coef_b, s):
    return coef_a * s + coef_b * pltpu.roll(s, S5_P, axis=1)


S5_GPB = LANE // S5_GROUP
S5_TC = S5_T * S5_GROUP


def _s5_in_kernel(z_ref, w_ref, yi_ref, ds_ref):
    nj = z_ref.shape[0] // S5_T
    ws = [z_ref[pl.ds(t, nj, stride=S5_T), :].T for t in range(S5_T)]
    for g in range(S5_GPB):
        vt = jnp.concatenate([w[S5_GROUP * g:S5_GROUP * (g + 1), :] for w in ws], axis=0)
        v = vt.T.astype(BF16)
        for d in range(2):
            o = jnp.dot(v, w_ref[d, g], preferred_element_type=F32)
            yi_ref[d, g] = o[:, :S5_TC]
            ds_ref[d, g] = o[:, S5_TC:]


def _s5_in(z, w_cat, nj):
    T = z.shape[0]
    R = T // S5_T
    col0 = HY_COLS // LANE
    return pl.pallas_call(
        _s5_in_kernel,
        grid=(R // nj, S5_CH // LANE),
        in_specs=[pl.BlockSpec((nj * S5_T, LANE), lambda i, c: (i, col0 + c)),
                  pl.BlockSpec((2, S5_GPB, S5_TC, S5_TC + 2 * S5_P), lambda i, c: (0, c, 0, 0))],
        out_specs=[pl.BlockSpec((2, S5_GPB, nj, S5_TC), lambda i, c: (0, c, i, 0)),
                   pl.BlockSpec((2, S5_GPB, nj, 2 * S5_P), lambda i, c: (0, c, i, 0))],
        out_shape=[jax.ShapeDtypeStruct((2, S5_NG, R, S5_TC), F32),
                   jax.ShapeDtypeStruct((2, S5_NG, R, 2 * S5_P), F32)],
        compiler_params=_cparams(("arbitrary", "arbitrary")),
    )(z, w_cat)


def _s5_scan_kernel(ds_ref, yi_ref, c_ref, cf_ref, p_ref, y_ref, sp_scr, *, parts, reverse):
    a1, a2 = cf_ref[0:1, :], cf_ref[1:2, :]
    rid = lax.broadcasted_iota(jnp.int32, (S5_SEG, 2 * S5_P), 0)
    first, last = (S5_SEG - 1, 0) if reverse else (0, S5_SEG - 1)
    shift = S5_SEG - 1 if reverse else 1
    nb = len(parts[0][0])
    zero = jnp.zeros((S5_SEG, 2 * S5_P), F32)
    s0 = [zero] * nb
    for pi, (bases, jj) in enumerate(parts):
        g1, g2 = cf_ref[2 + 2 * pi:3 + 2 * pi, :], cf_ref[3 + 2 * pi:4 + 2 * pi, :]

        def rows(b, k, bases=bases, jj=jj):
            return pl.ds(bases[b] + (jj - 1 - k if reverse else k), S5_SEG, stride=jj)

        def local_step(k, states, rows=rows):
            new = []
            for b in range(nb):
                sp_scr[rows(b, k), :] = states[b]
                new.append(_cmul(a1, a2, states[b]) + ds_ref[rows(b, k), :])
            return tuple(new)

        ends = lax.fori_loop(0, jj, local_step, (zero,) * nb)
        carries, nxt_s0 = [], []
        for b in range(nb):
            c = jnp.where(rid == first, s0[b], 0.0)
            for _ in range(S5_SEG - 1):
                c = jnp.where(rid == first, s0[b], pltpu.roll(ends[b] + _cmul(g1, g2, c), shift, axis=0))
            fin = ends[b] + _cmul(g1, g2, c)
            nxt_s0.append(jnp.broadcast_to(fin[last:last + 1, :], fin.shape))
            carries.append((c, pltpu.roll(c, S5_P, axis=1)))

        def fix_step(k, carry, rows=rows, carries=carries):
            p = p_ref[k]
            for b in range(nb):
                c, cs = carries[b]
                sp_scr[rows(b, k), :] += p[0:1, :] * c + p[1:2, :] * cs
            return carry

        lax.fori_loop(0, jj, fix_step, 0)
        s0 = nxt_s0
    y_ref[...] = yi_ref[...] + jnp.dot(sp_scr[...].astype(BF16), c_ref[...], preferred_element_type=F32)


def _s5_scan(d, ds, yi, c_cat, coefs, ptab, parts):
    R = ds.shape[2]
    jjm = ptab.shape[2]
    kern = functools.partial(_s5_scan_kernel, parts=parts, reverse=(d == 1))
    return pl.pallas_call(
        kern,
        grid=(S5_NG,),
        in_specs=[pl.BlockSpec((None, None, R, 2 * S5_P), lambda g: (d, g, 0, 0)),
                  pl.BlockSpec((None, None, R, S5_TC), lambda g: (d, g, 0, 0)),
                  pl.BlockSpec((None, None, 2 * S5_P, S5_TC), lambda g: (d, g, 0, 0)),
                  pl.BlockSpec((None, None, 8, 2 * S5_P), lambda g: (d, g, 0, 0)),
                  pl.BlockSpec((None, None, jjm, 2, 2 * S5_P), lambda g: (d, g, 0, 0, 0))],
        out_specs=pl.BlockSpec((None, R, S5_TC), lambda g: (g, 0, 0)),
        out_shape=jax.ShapeDtypeStruct((S5_NG, R, S5_TC), F32),
        scratch_shapes=[pltpu.VMEM((R, 2 * S5_P), F32)],
        compiler_params=_cparams(("arbitrary",)),
    )(ds, yi, c_cat, coefs, ptab)


def _s5_out_kernel(yf_ref, yb_ref, o_ref):
    nj = yf_ref.shape[1]
    yts = [(yf_ref[g] + yb_ref[g]).T for g in range(S5_GPB)]
    for t in range(S5_T):
        zt = jnp.concatenate([y[S5_GROUP * t:S5_GROUP * (t + 1), :] for y in yts], axis=0)
        o_ref[pl.ds(t, nj, stride=S5_T), :] = zt.T


def _s5_out(yf, yb, nj):
    R = yf.shape[1]
    spec = pl.BlockSpec((S5_GPB, nj, S5_TC), lambda i, c: (c, i, 0))
    return pl.pallas_call(
        _s5_out_kernel,
        grid=(R // nj, S5_CH // LANE),
        in_specs=[spec, spec],
        out_specs=pl.BlockSpec((nj * S5_T, LANE), lambda i, c: (i, c)),
        out_shape=jax.ShapeDtypeStruct((R * S5_T, S5_CH), F32),
        compiler_params=_cparams(("arbitrary", "arbitrary")),
    )(yf, yb)


def _s5_mixer(tok, z, s5):
    B = tok.B
    cl, cc = tok.L // S5_T, tok.CTX // S5_T
    jl, jc = cl // S5_SEG, cc // S5_SEG
    w_cat, c_cat, coefs, ptab = _s5_tables(*s5, jc, jl)
    yi, ds = _s5_in(z, w_cat, math.gcd(B * cl, B * cc, 64))
    lat = (tuple(b * cl for b in range(B)), jl)
    ctx = (tuple(B * cl + b * cc for b in range(B)), jc)
    yf = _s5_scan(0, ds, yi, c_cat, coefs, ptab, (ctx, lat))
    yb = _s5_scan(1, ds, yi, c_cat, coefs, ptab, (ctx, lat))
    return _s5_out(yf, yb, math.gcd(B * cl, B * cc, 64))


def _gelu_tanh(x):
    return 0.5 * x * (1.0 + jnp.tanh(math.sqrt(2.0 / math.pi) * (x + 0.044715 * (x * x * x))))


def _even_out_kernel(hl_ref, hc_ref, ys_ref, u_ref, h_ref, mod_ref, dsk_ref, wg_ref, wo_ref, o_ref, *, n_lat):
    y = _gelu_tanh(ys_ref[...] + dsk_ref[...] * u_ref[...])
    s = y * jax.nn.sigmoid(jnp.dot(y.astype(BF16), wg_ref[...], preferred_element_type=F32))
    hy = jnp.where(pl.program_id(0) < n_lat, hl_ref[...], hc_ref[...])
    ol = (jnp.dot(hy.astype(BF16), wo_ref[:HY_CH, :], preferred_element_type=F32)
          + jnp.dot(s.astype(BF16), wo_ref[HY_CH:, :], preferred_element_type=F32))
    o_ref[...] = h_ref[...] + mod_ref[2:3, :] * ol


def _even_out(tok, hl, hc, ys, z, h, mods_l, dsk, w_glu, w_out):
    tm = tok.tm
    return pl.pallas_call(
        functools.partial(_even_out_kernel, n_lat=tok.n_lat),
        grid=(tok.n_all,),
        in_specs=[pl.BlockSpec((tm, HY_CH), lambda i: (jnp.minimum(i, tok.n_lat - 1), 0)),
                  pl.BlockSpec((tm, HY_CH), lambda i: (jnp.maximum(i - tok.n_lat, 0), 0)),
                  pl.BlockSpec((tm, S5_CH), lambda i: (i, 0)),
                  pl.BlockSpec((tm, S5_CH), lambda i: (i, HY_COLS // S5_CH)),
                  pl.BlockSpec((tm, D), lambda i: (i, 0)),
                  pl.BlockSpec((None, N_MOD, D), lambda i: (tok.mod_row(i), 0, 0)),
                  pl.BlockSpec((1, S5_CH), lambda i: (0, 0)),
                  pl.BlockSpec((S5_CH, S5_CH), lambda i: (0, 0)),
                  pl.BlockSpec((D, D), lambda i: (0, 0))],
        out_specs=pl.BlockSpec((tm, D), lambda i: (i, 0)),
        out_shape=jax.ShapeDtypeStruct((tok.T, D), F32),
        compiler_params=_cparams(("arbitrary",)),
    )(hl, hc, ys, z, h, mods_l, dsk.reshape(1, S5_CH), w_glu, w_out)


def _ffn_kernel(h_ref, mod_ref, g_ref, wg_ref, wu_ref, wd_ref, o_ref, y_scr, acc_scr):
    j = pl.program_id(1)

    @pl.when(j == 0)
    def _():
        y_scr[...] = _ada_norm(h_ref[...], g_ref[...], mod_ref[...], 3, 4).astype(BF16)
        acc_scr[...] = jnp.zeros_like(acc_scr)

    y = y_scr[...]
    gate = jnp.dot(y, wg_ref[...], preferred_element_type=F32)
    up = jnp.dot(y, wu_ref[...], preferred_element_type=F32)
    act = (gate * jax.nn.sigmoid(gate) * up).astype(BF16)
    acc_scr[...] += jnp.dot(act, wd_ref[...], preferred_element_type=F32)

    @pl.when(j == pl.num_programs(1) - 1)
    def _():
        o_ref[...] = h_ref[...] + mod_ref[5:6, :] * acc_scr[...]


def _ffn(tok, h, mods_l, gain, wg, wu, wd):
    tm = tok.tm
    ff = wg.shape[1]
    tf = _pick(ff, (1408, 512, 256, 128))
    return pl.pallas_call(
        _ffn_kernel,
        grid=(tok.n_all, ff // tf),
        in_specs=[pl.BlockSpec((tm, D), lambda i, j: (i, 0)),
                  pl.BlockSpec((None, N_MOD, D), lambda i, j: (tok.mod_row(i), 0, 0)),
                  pl.BlockSpec((1, D), lambda i, j: (0, 0)),
                  pl.BlockSpec((D, tf), lambda i, j: (0, j)),
                  pl.BlockSpec((D, tf), lambda i, j: (0, j)),
                  pl.BlockSpec((tf, D), lambda i, j: (j, 0))],
        out_specs=pl.BlockSpec((tm, D), lambda i, j: (i, 0)),
        out_shape=jax.ShapeDtypeStruct((tok.T, D), F32),
        scratch_shapes=[pltpu.VMEM((tm, D), BF16), pltpu.VMEM((tm, D), F32)],
        compiler_params=_cparams(("arbitrary", "arbitrary")),
    )(h, mods_l, gain.reshape(1, D), wg, wu, wd)


def _even_layer(tok, h, mods_l, p):
    B, L, CTX = tok.B, tok.L, tok.CTX
    nl = B * L
    z = _norm_matmul(tok, h, mods_l, p['norm_mix'], p['w_in'].astype(BF16), EV_IN)
    v, x1, x2 = _short_conv(tok, z, p['conv_w'], p['conv_b'])
    ctx = lambda a: a[nl:].reshape(B, CTX, -1)
    if L % (16 * FFT_N2) == 0:
        hl = _hyena_long(v, x1, x2, B, L, p['hy'], p['hy_bias'])
    else:
        lat = lambda a: a[:nl].reshape(B, L, -1)
        hl = _hyena_sequence(lat(v), lat(x1), lat(x2), p['hy'], p['hy_bias']).reshape(nl, HY_CH)
    ys_all = _s5_mixer(tok, z, p['s5'])
    if p['need_ctx']:
        hc = _hyena_sequence(ctx(v), ctx(x1), ctx(x2), p['hy'], p['hy_bias']).reshape(B * CTX, HY_CH)
    else:
        hc = jnp.zeros((B * CTX, HY_CH), F32)
    h = _even_out(tok, hl, hc, ys_all, z, h, mods_l, p['s5_d'], p['s5_w_glu'].astype(BF16),
                  p['w_out'].astype(BF16))
    return _ffn(tok, h, mods_l, p['norm_ffn'], p['ff_wg'].astype(BF16), p['ff_wu'].astype(BF16),
                p['ff_wd'].astype(BF16))


def _rope_tables(L, tm):
    t = jnp.arange(L)
    row = (t // GRID_W).astype(F32)[:, None]
    col = (t % GRID_W).astype(F32)[:, None]

    def pattern(dim):
        nf = dim // 4
        inv = ROPE_BASE ** (-jnp.arange(nf, dtype=F32) / nf)
        ar, ac = row * inv[None, :], col * inv[None, :]
        cos = jnp.concatenate([jnp.cos(ar)] * 2 + [jnp.cos(ac)] * 2, axis=1)
        z = jnp.zeros((L, nf), F32)
        s_up = jnp.concatenate([-jnp.sin(ar), z, -jnp.sin(ac), z], axis=1)
        s_dn = jnp.concatenate([z, jnp.sin(ar), z, jnp.sin(ac)], axis=1)
        return cos, s_up, s_dn

    def pad_mla(a, fill):
        return jnp.concatenate([jnp.full((L, MLA_NOPE), fill, F32), a,
                                jnp.full((L, HEAD_PAD - MLA_NOPE - MLA_ROPE), fill, F32)], axis=1)

    cm, um, dm = pattern(MLA_ROPE)
    cg, ug, dg = pattern(GQA_HD)
    mla = jnp.stack([pad_mla(cm, 1.0), pad_mla(um, 0.0), pad_mla(dm, 0.0)])
    gqa = jnp.stack([jnp.tile(cg, (1, 2)), jnp.tile(ug, (1, 2)), jnp.tile(dg, (1, 2))])
    ident = jnp.stack([jnp.ones((tm, LANE), F32), jnp.zeros((tm, LANE), F32), jnp.zeros((tm, LANE), F32)])
    return jnp.stack([jnp.concatenate([mla, ident], axis=1), jnp.concatenate([gqa, ident], axis=1)])


def _rope(x, tab, w):
    outs = []
    for h in range(x.shape[1] // LANE):
        xs = x[:, h * LANE:(h + 1) * LANE]
        outs.append(xs * tab[0] + pltpu.roll(xs, LANE - w, axis=1) * tab[1] + pltpu.roll(xs, w, axis=1) * tab[2])
    return outs[0] if len(outs) == 1 else jnp.concatenate(outs, axis=1)


def _rms(x, g):
    return x * lax.rsqrt(jnp.mean(x * x, axis=-1, keepdims=True) + EPS) * g


_O_CQ, _O_CKV, _O_GQ, _O_GK, _O_GV, _O_KR = 0, 256, 384, 896, 1024, 1152


def _odd_proj_kernel(z_ref, tab_ref, qn_ref, kvn_ref, wuq_ref, wuk_ref, wuv_ref, e_ref,
                     q_ref, k_ref, v_ref, gq_ref, gk_ref, gv_ref):
    z = z_ref[...]
    mt, gt = tab_ref[0], tab_ref[1]
    qn = _rms(z[:, _O_CQ:_O_CKV], qn_ref[...]).astype(BF16)
    q = jnp.dot(qn, wuq_ref[...], preferred_element_type=F32)
    q_ref[...] = (_rope(q, mt, MLA_ROPE // 4) * (MLA_SCALE * LOG2E)).astype(BF16)
    kvn = _rms(z[:, _O_CKV:_O_GQ], kvn_ref[...]).astype(BF16)
    k = (jnp.dot(kvn, wuk_ref[...], preferred_element_type=F32)
         + jnp.dot(z[:, _O_KR:], e_ref[...], precision=HI, preferred_element_type=F32))
    k_ref[...] = _rope(k, mt, MLA_ROPE // 4).astype(BF16)
    v_ref[...] = jnp.dot(kvn, wuv_ref[...], preferred_element_type=F32).astype(BF16)
    gq_ref[...] = (_rope(z[:, _O_GQ:_O_GK], gt, GQA_HD // 4) * (GQA_SCALE * LOG2E)).astype(BF16)
    gk_ref[...] = _rope(z[:, _O_GK:_O_GV], gt, GQA_HD // 4).astype(BF16)
    gv_ref[...] = z[:, _O_GV:_O_KR].astype(BF16)


def _odd_proj(tok, z, tabs, q_norm, kv_norm, w_uq, w_ukv):
    tm = tok.tm
    hq = MLA_HEADS * HEAD_PAD
    wq = jnp.pad(w_uq.reshape(Q_LORA, MLA_HEADS, MLA_NOPE + MLA_ROPE),
                 ((0, 0), (0, 0), (0, HEAD_PAD - MLA_NOPE - MLA_ROPE))).reshape(Q_LORA, hq).astype(BF16)
    wkv = w_ukv.reshape(KV_LORA, MLA_HEADS, MLA_NOPE + MLA_V)
    wk = jnp.pad(wkv[..., :MLA_NOPE], ((0, 0), (0, 0), (0, HEAD_PAD - MLA_NOPE))).reshape(KV_LORA, hq).astype(BF16)
    wv = wkv[..., MLA_NOPE:].reshape(KV_LORA, MLA_HEADS * MLA_V).astype(BF16)
    eye = jnp.eye(MLA_ROPE, dtype=F32)
    e_head = jnp.pad(eye, ((0, LANE - MLA_ROPE), (MLA_NOPE, HEAD_PAD - MLA_NOPE - MLA_ROPE)))
    e = jnp.tile(e_head, (1, MLA_HEADS))
    tab_blk = lambda i: (0, 0, jnp.where(i < tok.n_lat, i % tok.per_seq, tok.per_seq), 0)
    full = lambda shape: pl.BlockSpec(shape, lambda i: (0,) * len(shape))
    widths = (hq, hq, MLA_HEADS * MLA_V, GQA_HEADS * GQA_HD, GQA_KV * GQA_HD, GQA_KV * GQA_HD)
    return pl.pallas_call(
        _odd_proj_kernel,
        grid=(tok.n_all,),
        in_specs=[pl.BlockSpec((tm, OD_IN_PAD), lambda i: (i, 0)),
                  pl.BlockSpec((2, 3, tm, LANE), tab_blk),
                  full((1, Q_LORA)), full((1, KV_LORA)), full((Q_LORA, hq)), full((KV_LORA, hq)),
                  full((KV_LORA, MLA_HEADS * MLA_V)), full((LANE, hq))],
        out_specs=[pl.BlockSpec((tm, w), lambda i: (i, 0)) for w in widths],
        out_shape=[jax.ShapeDtypeStruct((tok.T, w), BF16) for w in widths],
        compiler_params=_cparams(("arbitrary",)),
    )(z, tabs, q_norm.reshape(1, -1), kv_norm.reshape(1, -1), wq, wk, wv, e)


def _mla_attn_kernel(q_ref, k_ref, vt_ref, o_ref, s_scr, p_scr, *, tk, nk, ks):
    tq = q_ref.shape[0]
    nslab = tk // ks
    kq, kv = math.gcd(tk, MLA_KS_QK), math.gcd(tk, MLA_KS_PV)
    dn = (((1,), (1,)), ((), ()))
    qs = [q_ref[:, h * HEAD_PAD:(h + 1) * HEAD_PAD] for h in range(2)]

    def qk_slab(h, c, j, mx):
        if (j * ks) % kq:
            return mx
        r = pl.multiple_of(c * tk + j * ks, ks)
        s = lax.dot_general(k_ref[pl.ds(r, kq), h * HEAD_PAD:(h + 1) * HEAD_PAD], qs[h], dn,
                            preferred_element_type=F32)
        s_scr[h, j * ks:j * ks + kq, :] = s
        return jnp.maximum(mx, jnp.max(s, axis=0, keepdims=True))

    def pv_slab(h, c, j):
        if (j * ks) % kv:
            return 0.0
        return jnp.dot(vt_ref[c, h * MLA_VP:(h + 1) * MLA_VP, j * ks:j * ks + kv], p_scr[h, j * ks:j * ks + kv, :],
                       preferred_element_type=F32)

    def step(x, c_sm, c_pv, c_qk, st):
        y = 1 - x
        m, acc, mc = st[x]
        m_new = jnp.maximum(m, mc)
        alpha = jnp.exp2(m - m_new)
        acc_y = st[y][1]
        mx_y = jnp.full((1, tq), NEG, F32)
        for j in range(nslab):
            acc_y = acc_y + pv_slab(y, c_pv, j)
            mx_y = qk_slab(y, c_qk, j, mx_y)
            p_scr[x, j * ks:(j + 1) * ks, :] = jnp.exp2(s_scr[x, j * ks:(j + 1) * ks, :] - m_new).astype(BF16)
        new = [None, None]
        new[x] = (m_new, alpha * acc, mc)
        new[y] = (st[y][0], acc_y, mx_y)
        return tuple(new)

    def body(c, st):
        st = step(0, c, jnp.maximum(c - 1, 0), c, st)
        return step(1, c, c, jnp.minimum(c + 1, nk - 1), st)

    neg = jnp.full((1, tq), NEG, F32)
    acc0 = jnp.zeros((MLA_VP, tq), F32)
    p_scr[1] = jnp.zeros(p_scr.shape[1:], BF16)
    mx0 = neg
    for j in range(nslab):
        mx0 = qk_slab(0, 0, j, mx0)
    def trip(i, st):
        for u in range(MLA_UNROLL):
            st = body(i * MLA_UNROLL + u, st)
        return st

    st = lax.fori_loop(0, nk // MLA_UNROLL, trip, ((neg, acc0, mx0), (neg, acc0, neg)))
    for c in range(nk - nk % MLA_UNROLL, nk):
        st = body(jnp.int32(c), st)
    acc1 = st[1][1]
    for j in range(nslab):
        acc1 = acc1 + pv_slab(1, nk - 1, j)
    out_t = jnp.concatenate([a[:MLA_V] / a[MLA_V:MLA_V + 1] for a in (st[0][1], acc1)], axis=0)
    o_ref[...] = out_t.T.astype(o_ref.dtype)


MLA_TQ = (256, 128)
MLA_TK = (768, 512, 256, 128)
MLA_KS = 128
MLA_KS_QK = 384
MLA_KS_PV = 256
MLA_UNROLL = 5


def _mla_attention(q, k, v):
    B, Lq, _ = q.shape
    Nk = k.shape[1]
    tq = _pick(Lq, MLA_TQ)
    tk = _pick(Nk, MLA_TK)
    nk = Nk // tk
    hp = MLA_HEADS // 2
    vt = v.reshape(B, nk, tk, hp, 2, MLA_V).transpose(0, 3, 1, 4, 5, 2)
    vt = jnp.concatenate([vt, jnp.ones((B, hp, nk, 2, MLA_VP - MLA_V, tk), v.dtype)], axis=4)
    vt = vt.reshape(B, hp, nk, 2 * MLA_VP, tk)
    kern = functools.partial(_mla_attn_kernel, tk=tk, nk=nk, ks=math.gcd(tk, MLA_KS))
    return pl.pallas_call(
        kern,
        grid=(B, hp, Lq // tq),
        in_specs=[pl.BlockSpec((None, tq, 2 * HEAD_PAD), lambda b, h, i: (b, i, h)),
                  pl.BlockSpec((None, Nk, 2 * HEAD_PAD), lambda b, h, i: (b, 0, h)),
                  pl.BlockSpec((None, None, nk, 2 * MLA_VP, tk), lambda b, h, i: (b, h, 0, 0, 0))],
        out_specs=pl.BlockSpec((None, tq, 2 * MLA_V), lambda b, h, i: (b, i, h)),
        out_shape=jax.ShapeDtypeStruct((B, Lq, MLA_HEADS * MLA_V), BF16),
        scratch_shapes=[pltpu.VMEM((2, tk, tq), F32), pltpu.VMEM((2, tk, tq), BF16)],
        compiler_params=_cparams(("arbitrary", "arbitrary", "arbitrary")),
    )(q, k, vt)


def _gqa_kernel(sink_ref, q_ref, kc_ref, vct_ref, *rest, L, has_band):
    group = GQA_HEADS // GQA_KV
    gw = group * BLK
    if has_band:
        kp_ref, k_ref, kn_ref, vtp_ref, vt_ref, vtn_ref, bias_ref, o_ref = rest
        keys = jnp.concatenate([kp_ref[...], k_ref[...], kn_ref[...], kc_ref[...]], axis=0)
        vals_t = jnp.concatenate([vtp_ref[...], vt_ref[...], vtn_ref[...], vct_ref[...]], axis=1)
    else:
        (o_ref,) = rest
        keys, vals_t = kc_ref[...], vct_ref[...]
    dn = (((1,), (1,)), ((), ()))
    G = range(GQA_KV)
    qs = [jnp.concatenate([q_ref[:, (kh * group + g) * GQA_HD:(kh * group + g + 1) * GQA_HD] for g in range(group)],
                          axis=0) for kh in G]
    s = [lax.dot_general(keys[:, kh * GQA_HD:(kh + 1) * GQA_HD], qs[kh], dn, preferred_element_type=F32) for kh in G]
    if has_band:
        s = [x + bias_ref[...] for x in s]
    sink = [sink_ref[:, kh * gw:(kh + 1) * gw] for kh in G]
    m = [jnp.maximum(jnp.max(s[kh], axis=0, keepdims=True), sink[kh]) for kh in G]
    p = [jnp.exp2(s[kh] - m[kh]) for kh in G]
    den = [jnp.sum(p[kh], axis=0, keepdims=True) + jnp.exp2(sink[kh] - m[kh]) for kh in G]
    ot = [jnp.dot(vals_t[kh * GQA_HD:(kh + 1) * GQA_HD, :], p[kh].astype(BF16), preferred_element_type=F32)
          * (1.0 / den[kh]) for kh in G]
    o = jnp.concatenate(ot, axis=0).T
    o_ref[...] = jnp.concatenate([o[g * BLK:(g + 1) * BLK, kh * GQA_HD:(kh + 1) * GQA_HD]
                                  for kh in G for g in range(group)], axis=1).astype(o_ref.dtype)


def _gqa_attention(sink, q, kc, vc, k=None, v=None):
    B, Lq, _ = q.shape
    CTX = kc.shape[1]
    has_band = k is not None
    kw = GQA_KV * GQA_HD
    nb = Lq // BLK
    sink_row = jnp.repeat(sink.astype(F32) * LOG2E, BLK)[None, :]
    in_specs = [pl.BlockSpec((1, GQA_HEADS * BLK), lambda b, i: (0, 0)),
                pl.BlockSpec((None, BLK, GQA_HEADS * GQA_HD), lambda b, i: (b, i, 0)),
                pl.BlockSpec((None, CTX, kw), lambda b, i: (b, 0, 0)),
                pl.BlockSpec((None, kw, CTX), lambda b, i: (b, 0, 0))]
    args = [sink_row, q, kc, jnp.swapaxes(vc, 1, 2)]
    if has_band:
        prev = lambda i: jnp.maximum(i - 1, 0)
        nxt = lambda i: jnp.minimum(i + 1, nb - 1)
        in_specs += [pl.BlockSpec((None, BLK, kw), lambda b, i: (b, prev(i), 0)),
                     pl.BlockSpec((None, BLK, kw), lambda b, i: (b, i, 0)),
                     pl.BlockSpec((None, BLK, kw), lambda b, i: (b, nxt(i), 0)),
                     pl.BlockSpec((None, kw, BLK), lambda b, i: (b, 0, prev(i))),
                     pl.BlockSpec((None, kw, BLK), lambda b, i: (b, 0, i)),
                     pl.BlockSpec((None, kw, BLK), lambda b, i: (b, 0, nxt(i)))]
        vt = jnp.swapaxes(v, 1, 2)
        r = jnp.arange(3 * BLK + CTX)[:, None]
        c = jnp.arange(GQA_HEADS // GQA_KV * BLK)[None, :] % BLK
        band = (jnp.abs(r - BLK - c) <= WINDOW) | (r >= 3 * BLK)
        ok = jnp.stack([band & (r >= BLK), band, band & ((r < 2 * BLK) | (r >= 3 * BLK))])
        bias = jnp.where(ok, 0.0, NEG).astype(F32)
        in_specs += [pl.BlockSpec((None,) + bias.shape[1:],
                                  lambda b, i: (jnp.where(i == 0, 0, jnp.where(i == nb - 1, 2, 1)), 0, 0))]
        args += [k, k, k, vt, vt, vt, bias]
    kern = functools.partial(_gqa_kernel, L=Lq, has_band=has_band)
    return pl.pallas_call(
        kern,
        grid=(B, nb),
        in_specs=in_specs,
        out_specs=pl.BlockSpec((None, BLK, GQA_HEADS * GQA_HD), lambda b, i: (b, i, 0)),
        out_shape=jax.ShapeDtypeStruct((B, Lq, GQA_HEADS * GQA_HD), BF16),
        compiler_params=_cparams(("arbitrary", "arbitrary")),
    )(*args)


def _odd_out_kernel(a_ref, g_ref, h_ref, mod_ref, wo_ref, o_ref):
    half = a_ref.shape[1]
    ol = (jnp.dot(a_ref[...], wo_ref[:half, :], preferred_element_type=F32)
          + jnp.dot(g_ref[...], wo_ref[half:, :], preferred_element_type=F32))
    o_ref[...] = h_ref[...] + mod_ref[2:3, :] * ol


def _odd_out(tok, mla, gqa, h, mods_l, w_out):
    tm = tok.tm
    half = mla.shape[1]
    return pl.pallas_call(
        _odd_out_kernel,
        grid=(tok.n_all,),
        in_specs=[pl.BlockSpec((tm, half), lambda i: (i, 0)),
                  pl.BlockSpec((tm, half), lambda i: (i, 0)),
                  pl.BlockSpec((tm, D), lambda i: (i, 0)),
                  pl.BlockSpec((None, N_MOD, D), lambda i: (tok.mod_row(i), 0, 0)),
                  pl.BlockSpec((D, D), lambda i: (0, 0))],
        out_specs=pl.BlockSpec((tm, D), lambda i: (i, 0)),
        out_shape=jax.ShapeDtypeStruct((tok.T, D), F32),
        compiler_params=_cparams(("arbitrary",)),
    )(mla, gqa, h, mods_l, w_out)


MOE_TR = 512
MOE_TF = 1792
MOE_IDX_BLK = 1024
MOE_NF = EXP_FF // MOE_TF
ROUTE_W = 8


def _router_kernel(h_ref, mod_ref, g_ref, r_ref, y_ref, route_ref):
    y = _ada_norm(h_ref[...], g_ref[...], mod_ref[...], 3, 4)
    y_ref[...] = y
    logits = jnp.dot(y, r_ref[...], precision=HI, preferred_element_type=F32)
    lane = lax.broadcasted_iota(jnp.int32, logits.shape, 1)
    lg = jnp.where(lane < N_EXP, logits, -jnp.inf)
    m1 = jnp.max(lg, axis=-1, keepdims=True)
    i1 = jnp.min(jnp.where(lg == m1, lane, LANE), axis=-1, keepdims=True)
    lg2 = jnp.where(lane == i1, -jnp.inf, lg)
    m2 = jnp.max(lg2, axis=-1, keepdims=True)
    i2 = jnp.min(jnp.where(lg2 == m2, lane, LANE), axis=-1, keepdims=True)
    e = jnp.exp(m2 - m1)
    w1 = 1.0 / (1.0 + e)
    route = (jnp.where(lane == 0, w1, 0.0) + jnp.where(lane == 1, e * w1, 0.0)
             + jnp.where(lane == 2, i1.astype(F32), 0.0) + jnp.where(lane == 3, i2.astype(F32), 0.0))
    route_ref[...] = route[:, :ROUTE_W]


def _moe_router(tok, h, mods_l, gain, router):
    tm = tok.tm
    rp = jnp.pad(router, ((0, 0), (0, LANE - N_EXP)))
    return pl.pallas_call(
        _router_kernel,
        grid=(tok.n_all,),
        in_specs=[pl.BlockSpec((tm, D), lambda i: (i, 0)),
                  pl.BlockSpec((None, N_MOD, D), lambda i: (tok.mod_row(i), 0, 0)),
                  pl.BlockSpec((1, D), lambda i: (0, 0)),
                  pl.BlockSpec((D, LANE), lambda i: (0, 0))],
        out_specs=[pl.BlockSpec((tm, D), lambda i: (i, 0)), pl.BlockSpec((tm, ROUTE_W), lambda i: (i, 0))],
        out_shape=[jax.ShapeDtypeStruct((tok.T, D), F32), jax.ShapeDtypeStruct((tok.T, ROUTE_W), F32)],
        compiler_params=_cparams(("arbitrary",)),
    )(h, mods_l, gain.reshape(1, D), rp)


def _moe_plan(route, tr):
    T = route.shape[0]
    flat = route[:, 2:4].astype(jnp.int32).reshape(-1)
    onehot = (flat[:, None] == jnp.arange(N_EXP, dtype=jnp.int32)[None, :]).astype(jnp.int32)
    csum = jnp.cumsum(onehot, axis=0)
    rank = jnp.sum((csum - onehot) * onehot, axis=1)
    padded = (csum[-1] + tr - 1) // tr * tr
    ends = jnp.cumsum(padded)
    pos = (ends - padded)[flat] + rank
    tpb = MOE_IDX_BLK // tr
    n_tiles = -(-((2 * T + N_EXP * (tr - 1)) // tr) // tpb) * tpb
    src = jnp.zeros((n_tiles * tr,), jnp.int32).at[pos].set(jnp.arange(2 * T, dtype=jnp.int32) // 2,
                                                            unique_indices=True)
    starts = jnp.arange(n_tiles, dtype=jnp.int32) * tr
    tile_expert = jnp.minimum(jnp.sum((starts[:, None] >= ends[None, :]).astype(jnp.int32), axis=1), N_EXP - 1)
    n_valid = (ends[-1] // tr).astype(jnp.int32).reshape(1)
    return src, tile_expert.astype(jnp.int32), n_valid, pos.reshape(T, 2)


def _gather_rows(idx_ref, src_hbm, dst, sem, n):
    def issue(r, carry):
        pltpu.make_async_copy(src_hbm.at[pl.ds(idx_ref[r], 1), :], dst.at[pl.ds(r, 1), :], sem).start()
        return carry

    lax.fori_loop(0, n, issue, 0, unroll=8)


def _wait_rows(src_hbm, dst, sem, n):
    pltpu.make_async_copy(src_hbm.at[pl.ds(0, n), :], dst, sem).wait()


def _moe_expert_kernel(te_ref, nv_ref, idx_ref, idxn_ref, y_hbm, wg_ref, wu_ref, wd_ref, o_ref,
                       xbuf, y_scr, acc_scr, sem, *, tr):
    t, f = pl.program_id(0), pl.program_id(1)
    nf = pl.num_programs(1)
    nv = nv_ref[0]
    valid = t < nv
    slot = lax.rem(t, 2)
    per_f = tr // MOE_NF
    tpb = MOE_IDX_BLK // tr
    nxt_tile = jnp.minimum(t + 1, jnp.maximum(nv - 1, 0))

    @pl.when((f == 0) & (t == 0))
    def _():
        _gather_rows(idx_ref, y_hbm, xbuf.at[0], sem.at[0], tr)

    @pl.when((f == 0) & valid)
    def _():
        _wait_rows(y_hbm, xbuf.at[slot], sem.at[slot], tr)
        y_scr[...] = xbuf[slot].astype(BF16)
        acc_scr[...] = jnp.zeros_like(acc_scr)

    @pl.when(valid)
    def _():
        base = f * per_f
        ibase = lax.rem(nxt_tile, tpb) * tr + base
        nxt = xbuf.at[1 - slot]
        for r in range(per_f):
            pltpu.make_async_copy(y_hbm.at[pl.ds(idxn_ref[ibase + r], 1), :], nxt.at[pl.ds(base + r, 1), :],
                                  sem.at[1 - slot]).start()
        y = y_scr[...]
        gate = jnp.dot(y, wg_ref[...], preferred_element_type=F32)
        up = jnp.dot(y, wu_ref[...], preferred_element_type=F32)
        act = (gate * jax.nn.sigmoid(gate) * up).astype(BF16)
        acc_scr[...] += jnp.dot(act, wd_ref[...], preferred_element_type=F32)

    @pl.when((f == nf - 1) & (t == nv - 1))
    def _():
        _wait_rows(y_hbm, xbuf.at[1 - slot], sem.at[1 - slot], tr)

    @pl.when(f == nf - 1)
    def _():
        o_ref[...] = jnp.where(valid, acc_scr[...], 0.0)


def _moe_experts(y, src, tile_expert, n_valid, wg, wu, wd, tr):
    n_tiles = src.shape[0] // tr
    tf = MOE_TF
    tpb = MOE_IDX_BLK // tr
    kern = functools.partial(_moe_expert_kernel, tr=tr)
    smem = functools.partial(pl.BlockSpec, memory_space=pltpu.SMEM)
    grid_spec = pltpu.PrefetchScalarGridSpec(
        num_scalar_prefetch=2,
        grid=(n_tiles, EXP_FF // tf),
        in_specs=[smem((MOE_IDX_BLK,), lambda t, f, te, nv: (t // tpb,)),
                  smem((MOE_IDX_BLK,), lambda t, f, te, nv: (jnp.minimum(t + 1, jnp.maximum(nv[0] - 1, 0)) // tpb,)),
                  pl.BlockSpec(memory_space=pl.ANY),
                  pl.BlockSpec((None, D, tf), lambda t, f, te, nv: (te[t], 0, f)),
                  pl.BlockSpec((None, D, tf), lambda t, f, te, nv: (te[t], 0, f)),
                  pl.BlockSpec((None, tf, D), lambda t, f, te, nv: (te[t], f, 0))],
        out_specs=pl.BlockSpec((tr, D), lambda t, f, te, nv: (t, 0)),
        scratch_shapes=[pltpu.VMEM((2, tr, D), F32), pltpu.VMEM((tr, D), BF16), pltpu.VMEM((tr, D), F32),
                        pltpu.SemaphoreType.DMA((2,))])
    return pl.pallas_call(
        kern,
        grid_spec=grid_spec,
        out_shape=jax.ShapeDtypeStruct((n_tiles * tr, D), F32),
        compiler_params=_cparams(("arbitrary", "arbitrary")),
    )(tile_expert, n_valid, src, src, y, wg, wu, wd)


def _moe_combine_kernel(idx_ref, idxn_ref, o_hbm, h_ref, route_ref, mod_ref, out_ref, buf, sem, *, tm):
    i = pl.program_id(0)
    n = pl.num_programs(0)
    slot = lax.rem(i, 2)

    @pl.when(i == 0)
    def _():
        _gather_rows(idx_ref, o_hbm, buf.at[0], sem.at[0], 2 * tm)

    nxt = buf.at[1 - slot]
    for r in range(2 * tm):
        pltpu.make_async_copy(o_hbm.at[pl.ds(idxn_ref[r], 1), :], nxt.at[pl.ds(r, 1), :], sem.at[1 - slot]).start()
    _wait_rows(o_hbm, buf.at[slot], sem.at[slot], 2 * tm)
    r = route_ref[...]
    mix = r[:, 0:1] * buf[slot, :tm, :] + r[:, 1:2] * buf[slot, tm:, :]
    out_ref[...] = h_ref[...] + mod_ref[5:6, :] * mix

    @pl.when(i == n - 1)
    def _():
        _wait_rows(o_hbm, nxt, sem.at[1 - slot], 2 * tm)


def _moe_combine(tok, o_sorted, pos, h, route, mods_l):
    tm = tok.tm
    n = tok.n_all
    idx = pos.reshape(n, tm, 2).transpose(0, 2, 1).reshape(-1)
    kern = functools.partial(_moe_combine_kernel, tm=tm)
    smem = functools.partial(pl.BlockSpec, memory_space=pltpu.SMEM)
    return pl.pallas_call(
        kern,
        grid=(n,),
        in_specs=[smem((2 * tm,), lambda i: (i,)),
                  smem((2 * tm,), lambda i: (jnp.minimum(i + 1, n - 1),)),
                  pl.BlockSpec(memory_space=pl.ANY),
                  pl.BlockSpec((tm, D), lambda i: (i, 0)),
                  pl.BlockSpec((tm, ROUTE_W), lambda i: (i, 0)),
                  pl.BlockSpec((None, N_MOD, D), lambda i: (tok.mod_row(i), 0, 0))],
        out_specs=pl.BlockSpec((tm, D), lambda i: (i, 0)),
        out_shape=jax.ShapeDtypeStruct((tok.T, D), F32),
        scratch_shapes=[pltpu.VMEM((2, 2 * tm, D), F32), pltpu.SemaphoreType.DMA((2,))],
        compiler_params=_cparams(("arbitrary",)),
    )(idx, idx, o_sorted, h, route, mods_l)


def _moe(tok, h, mods_l, gain, router, wg, wu, wd):
    y, route = _moe_router(tok, h, mods_l, gain, router)
    src, tile_expert, n_valid, pos = _moe_plan(route, MOE_TR)
    o_sorted = _moe_experts(y, src, tile_expert, n_valid, wg, wu, wd, MOE_TR)
    return _moe_combine(tok, o_sorted, pos, h, route, mods_l)


def _odd_layer(tok, h, mods_l, tabs, p):
    B, L, CTX = tok.B, tok.L, tok.CTX
    nl = B * L
    w = p['w_in']
    w_in = jnp.concatenate([w[:, :Q_LORA + KV_LORA], w[:, 416:1184], w[:, 384:416],
                            jnp.zeros((D, OD_IN_PAD - 1184), w.dtype)], axis=1).astype(BF16)
    z = _norm_matmul(tok, h, mods_l, p['norm_mix'], w_in, OD_IN_PAD)
    q, k, v, gq, gk, gv = _odd_proj(tok, z, tabs, p['q_norm'], p['kv_norm'], p['w_uq'], p['w_ukv'])
    lat = lambda a: a[:nl].reshape(B, L, -1)
    ctx = lambda a: a[nl:].reshape(B, CTX, -1)
    cat = lambda a: jnp.concatenate([ctx(a), lat(a)], axis=1)
    mla_l = _mla_attention(lat(q), cat(k), cat(v))
    gqa_l = _gqa_attention(p['sink'], lat(gq), ctx(gk), ctx(gv), lat(gk), lat(gv))
    if p['need_ctx']:
        mla_c = _mla_attention(ctx(q), ctx(k), ctx(v))
        gqa_c = _gqa_attention(p['sink'], ctx(gq), ctx(gk), ctx(gv))
    else:
        mla_c = jnp.zeros((B, CTX, MLA_HEADS * MLA_V), BF16)
        gqa_c = jnp.zeros((B, CTX, GQA_HEADS * GQA_HD), BF16)
    flat = lambda a, c: jnp.concatenate([a.reshape(nl, -1), c.reshape(B * CTX, -1)], axis=0)
    h = _odd_out(tok, flat(mla_l, mla_c), flat(gqa_l, gqa_c), h, mods_l, p['w_out'].astype(BF16))
    return _moe(tok, h, mods_l, p['norm_ffn'], p['router'], p['moe_wg'].astype(BF16),
                p['moe_wu'].astype(BF16), p['moe_wd'].astype(BF16))


def _final_norm_kernel(h_ref, g_ref, o_ref):
    o_ref[...] = _rms(h_ref[...], g_ref[...])


def _final_norm(tok, h, gain):
    tm = tok.tm
    return pl.pallas_call(
        _final_norm_kernel,
        grid=(tok.n_lat,),
        in_specs=[pl.BlockSpec((tm, D), lambda i: (i, 0)), pl.BlockSpec((1, D), lambda i: (0, 0))],
        out_specs=pl.BlockSpec((tm, D), lambda i: (i, 0)),
        out_shape=jax.ShapeDtypeStruct((tok.B * tok.L, D), F32),
        compiler_params=_cparams(("arbitrary",)),
    )(h, gain.reshape(1, D))


def kernel(x, c, ctx, c_ctx, mod_w, mod_b, norm_mix, norm_ffn, final_norm,
           ev_w_in, ev_conv_w, ev_conv_b, hy_w1, hy_b1, hy_w2, hy_b2, hy_w3, hy_freq, hy_decay, hy_bias,
           s5_a_re, s5_a_im, s5_log_dt, s5_b_re, s5_b_im, s5_c_re, s5_c_im, s5_d, s5_w_glu, ev_w_out,
           ff_w_gate, ff_w_up, ff_w_down,
           od_w_in, mla_q_norm, mla_w_uq, mla_kv_norm, mla_w_ukv, gqa_sink, od_w_out,
           moe_router, moe_w_gate, moe_w_up, moe_w_down):
    B, L, _ = x.shape
    CTX = ctx.shape[1]
    tok = _Tok(B, L, CTX, _pick(math.gcd(L, B * CTX), (512, 256, 128)))
    cond_t = jnp.concatenate([c, c_ctx[None, :], jnp.zeros((8 - B - 1, D), F32)], axis=0).T
    mods = _modulations(cond_t, B + 1, mod_w, mod_b)
    tabs = _rope_tables(L, tok.tm)
    h = jnp.concatenate([x.reshape(B * L, D), ctx.reshape(B * CTX, D)], axis=0)
    for l in range(DEPTH):
        i = l // 2
        need_ctx = l < DEPTH - 1
        if l % 2 == 0:
            p = dict(norm_mix=norm_mix[l], norm_ffn=norm_ffn[l], w_in=ev_w_in[i], conv_w=ev_conv_w[i],
                     conv_b=ev_conv_b[i],
                     hy=(hy_w1[i], hy_b1[i], hy_w2[i], hy_b2[i], hy_w3[i], hy_freq[i], hy_decay[i]),
                     hy_bias=hy_bias[i],
                     s5=(s5_a_re[i], s5_a_im[i], s5_log_dt[i], s5_b_re[i], s5_b_im[i], s5_c_re[i], s5_c_im[i]),
                     s5_d=s5_d[i], s5_w_glu=s5_w_glu[i], w_out=ev_w_out[i],
                     ff_wg=ff_w_gate[i], ff_wu=ff_w_up[i], ff_wd=ff_w_down[i], need_ctx=need_ctx)
            h = _even_layer(tok, h, mods[l], p)
        else:
            p = dict(norm_mix=norm_mix[l], norm_ffn=norm_ffn[l], w_in=od_w_in[i], q_norm=mla_q_norm[i],
                     w_uq=mla_w_uq[i], kv_norm=mla_kv_norm[i], w_ukv=mla_w_ukv[i], sink=gqa_sink[i],
                     w_out=od_w_out[i], router=moe_router[i], moe_wg=moe_w_gate[i], moe_wu=moe_w_up[i],
                     moe_wd=moe_w_down[i], need_ctx=need_ctx)
            h = _odd_layer(tok, h, mods[l], tabs, p)
    return _final_norm(tok, h, final_norm).reshape(B, L, D)
```

```python
import functools
import math

import jax
import jax.numpy as jnp
from jax import lax
from jax.experimental import pallas as pl
from jax.experimental.pallas import tpu as pltpu

F32 = jnp.float32
BF16 = jnp.bfloat16
HI = lax.Precision.HIGHEST

D = 1024
DEPTH = 4
GRID_W = 64
EPS = 1e-6
NEG = -1e30
N_MOD = 6

HY_CH = 512
HY_ORDER = 2
HY_BANDS = 16
HY_EMB = 1 + 2 * HY_BANDS
HY_FFN = 64
HY_SHIFT = 0.05
HY_COLS = (HY_ORDER + 1) * HY_CH
S5_CH = 512
S5_GROUP = 16
S5_NG = S5_CH // S5_GROUP
S5_P = 64
S5_T = 16
S5_SEG = 8
EV_IN = HY_COLS + S5_CH

MLA_HEADS = 8
MLA_NOPE = 64
MLA_ROPE = 32
MLA_V = 64
Q_LORA = 256
KV_LORA = 128
GQA_HEADS = 8
GQA_KV = 2
GQA_HD = 64
WINDOW = 128
BLK = 128
ROPE_BASE = 10000.0
MLA_SCALE = (MLA_NOPE + MLA_ROPE) ** -0.5
GQA_SCALE = GQA_HD ** -0.5
LOG2E = math.log2(math.e)
HEAD_PAD = 128
MLA_VP = MLA_V + 16
OD_IN_PAD = 1280

D_FF = 2816
N_EXP = 8
EXP_FF = 3584

LANE = 128
FFT_N2 = 128
FFT_PAD = 8
VMEM_LIMIT = 56 * 1024 * 1024
CAST_BLOCK_BYTES = 8 * 1024 * 1024


def _cparams(sem):
    return pltpu.CompilerParams(dimension_semantics=sem, vmem_limit_bytes=VMEM_LIMIT)


def _pick(n, cands):
    for c in cands:
        if n % c == 0:
            return c
    raise ValueError(f"no tile for {n} in {cands}")


def _mod_kernel(ct_ref, w_ref, b_ref, o_ref, *, nrows):
    c = ct_ref[...]
    s = c * jax.nn.sigmoid(c)
    w = w_ref[...]
    rows = [jnp.sum(w * s[:, r:r + 1], axis=0, keepdims=True) for r in range(nrows)]
    rows.append(jnp.zeros((8 - nrows, w.shape[1]), F32))
    o_ref[...] = jnp.concatenate(rows, axis=0) + b_ref[...]


def _modulations(cond_t, nrows, mod_w, mod_b):
    tn = 1536
    out = pl.pallas_call(
        functools.partial(_mod_kernel, nrows=nrows),
        grid=(DEPTH, N_MOD * D // tn),
        in_specs=[pl.BlockSpec((D, 8), lambda l, j: (0, 0)),
                  pl.BlockSpec((None, D, tn), lambda l, j: (l, 0, j)),
                  pl.BlockSpec((None, 1, tn), lambda l, j: (l, 0, j))],
        out_specs=pl.BlockSpec((None, 8, tn), lambda l, j: (l, 0, j)),
        out_shape=jax.ShapeDtypeStruct((DEPTH, 8, N_MOD * D), F32),
        compiler_params=_cparams(("arbitrary", "arbitrary")),
    )(cond_t, mod_w, mod_b.reshape(DEPTH, 1, N_MOD * D))
    return out.reshape(DEPTH, 8, N_MOD, D)


def _cast_kernel(x_ref, o_ref):
    o_ref[...] = x_ref[...].astype(o_ref.dtype)


def _to_bf16(stack, layer):
    shape = stack.shape[1:]
    cols = shape[-1]
    w2 = stack.reshape(-1, cols)
    rows = w2.shape[0] // stack.shape[0]
    tr = next(t for t in (2048, 1024, 512, 256, 128, 8) if rows % t == 0 and t * cols * 4 <= CAST_BLOCK_BYTES)
    nblk = rows // tr
    out = pl.pallas_call(
        _cast_kernel,
        grid=(nblk,),
        in_specs=[pl.BlockSpec((tr, cols), lambda i: (layer * nblk + i, 0))],
        out_specs=pl.BlockSpec((tr, cols), lambda i: (i, 0)),
        out_shape=jax.ShapeDtypeStruct((rows, cols), BF16),
        compiler_params=_cparams(("arbitrary",)),
    )(w2)
    return out.reshape(shape)


class _Tok:
    def __init__(self, B, L, CTX, tm):
        assert L % tm == 0 and (B * CTX) % tm == 0
        self.B, self.L, self.CTX, self.tm = B, L, CTX, tm
        self.n_lat = B * L // tm
        self.n_all = self.n_lat + B * CTX // tm
        self.T = B * (L + CTX)
        self.per_seq = L // tm

    def mod_row(self, i):
        return jnp.where(i < self.n_lat, i // self.per_seq, self.B)


def _ada_norm(x, gain, mod, shift_idx, scale_idx):
    y = x * lax.rsqrt(jnp.mean(x * x, axis=-1, keepdims=True) + EPS) * gain
    return y * (1.0 + mod[scale_idx:scale_idx + 1, :]) + mod[shift_idx:shift_idx + 1, :]


def _norm_mm_kernel(h_ref, mod_ref, g_ref, w_ref, o_ref, y_scr):
    @pl.when(pl.program_id(1) == 0)
    def _():
        y_scr[...] = _ada_norm(h_ref[...], g_ref[...], mod_ref[...], 0, 1).astype(BF16)

    o_ref[...] = jnp.dot(y_scr[...], w_ref[...], preferred_element_type=F32).astype(o_ref.dtype)


def _norm_matmul(tok, h, mods_l, gain, w, tn, out_dtype=F32):
    tm, n = tok.tm, w.shape[1]
    return pl.pallas_call(
        _norm_mm_kernel,
        grid=(tok.n_all, n // tn),
        in_specs=[pl.BlockSpec((tm, D), lambda i, j: (i, 0)),
                  pl.BlockSpec((None, N_MOD, D), lambda i, j: (tok.mod_row(i), 0, 0)),
                  pl.BlockSpec((1, D), lambda i, j: (0, 0)),
                  pl.BlockSpec((D, tn), lambda i, j: (0, j))],
        out_specs=pl.BlockSpec((tm, tn), lambda i, j: (i, j)),
        out_shape=jax.ShapeDtypeStruct((tok.T, n), out_dtype),
        scratch_shapes=[pltpu.VMEM((tm, D), BF16)],
        compiler_params=_cparams(("arbitrary", "arbitrary")),
    )(h, mods_l, gain.reshape(1, D), w)


def _short_conv_kernel(z_ref, zp_ref, zn_ref, w_ref, b_ref, v_ref, x1_ref, x2_ref, *, tm, n_lat, L, CTX):
    i = pl.program_id(0)
    is_lat = i < n_lat
    seqlen = jnp.where(is_lat, L, CTX)
    off = jnp.where(is_lat, i * tm, (i - n_lat) * tm)
    first = lax.rem(off, seqlen) == 0
    last = lax.rem(off + tm, seqlen) == 0
    z = z_ref[...]
    prev_row = jnp.where(first, 0.0, zp_ref[7:8, :])
    next_row = jnp.where(last, 0.0, zn_ref[0:1, :])
    rid = lax.broadcasted_iota(jnp.int32, z.shape, 0)
    zm1 = jnp.where(rid == 0, prev_row, pltpu.roll(z, 1, axis=0))
    zp1 = jnp.where(rid == tm - 1, next_row, pltpu.roll(z, tm - 1, axis=0))
    out = b_ref[...] + zm1 * w_ref[0:1, :] + z * w_ref[1:2, :] + zp1 * w_ref[2:3, :]
    v_ref[...] = out[:, :HY_CH]
    x1_ref[...] = out[:, HY_CH:2 * HY_CH]
    x2_ref[...] = out[:, 2 * HY_CH:]


def _short_conv(tok, z, conv_w, conv_b):
    tm = _pick(math.gcd(tok.L, tok.CTX), (256, 128))
    n_lat = tok.B * tok.L // tm
    n_all = tok.T // tm
    r8 = tm // 8
    kern = functools.partial(_short_conv_kernel, tm=tm, n_lat=n_lat, L=tok.L, CTX=tok.CTX)
    o = jax.ShapeDtypeStruct((tok.T, HY_CH), F32)
    return pl.pallas_call(
        kern,
        grid=(n_all,),
        in_specs=[pl.BlockSpec((tm, HY_COLS), lambda i: (i, 0)),
                  pl.BlockSpec((8, HY_COLS), lambda i: (jnp.maximum(i * r8 - 1, 0), 0)),
                  pl.BlockSpec((8, HY_COLS), lambda i: (jnp.minimum((i + 1) * r8, n_all * r8 - 1), 0)),
                  pl.BlockSpec((8, HY_COLS), lambda i: (0, 0)),
                  pl.BlockSpec((1, HY_COLS), lambda i: (0, 0))],
        out_specs=[pl.BlockSpec((tm, HY_CH), lambda i: (i, 0))] * 3,
        out_shape=[o, o, o],
        compiler_params=_cparams(("arbitrary",)),
    )(z, z, z, jnp.pad(conv_w, ((0, 8 - conv_w.shape[0]), (0, 0))), conv_b.reshape(1, HY_COLS))


def _filter_kernel(f_ref, w1_ref, b1_ref, w2_ref, b2_ref, w3_ref, fr_ref, dec_ref, k_ref, s_ref):
    @pl.when(pl.program_id(0) == 0)
    def _():
        s_ref[...] = jnp.zeros_like(s_ref)

    f = f_ref[...]
    fr = fr_ref[...]
    hid = jnp.sin(fr * (jnp.dot(f, w1_ref[...], precision=HI, preferred_element_type=F32) + b1_ref[...]))
    hid = jnp.sin(fr * (jnp.dot(hid, w2_ref[...], precision=HI, preferred_element_type=F32) + b2_ref[...]))
    h = jnp.dot(hid, w3_ref[...], precision=HI, preferred_element_type=F32)
    t01 = f[:, 0:1]
    valid = f[:, LANE - 1:LANE]
    k = h * (jnp.exp(-t01 * jnp.abs(dec_ref[...])) + HY_SHIFT) * valid
    k_ref[0] = k[:, :HY_CH]
    k_ref[1] = k[:, HY_CH:]
    s_ref[...] += jnp.sum(jnp.abs(k), axis=0, keepdims=True)


def _hyena_filters(L, n, w1, b1, w2, b2, w3, freq, decay):
    row = jnp.arange(n)
    fwd = row < L
    bwd = row > n - L
    t = jnp.where(fwd, row, n - row).astype(F32)
    t01 = t / L
    bands = jnp.linspace(1e-4, HY_BANDS - 1, HY_BANDS, dtype=F32)
    ang = (2.0 * math.pi / L) * t[:, None] * bands[None, :]
    valid = (fwd | bwd).astype(F32)
    feats = jnp.concatenate([t01[:, None], jnp.cos(ang), -jnp.sin(ang),
                             jnp.zeros((n, LANE - 1 - HY_EMB), F32), valid[:, None]], axis=-1)
    w1p = jnp.pad(w1, ((0, LANE - HY_EMB), (0, 0)))
    tr = _pick(L, (512, 256))
    nb_half = n // 2 // tr
    ncol = HY_ORDER * HY_CH
    k, ssum = pl.pallas_call(
        _filter_kernel,
        grid=(n // tr,),
        in_specs=[pl.BlockSpec((tr, LANE), lambda i: (i, 0)),
                  pl.BlockSpec((LANE, HY_FFN), lambda i: (0, 0)),
                  pl.BlockSpec((1, HY_FFN), lambda i: (0, 0)),
                  pl.BlockSpec((HY_FFN, HY_FFN), lambda i: (0, 0)),
                  pl.BlockSpec((1, HY_FFN), lambda i: (0, 0)),
                  pl.BlockSpec((HY_FFN, ncol), lambda i: (0, jnp.where(i < nb_half, 0, 1))),
                  pl.BlockSpec((1, HY_FFN), lambda i: (0, 0)),
                  pl.BlockSpec((1, ncol), lambda i: (0, 0))],
        out_specs=[pl.BlockSpec((HY_ORDER, tr, HY_CH), lambda i: (0, i, 0)),
                   pl.BlockSpec((1, ncol), lambda i: (0, 0))],
        out_shape=[jax.ShapeDtypeStruct((HY_ORDER, n, HY_CH), F32),
                   jax.ShapeDtypeStruct((1, ncol), F32)],
        compiler_params=_cparams(("arbitrary",)),
    )(feats, w1p, b1.reshape(1, -1), w2, b2.reshape(1, -1), w3, freq.reshape(1, -1), decay.reshape(1, ncol))
    return k, (1.0 / ssum).reshape(HY_ORDER, 1, HY_CH)


def _dft_tables(n1, r_in):
    n = n1 * FFT_N2
    k1 = jnp.arange(n1)
    a1 = (2.0 * math.pi / n1) * ((k1[:, None] * jnp.arange(r_in)[None, :]) % n1).astype(F32)
    f1 = jnp.concatenate([jnp.cos(a1), -jnp.sin(a1)], axis=0)
    f3 = jnp.concatenate([jnp.cos(a1).T, -jnp.sin(a1).T], axis=1) / n
    n2 = jnp.arange(FFT_N2)
    kk = k1[:, None, None] + n1 * n2[None, :, None]
    ang = (2.0 * math.pi / n) * ((kk * n2[None, None, :]) % n).astype(F32)
    gr, gi = jnp.cos(ang), -jnp.sin(ang)
    g = jnp.concatenate([jnp.concatenate([gr, -gi], axis=2), jnp.concatenate([gi, gr], axis=2)], axis=1)
    return f1.astype(BF16), f3.astype(BF16), g.astype(BF16), jnp.swapaxes(g, 1, 2).astype(BF16)


def _fft1_kernel(f_ref, x_ref, o_ref):
    o_ref[...] = jnp.dot(f_ref[...], x_ref[...].astype(BF16), preferred_element_type=F32).astype(o_ref.dtype)


def _fft_stage1(f1, x):
    nb, r_in, cols = x.shape
    m = f1.shape[0]
    tn = _pick(cols, (8192, 4096, 2048))
    return pl.pallas_call(
        _fft1_kernel,
        grid=(nb, cols // tn),
        in_specs=[pl.BlockSpec((m, r_in), lambda b, j: (0, 0)),
                  pl.BlockSpec((None, r_in, tn), lambda b, j: (b, 0, j))],
        out_specs=pl.BlockSpec((None, m, tn), lambda b, j: (b, 0, j)),
        out_shape=jax.ShapeDtypeStruct((nb, m, cols), BF16),
        compiler_params=_cparams(("arbitrary", "arbitrary")),
    )(f1, x)


def _fft_filt_kernel(a_ref, g_ref, s_ref, o_ref):
    c = a_ref.shape[-1]
    a = a_ref[...].reshape(2 * FFT_N2, c)
    o_ref[...] = jnp.dot(g_ref[...], a, preferred_element_type=F32) * s_ref[...]


def _fft_filter_spectrum(a, g, inv):
    no, _, n1, _, c = a.shape
    return pl.pallas_call(
        _fft_filt_kernel,
        grid=(n1, no),
        in_specs=[pl.BlockSpec((None, 2, None, FFT_N2, c), lambda k, o: (o, 0, k, 0, 0)),
                  pl.BlockSpec((None, 2 * FFT_N2, 2 * FFT_N2), lambda k, o: (k, 0, 0)),
                  pl.BlockSpec((None, 1, c), lambda k, o: (o, 0, 0))],
        out_specs=pl.BlockSpec((None, None, 2 * FFT_N2, c), lambda k, o: (o, k, 0, 0)),
        out_shape=jax.ShapeDtypeStruct((no, n1, 2 * FFT_N2, c), F32),
        compiler_params=_cparams(("arbitrary", "arbitrary")),
    )(a, g, inv)


FFT_KB = 4


def _fft_mid_kernel(a_ref, g_ref, gt_ref, kh_ref, o_ref):
    kb, c = a_ref.shape[1], a_ref.shape[-1]
    K = range(kb)
    a = [jnp.concatenate([a_ref[0, k], a_ref[1, k]], axis=0) for k in K]
    x = [jnp.dot(g_ref[k], a[k], preferred_element_type=F32) for k in K]
    y = []
    for k in K:
        xr, xi = x[k][:FFT_N2], x[k][FFT_N2:]
        kr, ki = kh_ref[k, :FFT_N2, :], kh_ref[k, FFT_N2:, :]
        y.append(jnp.concatenate([xr * kr - xi * ki, xr * ki + xi * kr], axis=0).astype(BF16))
    bm = [jnp.dot(gt_ref[k], y[k], preferred_element_type=F32) for k in K]
    for k in K:
        o_ref[0, k] = bm[k][:FFT_N2].astype(o_ref.dtype)
        o_ref[1, k] = bm[k][FFT_N2:].astype(o_ref.dtype)


def _fft_mid(a, g, gt, khat):
    nb, _, n1, _, c = a.shape
    kb = math.gcd(n1, FFT_KB)
    return pl.pallas_call(
        _fft_mid_kernel,
        grid=(n1 // kb, nb),
        in_specs=[pl.BlockSpec((None, 2, kb, FFT_N2, c), lambda k, b: (b, 0, k, 0, 0)),
                  pl.BlockSpec((kb, 2 * FFT_N2, 2 * FFT_N2), lambda k, b: (k, 0, 0)),
                  pl.BlockSpec((kb, 2 * FFT_N2, 2 * FFT_N2), lambda k, b: (k, 0, 0)),
                  pl.BlockSpec((kb, 2 * FFT_N2, c), lambda k, b: (k, 0, 0))],
        out_specs=pl.BlockSpec((None, 2, kb, FFT_N2, c), lambda k, b: (b, 0, k, 0, 0)),
        out_shape=jax.ShapeDtypeStruct(a.shape, BF16),
        compiler_params=_cparams(("arbitrary", "arbitrary")),
    )(a, g, gt, khat)


def _fft3_kernel(f_ref, bm_ref, y_ref, gate_ref, bias_ref, o_ref):
    conv = jnp.dot(f_ref[...], bm_ref[...], preferred_element_type=F32)
    o_ref[...] = gate_ref[...] * (conv + y_ref[...] * bias_ref[...])


def _fft_stage3(f3, bm, y, gate, bias_row):
    nb, m, cols = bm.shape
    r = f3.shape[0]
    tn = _pick(cols, (8192, 4096, 2048))
    return pl.pallas_call(
        _fft3_kernel,
        grid=(nb, cols // tn),
        in_specs=[pl.BlockSpec((r, m), lambda b, j: (0, 0)),
                  pl.BlockSpec((None, m, tn), lambda b, j: (b, 0, j)),
                  pl.BlockSpec((None, r, tn), lambda b, j: (b, 0, j)),
                  pl.BlockSpec((None, r, tn), lambda b, j: (b, 0, j)),
                  pl.BlockSpec((1, tn), lambda b, j: (0, j))],
        out_specs=pl.BlockSpec((None, r, tn), lambda b, j: (b, 0, j)),
        out_shape=jax.ShapeDtypeStruct((nb, r, cols), F32),
        compiler_params=_cparams(("arbitrary", "arbitrary")),
    )(f3, bm, y, gate, bias_row)


def _fft1_tok_kernel(f_ref, x_ref, o_ref, s_scr, *, rows):
    n1 = f_ref.shape[0]
    f = f_ref[...]
    pitch = n1 + FFT_PAD

    def body(n2, carry):
        xs = x_ref[pl.ds(n2, rows, stride=FFT_N2), :].astype(BF16)
        s_scr[pl.ds(pl.multiple_of(n2 * pitch, 8), n1), :] = jnp.dot(f, xs, preferred_element_type=F32)
        return carry

    lax.fori_loop(0, FFT_N2, body, 0, unroll=8)

    def emit(k1, carry):
        o_ref[k1] = s_scr[pl.ds(k1, FFT_N2, stride=pitch), :].astype(o_ref.dtype)
        return carry

    lax.fori_loop(0, n1, emit, 0, unroll=4)


def _fft_stage1_tok(f1, x, nb):
    c = x.shape[1]
    rows = f1.shape[1]
    n = rows * FFT_N2
    n1 = f1.shape[0] // 2
    return pl.pallas_call(
        functools.partial(_fft1_tok_kernel, rows=rows),
        grid=(nb, c // LANE, 2),
        in_specs=[pl.BlockSpec((None, n1, rows), lambda b, j, r: (r, 0, 0)),
                  pl.BlockSpec((n, LANE), lambda b, j, r: (b, j))],
        out_specs=pl.BlockSpec((None, None, n1, FFT_N2, LANE), lambda b, j, r: (b, r, 0, 0, j)),
        out_shape=jax.ShapeDtypeStruct((nb, 2, n1, FFT_N2, c), BF16),
        scratch_shapes=[pltpu.VMEM((FFT_N2 * (n1 + FFT_PAD), LANE), F32)],
        compiler_params=_cparams(("arbitrary", "arbitrary", "arbitrary")),
    )(f1.reshape(2, n1, rows), x)


def _fft3_tok_kernel(f_ref, bm_ref, y_ref, gate_ref, bias_ref, o_ref, s_scr, t_scr, *, rows):
    half = pl.program_id(2)
    n1 = bm_ref.shape[0]
    sp, tp = FFT_N2 + FFT_PAD, rows + FFT_PAD

    def stage(k1, carry):
        s_scr[pl.ds(pl.multiple_of(k1 * sp, 8), FFT_N2), :] = bm_ref[k1].astype(F32)
        return carry

    lax.fori_loop(0, n1, stage, 0, unroll=4)
    f = f_ref[...]

    def part(n2):
        return jnp.dot(f, s_scr[pl.ds(n2, n1, stride=sp), :].astype(BF16), preferred_element_type=F32)

    def dst(n2):
        return pl.ds(pl.multiple_of(n2 * tp, 8), rows)

    @pl.when(half == 0)
    def _():
        def body(n2, carry):
            t_scr[dst(n2), :] = part(n2)
            return carry
        lax.fori_loop(0, FFT_N2, body, 0, unroll=8)

    @pl.when(half == 1)
    def _():
        def body(n2, carry):
            t_scr[dst(n2), :] += part(n2)
            return carry
        lax.fori_loop(0, FFT_N2, body, 0, unroll=8)

        def emit(r, carry):
            tok = pl.ds(pl.multiple_of(r * FFT_N2, FFT_N2), FFT_N2)
            conv = t_scr[pl.ds(r, FFT_N2, stride=tp), :]
            o_ref[tok, :] = gate_ref[tok, :] * (conv + y_ref[tok, :] * bias_ref[...])
            return carry
        lax.fori_loop(0, rows, emit, 0, unroll=2)


def _fft_stage3_tok(f3, bm, y, gate, bias):
    nb, _, n1, _, c = bm.shape
    rows = f3.shape[0]
    n = rows * FFT_N2
    tok = pl.BlockSpec((n, LANE), lambda b, j, r: (b, j))
    return pl.pallas_call(
        functools.partial(_fft3_tok_kernel, rows=rows),
        grid=(nb, c // LANE, 2),
        in_specs=[pl.BlockSpec((None, rows, n1), lambda b, j, r: (r, 0, 0)),
                  pl.BlockSpec((None, None, n1, FFT_N2, LANE), lambda b, j, r: (b, r, 0, 0, j)),
                  tok, tok,
                  pl.BlockSpec((1, LANE), lambda b, j, r: (0, j))],
        out_specs=tok,
        out_shape=jax.ShapeDtypeStruct((nb * n, c), F32),
        scratch_shapes=[pltpu.VMEM((n1 * (FFT_N2 + FFT_PAD), LANE), F32),
                        pltpu.VMEM((FFT_N2 * (rows + FFT_PAD), LANE), F32)],
        compiler_params=_cparams(("arbitrary", "arbitrary", "arbitrary")),
    )(f3.reshape(rows, 2, n1).transpose(1, 0, 2), bm, y, gate, bias.astype(F32).reshape(1, c))


def _hyena_long(v, x1, x2, B, L, hy, bias):
    r_in = L // FFT_N2
    n1 = 2 * r_in
    f1, f3, g, gt = _dft_tables(n1, r_in)
    f1k = _dft_tables(n1, n1)[0]
    k, inv = _hyena_filters(L, n1 * FFT_N2, *hy)
    khat = _fft_filter_spectrum(_fft_stage1_tok(f1k, k.reshape(HY_ORDER * n1 * FFT_N2, HY_CH), HY_ORDER), g, inv)
    y = v
    for o, gate in enumerate((x1, x2)):
        bm = _fft_mid(_fft_stage1_tok(f1, y, B), g, gt, khat[o])
        y = _fft_stage3_tok(f3, bm, y, gate, bias[o])
    return y


def _hyena_sequence(v, x1, x2, hy, bias):
    B, L, C = v.shape
    r_valid = L // FFT_N2
    r_in = max(r_valid, 16)
    n1 = max(2 * r_valid, r_in)
    n = n1 * FFT_N2
    if r_valid == r_in:
        return _hyena_long(v.reshape(B * L, C), x1.reshape(B * L, C), x2.reshape(B * L, C), B, L, hy,
                           bias).reshape(B, L, C)
    f1, f3, g, gt = _dft_tables(n1, r_in)
    f1k = _dft_tables(n1, n1)[0]
    k, inv = _hyena_filters(L, n, *hy)
    ak = _fft_stage1(f1k, k.reshape(HY_ORDER, n1, FFT_N2 * C))
    khat = _fft_filter_spectrum(ak.reshape(HY_ORDER, 2, n1, FFT_N2, C), g, inv)
    cols = FFT_N2 * C

    def view(a):
        a = a.reshape(B, r_valid, cols)
        return a if r_in == r_valid else jnp.pad(a, ((0, 0), (0, r_in - r_valid), (0, 0)))

    y = view(v)
    for o, gate in enumerate((view(x1), view(x2))):
        a = _fft_stage1(f1, y)
        bm = _fft_mid(a.reshape(B, 2, n1, FFT_N2, C), g, gt, khat[o])
        y = _fft_stage3(f3, bm.reshape(B, 2 * n1, cols), y, gate, jnp.tile(bias[o].astype(F32), FFT_N2)[None, :])
    return y[:, :r_valid].reshape(B, L, C)


def _s5_tables(a_re, a_im, log_dt, b_re, b_im, c_re, c_im, jj_ctx, jj_lat):
    lam = lax.complex(jnp.minimum(a_re.astype(F32), -1e-4), a_im.astype(F32))
    dt = jnp.exp(log_dt.astype(F32))[..., None]
    abar = jnp.exp(lam * dt)
    bbar = ((abar - 1.0) / lam)[..., None] * lax.complex(b_re.astype(F32), b_im.astype(F32))
    cmat = lax.complex(c_re.astype(F32), c_im.astype(F32))
    T = S5_T

    def powers(m):
        m = jnp.asarray(m, F32)
        return jnp.exp(lam * dt * m.reshape(m.shape + (1, 1, 1)))

    pw = powers(jnp.arange(T + 1))
    kt = jnp.real(jnp.einsum('dgop,tdgp,dgpi->tdgoi', cmat, pw[:T], bbar, precision=HI))
    tt = jnp.arange(T)
    lag = tt[None, :] - tt[:, None]
    w_intra = jnp.where((lag >= 0)[:, :, None, None, None, None],
                        kt[jnp.clip(lag, 0, T - 1)], 0.0)
    w_intra = jnp.stack([w_intra[:, :, 0], w_intra[::-1, ::-1, 1]], axis=2)
    w_intra = w_intra.transpose(2, 3, 0, 5, 1, 4).reshape(2, S5_NG, T * S5_GROUP, T * S5_GROUP)
    wb = pw[T - 1 - tt][..., None] * bbar[None]
    wb = jnp.stack([wb[:, 0], wb[::-1, 1]], axis=1)
    wb = wb.transpose(1, 2, 0, 4, 3).reshape(2, S5_NG, T * S5_GROUP, S5_P)
    w_cat = jnp.concatenate([w_intra, jnp.real(wb), jnp.imag(wb)], axis=-1)
    cp = cmat[None] * pw[1:, :, :, None, :]
    cp = jnp.stack([cp[:, 0], cp[::-1, 1]], axis=1)
    cp = cp.transpose(1, 2, 4, 0, 3).reshape(2, S5_NG, S5_P, T * S5_GROUP)
    c_cat = jnp.concatenate([jnp.real(cp), -jnp.imag(cp)], axis=2)

    def coef(z):
        zr, zi = jnp.real(z), jnp.imag(z)
        return jnp.stack([jnp.concatenate([zr, zr], -1), jnp.concatenate([-zi, zi], -1)], axis=-2)

    step = coef(powers(jnp.array(T)))
    seg_c, seg_l = (coef(powers(jnp.array(T * n))) for n in (jj_ctx, jj_lat))
    coefs = jnp.pad(jnp.concatenate([step, seg_c, seg_l], axis=2), ((0, 0), (0, 0), (0, 2), (0, 0)))
    ptab = coef(powers(T * jnp.arange(max(jj_ctx, jj_lat)))).transpose(1, 2, 0, 3, 4)
    return w_cat.astype(BF16), c_cat.astype(BF16), coefs, ptab


def _cmul(coef_a, coef_b, s):
    return coef_a * s + coef_b * pltpu.roll(s, S5_P, axis=1)


S5_GPB = LANE // S5_GROUP
S5_TC = S5_T * S5_GROUP


def _s5_in_kernel(z_ref, w_ref, yi_ref, ds_ref):
    nj = z_ref.shape[0] // S5_T
    ws = [z_ref[pl.ds(t, nj, stride=S5_T), :].T for t in range(S5_T)]
    for g in range(S5_GPB):
        vt = jnp.concatenate([w[S5_GROUP * g:S5_GROUP * (g + 1), :] for w in ws], axis=0)
        v = vt.T.astype(BF16)
        for d in range(2):
            o = jnp.dot(v, w_ref[d, g], preferred_element_type=F32)
            yi_ref[d, g] = o[:, :S5_TC]
            ds_ref[d, g] = o[:, S5_TC:]


def _s5_in(z, w_cat, nj):
    T = z.shape[0]
    R = T // S5_T
    col0 = HY_COLS // LANE
    return pl.pallas_call(
        _s5_in_kernel,
        grid=(R // nj, S5_CH // LANE),
        in_specs=[pl.BlockSpec((nj * S5_T, LANE), lambda i, c: (i, col0 + c)),
                  pl.BlockSpec((2, S5_GPB, S5_TC, S5_TC + 2 * S5_P), lambda i, c: (0, c, 0, 0))],
        out_specs=[pl.BlockSpec((2, S5_GPB, nj, S5_TC), lambda i, c: (0, c, i, 0)),
                   pl.BlockSpec((2, S5_GPB, nj, 2 * S5_P), lambda i, c: (0, c, i, 0))],
        out_shape=[jax.ShapeDtypeStruct((2, S5_NG, R, S5_TC), F32),
                   jax.ShapeDtypeStruct((2, S5_NG, R, 2 * S5_P), F32)],
        compiler_params=_cparams(("arbitrary", "arbitrary")),
    )(z, w_cat)


def _s5_scan_kernel(ds_ref, yi_ref, c_ref, cf_ref, p_ref, y_ref, sp_scr, *, parts, reverse):
    a1, a2 = cf_ref[0:1, :], cf_ref[1:2, :]
    rid = lax.broadcasted_iota(jnp.int32, (S5_SEG, 2 * S5_P), 0)
    first, last = (S5_SEG - 1, 0) if reverse else (0, S5_SEG - 1)
    shift = S5_SEG - 1 if reverse else 1
    nb = len(parts[0][0])
    zero = jnp.zeros((S5_SEG, 2 * S5_P), F32)
    s0 = [zero] * nb
    for pi, (bases, jj) in enumerate(parts):
        g1, g2 = cf_ref[2 + 2 * pi:3 + 2 * pi, :], cf_ref[3 + 2 * pi:4 + 2 * pi, :]

        def rows(b, k, bases=bases, jj=jj):
            return pl.ds(bases[b] + (jj - 1 - k if reverse else k), S5_SEG, stride=jj)

        def local_step(k, states, rows=rows):
            new = []
            for b in range(nb):
                sp_scr[rows(b, k), :] = states[b]
                new.append(_cmul(a1, a2, states[b]) + ds_ref[rows(b, k), :])
            return tuple(new)

        ends = lax.fori_loop(0, jj, local_step, (zero,) * nb)
        carries, nxt_s0 = [], []
        for b in range(nb):
            c = jnp.where(rid == first, s0[b], 0.0)
            for _ in range(S5_SEG - 1):
                c = jnp.where(rid == first, s0[b], pltpu.roll(ends[b] + _cmul(g1, g2, c), shift, axis=0))
            fin = ends[b] + _cmul(g1, g2, c)
            nxt_s0.append(jnp.broadcast_to(fin[last:last + 1, :], fin.shape))
            carries.append((c, pltpu.roll(c, S5_P, axis=1)))

        def fix_step(k, carry, rows=rows, carries=carries):
            p = p_ref[k]
            for b in range(nb):
                c, cs = carries[b]
                sp_scr[rows(b, k), :] += p[0:1, :] * c + p[1:2, :] * cs
            return carry

        lax.fori_loop(0, jj, fix_step, 0)
        s0 = nxt_s0
    y_ref[...] = yi_ref[...] + jnp.dot(sp_scr[...].astype(BF16), c_ref[...], preferred_element_type=F32)


def _s5_scan(d, ds, yi, c_cat, coefs, ptab, parts):
    R = ds.shape[2]
    jjm = ptab.shape[2]
    kern = functools.partial(_s5_scan_kernel, parts=parts, reverse=(d == 1))
    return pl.pallas_call(
        kern,
        grid=(S5_NG,),
        in_specs=[pl.BlockSpec((None, None, R, 2 * S5_P), lambda g: (d, g, 0, 0)),
                  pl.BlockSpec((None, None, R, S5_TC), lambda g: (d, g, 0, 0)),
                  pl.BlockSpec((None, None, 2 * S5_P, S5_TC), lambda g: (d, g, 0, 0)),
                  pl.BlockSpec((None, None, 8, 2 * S5_P), lambda g: (d, g, 0, 0)),
                  pl.BlockSpec((None, None, jjm, 2, 2 * S5_P), lambda g: (d, g, 0, 0, 0))],
        out_specs=pl.BlockSpec((None, R, S5_TC), lambda g: (g, 0, 0)),
        out_shape=jax.ShapeDtypeStruct((S5_NG, R, S5_TC), F32),
        scratch_shapes=[pltpu.VMEM((R, 2 * S5_P), F32)],
        compiler_params=_cparams(("arbitrary",)),
    )(ds, yi, c_cat, coefs, ptab)


def _s5_out_kernel(yf_ref, yb_ref, o_ref):
    nj = yf_ref.shape[1]
    yts = [(yf_ref[g] + yb_ref[g]).T for g in range(S5_GPB)]
    for t in range(S5_T):
        zt = jnp.concatenate([y[S5_GROUP * t:S5_GROUP * (t + 1), :] for y in yts], axis=0)
        o_ref[pl.ds(t, nj, stride=S5_T), :] = zt.T


def _s5_out(yf, yb, nj):
    R = yf.shape[1]
    spec = pl.BlockSpec((S5_GPB, nj, S5_TC), lambda i, c: (c, i, 0))
    return pl.pallas_call(
        _s5_out_kernel,
        grid=(R // nj, S5_CH // LANE),
        in_specs=[spec, spec],
        out_specs=pl.BlockSpec((nj * S5_T, LANE), lambda i, c: (i, c)),
        out_shape=jax.ShapeDtypeStruct((R * S5_T, S5_CH), F32),
        compiler_params=_cparams(("arbitrary", "arbitrary")),
    )(yf, yb)


def _s5_mixer(tok, z, s5):
    B = tok.B
    cl, cc = tok.L // S5_T, tok.CTX // S5_T
    jl, jc = cl // S5_SEG, cc // S5_SEG
    w_cat, c_cat, coefs, ptab = _s5_tables(*s5, jc, jl)
    yi, ds = _s5_in(z, w_cat, math.gcd(B * cl, B * cc, 64))
    lat = (tuple(b * cl for b in range(B)), jl)
    ctx = (tuple(B * cl + b * cc for b in range(B)), jc)
    yf = _s5_scan(0, ds, yi, c_cat, coefs, ptab, (ctx, lat))
    yb = _s5_scan(1, ds, yi, c_cat, coefs, ptab, (ctx, lat))
    return _s5_out(yf, yb, math.gcd(B * cl, B * cc, 64))


def _gelu_tanh(x):
    return 0.5 * x * (1.0 + jnp.tanh(math.sqrt(2.0 / math.pi) * (x + 0.044715 * (x * x * x))))


def _even_out_kernel(hl_ref, hc_ref, ys_ref, u_ref, h_ref, mod_ref, dsk_ref, wg_ref, wo_ref, o_ref, *, n_lat):
    y = _gelu_tanh(ys_ref[...] + dsk_ref[...] * u_ref[...])
    s = y * jax.nn.sigmoid(jnp.dot(y.astype(BF16), wg_ref[...], preferred_element_type=F32))
    hy = jnp.where(pl.program_id(0) < n_lat, hl_ref[...], hc_ref[...])
    ol = (jnp.dot(hy.astype(BF16), wo_ref[:HY_CH, :], preferred_element_type=F32)
          + jnp.dot(s.astype(BF16), wo_ref[HY_CH:, :], preferred_element_type=F32))
    o_ref[...] = h_ref[...] + mod_ref[2:3, :] * ol


def _even_out(tok, hl, hc, ys, z, h, mods_l, dsk, w_glu, w_out):
    tm = tok.tm
    return pl.pallas_call(
        functools.partial(_even_out_kernel, n_lat=tok.n_lat),
        grid=(tok.n_all,),
        in_specs=[pl.BlockSpec((tm, HY_CH), lambda i: (jnp.minimum(i, tok.n_lat - 1), 0)),
                  pl.BlockSpec((tm, HY_CH), lambda i: (jnp.maximum(i - tok.n_lat, 0), 0)),
                  pl.BlockSpec((tm, S5_CH), lambda i: (i, 0)),
                  pl.BlockSpec((tm, S5_CH), lambda i: (i, HY_COLS // S5_CH)),
                  pl.BlockSpec((tm, D), lambda i: (i, 0)),
                  pl.BlockSpec((None, N_MOD, D), lambda i: (tok.mod_row(i), 0, 0)),
                  pl.BlockSpec((1, S5_CH), lambda i: (0, 0)),
                  pl.BlockSpec((S5_CH, S5_CH), lambda i: (0, 0)),
                  pl.BlockSpec((D, D), lambda i: (0, 0))],
        out_specs=pl.BlockSpec((tm, D), lambda i: (i, 0)),
        out_shape=jax.ShapeDtypeStruct((tok.T, D), F32),
        compiler_params=_cparams(("arbitrary",)),
    )(hl, hc, ys, z, h, mods_l, dsk.reshape(1, S5_CH), w_glu, w_out)


def _ffn_kernel(h_ref, mod_ref, g_ref, wg_ref, wu_ref, wd_ref, o_ref, y_scr, acc_scr):
    j = pl.program_id(1)

    @pl.when(j == 0)
    def _():
        y_scr[...] = _ada_norm(h_ref[...], g_ref[...], mod_ref[...], 3, 4).astype(BF16)
        acc_scr[...] = jnp.zeros_like(acc_scr)

    y = y_scr[...]
    gate = jnp.dot(y, wg_ref[...], preferred_element_type=F32)
    up = jnp.dot(y, wu_ref[...], preferred_element_type=F32)
    act = (gate * jax.nn.sigmoid(gate) * up).astype(BF16)
    acc_scr[...] += jnp.dot(act, wd_ref[...], preferred_element_type=F32)

    @pl.when(j == pl.num_programs(1) - 1)
    def _():
        o_ref[...] = h_ref[...] + mod_ref[5:6, :] * acc_scr[...]


def _ffn(tok, h, mods_l, gain, wg, wu, wd):
    tm = tok.tm
    ff = wg.shape[1]
    tf = _pick(ff, (1408, 512, 256, 128))
    return pl.pallas_call(
        _ffn_kernel,
        grid=(tok.n_all, ff // tf),
        in_specs=[pl.BlockSpec((tm, D), lambda i, j: (i, 0)),
                  pl.BlockSpec((None, N_MOD, D), lambda i, j: (tok.mod_row(i), 0, 0)),
                  pl.BlockSpec((1, D), lambda i, j: (0, 0)),
                  pl.BlockSpec((D, tf), lambda i, j: (0, j)),
                  pl.BlockSpec((D, tf), lambda i, j: (0, j)),
                  pl.BlockSpec((tf, D), lambda i, j: (j, 0))],
        out_specs=pl.BlockSpec((tm, D), lambda i, j: (i, 0)),
        out_shape=jax.ShapeDtypeStruct((tok.T, D), F32),
        scratch_shapes=[pltpu.VMEM((tm, D), BF16), pltpu.VMEM((tm, D), F32)],
        compiler_params=_cparams(("arbitrary", "arbitrary")),
    )(h, mods_l, gain.reshape(1, D), wg, wu, wd)


def _even_layer(tok, h, mods_l, p):
    B, L, CTX = tok.B, tok.L, tok.CTX
    nl = B * L
    z = _norm_matmul(tok, h, mods_l, p['norm_mix'], p['w_in'].astype(BF16), EV_IN)
    v, x1, x2 = _short_conv(tok, z, p['conv_w'], p['conv_b'])
    ctx = lambda a: a[nl:].reshape(B, CTX, -1)
    if L % (16 * FFT_N2) == 0:
        hl = _hyena_long(v, x1, x2, B, L, p['hy'], p['hy_bias'])
    else:
        lat = lambda a: a[:nl].reshape(B, L, -1)
        hl = _hyena_sequence(lat(v), lat(x1), lat(x2), p['hy'], p['hy_bias']).reshape(nl, HY_CH)
    ys_all = _s5_mixer(tok, z, p['s5'])
    if p['need_ctx']:
        hc = _hyena_sequence(ctx(v), ctx(x1), ctx(x2), p['hy'], p['hy_bias']).reshape(B * CTX, HY_CH)
    else:
        hc = jnp.zeros((B * CTX, HY_CH), F32)
    h = _even_out(tok, hl, hc, ys_all, z, h, mods_l, p['s5_d'], p['s5_w_glu'].astype(BF16),
                  p['w_out'].astype(BF16))
    return _ffn(tok, h, mods_l, p['norm_ffn'], _to_bf16(*p['ff_wg']), _to_bf16(*p['ff_wu']), _to_bf16(*p['ff_wd']))


def _rope_tables(L, tm):
    t = jnp.arange(L)
    row = (t // GRID_W).astype(F32)[:, None]
    col = (t % GRID_W).astype(F32)[:, None]

    def pattern(dim):
        nf = dim // 4
        inv = ROPE_BASE ** (-jnp.arange(nf, dtype=F32) / nf)
        ar, ac = row * inv[None, :], col * inv[None, :]
        cos = jnp.concatenate([jnp.cos(ar)] * 2 + [jnp.cos(ac)] * 2, axis=1)
        z = jnp.zeros((L, nf), F32)
        s_up = jnp.concatenate([-jnp.sin(ar), z, -jnp.sin(ac), z], axis=1)
        s_dn = jnp.concatenate([z, jnp.sin(ar), z, jnp.sin(ac)], axis=1)
        return cos, s_up, s_dn

    def pad_mla(a, fill):
        return jnp.concatenate([jnp.full((L, MLA_NOPE), fill, F32), a,
                                jnp.full((L, HEAD_PAD - MLA_NOPE - MLA_ROPE), fill, F32)], axis=1)

    cm, um, dm = pattern(MLA_ROPE)
    cg, ug, dg = pattern(GQA_HD)
    mla = jnp.stack([pad_mla(cm, 1.0), pad_mla(um, 0.0), pad_mla(dm, 0.0)])
    gqa = jnp.stack([jnp.tile(cg, (1, 2)), jnp.tile(ug, (1, 2)), jnp.tile(dg, (1, 2))])
    ident = jnp.stack([jnp.ones((tm, LANE), F32), jnp.zeros((tm, LANE), F32), jnp.zeros((tm, LANE), F32)])
    return jnp.stack([jnp.concatenate([mla, ident], axis=1), jnp.concatenate([gqa, ident], axis=1)])


def _rope(x, tab, w):
    outs = []
    for h in range(x.shape[1] // LANE):
        xs = x[:, h * LANE:(h + 1) * LANE]
        outs.append(xs * tab[0] + pltpu.roll(xs, LANE - w, axis=1) * tab[1] + pltpu.roll(xs, w, axis=1) * tab[2])
    return outs[0] if len(outs) == 1 else jnp.concatenate(outs, axis=1)


def _rms(x, g):
    return x * lax.rsqrt(jnp.mean(x * x, axis=-1, keepdims=True) + EPS) * g


_O_CQ, _O_CKV, _O_GQ, _O_GK, _O_GV, _O_KR = 0, 256, 384, 896, 1024, 1152


def _odd_proj_kernel(z_ref, tab_ref, qn_ref, kvn_ref, wuq_ref, wuk_ref, wuv_ref, e_ref,
                     q_ref, k_ref, v_ref, gq_ref, gk_ref, gv_ref):
    z = z_ref[...]
    mt, gt = tab_ref[0], tab_ref[1]
    qn = _rms(z[:, _O_CQ:_O_CKV], qn_ref[...]).astype(BF16)
    q = jnp.dot(qn, wuq_ref[...], preferred_element_type=F32)
    q_ref[...] = (_rope(q, mt, MLA_ROPE // 4) * (MLA_SCALE * LOG2E)).astype(BF16)
    kvn = _rms(z[:, _O_CKV:_O_GQ], kvn_ref[...]).astype(BF16)
    k = (jnp.dot(kvn, wuk_ref[...], preferred_element_type=F32)
         + jnp.dot(z[:, _O_KR:], e_ref[...], precision=HI, preferred_element_type=F32))
    k_ref[...] = _rope(k, mt, MLA_ROPE // 4).astype(BF16)
    v_ref[...] = jnp.dot(kvn, wuv_ref[...], preferred_element_type=F32).astype(BF16)
    gq_ref[...] = (_rope(z[:, _O_GQ:_O_GK], gt, GQA_HD // 4) * (GQA_SCALE * LOG2E)).astype(BF16)
    gk_ref[...] = _rope(z[:, _O_GK:_O_GV], gt, GQA_HD // 4).astype(BF16)
    gv_ref[...] = z[:, _O_GV:_O_KR].astype(BF16)


def _odd_proj(tok, z, tabs, q_norm, kv_norm, w_uq, w_ukv):
    tm = tok.tm
    hq = MLA_HEADS * HEAD_PAD
    wq = jnp.pad(w_uq.reshape(Q_LORA, MLA_HEADS, MLA_NOPE + MLA_ROPE),
                 ((0, 0), (0, 0), (0, HEAD_PAD - MLA_NOPE - MLA_ROPE))).reshape(Q_LORA, hq).astype(BF16)
    wkv = w_ukv.reshape(KV_LORA, MLA_HEADS, MLA_NOPE + MLA_V)
    wk = jnp.pad(wkv[..., :MLA_NOPE], ((0, 0), (0, 0), (0, HEAD_PAD - MLA_NOPE))).reshape(KV_LORA, hq).astype(BF16)
    wv = wkv[..., MLA_NOPE:].reshape(KV_LORA, MLA_HEADS * MLA_V).astype(BF16)
    eye = jnp.eye(MLA_ROPE, dtype=F32)
    e_head = jnp.pad(eye, ((0, LANE - MLA_ROPE), (MLA_NOPE, HEAD_PAD - MLA_NOPE - MLA_ROPE)))
    e = jnp.tile(e_head, (1, MLA_HEADS))
    tab_blk = lambda i: (0, 0, jnp.where(i < tok.n_lat, i % tok.per_seq, tok.per_seq), 0)
    full = lambda shape: pl.BlockSpec(shape, lambda i: (0,) * len(shape))
    widths = (hq, hq, MLA_HEADS * MLA_V, GQA_HEADS * GQA_HD, GQA_KV * GQA_HD, GQA_KV * GQA_HD)
    return pl.pallas_call(
        _odd_proj_kernel,
        grid=(tok.n_all,),
        in_specs=[pl.BlockSpec((tm, OD_IN_PAD), lambda i: (i, 0)),
                  pl.BlockSpec((2, 3, tm, LANE), tab_blk),
                  full((1, Q_LORA)), full((1, KV_LORA)), full((Q_LORA, hq)), full((KV_LORA, hq)),
                  full((KV_LORA, MLA_HEADS * MLA_V)), full((LANE, hq))],
        out_specs=[pl.BlockSpec((tm, w), lambda i: (i, 0)) for w in widths],
        out_shape=[jax.ShapeDtypeStruct((tok.T, w), BF16) for w in widths],
        compiler_params=_cparams(("arbitrary",)),
    )(z, tabs, q_norm.reshape(1, -1), kv_norm.reshape(1, -1), wq, wk, wv, e)


def _mla_attn_kernel(q_ref, k_ref, vt_ref, o_ref, s_scr, p_scr, *, tk, nk, ks):
    tq = q_ref.shape[0]
    nslab = tk // ks
    kq, kv = math.gcd(tk, MLA_KS_QK), math.gcd(tk, MLA_KS_PV)
    dn = (((1,), (1,)), ((), ()))
    qs = [q_ref[:, h * HEAD_PAD:(h + 1) * HEAD_PAD] for h in range(2)]

    def qk_slab(h, c, j, mx):
        if (j * ks) % kq:
            return mx
        r = pl.multiple_of(c * tk + j * ks, ks)
        s = lax.dot_general(k_ref[pl.ds(r, kq), h * HEAD_PAD:(h + 1) * HEAD_PAD], qs[h], dn,
                            preferred_element_type=F32)
        s_scr[h, j * ks:j * ks + kq, :] = s
        return jnp.maximum(mx, jnp.max(s, axis=0, keepdims=True))

    def pv_slab(h, c, j):
        if (j * ks) % kv:
            return 0.0
        return jnp.dot(vt_ref[c, h * MLA_VP:(h + 1) * MLA_VP, j * ks:j * ks + kv], p_scr[h, j * ks:j * ks + kv, :],
                       preferred_element_type=F32)

    def step(x, c_sm, c_pv, c_qk, st):
        y = 1 - x
        m, acc, mc = st[x]
        m_new = jnp.maximum(m, mc)
        alpha = jnp.exp2(m - m_new)
        acc_y = st[y][1]
        mx_y = jnp.full((1, tq), NEG, F32)
        for j in range(nslab):
            acc_y = acc_y + pv_slab(y, c_pv, j)
            mx_y = qk_slab(y, c_qk, j, mx_y)
            p_scr[x, j * ks:(j + 1) * ks, :] = jnp.exp2(s_scr[x, j * ks:(j + 1) * ks, :] - m_new).astype(BF16)
        new = [None, None]
        new[x] = (m_new, alpha * acc, mc)
        new[y] = (st[y][0], acc_y, mx_y)
        return tuple(new)

    def body(c, st):
        st = step(0, c, jnp.maximum(c - 1, 0), c, st)
        return step(1, c, c, jnp.minimum(c + 1, nk - 1), st)

    neg = jnp.full((1, tq), NEG, F32)
    acc0 = jnp.zeros((MLA_VP, tq), F32)
    p_scr[1] = jnp.zeros(p_scr.shape[1:], BF16)
    mx0 = neg
    for j in range(nslab):
        mx0 = qk_slab(0, 0, j, mx0)
    def trip(i, st):
        for u in range(MLA_UNROLL):
            st = body(i * MLA_UNROLL + u, st)
        return st

    st = lax.fori_loop(0, nk // MLA_UNROLL, trip, ((neg, acc0, mx0), (neg, acc0, neg)))
    for c in range(nk - nk % MLA_UNROLL, nk):
        st = body(jnp.int32(c), st)
    acc1 = st[1][1]
    for j in range(nslab):
        acc1 = acc1 + pv_slab(1, nk - 1, j)
    out_t = jnp.concatenate([a[:MLA_V] / a[MLA_V:MLA_V + 1] for a in (st[0][1], acc1)], axis=0)
    o_ref[...] = out_t.T.astype(o_ref.dtype)


MLA_TQ = (512, 256, 128)
MLA_TK = (768, 512, 256, 128)
MLA_KS = 128
MLA_KS_QK = 768
MLA_KS_PV = 256
MLA_UNROLL = 11


def _mla_attention(q, k, v):
    B, Lq, _ = q.shape
    Nk = k.shape[1]
    tq = _pick(Lq, MLA_TQ)
    tk = _pick(Nk, MLA_TK)
    nk = Nk // tk
    hp = MLA_HEADS // 2
    vt = v.reshape(B, nk, tk, hp, 2, MLA_V).transpose(0, 3, 1, 4, 5, 2)
    vt = jnp.concatenate([vt, jnp.ones((B, hp, nk, 2, MLA_VP - MLA_V, tk), v.dtype)], axis=4)
    vt = vt.reshape(B, hp, nk, 2 * MLA_VP, tk)
    kern = functools.partial(_mla_attn_kernel, tk=tk, nk=nk, ks=math.gcd(tk, MLA_KS))
    return pl.pallas_call(
        kern,
        grid=(B, hp, Lq // tq),
        in_specs=[pl.BlockSpec((None, tq, 2 * HEAD_PAD), lambda b, h, i: (b, i, h)),
                  pl.BlockSpec((None, Nk, 2 * HEAD_PAD), lambda b, h, i: (b, 0, h)),
                  pl.BlockSpec((None, None, nk, 2 * MLA_VP, tk), lambda b, h, i: (b, h, 0, 0, 0))],
        out_specs=pl.BlockSpec((None, tq, 2 * MLA_V), lambda b, h, i: (b, i, h)),
        out_shape=jax.ShapeDtypeStruct((B, Lq, MLA_HEADS * MLA_V), BF16),
        scratch_shapes=[pltpu.VMEM((2, tk, tq), F32), pltpu.VMEM((2, tk, tq), BF16)],
        compiler_params=_cparams(("arbitrary", "arbitrary", "arbitrary")),
    )(q, k, vt)


def _gqa_kernel(sink_ref, q_ref, kc_ref, vct_ref, *rest, L, has_band):
    group = GQA_HEADS // GQA_KV
    gw = group * BLK
    if has_band:
        kp_ref, k_ref, kn_ref, vtp_ref, vt_ref, vtn_ref, bias_ref, o_ref = rest
        keys = jnp.concatenate([kp_ref[...], k_ref[...], kn_ref[...], kc_ref[...]], axis=0)
        vals_t = jnp.concatenate([vtp_ref[...], vt_ref[...], vtn_ref[...], vct_ref[...]], axis=1)
    else:
        (o_ref,) = rest
        keys, vals_t = kc_ref[...], vct_ref[...]
    dn = (((1,), (1,)), ((), ()))
    G = range(GQA_KV)
    qs = [jnp.concatenate([q_ref[:, (kh * group + g) * GQA_HD:(kh * group + g + 1) * GQA_HD] for g in range(group)],
                          axis=0) for kh in G]
    s = [lax.dot_general(keys[:, kh * GQA_HD:(kh + 1) * GQA_HD], qs[kh], dn, preferred_element_type=F32) for kh in G]
    if has_band:
        s = [x + bias_ref[...] for x in s]
    sink = [sink_ref[:, kh * gw:(kh + 1) * gw] for kh in G]
    m = [jnp.maximum(jnp.max(s[kh], axis=0, keepdims=True), sink[kh]) for kh in G]
    p = [jnp.exp2(s[kh] - m[kh]) for kh in G]
    den = [jnp.sum(p[kh], axis=0, keepdims=True) + jnp.exp2(sink[kh] - m[kh]) for kh in G]
    ot = [jnp.dot(vals_t[kh * GQA_HD:(kh + 1) * GQA_HD, :], p[kh].astype(BF16), preferred_element_type=F32)
          * (1.0 / den[kh]) for kh in G]
    o = jnp.concatenate(ot, axis=0).T
    o_ref[...] = jnp.concatenate([o[g * BLK:(g + 1) * BLK, kh * GQA_HD:(kh + 1) * GQA_HD]
                                  for kh in G for g in range(group)], axis=1).astype(o_ref.dtype)


def _gqa_attention(sink, q, kc, vc, k=None, v=None):
    B, Lq, _ = q.shape
    CTX = kc.shape[1]
    has_band = k is not None
    kw = GQA_KV * GQA_HD
    nb = Lq // BLK
    sink_row = jnp.repeat(sink.astype(F32) * LOG2E, BLK)[None, :]
    in_specs = [pl.BlockSpec((1, GQA_HEADS * BLK), lambda b, i: (0, 0)),
                pl.BlockSpec((None, BLK, GQA_HEADS * GQA_HD), lambda b, i: (b, i, 0)),
                pl.BlockSpec((None, CTX, kw), lambda b, i: (b, 0, 0)),
                pl.BlockSpec((None, kw, CTX), lambda b, i: (b, 0, 0))]
    args = [sink_row, q, kc, jnp.swapaxes(vc, 1, 2)]
    if has_band:
        prev = lambda i: jnp.maximum(i - 1, 0)
        nxt = lambda i: jnp.minimum(i + 1, nb - 1)
        in_specs += [pl.BlockSpec((None, BLK, kw), lambda b, i: (b, prev(i), 0)),
                     pl.BlockSpec((None, BLK, kw), lambda b, i: (b, i, 0)),
                     pl.BlockSpec((None, BLK, kw), lambda b, i: (b, nxt(i), 0)),
                     pl.BlockSpec((None, kw, BLK), lambda b, i: (b, 0, prev(i))),
                     pl.BlockSpec((None, kw, BLK), lambda b, i: (b, 0, i)),
                     pl.BlockSpec((None, kw, BLK), lambda b, i: (b, 0, nxt(i)))]
        vt = jnp.swapaxes(v, 1, 2)
        r = jnp.arange(3 * BLK + CTX)[:, None]
        c = jnp.arange(GQA_HEADS // GQA_KV * BLK)[None, :] % BLK
        band = (jnp.abs(r - BLK - c) <= WINDOW) | (r >= 3 * BLK)
        ok = jnp.stack([band & (r >= BLK), band, band & ((r < 2 * BLK) | (r >= 3 * BLK))])
        bias = jnp.where(ok, 0.0, NEG).astype(F32)
        in_specs += [pl.BlockSpec((None,) + bias.shape[1:],
                                  lambda b, i: (jnp.where(i == 0, 0, jnp.where(i == nb - 1, 2, 1)), 0, 0))]
        args += [k, k, k, vt, vt, vt, bias]
    kern = functools.partial(_gqa_kernel, L=Lq, has_band=has_band)
    return pl.pallas_call(
        kern,
        grid=(B, nb),
        in_specs=in_specs,
        out_specs=pl.BlockSpec((None, BLK, GQA_HEADS * GQA_HD), lambda b, i: (b, i, 0)),
        out_shape=jax.ShapeDtypeStruct((B, Lq, GQA_HEADS * GQA_HD), BF16),
        compiler_params=_cparams(("arbitrary", "arbitrary")),
    )(*args)


def _odd_out_kernel(a_ref, g_ref, h_ref, mod_ref, wo_ref, o_ref):
    half = a_ref.shape[1]
    ol = (jnp.dot(a_ref[...], wo_ref[:half, :], preferred_element_type=F32)
          + jnp.dot(g_ref[...], wo_ref[half:, :], preferred_element_type=F32))
    o_ref[...] = h_ref[...] + mod_ref[2:3, :] * ol


def _odd_out(tok, mla, gqa, h, mods_l, w_out):
    tm = tok.tm
    half = mla.shape[1]
    return pl.pallas_call(
        _odd_out_kernel,
        grid=(tok.n_all,),
        in_specs=[pl.BlockSpec((tm, half), lambda i: (i, 0)),
                  pl.BlockSpec((tm, half), lambda i: (i, 0)),
                  pl.BlockSpec((tm, D), lambda i: (i, 0)),
                  pl.BlockSpec((None, N_MOD, D), lambda i: (tok.mod_row(i), 0, 0)),
                  pl.BlockSpec((D, D), lambda i: (0, 0))],
        out_specs=pl.BlockSpec((tm, D), lambda i: (i, 0)),
        out_shape=jax.ShapeDtypeStruct((tok.T, D), F32),
        compiler_params=_cparams(("arbitrary",)),
    )(mla, gqa, h, mods_l, w_out)


MOE_TR = 512
MOE_TF = 1792
MOE_IDX_BLK = 1024
MOE_NF = EXP_FF // MOE_TF
ROUTE_W = 8


def _router_kernel(h_ref, mod_ref, g_ref, r_ref, y_ref, route_ref):
    y = _ada_norm(h_ref[...], g_ref[...], mod_ref[...], 3, 4)
    y_ref[...] = y
    logits = jnp.dot(y, r_ref[...], precision=HI, preferred_element_type=F32)
    lane = lax.broadcasted_iota(jnp.int32, logits.shape, 1)
    lg = jnp.where(lane < N_EXP, logits, -jnp.inf)
    m1 = jnp.max(lg, axis=-1, keepdims=True)
    i1 = jnp.min(jnp.where(lg == m1, lane, LANE), axis=-1, keepdims=True)
    lg2 = jnp.where(lane == i1, -jnp.inf, lg)
    m2 = jnp.max(lg2, axis=-1, keepdims=True)
    i2 = jnp.min(jnp.where(lg2 == m2, lane, LANE), axis=-1, keepdims=True)
    e = jnp.exp(m2 - m1)
    w1 = 1.0 / (1.0 + e)
    route = (jnp.where(lane == 0, w1, 0.0) + jnp.where(lane == 1, e * w1, 0.0)
             + jnp.where(lane == 2, i1.astype(F32), 0.0) + jnp.where(lane == 3, i2.astype(F32), 0.0))
    route_ref[...] = route[:, :ROUTE_W]


def _moe_router(tok, h, mods_l, gain, router):
    tm = tok.tm
    rp = jnp.pad(router, ((0, 0), (0, LANE - N_EXP)))
    return pl.pallas_call(
        _router_kernel,
        grid=(tok.n_all,),
        in_specs=[pl.BlockSpec((tm, D), lambda i: (i, 0)),
                  pl.BlockSpec((None, N_MOD, D), lambda i: (tok.mod_row(i), 0, 0)),
                  pl.BlockSpec((1, D), lambda i: (0, 0)),
                  pl.BlockSpec((D, LANE), lambda i: (0, 0))],
        out_specs=[pl.BlockSpec((tm, D), lambda i: (i, 0)), pl.BlockSpec((tm, ROUTE_W), lambda i: (i, 0))],
        out_shape=[jax.ShapeDtypeStruct((tok.T, D), F32), jax.ShapeDtypeStruct((tok.T, ROUTE_W), F32)],
        compiler_params=_cparams(("arbitrary",)),
    )(h, mods_l, gain.reshape(1, D), rp)


def _moe_plan(route, tr):
    T = route.shape[0]
    flat = route[:, 2:4].astype(jnp.int32).reshape(-1)
    onehot = (flat[:, None] == jnp.arange(N_EXP, dtype=jnp.int32)[None, :]).astype(jnp.int32)
    csum = jnp.cumsum(onehot, axis=0)
    rank = jnp.sum((csum - onehot) * onehot, axis=1)
    padded = (csum[-1] + tr - 1) // tr * tr
    ends = jnp.cumsum(padded)
    pos = (ends - padded)[flat] + rank
    tpb = MOE_IDX_BLK // tr
    n_tiles = -(-((2 * T + N_EXP * (tr - 1)) // tr) // tpb) * tpb
    src = jnp.zeros((n_tiles * tr,), jnp.int32).at[pos].set(jnp.arange(2 * T, dtype=jnp.int32) // 2,
                                                            unique_indices=True)
    starts = jnp.arange(n_tiles, dtype=jnp.int32) * tr
    tile_expert = jnp.minimum(jnp.sum((starts[:, None] >= ends[None, :]).astype(jnp.int32), axis=1), N_EXP - 1)
    n_valid = (ends[-1] // tr).astype(jnp.int32).reshape(1)
    return src, tile_expert.astype(jnp.int32), n_valid, pos.reshape(T, 2)


def _gather_rows(idx_ref, src_hbm, dst, sem, n):
    def issue(r, carry):
        pltpu.make_async_copy(src_hbm.at[pl.ds(idx_ref[r], 1), :], dst.at[pl.ds(r, 1), :], sem).start()
        return carry

    lax.fori_loop(0, n, issue, 0, unroll=8)


def _wait_rows(src_hbm, dst, sem, n):
    pltpu.make_async_copy(src_hbm.at[pl.ds(0, n), :], dst, sem).wait()


def _moe_expert_kernel(te_ref, nv_ref, idx_ref, idxn_ref, y_hbm, wg_ref, wu_ref, wd_ref, o_ref,
                       xbuf, y_scr, acc_scr, sem, *, tr):
    t, f = pl.program_id(0), pl.program_id(1)
    nf = pl.num_programs(1)
    nv = nv_ref[0]
    valid = t < nv
    slot = lax.rem(t, 2)
    per_f = tr // MOE_NF
    tpb = MOE_IDX_BLK // tr
    nxt_tile = jnp.minimum(t + 1, jnp.maximum(nv - 1, 0))

    @pl.when((f == 0) & (t == 0))
    def _():
        _gather_rows(idx_ref, y_hbm, xbuf.at[0], sem.at[0], tr)

    @pl.when((f == 0) & valid)
    def _():
        _wait_rows(y_hbm, xbuf.at[slot], sem.at[slot], tr)
        y_scr[...] = xbuf[slot].astype(BF16)
        acc_scr[...] = jnp.zeros_like(acc_scr)

    @pl.when(valid)
    def _():
        base = f * per_f
        ibase = lax.rem(nxt_tile, tpb) * tr + base
        nxt = xbuf.at[1 - slot]
        for r in range(per_f):
            pltpu.make_async_copy(y_hbm.at[pl.ds(idxn_ref[ibase + r], 1), :], nxt.at[pl.ds(base + r, 1), :],
                                  sem.at[1 - slot]).start()
        y = y_scr[...]
        gate = jnp.dot(y, wg_ref[...], preferred_element_type=F32)
        up = jnp.dot(y, wu_ref[...], preferred_element_type=F32)
        act = (gate * jax.nn.sigmoid(gate) * up).astype(BF16)
        acc_scr[...] += jnp.dot(act, wd_ref[...], preferred_element_type=F32)

    @pl.when((f == nf - 1) & (t == nv - 1))
    def _():
        _wait_rows(y_hbm, xbuf.at[1 - slot], sem.at[1 - slot], tr)

    @pl.when(f == nf - 1)
    def _():
        o_ref[...] = jnp.where(valid, acc_scr[...], 0.0)


def _moe_experts(y, src, tile_expert, n_valid, wg, wu, wd, tr):
    n_tiles = src.shape[0] // tr
    tf = MOE_TF
    tpb = MOE_IDX_BLK // tr
    kern = functools.partial(_moe_expert_kernel, tr=tr)
    smem = functools.partial(pl.BlockSpec, memory_space=pltpu.SMEM)
    grid_spec = pltpu.PrefetchScalarGridSpec(
        num_scalar_prefetch=2,
        grid=(n_tiles, EXP_FF // tf),
        in_specs=[smem((MOE_IDX_BLK,), lambda t, f, te, nv: (t // tpb,)),
                  smem((MOE_IDX_BLK,), lambda t, f, te, nv: (jnp.minimum(t + 1, jnp.maximum(nv[0] - 1, 0)) // tpb,)),
                  pl.BlockSpec(memory_space=pl.ANY),
                  pl.BlockSpec((None, D, tf), lambda t, f, te, nv: (te[t], 0, f)),
                  pl.BlockSpec((None, D, tf), lambda t, f, te, nv: (te[t], 0, f)),
                  pl.BlockSpec((None, tf, D), lambda t, f, te, nv: (te[t], f, 0))],
        out_specs=pl.BlockSpec((tr, D), lambda t, f, te, nv: (t, 0)),
        scratch_shapes=[pltpu.VMEM((2, tr, D), F32), pltpu.VMEM((tr, D), BF16), pltpu.VMEM((tr, D), F32),
                        pltpu.SemaphoreType.DMA((2,))])
    return pl.pallas_call(
        kern,
        grid_spec=grid_spec,
        out_shape=jax.ShapeDtypeStruct((n_tiles * tr, D), F32),
        compiler_params=_cparams(("arbitrary", "arbitrary")),
    )(tile_expert, n_valid, src, src, y, wg, wu, wd)


def _moe_combine_kernel(idx_ref, idxn_ref, o_hbm, h_ref, route_ref, mod_ref, out_ref, buf, sem, *, tm):
    i = pl.program_id(0)
    n = pl.num_programs(0)
    slot = lax.rem(i, 2)

    @pl.when(i == 0)
    def _():
        _gather_rows(idx_ref, o_hbm, buf.at[0], sem.at[0], 2 * tm)

    nxt = buf.at[1 - slot]
    for r in range(2 * tm):
        pltpu.make_async_copy(o_hbm.at[pl.ds(idxn_ref[r], 1), :], nxt.at[pl.ds(r, 1), :], sem.at[1 - slot]).start()
    _wait_rows(o_hbm, buf.at[slot], sem.at[slot], 2 * tm)
    r = route_ref[...]
    mix = r[:, 0:1] * buf[slot, :tm, :] + r[:, 1:2] * buf[slot, tm:, :]
    out_ref[...] = h_ref[...] + mod_ref[5:6, :] * mix

    @pl.when(i == n - 1)
    def _():
        _wait_rows(o_hbm, nxt, sem.at[1 - slot], 2 * tm)


def _moe_combine(tok, o_sorted, pos, h, route, mods_l):
    tm = tok.tm
    n = tok.n_all
    idx = pos.reshape(n, tm, 2).transpose(0, 2, 1).reshape(-1)
    kern = functools.partial(_moe_combine_kernel, tm=tm)
    smem = functools.partial(pl.BlockSpec, memory_space=pltpu.SMEM)
    return pl.pallas_call(
        kern,
        grid=(n,),
        in_specs=[smem((2 * tm,), lambda i: (i,)),
                  smem((2 * tm,), lambda i: (jnp.minimum(i + 1, n - 1),)),
                  pl.BlockSpec(memory_space=pl.ANY),
                  pl.BlockSpec((tm, D), lambda i: (i, 0)),
                  pl.BlockSpec((tm, ROUTE_W), lambda i: (i, 0)),
                  pl.BlockSpec((None, N_MOD, D), lambda i: (tok.mod_row(i), 0, 0))],
        out_specs=pl.BlockSpec((tm, D), lambda i: (i, 0)),
        out_shape=jax.ShapeDtypeStruct((tok.T, D), F32),
        scratch_shapes=[pltpu.VMEM((2, 2 * tm, D), F32), pltpu.SemaphoreType.DMA((2,))],
        compiler_params=_cparams(("arbitrary",)),
    )(idx, idx, o_sorted, h, route, mods_l)


def _moe(tok, h, mods_l, gain, router, wg, wu, wd):
    y, route = _moe_router(tok, h, mods_l, gain, router)
    src, tile_expert, n_valid, pos = _moe_plan(route, MOE_TR)
    o_sorted = _moe_experts(y, src, tile_expert, n_valid, wg, wu, wd, MOE_TR)
    return _moe_combine(tok, o_sorted, pos, h, route, mods_l)


def _odd_layer(tok, h, mods_l, tabs, p):
    B, L, CTX = tok.B, tok.L, tok.CTX
    nl = B * L
    w = p['w_in']
    w_in = jnp.concatenate([w[:, :Q_LORA + KV_LORA], w[:, 416:1184], w[:, 384:416],
                            jnp.zeros((D, OD_IN_PAD - 1184), w.dtype)], axis=1).astype(BF16)
    z = _norm_matmul(tok, h, mods_l, p['norm_mix'], w_in, OD_IN_PAD)
    q, k, v, gq, gk, gv = _odd_proj(tok, z, tabs, p['q_norm'], p['kv_norm'], p['w_uq'], p['w_ukv'])
    lat = lambda a: a[:nl].reshape(B, L, -1)
    ctx = lambda a: a[nl:].reshape(B, CTX, -1)
    cat = lambda a: jnp.concatenate([ctx(a), lat(a)], axis=1)
    mla_l = _mla_attention(lat(q), cat(k), cat(v))
    gqa_l = _gqa_attention(p['sink'], lat(gq), ctx(gk), ctx(gv), lat(gk), lat(gv))
    if p['need_ctx']:
        mla_c = _mla_attention(ctx(q), ctx(k), ctx(v))
        gqa_c = _gqa_attention(p['sink'], ctx(gq), ctx(gk), ctx(gv))
    else:
        mla_c = jnp.zeros((B, CTX, MLA_HEADS * MLA_V), BF16)
        gqa_c = jnp.zeros((B, CTX, GQA_HEADS * GQA_HD), BF16)
    flat = lambda a, c: jnp.concatenate([a.reshape(nl, -1), c.reshape(B * CTX, -1)], axis=0)
    h = _odd_out(tok, flat(mla_l, mla_c), flat(gqa_l, gqa_c), h, mods_l, p['w_out'].astype(BF16))
    return _moe(tok, h, mods_l, p['norm_ffn'], p['router'], _to_bf16(*p['moe_wg']), _to_bf16(*p['moe_wu']),
                _to_bf16(*p['moe_wd']))


def _final_norm_kernel(h_ref, g_ref, o_ref):
    o_ref[...] = _rms(h_ref[...], g_ref[...])


def _final_norm(tok, h, gain):
    tm = tok.tm
    return pl.pallas_call(
        _final_norm_kernel,
        grid=(tok.n_lat,),
        in_specs=[pl.BlockSpec((tm, D), lambda i: (i, 0)), pl.BlockSpec((1, D), lambda i: (0, 0))],
        out_specs=pl.BlockSpec((tm, D), lambda i: (i, 0)),
        out_shape=jax.ShapeDtypeStruct((tok.B * tok.L, D), F32),
        compiler_params=_cparams(("arbitrary",)),
    )(h, gain.reshape(1, D))


def kernel(x, c, ctx, c_ctx, mod_w, mod_b, norm_mix, norm_ffn, final_norm,
           ev_w_in, ev_conv_w, ev_conv_b, hy_w1, hy_b1, hy_w2, hy_b2, hy_w3, hy_freq, hy_decay, hy_bias,
           s5_a_re, s5_a_im, s5_log_dt, s5_b_re, s5_b_im, s5_c_re, s5_c_im, s5_d, s5_w_glu, ev_w_out,
           ff_w_gate, ff_w_up, ff_w_down,
           od_w_in, mla_q_norm, mla_w_uq, mla_kv_norm, mla_w_ukv, gqa_sink, od_w_out,
           moe_router, moe_w_gate, moe_w_up, moe_w_down):
    B, L, _ = x.shape
    CTX = ctx.shape[1]
    tok = _Tok(B, L, CTX, _pick(math.gcd(L, B * CTX), (512, 256, 128)))
    cond_t = jnp.concatenate([c, c_ctx[None, :], jnp.zeros((8 - B - 1, D), F32)], axis=0).T
    mods = _modulations(cond_t, B + 1, mod_w, mod_b)
    tabs = _rope_tables(L, tok.tm)
    h = jnp.concatenate([x.reshape(B * L, D), ctx.reshape(B * CTX, D)], axis=0)
    for l in range(DEPTH):
        i = l // 2
        need_ctx = l < DEPTH - 1
        if l % 2 == 0:
            p = dict(norm_mix=norm_mix[l], norm_ffn=norm_ffn[l], w_in=ev_w_in[i], conv_w=ev_conv_w[i],
                     conv_b=ev_conv_b[i],
                     hy=(hy_w1[i], hy_b1[i], hy_w2[i], hy_b2[i], hy_w3[i], hy_freq[i], hy_decay[i]),
                     hy_bias=hy_bias[i],
                     s5=(s5_a_re[i], s5_a_im[i], s5_log_dt[i], s5_b_re[i], s5_b_im[i], s5_c_re[i], s5_c_im[i]),
                     s5_d=s5_d[i], s5_w_glu=s5_w_glu[i], w_out=ev_w_out[i],
                     ff_wg=(ff_w_gate, i), ff_wu=(ff_w_up, i), ff_wd=(ff_w_down, i), need_ctx=need_ctx)
            h = _even_layer(tok, h, mods[l], p)
        else:
            p = dict(norm_mix=norm_mix[l], norm_ffn=norm_ffn[l], w_in=od_w_in[i], q_norm=mla_q_norm[i],
                     w_uq=mla_w_uq[i], kv_norm=mla_kv_norm[i], w_ukv=mla_w_ukv[i], sink=gqa_sink[i],
                     w_out=od_w_out[i], router=moe_router[i], moe_wg=(moe_w_gate, i), moe_wu=(moe_w_up, i),
                     moe_wd=(moe_w_down, i), need_ctx=need_ctx)
            h = _odd_layer(tok, h, mods[l], tabs, p)
    return _final_norm(tok, h, final_norm).reshape(B, L, D)
```

```python
import functools
import math

import jax
import jax.numpy as jnp
from jax import lax
from jax.experimental import pallas as pl
from jax.experimental.pallas import tpu as pltpu

F32 = jnp.float32
BF16 = jnp.bfloat16
HI = lax.Precision.HIGHEST

D = 1024
DEPTH = 4
GRID_W = 64
EPS = 1e-6
NEG = -1e30
N_MOD = 6

HY_CH = 512
HY_ORDER = 2
HY_BANDS = 16
HY_EMB = 1 + 2 * HY_BANDS
HY_FFN = 64
HY_SHIFT = 0.05
HY_COLS = (HY_ORDER + 1) * HY_CH
S5_CH = 512
S5_GROUP = 16
S5_NG = S5_CH // S5_GROUP
S5_P = 64
S5_T = 16
S5_SEG = 8
EV_IN = HY_COLS + S5_CH

MLA_HEADS = 8
MLA_NOPE = 64
MLA_ROPE = 32
MLA_V = 64
Q_LORA = 256
KV_LORA = 128
GQA_HEADS = 8
GQA_KV = 2
GQA_HD = 64
WINDOW = 128
BLK = 128
ROPE_BASE = 10000.0
MLA_SCALE = (MLA_NOPE + MLA_ROPE) ** -0.5
GQA_SCALE = GQA_HD ** -0.5
LOG2E = math.log2(math.e)
HEAD_PAD = 128
MLA_VP = MLA_V + 16
OD_IN_PAD = 1280

D_FF = 2816
N_EXP = 8
EXP_FF = 3584

LANE = 128
FFT_N2 = 128
FFT_PAD = 8
VMEM_LIMIT = 56 * 1024 * 1024
CAST_BLOCK_BYTES = 8 * 1024 * 1024


def _cparams(sem):
    return pltpu.CompilerParams(dimension_semantics=sem, vmem_limit_bytes=VMEM_LIMIT)


def _pick(n, cands):
    for c in cands:
        if n % c == 0:
            return c
    raise ValueError(f"no tile for {n} in {cands}")


def _mod_kernel(ct_ref, w_ref, b_ref, o_ref, *, nrows):
    c = ct_ref[...]
    s = c * jax.nn.sigmoid(c)
    w = w_ref[...]
    rows = [jnp.sum(w * s[:, r:r + 1], axis=0, keepdims=True) for r in range(nrows)]
    rows.append(jnp.zeros((8 - nrows, w.shape[1]), F32))
    o_ref[...] = jnp.concatenate(rows, axis=0) + b_ref[...]


def _modulations(cond_t, nrows, mod_w, mod_b):
    tn = 1536
    out = pl.pallas_call(
        functools.partial(_mod_kernel, nrows=nrows),
        grid=(DEPTH, N_MOD * D // tn),
        in_specs=[pl.BlockSpec((D, 8), lambda l, j: (0, 0)),
                  pl.BlockSpec((None, D, tn), lambda l, j: (l, 0, j)),
                  pl.BlockSpec((None, 1, tn), lambda l, j: (l, 0, j))],
        out_specs=pl.BlockSpec((None, 8, tn), lambda l, j: (l, 0, j)),
        out_shape=jax.ShapeDtypeStruct((DEPTH, 8, N_MOD * D), F32),
        compiler_params=_cparams(("arbitrary", "arbitrary")),
    )(cond_t, mod_w, mod_b.reshape(DEPTH, 1, N_MOD * D))
    return out.reshape(DEPTH, 8, N_MOD, D)


def _cast_kernel(x_ref, o_ref):
    o_ref[...] = x_ref[...].astype(o_ref.dtype)


def _to_bf16(stack, layer):
    shape = stack.shape[1:]
    cols = shape[-1]
    w2 = stack.reshape(-1, cols)
    rows = w2.shape[0] // stack.shape[0]
    tr = next(t for t in (2048, 1024, 512, 256, 128, 8) if rows % t == 0 and t * cols * 4 <= CAST_BLOCK_BYTES)
    nblk = rows // tr
    out = pl.pallas_call(
        _cast_kernel,
        grid=(nblk,),
        in_specs=[pl.BlockSpec((tr, cols), lambda i: (layer * nblk + i, 0))],
        out_specs=pl.BlockSpec((tr, cols), lambda i: (i, 0)),
        out_shape=jax.ShapeDtypeStruct((rows, cols), BF16),
        compiler_params=_cparams(("arbitrary",)),
    )(w2)
    return out.reshape(shape)


class _Tok:
    def __init__(self, B, L, CTX, tm):
        assert L % tm == 0 and (B * CTX) % tm == 0
        self.B, self.L, self.CTX, self.tm = B, L, CTX, tm
        self.n_lat = B * L // tm
        self.n_all = self.n_lat + B * CTX // tm
        self.T = B * (L + CTX)
        self.per_seq = L // tm

    def mod_row(self, i):
        return jnp.where(i < self.n_lat, i // self.per_seq, self.B)


def _ada_norm(x, gain, mod, shift_idx, scale_idx):
    y = x * lax.rsqrt(jnp.mean(x * x, axis=-1, keepdims=True) + EPS) * gain
    return y * (1.0 + mod[scale_idx:scale_idx + 1, :]) + mod[shift_idx:shift_idx + 1, :]


def _norm_mm_kernel(h_ref, mod_ref, g_ref, w_ref, o_ref, y_scr):
    @pl.when(pl.program_id(1) == 0)
    def _():
        y_scr[...] = _ada_norm(h_ref[...], g_ref[...], mod_ref[...], 0, 1).astype(BF16)

    o_ref[...] = jnp.dot(y_scr[...], w_ref[...], preferred_element_type=F32).astype(o_ref.dtype)


def _norm_matmul(tok, h, mods_l, gain, w, tn, out_dtype=F32):
    tm, n = tok.tm, w.shape[1]
    return pl.pallas_call(
        _norm_mm_kernel,
        grid=(tok.n_all, n // tn),
        in_specs=[pl.BlockSpec((tm, D), lambda i, j: (i, 0)),
                  pl.BlockSpec((None, N_MOD, D), lambda i, j: (tok.mod_row(i), 0, 0)),
                  pl.BlockSpec((1, D), lambda i, j: (0, 0)),
                  pl.BlockSpec((D, tn), lambda i, j: (0, j))],
        out_specs=pl.BlockSpec((tm, tn), lambda i, j: (i, j)),
        out_shape=jax.ShapeDtypeStruct((tok.T, n), out_dtype),
        scratch_shapes=[pltpu.VMEM((tm, D), BF16)],
        compiler_params=_cparams(("arbitrary", "arbitrary")),
    )(h, mods_l, gain.reshape(1, D), w)


def _short_conv_kernel(z_ref, zp_ref, zn_ref, w_ref, b_ref, v_ref, x1_ref, x2_ref, *, tm, n_lat, L, CTX):
    i = pl.program_id(0)
    is_lat = i < n_lat
    seqlen = jnp.where(is_lat, L, CTX)
    off = jnp.where(is_lat, i * tm, (i - n_lat) * tm)
    first = lax.rem(off, seqlen) == 0
    last = lax.rem(off + tm, seqlen) == 0
    z = z_ref[...]
    prev_row = jnp.where(first, 0.0, zp_ref[7:8, :])
    next_row = jnp.where(last, 0.0, zn_ref[0:1, :])
    rid = lax.broadcasted_iota(jnp.int32, z.shape, 0)
    zm1 = jnp.where(rid == 0, prev_row, pltpu.roll(z, 1, axis=0))
    zp1 = jnp.where(rid == tm - 1, next_row, pltpu.roll(z, tm - 1, axis=0))
    out = b_ref[...] + zm1 * w_ref[0:1, :] + z * w_ref[1:2, :] + zp1 * w_ref[2:3, :]
    v_ref[...] = out[:, :HY_CH]
    x1_ref[...] = out[:, HY_CH:2 * HY_CH]
    x2_ref[...] = out[:, 2 * HY_CH:]


def _short_conv(tok, z, conv_w, conv_b):
    tm = _pick(math.gcd(tok.L, tok.CTX), (256, 128))
    n_lat = tok.B * tok.L // tm
    n_all = tok.T // tm
    r8 = tm // 8
    kern = functools.partial(_short_conv_kernel, tm=tm, n_lat=n_lat, L=tok.L, CTX=tok.CTX)
    o = jax.ShapeDtypeStruct((tok.T, HY_CH), F32)
    return pl.pallas_call(
        kern,
        grid=(n_all,),
        in_specs=[pl.BlockSpec((tm, HY_COLS), lambda i: (i, 0)),
                  pl.BlockSpec((8, HY_COLS), lambda i: (jnp.maximum(i * r8 - 1, 0), 0)),
                  pl.BlockSpec((8, HY_COLS), lambda i: (jnp.minimum((i + 1) * r8, n_all * r8 - 1), 0)),
                  pl.BlockSpec((8, HY_COLS), lambda i: (0, 0)),
                  pl.BlockSpec((1, HY_COLS), lambda i: (0, 0))],
        out_specs=[pl.BlockSpec((tm, HY_CH), lambda i: (i, 0))] * 3,
        out_shape=[o, o, o],
        compiler_params=_cparams(("arbitrary",)),
    )(z, z, z, jnp.pad(conv_w, ((0, 8 - conv_w.shape[0]), (0, 0))), conv_b.reshape(1, HY_COLS))


def _filter_kernel(f_ref, w1_ref, b1_ref, w2_ref, b2_ref, w3_ref, fr_ref, dec_ref, k_ref, s_ref):
    @pl.when(pl.program_id(0) == 0)
    def _():
        s_ref[...] = jnp.zeros_like(s_ref)

    f = f_ref[...]
    fr = fr_ref[...]
    hid = jnp.sin(fr * (jnp.dot(f, w1_ref[...], precision=HI, preferred_element_type=F32) + b1_ref[...]))
    hid = jnp.sin(fr * (jnp.dot(hid, w2_ref[...], precision=HI, preferred_element_type=F32) + b2_ref[...]))
    h = jnp.dot(hid, w3_ref[...], precision=HI, preferred_element_type=F32)
    t01 = f[:, 0:1]
    valid = f[:, LANE - 1:LANE]
    k = h * (jnp.exp(-t01 * jnp.abs(dec_ref[...])) + HY_SHIFT) * valid
    k_ref[0] = k[:, :HY_CH]
    k_ref[1] = k[:, HY_CH:]
    s_ref[...] += jnp.sum(jnp.abs(k), axis=0, keepdims=True)


def _hyena_filters(L, n, w1, b1, w2, b2, w3, freq, decay):
    row = jnp.arange(n)
    fwd = row < L
    bwd = row > n - L
    t = jnp.where(fwd, row, n - row).astype(F32)
    t01 = t / L
    bands = jnp.linspace(1e-4, HY_BANDS - 1, HY_BANDS, dtype=F32)
    ang = (2.0 * math.pi / L) * t[:, None] * bands[None, :]
    valid = (fwd | bwd).astype(F32)
    feats = jnp.concatenate([t01[:, None], jnp.cos(ang), -jnp.sin(ang),
                             jnp.zeros((n, LANE - 1 - HY_EMB), F32), valid[:, None]], axis=-1)
    w1p = jnp.pad(w1, ((0, LANE - HY_EMB), (0, 0)))
    tr = _pick(L, (512, 256))
    nb_half = n // 2 // tr
    ncol = HY_ORDER * HY_CH
    k, ssum = pl.pallas_call(
        _filter_kernel,
        grid=(n // tr,),
        in_specs=[pl.BlockSpec((tr, LANE), lambda i: (i, 0)),
                  pl.BlockSpec((LANE, HY_FFN), lambda i: (0, 0)),
                  pl.BlockSpec((1, HY_FFN), lambda i: (0, 0)),
                  pl.BlockSpec((HY_FFN, HY_FFN), lambda i: (0, 0)),
                  pl.BlockSpec((1, HY_FFN), lambda i: (0, 0)),
                  pl.BlockSpec((HY_FFN, ncol), lambda i: (0, jnp.where(i < nb_half, 0, 1))),
                  pl.BlockSpec((1, HY_FFN), lambda i: (0, 0)),
                  pl.BlockSpec((1, ncol), lambda i: (0, 0))],
        out_specs=[pl.BlockSpec((HY_ORDER, tr, HY_CH), lambda i: (0, i, 0)),
                   pl.BlockSpec((1, ncol), lambda i: (0, 0))],
        out_shape=[jax.ShapeDtypeStruct((HY_ORDER, n, HY_CH), F32),
                   jax.ShapeDtypeStruct((1, ncol), F32)],
        compiler_params=_cparams(("arbitrary",)),
    )(feats, w1p, b1.reshape(1, -1), w2, b2.reshape(1, -1), w3, freq.reshape(1, -1), decay.reshape(1, ncol))
    return k, (1.0 / ssum).reshape(HY_ORDER, 1, HY_CH)


def _dft_tables(n1, r_in):
    n = n1 * FFT_N2
    k1 = jnp.arange(n1)
    a1 = (2.0 * math.pi / n1) * ((k1[:, None] * jnp.arange(r_in)[None, :]) % n1).astype(F32)
    f1 = jnp.concatenate([jnp.cos(a1), -jnp.sin(a1)], axis=0)
    f3 = jnp.concatenate([jnp.cos(a1).T, -jnp.sin(a1).T], axis=1) / n
    n2 = jnp.arange(FFT_N2)
    kk = k1[:, None, None] + n1 * n2[None, :, None]
    ang = (2.0 * math.pi / n) * ((kk * n2[None, None, :]) % n).astype(F32)
    gr, gi = jnp.cos(ang), -jnp.sin(ang)
    g = jnp.concatenate([jnp.concatenate([gr, -gi], axis=2), jnp.concatenate([gi, gr], axis=2)], axis=1)
    return f1.astype(BF16), f3.astype(BF16), g.astype(BF16), jnp.swapaxes(g, 1, 2).astype(BF16)


def _fft1_kernel(f_ref, x_ref, o_ref):
    o_ref[...] = jnp.dot(f_ref[...], x_ref[...].astype(BF16), preferred_element_type=F32).astype(o_ref.dtype)


def _fft_stage1(f1, x):
    nb, r_in, cols = x.shape
    m = f1.shape[0]
    tn = _pick(cols, (8192, 4096, 2048))
    return pl.pallas_call(
        _fft1_kernel,
        grid=(nb, cols // tn),
        in_specs=[pl.BlockSpec((m, r_in), lambda b, j: (0, 0)),
                  pl.BlockSpec((None, r_in, tn), lambda b, j: (b, 0, j))],
        out_specs=pl.BlockSpec((None, m, tn), lambda b, j: (b, 0, j)),
        out_shape=jax.ShapeDtypeStruct((nb, m, cols), BF16),
        compiler_params=_cparams(("arbitrary", "arbitrary")),
    )(f1, x)


def _fft_filt_kernel(a_ref, g_ref, s_ref, o_ref):
    c = a_ref.shape[-1]
    a = a_ref[...].reshape(2 * FFT_N2, c)
    o_ref[...] = jnp.dot(g_ref[...], a, preferred_element_type=F32) * s_ref[...]


def _fft_filter_spectrum(a, g, inv):
    no, _, n1, _, c = a.shape
    return pl.pallas_call(
        _fft_filt_kernel,
        grid=(n1, no),
        in_specs=[pl.BlockSpec((None, 2, None, FFT_N2, c), lambda k, o: (o, 0, k, 0, 0)),
                  pl.BlockSpec((None, 2 * FFT_N2, 2 * FFT_N2), lambda k, o: (k, 0, 0)),
                  pl.BlockSpec((None, 1, c), lambda k, o: (o, 0, 0))],
        out_specs=pl.BlockSpec((None, None, 2 * FFT_N2, c), lambda k, o: (o, k, 0, 0)),
        out_shape=jax.ShapeDtypeStruct((no, n1, 2 * FFT_N2, c), F32),
        compiler_params=_cparams(("arbitrary", "arbitrary")),
    )(a, g, inv)


FFT_KB = 4


def _fft_mid_kernel(a_ref, g_ref, gt_ref, kh_ref, o_ref):
    kb, c = a_ref.shape[1], a_ref.shape[-1]
    K = range(kb)
    a = [jnp.concatenate([a_ref[0, k], a_ref[1, k]], axis=0) for k in K]
    x = [jnp.dot(g_ref[k], a[k], preferred_element_type=F32) for k in K]
    y = []
    for k in K:
        xr, xi = x[k][:FFT_N2], x[k][FFT_N2:]
        kr, ki = kh_ref[k, :FFT_N2, :], kh_ref[k, FFT_N2:, :]
        y.append(jnp.concatenate([xr * kr - xi * ki, xr * ki + xi * kr], axis=0).astype(BF16))
    bm = [jnp.dot(gt_ref[k], y[k], preferred_element_type=F32) for k in K]
    for k in K:
        o_ref[0, k] = bm[k][:FFT_N2].astype(o_ref.dtype)
        o_ref[1, k] = bm[k][FFT_N2:].astype(o_ref.dtype)


def _fft_mid(a, g, gt, khat):
    nb, _, n1, _, c = a.shape
    kb = math.gcd(n1, FFT_KB)
    return pl.pallas_call(
        _fft_mid_kernel,
        grid=(n1 // kb, nb),
        in_specs=[pl.BlockSpec((None, 2, kb, FFT_N2, c), lambda k, b: (b, 0, k, 0, 0)),
                  pl.BlockSpec((kb, 2 * FFT_N2, 2 * FFT_N2), lambda k, b: (k, 0, 0)),
                  pl.BlockSpec((kb, 2 * FFT_N2, 2 * FFT_N2), lambda k, b: (k, 0, 0)),
                  pl.BlockSpec((kb, 2 * FFT_N2, c), lambda k, b: (k, 0, 0))],
        out_specs=pl.BlockSpec((None, 2, kb, FFT_N2, c), lambda k, b: (b, 0, k, 0, 0)),
        out_shape=jax.ShapeDtypeStruct(a.shape, BF16),
        compiler_params=_cparams(("arbitrary", "arbitrary")),
    )(a, g, gt, khat)


def _fft3_kernel(f_ref, bm_ref, y_ref, gate_ref, bias_ref, o_ref):
    conv = jnp.dot(f_ref[...], bm_ref[...], preferred_element_type=F32)
    o_ref[...] = gate_ref[...] * (conv + y_ref[...] * bias_ref[...])


def _fft_stage3(f3, bm, y, gate, bias_row):
    nb, m, cols = bm.shape
    r = f3.shape[0]
    tn = _pick(cols, (8192, 4096, 2048))
    return pl.pallas_call(
        _fft3_kernel,
        grid=(nb, cols // tn),
        in_specs=[pl.BlockSpec((r, m), lambda b, j: (0, 0)),
                  pl.BlockSpec((None, m, tn), lambda b, j: (b, 0, j)),
                  pl.BlockSpec((None, r, tn), lambda b, j: (b, 0, j)),
                  pl.BlockSpec((None, r, tn), lambda b, j: (b, 0, j)),
                  pl.BlockSpec((1, tn), lambda b, j: (0, j))],
        out_specs=pl.BlockSpec((None, r, tn), lambda b, j: (b, 0, j)),
        out_shape=jax.ShapeDtypeStruct((nb, r, cols), F32),
        compiler_params=_cparams(("arbitrary", "arbitrary")),
    )(f3, bm, y, gate, bias_row)


def _fft1_tok_kernel(f_ref, x_ref, o_ref, s_scr, *, rows):
    n1 = f_ref.shape[0]
    f = f_ref[...]
    pitch = n1 + FFT_PAD

    def body(n2, carry):
        xs = x_ref[pl.ds(n2, rows, stride=FFT_N2), :].astype(BF16)
        s_scr[pl.ds(pl.multiple_of(n2 * pitch, 8), n1), :] = jnp.dot(f, xs, preferred_element_type=F32)
        return carry

    lax.fori_loop(0, FFT_N2, body, 0, unroll=8)

    def emit(k1, carry):
        o_ref[k1] = s_scr[pl.ds(k1, FFT_N2, stride=pitch), :].astype(o_ref.dtype)
        return carry

    lax.fori_loop(0, n1, emit, 0, unroll=4)


def _fft_stage1_tok(f1, x, nb):
    c = x.shape[1]
    rows = f1.shape[1]
    n = rows * FFT_N2
    n1 = f1.shape[0] // 2
    return pl.pallas_call(
        functools.partial(_fft1_tok_kernel, rows=rows),
        grid=(nb, c // LANE, 2),
        in_specs=[pl.BlockSpec((None, n1, rows), lambda b, j, r: (r, 0, 0)),
                  pl.BlockSpec((n, LANE), lambda b, j, r: (b, j))],
        out_specs=pl.BlockSpec((None, None, n1, FFT_N2, LANE), lambda b, j, r: (b, r, 0, 0, j)),
        out_shape=jax.ShapeDtypeStruct((nb, 2, n1, FFT_N2, c), BF16),
        scratch_shapes=[pltpu.VMEM((FFT_N2 * (n1 + FFT_PAD), LANE), F32)],
        compiler_params=_cparams(("arbitrary", "arbitrary", "arbitrary")),
    )(f1.reshape(2, n1, rows), x)


def _fft3_tok_kernel(f_ref, bm_ref, y_ref, gate_ref, bias_ref, o_ref, s_scr, t_scr, *, rows):
    half = pl.program_id(2)
    n1 = bm_ref.shape[0]
    sp, tp = FFT_N2 + FFT_PAD, rows + FFT_PAD

    def stage(k1, carry):
        s_scr[pl.ds(pl.multiple_of(k1 * sp, 8), FFT_N2), :] = bm_ref[k1].astype(F32)
        return carry

    lax.fori_loop(0, n1, stage, 0, unroll=4)
    f = f_ref[...]

    def part(n2):
        return jnp.dot(f, s_scr[pl.ds(n2, n1, stride=sp), :].astype(BF16), preferred_element_type=F32)

    def dst(n2):
        return pl.ds(pl.multiple_of(n2 * tp, 8), rows)

    @pl.when(half == 0)
    def _():
        def body(n2, carry):
            t_scr[dst(n2), :] = part(n2)
            return carry
        lax.fori_loop(0, FFT_N2, body, 0, unroll=8)

    @pl.when(half == 1)
    def _():
        def body(n2, carry):
            t_scr[dst(n2), :] += part(n2)
            return carry
        lax.fori_loop(0, FFT_N2, body, 0, unroll=8)

        def emit(r, carry):
            tok = pl.ds(pl.multiple_of(r * FFT_N2, FFT_N2), FFT_N2)
            conv = t_scr[pl.ds(r, FFT_N2, stride=tp), :]
            o_ref[tok, :] = gate_ref[tok, :] * (conv + y_ref[tok, :] * bias_ref[...])
            return carry
        lax.fori_loop(0, rows, emit, 0, unroll=2)


def _fft_stage3_tok(f3, bm, y, gate, bias):
    nb, _, n1, _, c = bm.shape
    rows = f3.shape[0]
    n = rows * FFT_N2
    tok = pl.BlockSpec((n, LANE), lambda b, j, r: (b, j))
    return pl.pallas_call(
        functools.partial(_fft3_tok_kernel, rows=rows),
        grid=(nb, c // LANE, 2),
        in_specs=[pl.BlockSpec((None, rows, n1), lambda b, j, r: (r, 0, 0)),
                  pl.BlockSpec((None, None, n1, FFT_N2, LANE), lambda b, j, r: (b, r, 0, 0, j)),
                  tok, tok,
                  pl.BlockSpec((1, LANE), lambda b, j, r: (0, j))],
        out_specs=tok,
        out_shape=jax.ShapeDtypeStruct((nb * n, c), F32),
        scratch_shapes=[pltpu.VMEM((n1 * (FFT_N2 + FFT_PAD), LANE), F32),
                        pltpu.VMEM((FFT_N2 * (rows + FFT_PAD), LANE), F32)],
        compiler_params=_cparams(("arbitrary", "arbitrary", "arbitrary")),
    )(f3.reshape(rows, 2, n1).transpose(1, 0, 2), bm, y, gate, bias.astype(F32).reshape(1, c))


def _hyena_long(v, x1, x2, B, L, hy, bias):
    r_in = L // FFT_N2
    n1 = 2 * r_in
    f1, f3, g, gt = _dft_tables(n1, r_in)
    f1k = _dft_tables(n1, n1)[0]
    k, inv = _hyena_filters(L, n1 * FFT_N2, *hy)
    khat = _fft_filter_spectrum(_fft_stage1_tok(f1k, k.reshape(HY_ORDER * n1 * FFT_N2, HY_CH), HY_ORDER), g, inv)
    y = v
    for o, gate in enumerate((x1, x2)):
        bm = _fft_mid(_fft_stage1_tok(f1, y, B), g, gt, khat[o])
        y = _fft_stage3_tok(f3, bm, y, gate, bias[o])
    return y


def _hyena_sequence(v, x1, x2, hy, bias):
    B, L, C = v.shape
    r_valid = L // FFT_N2
    r_in = max(r_valid, 16)
    n1 = max(2 * r_valid, r_in)
    n = n1 * FFT_N2
    if r_valid == r_in:
        return _hyena_long(v.reshape(B * L, C), x1.reshape(B * L, C), x2.reshape(B * L, C), B, L, hy,
                           bias).reshape(B, L, C)
    f1, f3, g, gt = _dft_tables(n1, r_in)
    f1k = _dft_tables(n1, n1)[0]
    k, inv = _hyena_filters(L, n, *hy)
    ak = _fft_stage1(f1k, k.reshape(HY_ORDER, n1, FFT_N2 * C))
    khat = _fft_filter_spectrum(ak.reshape(HY_ORDER, 2, n1, FFT_N2, C), g, inv)
    cols = FFT_N2 * C

    def view(a):
        a = a.reshape(B, r_valid, cols)
        return a if r_in == r_valid else jnp.pad(a, ((0, 0), (0, r_in - r_valid), (0, 0)))

    y = view(v)
    for o, gate in enumerate((view(x1), view(x2))):
        a = _fft_stage1(f1, y)
        bm = _fft_mid(a.reshape(B, 2, n1, FFT_N2, C), g, gt, khat[o])
        y = _fft_stage3(f3, bm.reshape(B, 2 * n1, cols), y, gate, jnp.tile(bias[o].astype(F32), FFT_N2)[None, :])
    return y[:, :r_valid].reshape(B, L, C)


def _s5_tables(a_re, a_im, log_dt, b_re, b_im, c_re, c_im, jj_ctx, jj_lat):
    lam = lax.complex(jnp.minimum(a_re.astype(F32), -1e-4), a_im.astype(F32))
    dt = jnp.exp(log_dt.astype(F32))[..., None]
    abar = jnp.exp(lam * dt)
    bbar = ((abar - 1.0) / lam)[..., None] * lax.complex(b_re.astype(F32), b_im.astype(F32))
    cmat = lax.complex(c_re.astype(F32), c_im.astype(F32))
    T = S5_T

    def powers(m):
        m = jnp.asarray(m, F32)
        return jnp.exp(lam * dt * m.reshape(m.shape + (1, 1, 1)))

    pw = powers(jnp.arange(T + 1))
    kt = jnp.real(jnp.einsum('dgop,tdgp,dgpi->tdgoi', cmat, pw[:T], bbar, precision=HI))
    tt = jnp.arange(T)
    lag = tt[None, :] - tt[:, None]
    w_intra = jnp.where((lag >= 0)[:, :, None, None, None, None],
                        kt[jnp.clip(lag, 0, T - 1)], 0.0)
    w_intra = jnp.stack([w_intra[:, :, 0], w_intra[::-1, ::-1, 1]], axis=2)
    w_intra = w_intra.transpose(2, 3, 0, 5, 1, 4).reshape(2, S5_NG, T * S5_GROUP, T * S5_GROUP)
    wb = pw[T - 1 - tt][..., None] * bbar[None]
    wb = jnp.stack([wb[:, 0], wb[::-1, 1]], axis=1)
    wb = wb.transpose(1, 2, 0, 4, 3).reshape(2, S5_NG, T * S5_GROUP, S5_P)
    w_cat = jnp.concatenate([w_intra, jnp.real(wb), jnp.imag(wb)], axis=-1)
    cp = cmat[None] * pw[1:, :, :, None, :]
    cp = jnp.stack([cp[:, 0], cp[::-1, 1]], axis=1)
    cp = cp.transpose(1, 2, 4, 0, 3).reshape(2, S5_NG, S5_P, T * S5_GROUP)
    c_cat = jnp.concatenate([jnp.real(cp), -jnp.imag(cp)], axis=2)

    def coef(z):
        zr, zi = jnp.real(z), jnp.imag(z)
        return jnp.stack([jnp.concatenate([zr, zr], -1), jnp.concatenate([-zi, zi], -1)], axis=-2)

    step = coef(powers(jnp.array(T)))
    seg_c, seg_l = (coef(powers(jnp.array(T * n))) for n in (jj_ctx, jj_lat))
    coefs = jnp.pad(jnp.concatenate([step, seg_c, seg_l], axis=2), ((0, 0), (0, 0), (0, 2), (0, 0)))
    ptab = coef(powers(T * jnp.arange(max(jj_ctx, jj_lat)))).transpose(1, 2, 0, 3, 4)
    return w_cat.astype(BF16), c_cat.astype(BF16), coefs, ptab


def _cmul(coef_a, coef_b, s):
    return coef_a * s + coef_b * pltpu.roll(s, S5_P, axis=1)


S5_GPB = LANE // S5_GROUP
S5_TC = S5_T * S5_GROUP


def _s5_in_kernel(z_ref, w_ref, yi_ref, ds_ref):
    nj = z_ref.shape[0] // S5_T
    ws = [z_ref[pl.ds(t, nj, stride=S5_T), :].T for t in range(S5_T)]
    for g in range(S5_GPB):
        vt = jnp.concatenate([w[S5_GROUP * g:S5_GROUP * (g + 1), :] for w in ws], axis=0)
        v = vt.T.astype(BF16)
        for d in range(2):
            o = jnp.dot(v, w_ref[d, g], preferred_element_type=F32)
            yi_ref[d, g] = o[:, :S5_TC]
            ds_ref[d, g] = o[:, S5_TC:]


def _s5_in(z, w_cat, nj):
    T = z.shape[0]
    R = T // S5_T
    col0 = HY_COLS // LANE
    return pl.pallas_call(
        _s5_in_kernel,
        grid=(R // nj, S5_CH // LANE),
        in_specs=[pl.BlockSpec((nj * S5_T, LANE), lambda i, c: (i, col0 + c)),
                  pl.BlockSpec((2, S5_GPB, S5_TC, S5_TC + 2 * S5_P), lambda i, c: (0, c, 0, 0))],
        out_specs=[pl.BlockSpec((2, S5_GPB, nj, S5_TC), lambda i, c: (0, c, i, 0)),
                   pl.BlockSpec((2, S5_GPB, nj, 2 * S5_P), lambda i, c: (0, c, i, 0))],
        out_shape=[jax.ShapeDtypeStruct((2, S5_NG, R, S5_TC), F32),
                   jax.ShapeDtypeStruct((2, S5_NG, R, 2 * S5_P), F32)],
        compiler_params=_cparams(("arbitrary", "arbitrary")),
    )(z, w_cat)


def _s5_scan_kernel(ds_ref, yi_ref, c_ref, cf_ref, p_ref, y_ref, sp_scr, *, parts, reverse):
    a1, a2 = cf_ref[0:1, :], cf_ref[1:2, :]
    rid = lax.broadcasted_iota(jnp.int32, (S5_SEG, 2 * S5_P), 0)
    first, last = (S5_SEG - 1, 0) if reverse else (0, S5_SEG - 1)
    shift = S5_SEG - 1 if reverse else 1
    nb = len(parts[0][0])
    zero = jnp.zeros((S5_SEG, 2 * S5_P), F32)
    s0 = [zero] * nb
    for pi, (bases, jj) in enumerate(parts):
        g1, g2 = cf_ref[2 + 2 * pi:3 + 2 * pi, :], cf_ref[3 + 2 * pi:4 + 2 * pi, :]

        def rows(b, k, bases=bases, jj=jj):
            return pl.ds(bases[b] + (jj - 1 - k if reverse else k), S5_SEG, stride=jj)

        def local_step(k, states, rows=rows):
            new = []
            for b in range(nb):
                sp_scr[rows(b, k), :] = states[b]
                new.append(_cmul(a1, a2, states[b]) + ds_ref[rows(b, k), :])
            return tuple(new)

        ends = lax.fori_loop(0, jj, local_step, (zero,) * nb)
        carries, nxt_s0 = [], []
        for b in range(nb):
            c = jnp.where(rid == first, s0[b], 0.0)
            for _ in range(S5_SEG - 1):
                c = jnp.where(rid == first, s0[b], pltpu.roll(ends[b] + _cmul(g1, g2, c), shift, axis=0))
            fin = ends[b] + _cmul(g1, g2, c)
            nxt_s0.append(jnp.broadcast_to(fin[last:last + 1, :], fin.shape))
            carries.append((c, pltpu.roll(c, S5_P, axis=1)))

        def fix_step(k, carry, rows=rows, carries=carries):
            p = p_ref[k]
            for b in range(nb):
                c, cs = carries[b]
                sp_scr[rows(b, k), :] += p[0:1, :] * c + p[1:2, :] * cs
            return carry

        lax.fori_loop(0, jj, fix_step, 0)
        s0 = nxt_s0
    y_ref[...] = yi_ref[...] + jnp.dot(sp_scr[...].astype(BF16), c_ref[...], preferred_element_type=F32)


def _s5_scan(d, ds, yi, c_cat, coefs, ptab, parts):
    R = ds.shape[2]
    jjm = ptab.shape[2]
    kern = functools.partial(_s5_scan_kernel, parts=parts, reverse=(d == 1))
    return pl.pallas_call(
        kern,
        grid=(S5_NG,),
        in_specs=[pl.BlockSpec((None, None, R, 2 * S5_P), lambda g: (d, g, 0, 0)),
                  pl.BlockSpec((None, None, R, S5_TC), lambda g: (d, g, 0, 0)),
                  pl.BlockSpec((None, None, 2 * S5_P, S5_TC), lambda g: (d, g, 0, 0)),
                  pl.BlockSpec((None, None, 8, 2 * S5_P), lambda g: (d, g, 0, 0)),
                  pl.BlockSpec((None, None, jjm, 2, 2 * S5_P), lambda g: (d, g, 0, 0, 0))],
        out_specs=pl.BlockSpec((None, R, S5_TC), lambda g: (g, 0, 0)),
        out_shape=jax.ShapeDtypeStruct((S5_NG, R, S5_TC), F32),
        scratch_shapes=[pltpu.VMEM((R, 2 * S5_P), F32)],
        compiler_params=_cparams(("arbitrary",)),
    )(ds, yi, c_cat, coefs, ptab)


def _s5_out_kernel(yf_ref, yb_ref, o_ref):
    nj = yf_ref.shape[1]
    yts = [(yf_ref[g] + yb_ref[g]).T for g in range(S5_GPB)]
    for t in range(S5_T):
        zt = jnp.concatenate([y[S5_GROUP * t:S5_GROUP * (t + 1), :] for y in yts], axis=0)
        o_ref[pl.ds(t, nj, stride=S5_T), :] = zt.T


def _s5_out(yf, yb, nj):
    R = yf.shape[1]
    spec = pl.BlockSpec((S5_GPB, nj, S5_TC), lambda i, c: (c, i, 0))
    return pl.pallas_call(
        _s5_out_kernel,
        grid=(R // nj, S5_CH // LANE),
        in_specs=[spec, spec],
        out_specs=pl.BlockSpec((nj * S5_T, LANE), lambda i, c: (i, c)),
        out_shape=jax.ShapeDtypeStruct((R * S5_T, S5_CH), F32),
        compiler_params=_cparams(("arbitrary", "arbitrary")),
    )(yf, yb)


def _s5_mixer(tok, z, s5):
    B = tok.B
    cl, cc = tok.L // S5_T, tok.CTX // S5_T
    jl, jc = cl // S5_SEG, cc // S5_SEG
    w_cat, c_cat, coefs, ptab = _s5_tables(*s5, jc, jl)
    yi, ds = _s5_in(z, w_cat, math.gcd(B * cl, B * cc, 64))
    lat = (tuple(b * cl for b in range(B)), jl)
    ctx = (tuple(B * cl + b * cc for b in range(B)), jc)
    yf = _s5_scan(0, ds, yi, c_cat, coefs, ptab, (ctx, lat))
    yb = _s5_scan(1, ds, yi, c_cat, coefs, ptab, (ctx, lat))
    return _s5_out(yf, yb, math.gcd(B * cl, B * cc, 64))


def _gelu_tanh(x):
    return 0.5 * x * (1.0 + jnp.tanh(math.sqrt(2.0 / math.pi) * (x + 0.044715 * (x * x * x))))


def _even_out_kernel(hl_ref, hc_ref, ys_ref, u_ref, h_ref, mod_ref, dsk_ref, wg_ref, wo_ref, o_ref, *, n_lat):
    y = _gelu_tanh(ys_ref[...] + dsk_ref[...] * u_ref[...])
    s = y * jax.nn.sigmoid(jnp.dot(y.astype(BF16), wg_ref[...], preferred_element_type=F32))
    hy = jnp.where(pl.program_id(0) < n_lat, hl_ref[...], hc_ref[...])
    ol = (jnp.dot(hy.astype(BF16), wo_ref[:HY_CH, :], preferred_element_type=F32)
          + jnp.dot(s.astype(BF16), wo_ref[HY_CH:, :], preferred_element_type=F32))
    o_ref[...] = h_ref[...] + mod_ref[2:3, :] * ol


def _even_out(tok, hl, hc, ys, z, h, mods_l, dsk, w_glu, w_out):
    tm = tok.tm
    return pl.pallas_call(
        functools.partial(_even_out_kernel, n_lat=tok.n_lat),
        grid=(tok.n_all,),
        in_specs=[pl.BlockSpec((tm, HY_CH), lambda i: (jnp.minimum(i, tok.n_lat - 1), 0)),
                  pl.BlockSpec((tm, HY_CH), lambda i: (jnp.maximum(i - tok.n_lat, 0), 0)),
                  pl.BlockSpec((tm, S5_CH), lambda i: (i, 0)),
                  pl.BlockSpec((tm, S5_CH), lambda i: (i, HY_COLS // S5_CH)),
                  pl.BlockSpec((tm, D), lambda i: (i, 0)),
                  pl.BlockSpec((None, N_MOD, D), lambda i: (tok.mod_row(i), 0, 0)),
                  pl.BlockSpec((1, S5_CH), lambda i: (0, 0)),
                  pl.BlockSpec((S5_CH, S5_CH), lambda i: (0, 0)),
                  pl.BlockSpec((D, D), lambda i: (0, 0))],
        out_specs=pl.BlockSpec((tm, D), lambda i: (i, 0)),
        out_shape=jax.ShapeDtypeStruct((tok.T, D), F32),
        compiler_params=_cparams(("arbitrary",)),
    )(hl, hc, ys, z, h, mods_l, dsk.reshape(1, S5_CH), w_glu, w_out)


def _ffn_kernel(h_ref, mod_ref, g_ref, wg_ref, wu_ref, wd_ref, o_ref, y_scr, acc_scr):
    j = pl.program_id(1)

    @pl.when(j == 0)
    def _():
        y_scr[...] = _ada_norm(h_ref[...], g_ref[...], mod_ref[...], 3, 4).astype(BF16)
        acc_scr[...] = jnp.zeros_like(acc_scr)

    y = y_scr[...]
    gate = jnp.dot(y, wg_ref[...], preferred_element_type=F32)
    up = jnp.dot(y, wu_ref[...], preferred_element_type=F32)
    act = (gate * jax.nn.sigmoid(gate) * up).astype(BF16)
    acc_scr[...] += jnp.dot(act, wd_ref[...], preferred_element_type=F32)

    @pl.when(j == pl.num_programs(1) - 1)
    def _():
        o_ref[...] = h_ref[...] + mod_ref[5:6, :] * acc_scr[...]


def _ffn(tok, h, mods_l, gain, wg, wu, wd):
    tm = tok.tm
    ff = wg.shape[1]
    tf = _pick(ff, (1408, 512, 256, 128))
    return pl.pallas_call(
        _ffn_kernel,
        grid=(tok.n_all, ff // tf),
        in_specs=[pl.BlockSpec((tm, D), lambda i, j: (i, 0)),
                  pl.BlockSpec((None, N_MOD, D), lambda i, j: (tok.mod_row(i), 0, 0)),
                  pl.BlockSpec((1, D), lambda i, j: (0, 0)),
                  pl.BlockSpec((D, tf), lambda i, j: (0, j)),
                  pl.BlockSpec((D, tf), lambda i, j: (0, j)),
                  pl.BlockSpec((tf, D), lambda i, j: (j, 0))],
        out_specs=pl.BlockSpec((tm, D), lambda i, j: (i, 0)),
        out_shape=jax.ShapeDtypeStruct((tok.T, D), F32),
        scratch_shapes=[pltpu.VMEM((tm, D), BF16), pltpu.VMEM((tm, D), F32)],
        compiler_params=_cparams(("arbitrary", "arbitrary")),
    )(h, mods_l, gain.reshape(1, D), wg, wu, wd)


def _even_layer(tok, h, mods_l, p):
    B, L, CTX = tok.B, tok.L, tok.CTX
    nl = B * L
    z = _norm_matmul(tok, h, mods_l, p['norm_mix'], p['w_in'].astype(BF16), EV_IN)
    v, x1, x2 = _short_conv(tok, z, p['conv_w'], p['conv_b'])
    ctx = lambda a: a[nl:].reshape(B, CTX, -1)
    if L % (16 * FFT_N2) == 0:
        hl = _hyena_long(v, x1, x2, B, L, p['hy'], p['hy_bias'])
    else:
        lat = lambda a: a[:nl].reshape(B, L, -1)
        hl = _hyena_sequence(lat(v), lat(x1), lat(x2), p['hy'], p['hy_bias']).reshape(nl, HY_CH)
    ys_all = _s5_mixer(tok, z, p['s5'])
    if p['need_ctx']:
        hc = _hyena_sequence(ctx(v), ctx(x1), ctx(x2), p['hy'], p['hy_bias']).reshape(B * CTX, HY_CH)
    else:
        hc = jnp.zeros((B * CTX, HY_CH), F32)
    h = _even_out(tok, hl, hc, ys_all, z, h, mods_l, p['s5_d'], p['s5_w_glu'].astype(BF16),
                  p['w_out'].astype(BF16))
    return _ffn(tok, h, mods_l, p['norm_ffn'], _to_bf16(*p['ff_wg']), _to_bf16(*p['ff_wu']), _to_bf16(*p['ff_wd']))


def _rope_tables(L, tm):
    t = jnp.arange(L)
    row = (t // GRID_W).astype(F32)[:, None]
    col = (t % GRID_W).astype(F32)[:, None]

    def pattern(dim):
        nf = dim // 4
        inv = ROPE_BASE ** (-jnp.arange(nf, dtype=F32) / nf)
        ar, ac = row * inv[None, :], col * inv[None, :]
        cos = jnp.concatenate([jnp.cos(ar)] * 2 + [jnp.cos(ac)] * 2, axis=1)
        z = jnp.zeros((L, nf), F32)
        s_up = jnp.concatenate([-jnp.sin(ar), z, -jnp.sin(ac), z], axis=1)
        s_dn = jnp.concatenate([z, jnp.sin(ar), z, jnp.sin(ac)], axis=1)
        return cos, s_up, s_dn

    def pad_mla(a, fill):
        return jnp.concatenate([jnp.full((L, MLA_NOPE), fill, F32), a,
                                jnp.full((L, HEAD_PAD - MLA_NOPE - MLA_ROPE), fill, F32)], axis=1)

    cm, um, dm = pattern(MLA_ROPE)
    cg, ug, dg = pattern(GQA_HD)
    mla = jnp.stack([pad_mla(cm, 1.0), pad_mla(um, 0.0), pad_mla(dm, 0.0)])
    gqa = jnp.stack([jnp.tile(cg, (1, 2)), jnp.tile(ug, (1, 2)), jnp.tile(dg, (1, 2))])
    ident = jnp.stack([jnp.ones((tm, LANE), F32), jnp.zeros((tm, LANE), F32), jnp.zeros((tm, LANE), F32)])
    return jnp.stack([jnp.concatenate([mla, ident], axis=1), jnp.concatenate([gqa, ident], axis=1)])


def _rope(x, tab, w):
    outs = []
    for h in range(x.shape[1] // LANE):
        xs = x[:, h * LANE:(h + 1) * LANE]
        outs.append(xs * tab[0] + pltpu.roll(xs, LANE - w, axis=1) * tab[1] + pltpu.roll(xs, w, axis=1) * tab[2])
    return outs[0] if len(outs) == 1 else jnp.concatenate(outs, axis=1)


def _rms(x, g):
    return x * lax.rsqrt(jnp.mean(x * x, axis=-1, keepdims=True) + EPS) * g


_O_CQ, _O_CKV, _O_GQ, _O_GK, _O_GV, _O_KR = 0, 256, 384, 896, 1024, 1152


def _odd_proj_kernel(z_ref, tab_ref, qn_ref, kvn_ref, wuq_ref, wuk_ref, wuv_ref, e_ref,
                     q_ref, k_ref, v_ref, gq_ref, gk_ref, gv_ref):
    z = z_ref[...]
    mt, gt = tab_ref[0], tab_ref[1]
    qn = _rms(z[:, _O_CQ:_O_CKV], qn_ref[...]).astype(BF16)
    q = jnp.dot(qn, wuq_ref[...], preferred_element_type=F32)
    q_ref[...] = (_rope(q, mt, MLA_ROPE // 4) * (MLA_SCALE * LOG2E)).astype(BF16)
    kvn = _rms(z[:, _O_CKV:_O_GQ], kvn_ref[...]).astype(BF16)
    k = (jnp.dot(kvn, wuk_ref[...], preferred_element_type=F32)
         + jnp.dot(z[:, _O_KR:], e_ref[...], precision=HI, preferred_element_type=F32))
    k_ref[...] = _rope(k, mt, MLA_ROPE // 4).astype(BF16)
    v_ref[...] = jnp.dot(kvn, wuv_ref[...], preferred_element_type=F32).astype(BF16)
    gq_ref[...] = (_rope(z[:, _O_GQ:_O_GK], gt, GQA_HD // 4) * (GQA_SCALE * LOG2E)).astype(BF16)
    gk_ref[...] = _rope(z[:, _O_GK:_O_GV], gt, GQA_HD // 4).astype(BF16)
    gv_ref[...] = z[:, _O_GV:_O_KR].astype(BF16)


def _odd_proj(tok, z, tabs, q_norm, kv_norm, w_uq, w_ukv):
    tm = tok.tm
    hq = MLA_HEADS * HEAD_PAD
    wq = jnp.pad(w_uq.reshape(Q_LORA, MLA_HEADS, MLA_NOPE + MLA_ROPE),
                 ((0, 0), (0, 0), (0, HEAD_PAD - MLA_NOPE - MLA_ROPE))).reshape(Q_LORA, hq).astype(BF16)
    wkv = w_ukv.reshape(KV_LORA, MLA_HEADS, MLA_NOPE + MLA_V)
    wk = jnp.pad(wkv[..., :MLA_NOPE], ((0, 0), (0, 0), (0, HEAD_PAD - MLA_NOPE))).reshape(KV_LORA, hq).astype(BF16)
    wv = wkv[..., MLA_NOPE:].reshape(KV_LORA, MLA_HEADS * MLA_V).astype(BF16)
    eye = jnp.eye(MLA_ROPE, dtype=F32)
    e_head = jnp.pad(eye, ((0, LANE - MLA_ROPE), (MLA_NOPE, HEAD_PAD - MLA_NOPE - MLA_ROPE)))
    e = jnp.tile(e_head, (1, MLA_HEADS))
    tab_blk = lambda i: (0, 0, jnp.where(i < tok.n_lat, i % tok.per_seq, tok.per_seq), 0)
    full = lambda shape: pl.BlockSpec(shape, lambda i: (0,) * len(shape))
    widths = (hq, hq, MLA_HEADS * MLA_V, GQA_HEADS * GQA_HD, GQA_KV * GQA_HD, GQA_KV * GQA_HD)
    return pl.pallas_call(
        _odd_proj_kernel,
        grid=(tok.n_all,),
        in_specs=[pl.BlockSpec((tm, OD_IN_PAD), lambda i: (i, 0)),
                  pl.BlockSpec((2, 3, tm, LANE), tab_blk),
                  full((1, Q_LORA)), full((1, KV_LORA)), full((Q_LORA, hq)), full((KV_LORA, hq)),
                  full((KV_LORA, MLA_HEADS * MLA_V)), full((LANE, hq))],
        out_specs=[pl.BlockSpec((tm, w), lambda i: (i, 0)) for w in widths],
        out_shape=[jax.ShapeDtypeStruct((tok.T, w), BF16) for w in widths],
        compiler_params=_cparams(("arbitrary",)),
    )(z, tabs, q_norm.reshape(1, -1), kv_norm.reshape(1, -1), wq, wk, wv, e)


def _mla_attn_kernel(q_ref, k_ref, vt_ref, o_ref, s_scr, p_scr, *, tk, nk, ks):
    tq = q_ref.shape[0]
    nslab = tk // ks
    kq, kv = math.gcd(tk, MLA_KS_QK), math.gcd(tk, MLA_KS_PV)
    dn = (((1,), (1,)), ((), ()))
    qs = [q_ref[:, h * HEAD_PAD:(h + 1) * HEAD_PAD] for h in range(2)]

    def qk_slab(h, c, j, mx):
        if (j * ks) % kq:
            return mx
        r = pl.multiple_of(c * tk + j * ks, ks)
        s = lax.dot_general(k_ref[pl.ds(r, kq), h * HEAD_PAD:(h + 1) * HEAD_PAD], qs[h], dn,
                            preferred_element_type=F32)
        s_scr[h, j * ks:j * ks + kq, :] = s
        return jnp.maximum(mx, jnp.max(s, axis=0, keepdims=True))

    def pv_slab(h, c, j):
        if (j * ks) % kv:
            return 0.0
        return jnp.dot(vt_ref[c, h * MLA_VP:(h + 1) * MLA_VP, j * ks:j * ks + kv], p_scr[h, j * ks:j * ks + kv, :],
                       preferred_element_type=F32)

    def step(x, c_sm, c_pv, c_qk, st):
        y = 1 - x
        m, acc, mc = st[x]
        m_new = jnp.maximum(m, mc)
        alpha = jnp.exp2(m - m_new)
        acc_y = st[y][1]
        mx_y = jnp.full((1, tq), NEG, F32)
        for j in range(nslab):
            acc_y = acc_y + pv_slab(y, c_pv, j)
            mx_y = qk_slab(y, c_qk, j, mx_y)
            p_scr[x, j * ks:(j + 1) * ks, :] = jnp.exp2(s_scr[x, j * ks:(j + 1) * ks, :] - m_new).astype(BF16)
        new = [None, None]
        new[x] = (m_new, alpha * acc, mc)
        new[y] = (st[y][0], acc_y, mx_y)
        return tuple(new)

    def body(c, st):
        st = step(0, c, jnp.maximum(c - 1, 0), c, st)
        return step(1, c, c, jnp.minimum(c + 1, nk - 1), st)

    neg = jnp.full((1, tq), NEG, F32)
    acc0 = jnp.zeros((MLA_VP, tq), F32)
    p_scr[1] = jnp.zeros(p_scr.shape[1:], BF16)
    mx0 = neg
    for j in range(nslab):
        mx0 = qk_slab(0, 0, j, mx0)
    def trip(i, st):
        for u in range(MLA_UNROLL):
            st = body(i * MLA_UNROLL + u, st)
        return st

    st = lax.fori_loop(0, nk // MLA_UNROLL, trip, ((neg, acc0, mx0), (neg, acc0, neg)))
    for c in range(nk - nk % MLA_UNROLL, nk):
        st = body(jnp.int32(c), st)
    acc1 = st[1][1]
    for j in range(nslab):
        acc1 = acc1 + pv_slab(1, nk - 1, j)
    out_t = jnp.concatenate([a[:MLA_V] / a[MLA_V:MLA_V + 1] for a in (st[0][1], acc1)], axis=0)
    o_ref[...] = out_t.T.astype(o_ref.dtype)


MLA_TQ = (512, 256, 128)
MLA_TK = (768, 512, 256, 128)
MLA_KS = 128
MLA_KS_QK = 768
MLA_KS_PV = 256
MLA_UNROLL = 11


def _mla_attention(q, row0, Lq, k, v):
    B, Nk = k.shape[0], k.shape[1]
    tq = _pick(Lq, MLA_TQ)
    tk = _pick(Nk, MLA_TK)
    nk = Nk // tk
    hp = MLA_HEADS // 2
    vt = v.reshape(B, nk, tk, hp, 2, MLA_V).transpose(0, 3, 1, 4, 5, 2)
    vt = jnp.concatenate([vt, jnp.ones((B, hp, nk, 2, MLA_VP - MLA_V, tk), v.dtype)], axis=4)
    vt = vt.reshape(B, hp, nk, 2 * MLA_VP, tk)
    kern = functools.partial(_mla_attn_kernel, tk=tk, nk=nk, ks=math.gcd(tk, MLA_KS))
    return pl.pallas_call(
        kern,
        grid=(B, hp, Lq // tq),
        in_specs=[pl.BlockSpec((tq, 2 * HEAD_PAD), lambda b, h, i: ((row0 + b * Lq) // tq + i, h)),
                  pl.BlockSpec((None, Nk, 2 * HEAD_PAD), lambda b, h, i: (b, 0, h)),
                  pl.BlockSpec((None, None, nk, 2 * MLA_VP, tk), lambda b, h, i: (b, h, 0, 0, 0))],
        out_specs=pl.BlockSpec((None, tq, 2 * MLA_V), lambda b, h, i: (b, i, h)),
        out_shape=jax.ShapeDtypeStruct((B, Lq, MLA_HEADS * MLA_V), BF16),
        scratch_shapes=[pltpu.VMEM((2, tk, tq), F32), pltpu.VMEM((2, tk, tq), BF16)],
        compiler_params=_cparams(("arbitrary", "arbitrary", "arbitrary")),
    )(q, k, vt)


def _gqa_kernel(sink_ref, q_ref, kc_ref, vct_ref, *rest, L, has_band):
    group = GQA_HEADS // GQA_KV
    gw = group * BLK
    if has_band:
        kp_ref, k_ref, kn_ref, vtp_ref, vt_ref, vtn_ref, bias_ref, o_ref = rest
        keys = jnp.concatenate([kp_ref[...], k_ref[...], kn_ref[...], kc_ref[...]], axis=0)
        vals_t = jnp.concatenate([vtp_ref[...], vt_ref[...], vtn_ref[...], vct_ref[...]], axis=1)
    else:
        (o_ref,) = rest
        keys, vals_t = kc_ref[...], vct_ref[...]
    dn = (((1,), (1,)), ((), ()))
    G = range(GQA_KV)
    qs = [jnp.concatenate([q_ref[:, (kh * group + g) * GQA_HD:(kh * group + g + 1) * GQA_HD] for g in range(group)],
                          axis=0) for kh in G]
    s = [lax.dot_general(keys[:, kh * GQA_HD:(kh + 1) * GQA_HD], qs[kh], dn, preferred_element_type=F32) for kh in G]
    if has_band:
        s = [x + bias_ref[...] for x in s]
    sink = [sink_ref[:, kh * gw:(kh + 1) * gw] for kh in G]
    m = [jnp.maximum(jnp.max(s[kh], axis=0, keepdims=True), sink[kh]) for kh in G]
    p = [jnp.exp2(s[kh] - m[kh]) for kh in G]
    den = [jnp.sum(p[kh], axis=0, keepdims=True) + jnp.exp2(sink[kh] - m[kh]) for kh in G]
    ot = [jnp.dot(vals_t[kh * GQA_HD:(kh + 1) * GQA_HD, :], p[kh].astype(BF16), preferred_element_type=F32)
          * (1.0 / den[kh]) for kh in G]
    o = jnp.concatenate(ot, axis=0).T
    o_ref[...] = jnp.concatenate([o[g * BLK:(g + 1) * BLK, kh * GQA_HD:(kh + 1) * GQA_HD]
                                  for kh in G for g in range(group)], axis=1).astype(o_ref.dtype)


def _gqa_attention(sink, q, row0, Lq, kc, vc, k=None, v=None):
    B, CTX = kc.shape[0], kc.shape[1]
    has_band = k is not None
    kw = GQA_KV * GQA_HD
    nb = Lq // BLK
    sink_row = jnp.repeat(sink.astype(F32) * LOG2E, BLK)[None, :]
    in_specs = [pl.BlockSpec((1, GQA_HEADS * BLK), lambda b, i: (0, 0)),
                pl.BlockSpec((BLK, GQA_HEADS * GQA_HD), lambda b, i: ((row0 + b * Lq) // BLK + i, 0)),
                pl.BlockSpec((None, CTX, kw), lambda b, i: (b, 0, 0)),
                pl.BlockSpec((None, kw, CTX), lambda b, i: (b, 0, 0))]
    args = [sink_row, q, kc, jnp.swapaxes(vc, 1, 2)]
    if has_band:
        prev = lambda i: jnp.maximum(i - 1, 0)
        nxt = lambda i: jnp.minimum(i + 1, nb - 1)
        in_specs += [pl.BlockSpec((None, BLK, kw), lambda b, i: (b, prev(i), 0)),
                     pl.BlockSpec((None, BLK, kw), lambda b, i: (b, i, 0)),
                     pl.BlockSpec((None, BLK, kw), lambda b, i: (b, nxt(i), 0)),
                     pl.BlockSpec((None, kw, BLK), lambda b, i: (b, 0, prev(i))),
                     pl.BlockSpec((None, kw, BLK), lambda b, i: (b, 0, i)),
                     pl.BlockSpec((None, kw, BLK), lambda b, i: (b, 0, nxt(i)))]
        vt = jnp.swapaxes(v, 1, 2)
        r = jnp.arange(3 * BLK + CTX)[:, None]
        c = jnp.arange(GQA_HEADS // GQA_KV * BLK)[None, :] % BLK
        band = (jnp.abs(r - BLK - c) <= WINDOW) | (r >= 3 * BLK)
        ok = jnp.stack([band & (r >= BLK), band, band & ((r < 2 * BLK) | (r >= 3 * BLK))])
        bias = jnp.where(ok, 0.0, NEG).astype(F32)
        in_specs += [pl.BlockSpec((None,) + bias.shape[1:],
                                  lambda b, i: (jnp.where(i == 0, 0, jnp.where(i == nb - 1, 2, 1)), 0, 0))]
        args += [k, k, k, vt, vt, vt, bias]
    kern = functools.partial(_gqa_kernel, L=Lq, has_band=has_band)
    return pl.pallas_call(
        kern,
        grid=(B, nb),
        in_specs=in_specs,
        out_specs=pl.BlockSpec((None, BLK, GQA_HEADS * GQA_HD), lambda b, i: (b, i, 0)),
        out_shape=jax.ShapeDtypeStruct((B, Lq, GQA_HEADS * GQA_HD), BF16),
        compiler_params=_cparams(("arbitrary", "arbitrary")),
    )(*args)


def _odd_out_kernel(al_ref, ac_ref, gl_ref, gc_ref, h_ref, mod_ref, wo_ref, o_ref, *, n_lat):
    half = al_ref.shape[1]
    is_lat = pl.program_id(0) < n_lat
    a = jnp.where(is_lat, al_ref[...], ac_ref[...])
    g = jnp.where(is_lat, gl_ref[...], gc_ref[...])
    ol = (jnp.dot(a, wo_ref[:half, :], preferred_element_type=F32)
          + jnp.dot(g, wo_ref[half:, :], preferred_element_type=F32))
    o_ref[...] = h_ref[...] + mod_ref[2:3, :] * ol


def _odd_out(tok, mla_l, mla_c, gqa_l, gqa_c, h, mods_l, w_out):
    tm = tok.tm
    half = mla_l.shape[1]
    lat = pl.BlockSpec((tm, half), lambda i: (jnp.minimum(i, tok.n_lat - 1), 0))
    ctx = pl.BlockSpec((tm, half), lambda i: (jnp.maximum(i - tok.n_lat, 0), 0))
    return pl.pallas_call(
        functools.partial(_odd_out_kernel, n_lat=tok.n_lat),
        grid=(tok.n_all,),
        in_specs=[lat, ctx, lat, ctx,
                  pl.BlockSpec((tm, D), lambda i: (i, 0)),
                  pl.BlockSpec((None, N_MOD, D), lambda i: (tok.mod_row(i), 0, 0)),
                  pl.BlockSpec((D, D), lambda i: (0, 0))],
        out_specs=pl.BlockSpec((tm, D), lambda i: (i, 0)),
        out_shape=jax.ShapeDtypeStruct((tok.T, D), F32),
        compiler_params=_cparams(("arbitrary",)),
    )(mla_l, mla_c, gqa_l, gqa_c, h, mods_l, w_out)


MOE_TR = 512
MOE_TF = 1792
MOE_IDX_BLK = 1024
MOE_NF = EXP_FF // MOE_TF
ROUTE_W = 8


def _router_kernel(h_ref, mod_ref, g_ref, r_ref, y_ref, route_ref):
    y = _ada_norm(h_ref[...], g_ref[...], mod_ref[...], 3, 4)
    y_ref[...] = y
    logits = jnp.dot(y, r_ref[...], precision=HI, preferred_element_type=F32)
    lane = lax.broadcasted_iota(jnp.int32, logits.shape, 1)
    lg = jnp.where(lane < N_EXP, logits, -jnp.inf)
    m1 = jnp.max(lg, axis=-1, keepdims=True)
    i1 = jnp.min(jnp.where(lg == m1, lane, LANE), axis=-1, keepdims=True)
    lg2 = jnp.where(lane == i1, -jnp.inf, lg)
    m2 = jnp.max(lg2, axis=-1, keepdims=True)
    i2 = jnp.min(jnp.where(lg2 == m2, lane, LANE), axis=-1, keepdims=True)
    e = jnp.exp(m2 - m1)
    w1 = 1.0 / (1.0 + e)
    route = (jnp.where(lane == 0, w1, 0.0) + jnp.where(lane == 1, e * w1, 0.0)
             + jnp.where(lane == 2, i1.astype(F32), 0.0) + jnp.where(lane == 3, i2.astype(F32), 0.0))
    route_ref[...] = route[:, :ROUTE_W]


def _moe_router(tok, h, mods_l, gain, router):
    tm = tok.tm
    rp = jnp.pad(router, ((0, 0), (0, LANE - N_EXP)))
    return pl.pallas_call(
        _router_kernel,
        grid=(tok.n_all,),
        in_specs=[pl.BlockSpec((tm, D), lambda i: (i, 0)),
                  pl.BlockSpec((None, N_MOD, D), lambda i: (tok.mod_row(i), 0, 0)),
                  pl.BlockSpec((1, D), lambda i: (0, 0)),
                  pl.BlockSpec((D, LANE), lambda i: (0, 0))],
        out_specs=[pl.BlockSpec((tm, D), lambda i: (i, 0)), pl.BlockSpec((tm, ROUTE_W), lambda i: (i, 0))],
        out_shape=[jax.ShapeDtypeStruct((tok.T, D), F32), jax.ShapeDtypeStruct((tok.T, ROUTE_W), F32)],
        compiler_params=_cparams(("arbitrary",)),
    )(h, mods_l, gain.reshape(1, D), rp)


def _moe_plan(route, tr):
    T = route.shape[0]
    flat = route[:, 2:4].astype(jnp.int32).reshape(-1)
    onehot = (flat[:, None] == jnp.arange(N_EXP, dtype=jnp.int32)[None, :]).astype(jnp.int32)
    csum = jnp.cumsum(onehot, axis=0)
    rank = jnp.sum((csum - onehot) * onehot, axis=1)
    padded = (csum[-1] + tr - 1) // tr * tr
    ends = jnp.cumsum(padded)
    pos = (ends - padded)[flat] + rank
    tpb = MOE_IDX_BLK // tr
    n_tiles = -(-((2 * T + N_EXP * (tr - 1)) // tr) // tpb) * tpb
    src = jnp.zeros((n_tiles * tr,), jnp.int32).at[pos].set(jnp.arange(2 * T, dtype=jnp.int32) // 2,
                                                            unique_indices=True)
    starts = jnp.arange(n_tiles, dtype=jnp.int32) * tr
    tile_expert = jnp.minimum(jnp.sum((starts[:, None] >= ends[None, :]).astype(jnp.int32), axis=1), N_EXP - 1)
    n_valid = (ends[-1] // tr).astype(jnp.int32).reshape(1)
    return src, tile_expert.astype(jnp.int32), n_valid, pos.reshape(T, 2)


def _gather_rows(idx_ref, src_hbm, dst, sem, n):
    def issue(r, carry):
        pltpu.make_async_copy(src_hbm.at[pl.ds(idx_ref[r], 1), :], dst.at[pl.ds(r, 1), :], sem).start()
        return carry

    lax.fori_loop(0, n, issue, 0, unroll=8)


def _wait_rows(src_hbm, dst, sem, n):
    pltpu.make_async_copy(src_hbm.at[pl.ds(0, n), :], dst, sem).wait()


def _moe_expert_kernel(te_ref, nv_ref, idx_ref, idxn_ref, y_hbm, wg_ref, wu_ref, wd_ref, o_ref,
                       xbuf, y_scr, acc_scr, sem, *, tr):
    t, f = pl.program_id(0), pl.program_id(1)
    nf = pl.num_programs(1)
    nv = nv_ref[0]
    valid = t < nv
    slot = lax.rem(t, 2)
    per_f = tr // MOE_NF
    tpb = MOE_IDX_BLK // tr
    nxt_tile = jnp.minimum(t + 1, jnp.maximum(nv - 1, 0))

    @pl.when((f == 0) & (t == 0))
    def _():
        _gather_rows(idx_ref, y_hbm, xbuf.at[0], sem.at[0], tr)

    @pl.when((f == 0) & valid)
    def _():
        _wait_rows(y_hbm, xbuf.at[slot], sem.at[slot], tr)
        y_scr[...] = xbuf[slot].astype(BF16)
        acc_scr[...] = jnp.zeros_like(acc_scr)

    @pl.when(valid)
    def _():
        base = f * per_f
        ibase = lax.rem(nxt_tile, tpb) * tr + base
        nxt = xbuf.at[1 - slot]
        for r in range(per_f):
            pltpu.make_async_copy(y_hbm.at[pl.ds(idxn_ref[ibase + r], 1), :], nxt.at[pl.ds(base + r, 1), :],
                                  sem.at[1 - slot]).start()
        y = y_scr[...]
        gate = jnp.dot(y, wg_ref[...], preferred_element_type=F32)
        up = jnp.dot(y, wu_ref[...], preferred_element_type=F32)
        act = (gate * jax.nn.sigmoid(gate) * up).astype(BF16)
        acc_scr[...] += jnp.dot(act, wd_ref[...], preferred_element_type=F32)

    @pl.when((f == nf - 1) & (t == nv - 1))
    def _():
        _wait_rows(y_hbm, xbuf.at[1 - slot], sem.at[1 - slot], tr)

    @pl.when(f == nf - 1)
    def _():
        o_ref[...] = jnp.where(valid, acc_scr[...], 0.0)


def _moe_experts(y, src, tile_expert, n_valid, wg, wu, wd, tr):
    n_tiles = src.shape[0] // tr
    tf = MOE_TF
    tpb = MOE_IDX_BLK // tr
    kern = functools.partial(_moe_expert_kernel, tr=tr)
    smem = functools.partial(pl.BlockSpec, memory_space=pltpu.SMEM)
    grid_spec = pltpu.PrefetchScalarGridSpec(
        num_scalar_prefetch=2,
        grid=(n_tiles, EXP_FF // tf),
        in_specs=[smem((MOE_IDX_BLK,), lambda t, f, te, nv: (t // tpb,)),
                  smem((MOE_IDX_BLK,), lambda t, f, te, nv: (jnp.minimum(t + 1, jnp.maximum(nv[0] - 1, 0)) // tpb,)),
                  pl.BlockSpec(memory_space=pl.ANY),
                  pl.BlockSpec((None, D, tf), lambda t, f, te, nv: (te[t], 0, f)),
                  pl.BlockSpec((None, D, tf), lambda t, f, te, nv: (te[t], 0, f)),
                  pl.BlockSpec((None, tf, D), lambda t, f, te, nv: (te[t], f, 0))],
        out_specs=pl.BlockSpec((tr, D), lambda t, f, te, nv: (t, 0)),
        scratch_shapes=[pltpu.VMEM((2, tr, D), F32), pltpu.VMEM((tr, D), BF16), pltpu.VMEM((tr, D), F32),
                        pltpu.SemaphoreType.DMA((2,))])
    return pl.pallas_call(
        kern,
        grid_spec=grid_spec,
        out_shape=jax.ShapeDtypeStruct((n_tiles * tr, D), F32),
        compiler_params=_cparams(("arbitrary", "arbitrary")),
    )(tile_expert, n_valid, src, src, y, wg, wu, wd)


def _moe_combine_kernel(idx_ref, idxn_ref, o_hbm, h_ref, route_ref, mod_ref, out_ref, buf, sem, *, tm):
    i = pl.program_id(0)
    n = pl.num_programs(0)
    slot = lax.rem(i, 2)

    @pl.when(i == 0)
    def _():
        _gather_rows(idx_ref, o_hbm, buf.at[0], sem.at[0], 2 * tm)

    nxt = buf.at[1 - slot]
    for r in range(2 * tm):
        pltpu.make_async_copy(o_hbm.at[pl.ds(idxn_ref[r], 1), :], nxt.at[pl.ds(r, 1), :], sem.at[1 - slot]).start()
    _wait_rows(o_hbm, buf.at[slot], sem.at[slot], 2 * tm)
    r = route_ref[...]
    mix = r[:, 0:1] * buf[slot, :tm, :] + r[:, 1:2] * buf[slot, tm:, :]
    out_ref[...] = h_ref[...] + mod_ref[5:6, :] * mix

    @pl.when(i == n - 1)
    def _():
        _wait_rows(o_hbm, nxt, sem.at[1 - slot], 2 * tm)


def _moe_combine(tok, o_sorted, pos, h, route, mods_l):
    tm = tok.tm
    n = tok.n_all
    idx = pos.reshape(n, tm, 2).transpose(0, 2, 1).reshape(-1)
    kern = functools.partial(_moe_combine_kernel, tm=tm)
    smem = functools.partial(pl.BlockSpec, memory_space=pltpu.SMEM)
    return pl.pallas_call(
        kern,
        grid=(n,),
        in_specs=[smem((2 * tm,), lambda i: (i,)),
                  smem((2 * tm,), lambda i: (jnp.minimum(i + 1, n - 1),)),
                  pl.BlockSpec(memory_space=pl.ANY),
                  pl.BlockSpec((tm, D), lambda i: (i, 0)),
                  pl.BlockSpec((tm, ROUTE_W), lambda i: (i, 0)),
                  pl.BlockSpec((None, N_MOD, D), lambda i: (tok.mod_row(i), 0, 0))],
        out_specs=pl.BlockSpec((tm, D), lambda i: (i, 0)),
        out_shape=jax.ShapeDtypeStruct((tok.T, D), F32),
        scratch_shapes=[pltpu.VMEM((2, 2 * tm, D), F32), pltpu.SemaphoreType.DMA((2,))],
        compiler_params=_cparams(("arbitrary",)),
    )(idx, idx, o_sorted, h, route, mods_l)


def _moe(tok, h, mods_l, gain, router, wg, wu, wd):
    y, route = _moe_router(tok, h, mods_l, gain, router)
    src, tile_expert, n_valid, pos = _moe_plan(route, MOE_TR)
    o_sorted = _moe_experts(y, src, tile_expert, n_valid, wg, wu, wd, MOE_TR)
    return _moe_combine(tok, o_sorted, pos, h, route, mods_l)


def _odd_layer(tok, h, mods_l, tabs, p):
    B, L, CTX = tok.B, tok.L, tok.CTX
    nl = B * L
    w = p['w_in']
    w_in = jnp.concatenate([w[:, :Q_LORA + KV_LORA], w[:, 416:1184], w[:, 384:416],
                            jnp.zeros((D, OD_IN_PAD - 1184), w.dtype)], axis=1).astype(BF16)
    z = _norm_matmul(tok, h, mods_l, p['norm_mix'], w_in, OD_IN_PAD)
    q, k, v, gq, gk, gv = _odd_proj(tok, z, tabs, p['q_norm'], p['kv_norm'], p['w_uq'], p['w_ukv'])
    lat = lambda a: a[:nl].reshape(B, L, -1)
    ctx = lambda a: a[nl:].reshape(B, CTX, -1)
    cat = lambda a: jnp.concatenate([ctx(a), lat(a)], axis=1)
    mla_l = _mla_attention(q, 0, L, cat(k), cat(v))
    gqa_l = _gqa_attention(p['sink'], gq, 0, L, ctx(gk), ctx(gv), lat(gk), lat(gv))
    if p['need_ctx']:
        mla_c = _mla_attention(q, nl, CTX, ctx(k), ctx(v))
        gqa_c = _gqa_attention(p['sink'], gq, nl, CTX, ctx(gk), ctx(gv))
    else:
        mla_c = jnp.zeros((B, CTX, MLA_HEADS * MLA_V), BF16)
        gqa_c = jnp.zeros((B, CTX, GQA_HEADS * GQA_HD), BF16)
    rows = lambda a: a.reshape(-1, a.shape[-1])
    h = _odd_out(tok, rows(mla_l), rows(mla_c), rows(gqa_l), rows(gqa_c), h, mods_l, p['w_out'].astype(BF16))
    return _moe(tok, h, mods_l, p['norm_ffn'], p['router'], _to_bf16(*p['moe_wg']), _to_bf16(*p['moe_wu']),
                _to_bf16(*p['moe_wd']))


def _final_norm_kernel(h_ref, g_ref, o_ref):
    o_ref[...] = _rms(h_ref[...], g_ref[...])


def _final_norm(tok, h, gain):
    tm = tok.tm
    return pl.pallas_call(
        _final_norm_kernel,
        grid=(tok.n_lat,),
        in_specs=[pl.BlockSpec((tm, D), lambda i: (i, 0)), pl.BlockSpec((1, D), lambda i: (0, 0))],
        out_specs=pl.BlockSpec((tm, D), lambda i: (i, 0)),
        out_shape=jax.ShapeDtypeStruct((tok.B * tok.L, D), F32),
        compiler_params=_cparams(("arbitrary",)),
    )(h, gain.reshape(1, D))


def kernel(x, c, ctx, c_ctx, mod_w, mod_b, norm_mix, norm_ffn, final_norm,
           ev_w_in, ev_conv_w, ev_conv_b, hy_w1, hy_b1, hy_w2, hy_b2, hy_w3, hy_freq, hy_decay, hy_bias,
           s5_a_re, s5_a_im, s5_log_dt, s5_b_re, s5_b_im, s5_c_re, s5_c_im, s5_d, s5_w_glu, ev_w_out,
           ff_w_gate, ff_w_up, ff_w_down,
           od_w_in, mla_q_norm, mla_w_uq, mla_kv_norm, mla_w_ukv, gqa_sink, od_w_out,
           moe_router, moe_w_gate, moe_w_up, moe_w_down):
    B, L, _ = x.shape
    CTX = ctx.shape[1]
    tok = _Tok(B, L, CTX, _pick(math.gcd(L, B * CTX), (512, 256, 128)))
    cond_t = jnp.concatenate([c, c_ctx[None, :], jnp.zeros((8 - B - 1, D), F32)], axis=0).T
    mods = _modulations(cond_t, B + 1, mod_w, mod_b)
    tabs = _rope_tables(L, tok.tm)
    h = jnp.concatenate([x.reshape(B * L, D), ctx.reshape(B * CTX, D)], axis=0)
    for l in range(DEPTH):
        i = l // 2
        need_ctx = l < DEPTH - 1
        if l % 2 == 0:
            p = dict(norm_mix=norm_mix[l], norm_ffn=norm_ffn[l], w_in=ev_w_in[i], conv_w=ev_conv_w[i],
                     conv_b=ev_conv_b[i],
                     hy=(hy_w1[i], hy_b1[i], hy_w2[i], hy_b2[i], hy_w3[i], hy_freq[i], hy_decay[i]),
                     hy_bias=hy_bias[i],
                     s5=(s5_a_re[i], s5_a_im[i], s5_log_dt[i], s5_b_re[i], s5_b_im[i], s5_c_re[i], s5_c_im[i]),
                     s5_d=s5_d[i], s5_w_glu=s5_w_glu[i], w_out=ev_w_out[i],
                     ff_wg=(ff_w_gate, i), ff_wu=(ff_w_up, i), ff_wd=(ff_w_down, i), need_ctx=need_ctx)
            h = _even_layer(tok, h, mods[l], p)
        else:
            p = dict(norm_mix=norm_mix[l], norm_ffn=norm_ffn[l], w_in=od_w_in[i], q_norm=mla_q_norm[i],
                     w_uq=mla_w_uq[i], kv_norm=mla_kv_norm[i], w_ukv=mla_w_ukv[i], sink=gqa_sink[i],
                     w_out=od_w_out[i], router=moe_router[i], moe_wg=(moe_w_gate, i), moe_wu=(moe_w_up, i),
                     moe_wd=(moe_w_down, i), need_ctx=need_ctx)
            h = _odd_layer(tok, h, mods[l], tabs, p)
    return _final_norm(tok, h, final_norm).reshape(B, L, D)
```

```python
import functools
import math

import jax
import jax.numpy as jnp
from jax import lax
from jax.experimental import pallas as pl
from jax.experimental.pallas import tpu as pltpu

F32 = jnp.float32
BF16 = jnp.bfloat16
HI = lax.Precision.HIGHEST

D = 1024
DEPTH = 4
GRID_W = 64
EPS = 1e-6
NEG = -1e30
N_MOD = 6

HY_CH = 512
HY_ORDER = 2
HY_BANDS = 16
HY_EMB = 1 + 2 * HY_BANDS
HY_FFN = 64
HY_SHIFT = 0.05
HY_COLS = (HY_ORDER + 1) * HY_CH
S5_CH = 512
S5_GROUP = 16
S5_NG = S5_CH // S5_GROUP
S5_P = 64
S5_T = 16
S5_SEG = 8
EV_IN = HY_COLS + S5_CH

MLA_HEADS = 8
MLA_NOPE = 64
MLA_ROPE = 32
MLA_V = 64
Q_LORA = 256
KV_LORA = 128
GQA_HEADS = 8
GQA_KV = 2
GQA_HD = 64
WINDOW = 128
BLK = 128
ROPE_BASE = 10000.0
MLA_SCALE = (MLA_NOPE + MLA_ROPE) ** -0.5
GQA_SCALE = GQA_HD ** -0.5
LOG2E = math.log2(math.e)
HEAD_PAD = 128
MLA_VP = MLA_V + 16
OD_IN_PAD = 1280

D_FF = 2816
N_EXP = 8
EXP_FF = 3584

LANE = 128
FFT_N2 = 128
FFT_PAD = 8
VMEM_LIMIT = 56 * 1024 * 1024
CAST_BLOCK_BYTES = 8 * 1024 * 1024


def _cparams(sem):
    return pltpu.CompilerParams(dimension_semantics=sem, vmem_limit_bytes=VMEM_LIMIT)


def _pick(n, cands):
    for c in cands:
        if n % c == 0:
            return c
    raise ValueError(f"no tile for {n} in {cands}")


def _mod_kernel(ct_ref, w_ref, b_ref, o_ref, *, nrows):
    c = ct_ref[...]
    s = c * jax.nn.sigmoid(c)
    w = w_ref[...]
    rows = [jnp.sum(w * s[:, r:r + 1], axis=0, keepdims=True) for r in range(nrows)]
    rows.append(jnp.zeros((8 - nrows, w.shape[1]), F32))
    o_ref[...] = jnp.concatenate(rows, axis=0) + b_ref[...]


def _modulations(cond_t, nrows, mod_w, mod_b):
    tn = 1536
    out = pl.pallas_call(
        functools.partial(_mod_kernel, nrows=nrows),
        grid=(DEPTH, N_MOD * D // tn),
        in_specs=[pl.BlockSpec((D, 8), lambda l, j: (0, 0)),
                  pl.BlockSpec((None, D, tn), lambda l, j: (l, 0, j)),
                  pl.BlockSpec((None, 1, tn), lambda l, j: (l, 0, j))],
        out_specs=pl.BlockSpec((None, 8, tn), lambda l, j: (l, 0, j)),
        out_shape=jax.ShapeDtypeStruct((DEPTH, 8, N_MOD * D), F32),
        compiler_params=_cparams(("arbitrary", "arbitrary")),
    )(cond_t, mod_w, mod_b.reshape(DEPTH, 1, N_MOD * D))
    return out.reshape(DEPTH, 8, N_MOD, D)


def _cast_kernel(x_ref, o_ref):
    o_ref[...] = x_ref[...].astype(o_ref.dtype)


def _to_bf16(stack, layer):
    shape = stack.shape[1:]
    cols = shape[-1]
    w2 = stack.reshape(-1, cols)
    rows = w2.shape[0] // stack.shape[0]
    tr = next(t for t in (2048, 1024, 512, 256, 128, 8) if rows % t == 0 and t * cols * 4 <= CAST_BLOCK_BYTES)
    nblk = rows // tr
    out = pl.pallas_call(
        _cast_kernel,
        grid=(nblk,),
        in_specs=[pl.BlockSpec((tr, cols), lambda i: (layer * nblk + i, 0))],
        out_specs=pl.BlockSpec((tr, cols), lambda i: (i, 0)),
        out_shape=jax.ShapeDtypeStruct((rows, cols), BF16),
        compiler_params=_cparams(("arbitrary",)),
    )(w2)
    return out.reshape(shape)


class _Tok:
    def __init__(self, B, L, CTX, tm):
        assert L % tm == 0 and (B * CTX) % tm == 0
        self.B, self.L, self.CTX, self.tm = B, L, CTX, tm
        self.n_lat = B * L // tm
        self.n_all = self.n_lat + B * CTX // tm
        self.T = B * (L + CTX)
        self.per_seq = L // tm

    def mod_row(self, i):
        return jnp.where(i < self.n_lat, i // self.per_seq, self.B)


def _ada_norm(x, gain, mod, shift_idx, scale_idx):
    y = x * lax.rsqrt(jnp.mean(x * x, axis=-1, keepdims=True) + EPS) * gain
    return y * (1.0 + mod[scale_idx:scale_idx + 1, :]) + mod[shift_idx:shift_idx + 1, :]


def _norm_mm_kernel(h_ref, mod_ref, g_ref, w_ref, o_ref, y_scr):
    @pl.when(pl.program_id(1) == 0)
    def _():
        y_scr[...] = _ada_norm(h_ref[...], g_ref[...], mod_ref[...], 0, 1).astype(BF16)

    o_ref[...] = jnp.dot(y_scr[...], w_ref[...], preferred_element_type=F32).astype(o_ref.dtype)


def _norm_matmul(tok, h, mods_l, gain, w, tn, out_dtype=F32):
    tm, n = tok.tm, w.shape[1]
    return pl.pallas_call(
        _norm_mm_kernel,
        grid=(tok.n_all, n // tn),
        in_specs=[pl.BlockSpec((tm, D), lambda i, j: (i, 0)),
                  pl.BlockSpec((None, N_MOD, D), lambda i, j: (tok.mod_row(i), 0, 0)),
                  pl.BlockSpec((1, D), lambda i, j: (0, 0)),
                  pl.BlockSpec((D, tn), lambda i, j: (0, j))],
        out_specs=pl.BlockSpec((tm, tn), lambda i, j: (i, j)),
        out_shape=jax.ShapeDtypeStruct((tok.T, n), out_dtype),
        scratch_shapes=[pltpu.VMEM((tm, D), BF16)],
        compiler_params=_cparams(("arbitrary", "arbitrary")),
    )(h, mods_l, gain.reshape(1, D), w)


def _short_conv_kernel(z_ref, zp_ref, zn_ref, w_ref, b_ref, v_ref, x1_ref, x2_ref, *, tm, n_lat, L, CTX):
    i = pl.program_id(0)
    is_lat = i < n_lat
    seqlen = jnp.where(is_lat, L, CTX)
    off = jnp.where(is_lat, i * tm, (i - n_lat) * tm)
    first = lax.rem(off, seqlen) == 0
    last = lax.rem(off + tm, seqlen) == 0
    z = z_ref[...]
    prev_row = jnp.where(first, 0.0, zp_ref[7:8, :])
    next_row = jnp.where(last, 0.0, zn_ref[0:1, :])
    rid = lax.broadcasted_iota(jnp.int32, z.shape, 0)
    zm1 = jnp.where(rid == 0, prev_row, pltpu.roll(z, 1, axis=0))
    zp1 = jnp.where(rid == tm - 1, next_row, pltpu.roll(z, tm - 1, axis=0))
    out = b_ref[...] + zm1 * w_ref[0:1, :] + z * w_ref[1:2, :] + zp1 * w_ref[2:3, :]
    v_ref[...] = out[:, :HY_CH]
    x1_ref[...] = out[:, HY_CH:2 * HY_CH]
    x2_ref[...] = out[:, 2 * HY_CH:]


def _short_conv(tok, z, conv_w, conv_b):
    tm = _pick(math.gcd(tok.L, tok.CTX), (256, 128))
    n_lat = tok.B * tok.L // tm
    n_all = tok.T // tm
    r8 = tm // 8
    kern = functools.partial(_short_conv_kernel, tm=tm, n_lat=n_lat, L=tok.L, CTX=tok.CTX)
    o = jax.ShapeDtypeStruct((tok.T, HY_CH), F32)
    return pl.pallas_call(
        kern,
        grid=(n_all,),
        in_specs=[pl.BlockSpec((tm, HY_COLS), lambda i: (i, 0)),
                  pl.BlockSpec((8, HY_COLS), lambda i: (jnp.maximum(i * r8 - 1, 0), 0)),
                  pl.BlockSpec((8, HY_COLS), lambda i: (jnp.minimum((i + 1) * r8, n_all * r8 - 1), 0)),
                  pl.BlockSpec((8, HY_COLS), lambda i: (0, 0)),
                  pl.BlockSpec((1, HY_COLS), lambda i: (0, 0))],
        out_specs=[pl.BlockSpec((tm, HY_CH), lambda i: (i, 0))] * 3,
        out_shape=[o, o, o],
        compiler_params=_cparams(("arbitrary",)),
    )(z, z, z, jnp.pad(conv_w, ((0, 8 - conv_w.shape[0]), (0, 0))), conv_b.reshape(1, HY_COLS))


def _filter_kernel(f_ref, w1_ref, b1_ref, w2_ref, b2_ref, w3_ref, fr_ref, dec_ref, k_ref, s_ref):
    @pl.when(pl.program_id(0) == 0)
    def _():
        s_ref[...] = jnp.zeros_like(s_ref)

    f = f_ref[...]
    fr = fr_ref[...]
    hid = jnp.sin(fr * (jnp.dot(f, w1_ref[...], precision=HI, preferred_element_type=F32) + b1_ref[...]))
    hid = jnp.sin(fr * (jnp.dot(hid, w2_ref[...], precision=HI, preferred_element_type=F32) + b2_ref[...]))
    h = jnp.dot(hid, w3_ref[...], precision=HI, preferred_element_type=F32)
    t01 = f[:, 0:1]
    valid = f[:, LANE - 1:LANE]
    k = h * (jnp.exp(-t01 * jnp.abs(dec_ref[...])) + HY_SHIFT) * valid
    k_ref[0] = k[:, :HY_CH]
    k_ref[1] = k[:, HY_CH:]
    s_ref[...] += jnp.sum(jnp.abs(k), axis=0, keepdims=True)


def _hyena_filters(L, n, w1, b1, w2, b2, w3, freq, decay):
    row = jnp.arange(n)
    fwd = row < L
    bwd = row > n - L
    t = jnp.where(fwd, row, n - row).astype(F32)
    t01 = t / L
    bands = jnp.linspace(1e-4, HY_BANDS - 1, HY_BANDS, dtype=F32)
    ang = (2.0 * math.pi / L) * t[:, None] * bands[None, :]
    valid = (fwd | bwd).astype(F32)
    feats = jnp.concatenate([t01[:, None], jnp.cos(ang), -jnp.sin(ang),
                             jnp.zeros((n, LANE - 1 - HY_EMB), F32), valid[:, None]], axis=-1)
    w1p = jnp.pad(w1, ((0, LANE - HY_EMB), (0, 0)))
    tr = _pick(L, (512, 256))
    nb_half = n // 2 // tr
    ncol = HY_ORDER * HY_CH
    k, ssum = pl.pallas_call(
        _filter_kernel,
        grid=(n // tr,),
        in_specs=[pl.BlockSpec((tr, LANE), lambda i: (i, 0)),
                  pl.BlockSpec((LANE, HY_FFN), lambda i: (0, 0)),
                  pl.BlockSpec((1, HY_FFN), lambda i: (0, 0)),
                  pl.BlockSpec((HY_FFN, HY_FFN), lambda i: (0, 0)),
                  pl.BlockSpec((1, HY_FFN), lambda i: (0, 0)),
                  pl.BlockSpec((HY_FFN, ncol), lambda i: (0, jnp.where(i < nb_half, 0, 1))),
                  pl.BlockSpec((1, HY_FFN), lambda i: (0, 0)),
                  pl.BlockSpec((1, ncol), lambda i: (0, 0))],
        out_specs=[pl.BlockSpec((HY_ORDER, tr, HY_CH), lambda i: (0, i, 0)),
                   pl.BlockSpec((1, ncol), lambda i: (0, 0))],
        out_shape=[jax.ShapeDtypeStruct((HY_ORDER, n, HY_CH), F32),
                   jax.ShapeDtypeStruct((1, ncol), F32)],
        compiler_params=_cparams(("arbitrary",)),
    )(feats, w1p, b1.reshape(1, -1), w2, b2.reshape(1, -1), w3, freq.reshape(1, -1), decay.reshape(1, ncol))
    return k, (1.0 / ssum).reshape(HY_ORDER, 1, HY_CH)


def _dft_tables(n1, r_in):
    n = n1 * FFT_N2
    k1 = jnp.arange(n1)
    a1 = (2.0 * math.pi / n1) * ((k1[:, None] * jnp.arange(r_in)[None, :]) % n1).astype(F32)
    f1 = jnp.concatenate([jnp.cos(a1), -jnp.sin(a1)], axis=0)
    f3 = jnp.concatenate([jnp.cos(a1).T, -jnp.sin(a1).T], axis=1) / n
    n2 = jnp.arange(FFT_N2)
    kk = k1[:, None, None] + n1 * n2[None, :, None]
    ang = (2.0 * math.pi / n) * ((kk * n2[None, None, :]) % n).astype(F32)
    gr, gi = jnp.cos(ang), -jnp.sin(ang)
    g = jnp.concatenate([jnp.concatenate([gr, -gi], axis=2), jnp.concatenate([gi, gr], axis=2)], axis=1)
    return f1.astype(BF16), f3.astype(BF16), g.astype(BF16), jnp.swapaxes(g, 1, 2).astype(BF16)


def _fft1_kernel(f_ref, x_ref, o_ref):
    o_ref[...] = jnp.dot(f_ref[...], x_ref[...].astype(BF16), preferred_element_type=F32).astype(o_ref.dtype)


def _fft_stage1(f1, x):
    nb, r_in, cols = x.shape
    m = f1.shape[0]
    tn = _pick(cols, (8192, 4096, 2048))
    return pl.pallas_call(
        _fft1_kernel,
        grid=(nb, cols // tn),
        in_specs=[pl.BlockSpec((m, r_in), lambda b, j: (0, 0)),
                  pl.BlockSpec((None, r_in, tn), lambda b, j: (b, 0, j))],
        out_specs=pl.BlockSpec((None, m, tn), lambda b, j: (b, 0, j)),
        out_shape=jax.ShapeDtypeStruct((nb, m, cols), BF16),
        compiler_params=_cparams(("arbitrary", "arbitrary")),
    )(f1, x)


def _fft_filt_kernel(a_ref, g_ref, s_ref, o_ref):
    c = a_ref.shape[-1]
    a = a_ref[...].reshape(2 * FFT_N2, c)
    o_ref[...] = jnp.dot(g_ref[...], a, preferred_element_type=F32) * s_ref[...]


def _fft_filter_spectrum(a, g, inv):
    no, _, n1, _, c = a.shape
    return pl.pallas_call(
        _fft_filt_kernel,
        grid=(n1, no),
        in_specs=[pl.BlockSpec((None, 2, None, FFT_N2, c), lambda k, o: (o, 0, k, 0, 0)),
                  pl.BlockSpec((None, 2 * FFT_N2, 2 * FFT_N2), lambda k, o: (k, 0, 0)),
                  pl.BlockSpec((None, 1, c), lambda k, o: (o, 0, 0))],
        out_specs=pl.BlockSpec((None, None, 2 * FFT_N2, c), lambda k, o: (o, k, 0, 0)),
        out_shape=jax.ShapeDtypeStruct((no, n1, 2 * FFT_N2, c), F32),
        compiler_params=_cparams(("arbitrary", "arbitrary")),
    )(a, g, inv)


FFT_KB = 4


def _fft_mid_kernel(a_ref, g_ref, gt_ref, kh_ref, o_ref):
    kb, c = a_ref.shape[1], a_ref.shape[-1]
    K = range(kb)
    a = [jnp.concatenate([a_ref[0, k], a_ref[1, k]], axis=0) for k in K]
    x = [jnp.dot(g_ref[k], a[k], preferred_element_type=F32) for k in K]
    y = []
    for k in K:
        xr, xi = x[k][:FFT_N2], x[k][FFT_N2:]
        kr, ki = kh_ref[k, :FFT_N2, :], kh_ref[k, FFT_N2:, :]
        y.append(jnp.concatenate([xr * kr - xi * ki, xr * ki + xi * kr], axis=0).astype(BF16))
    bm = [jnp.dot(gt_ref[k], y[k], preferred_element_type=F32) for k in K]
    for k in K:
        o_ref[0, k] = bm[k][:FFT_N2].astype(o_ref.dtype)
        o_ref[1, k] = bm[k][FFT_N2:].astype(o_ref.dtype)


def _fft_mid(a, g, gt, khat):
    nb, _, n1, _, c = a.shape
    kb = math.gcd(n1, FFT_KB)
    return pl.pallas_call(
        _fft_mid_kernel,
        grid=(n1 // kb, nb),
        in_specs=[pl.BlockSpec((None, 2, kb, FFT_N2, c), lambda k, b: (b, 0, k, 0, 0)),
                  pl.BlockSpec((kb, 2 * FFT_N2, 2 * FFT_N2), lambda k, b: (k, 0, 0)),
                  pl.BlockSpec((kb, 2 * FFT_N2, 2 * FFT_N2), lambda k, b: (k, 0, 0)),
                  pl.BlockSpec((kb, 2 * FFT_N2, c), lambda k, b: (k, 0, 0))],
        out_specs=pl.BlockSpec((None, 2, kb, FFT_N2, c), lambda k, b: (b, 0, k, 0, 0)),
        out_shape=jax.ShapeDtypeStruct(a.shape, BF16),
        compiler_params=_cparams(("arbitrary", "arbitrary")),
    )(a, g, gt, khat)


def _fft3_kernel(f_ref, bm_ref, y_ref, gate_ref, bias_ref, o_ref):
    conv = jnp.dot(f_ref[...], bm_ref[...], preferred_element_type=F32)
    o_ref[...] = gate_ref[...] * (conv + y_ref[...] * bias_ref[...])


def _fft_stage3(f3, bm, y, gate, bias_row):
    nb, m, cols = bm.shape
    r = f3.shape[0]
    tn = _pick(cols, (8192, 4096, 2048))
    return pl.pallas_call(
        _fft3_kernel,
        grid=(nb, cols // tn),
        in_specs=[pl.BlockSpec((r, m), lambda b, j: (0, 0)),
                  pl.BlockSpec((None, m, tn), lambda b, j: (b, 0, j)),
                  pl.BlockSpec((None, r, tn), lambda b, j: (b, 0, j)),
                  pl.BlockSpec((None, r, tn), lambda b, j: (b, 0, j)),
                  pl.BlockSpec((1, tn), lambda b, j: (0, j))],
        out_specs=pl.BlockSpec((None, r, tn), lambda b, j: (b, 0, j)),
        out_shape=jax.ShapeDtypeStruct((nb, r, cols), F32),
        compiler_params=_cparams(("arbitrary", "arbitrary")),
    )(f3, bm, y, gate, bias_row)


def _fft1_tok_kernel(f_ref, x_ref, o_ref, s_scr, *, rows):
    n1 = f_ref.shape[0]
    f = f_ref[...]
    pitch = n1 + FFT_PAD

    def body(n2, carry):
        xs = x_ref[pl.ds(n2, rows, stride=FFT_N2), :].astype(BF16)
        s_scr[pl.ds(pl.multiple_of(n2 * pitch, 8), n1), :] = jnp.dot(f, xs, preferred_element_type=F32)
        return carry

    lax.fori_loop(0, FFT_N2, body, 0, unroll=8)

    def emit(k1, carry):
        o_ref[k1] = s_scr[pl.ds(k1, FFT_N2, stride=pitch), :].astype(o_ref.dtype)
        return carry

    lax.fori_loop(0, n1, emit, 0, unroll=4)


def _fft_stage1_tok(f1, x, nb):
    c = x.shape[1]
    rows = f1.shape[1]
    n = rows * FFT_N2
    n1 = f1.shape[0] // 2
    return pl.pallas_call(
        functools.partial(_fft1_tok_kernel, rows=rows),
        grid=(nb, c // LANE, 2),
        in_specs=[pl.BlockSpec((None, n1, rows), lambda b, j, r: (r, 0, 0)),
                  pl.BlockSpec((n, LANE), lambda b, j, r: (b, j))],
        out_specs=pl.BlockSpec((None, None, n1, FFT_N2, LANE), lambda b, j, r: (b, r, 0, 0, j)),
        out_shape=jax.ShapeDtypeStruct((nb, 2, n1, FFT_N2, c), BF16),
        scratch_shapes=[pltpu.VMEM((FFT_N2 * (n1 + FFT_PAD), LANE), F32)],
        compiler_params=_cparams(("arbitrary", "arbitrary", "arbitrary")),
    )(f1.reshape(2, n1, rows), x)


def _fft3_tok_kernel(f_ref, bm_ref, y_ref, gate_ref, bias_ref, o_ref, s_scr, t_scr, *, rows):
    half = pl.program_id(2)
    n1 = bm_ref.shape[0]
    sp, tp = FFT_N2 + FFT_PAD, rows + FFT_PAD

    def stage(k1, carry):
        s_scr[pl.ds(pl.multiple_of(k1 * sp, 8), FFT_N2), :] = bm_ref[k1].astype(F32)
        return carry

    lax.fori_loop(0, n1, stage, 0, unroll=4)
    f = f_ref[...]

    def part(n2):
        return jnp.dot(f, s_scr[pl.ds(n2, n1, stride=sp), :].astype(BF16), preferred_element_type=F32)

    def dst(n2):
        return pl.ds(pl.multiple_of(n2 * tp, 8), rows)

    @pl.when(half == 0)
    def _():
        def body(n2, carry):
            t_scr[dst(n2), :] = part(n2)
            return carry
        lax.fori_loop(0, FFT_N2, body, 0, unroll=8)

    @pl.when(half == 1)
    def _():
        def body(n2, carry):
            t_scr[dst(n2), :] += part(n2)
            return carry
        lax.fori_loop(0, FFT_N2, body, 0, unroll=8)

        def emit(r, carry):
            tok = pl.ds(pl.multiple_of(r * FFT_N2, FFT_N2), FFT_N2)
            conv = t_scr[pl.ds(r, FFT_N2, stride=tp), :]
            o_ref[tok, :] = gate_ref[tok, :] * (conv + y_ref[tok, :] * bias_ref[...])
            return carry
        lax.fori_loop(0, rows, emit, 0, unroll=2)


def _fft_stage3_tok(f3, bm, y, gate, bias):
    nb, _, n1, _, c = bm.shape
    rows = f3.shape[0]
    n = rows * FFT_N2
    tok = pl.BlockSpec((n, LANE), lambda b, j, r: (b, j))
    return pl.pallas_call(
        functools.partial(_fft3_tok_kernel, rows=rows),
        grid=(nb, c // LANE, 2),
        in_specs=[pl.BlockSpec((None, rows, n1), lambda b, j, r: (r, 0, 0)),
                  pl.BlockSpec((None, None, n1, FFT_N2, LANE), lambda b, j, r: (b, r, 0, 0, j)),
                  tok, tok,
                  pl.BlockSpec((1, LANE), lambda b, j, r: (0, j))],
        out_specs=tok,
        out_shape=jax.ShapeDtypeStruct((nb * n, c), F32),
        scratch_shapes=[pltpu.VMEM((n1 * (FFT_N2 + FFT_PAD), LANE), F32),
                        pltpu.VMEM((FFT_N2 * (rows + FFT_PAD), LANE), F32)],
        compiler_params=_cparams(("arbitrary", "arbitrary", "arbitrary")),
    )(f3.reshape(rows, 2, n1).transpose(1, 0, 2), bm, y, gate, bias.astype(F32).reshape(1, c))


def _hyena_long(v, x1, x2, B, L, hy, bias):
    r_in = L // FFT_N2
    n1 = 2 * r_in
    f1, f3, g, gt = _dft_tables(n1, r_in)
    f1k = _dft_tables(n1, n1)[0]
    k, inv = _hyena_filters(L, n1 * FFT_N2, *hy)
    khat = _fft_filter_spectrum(_fft_stage1_tok(f1k, k.reshape(HY_ORDER * n1 * FFT_N2, HY_CH), HY_ORDER), g, inv)
    y = v
    for o, gate in enumerate((x1, x2)):
        bm = _fft_mid(_fft_stage1_tok(f1, y, B), g, gt, khat[o])
        y = _fft_stage3_tok(f3, bm, y, gate, bias[o])
    return y


def _hyena_sequence(v, x1, x2, hy, bias):
    B, L, C = v.shape
    r_valid = L // FFT_N2
    r_in = max(r_valid, 16)
    n1 = max(2 * r_valid, r_in)
    n = n1 * FFT_N2
    if r_valid == r_in:
        return _hyena_long(v.reshape(B * L, C), x1.reshape(B * L, C), x2.reshape(B * L, C), B, L, hy,
                           bias).reshape(B, L, C)
    f1, f3, g, gt = _dft_tables(n1, r_in)
    f1k = _dft_tables(n1, n1)[0]
    k, inv = _hyena_filters(L, n, *hy)
    ak = _fft_stage1(f1k, k.reshape(HY_ORDER, n1, FFT_N2 * C))
    khat = _fft_filter_spectrum(ak.reshape(HY_ORDER, 2, n1, FFT_N2, C), g, inv)
    cols = FFT_N2 * C

    def view(a):
        a = a.reshape(B, r_valid, cols)
        return a if r_in == r_valid else jnp.pad(a, ((0, 0), (0, r_in - r_valid), (0, 0)))

    y = view(v)
    for o, gate in enumerate((view(x1), view(x2))):
        a = _fft_stage1(f1, y)
        bm = _fft_mid(a.reshape(B, 2, n1, FFT_N2, C), g, gt, khat[o])
        y = _fft_stage3(f3, bm.reshape(B, 2 * n1, cols), y, gate, jnp.tile(bias[o].astype(F32), FFT_N2)[None, :])
    return y[:, :r_valid].reshape(B, L, C)


def _s5_tables(a_re, a_im, log_dt, b_re, b_im, c_re, c_im, jj_ctx, jj_lat):
    lam = lax.complex(jnp.minimum(a_re.astype(F32), -1e-4), a_im.astype(F32))
    dt = jnp.exp(log_dt.astype(F32))[..., None]
    abar = jnp.exp(lam * dt)
    bbar = ((abar - 1.0) / lam)[..., None] * lax.complex(b_re.astype(F32), b_im.astype(F32))
    cmat = lax.complex(c_re.astype(F32), c_im.astype(F32))
    T = S5_T

    def powers(m):
        m = jnp.asarray(m, F32)
        return jnp.exp(lam * dt * m.reshape(m.shape + (1, 1, 1)))

    pw = powers(jnp.arange(T + 1))
    kt = jnp.real(jnp.einsum('dgop,tdgp,dgpi->tdgoi', cmat, pw[:T], bbar, precision=HI))
    tt = jnp.arange(T)
    lag = tt[None, :] - tt[:, None]
    w_intra = jnp.where((lag >= 0)[:, :, None, None, None, None],
                        kt[jnp.clip(lag, 0, T - 1)], 0.0)
    w_intra = jnp.stack([w_intra[:, :, 0], w_intra[::-1, ::-1, 1]], axis=2)
    w_intra = w_intra.transpose(2, 3, 0, 5, 1, 4).reshape(2, S5_NG, T * S5_GROUP, T * S5_GROUP)
    wb = pw[T - 1 - tt][..., None] * bbar[None]
    wb = jnp.stack([wb[:, 0], wb[::-1, 1]], axis=1)
    wb = wb.transpose(1, 2, 0, 4, 3).reshape(2, S5_NG, T * S5_GROUP, S5_P)
    w_cat = jnp.concatenate([w_intra, jnp.real(wb), jnp.imag(wb)], axis=-1)
    cp = cmat[None] * pw[1:, :, :, None, :]
    cp = jnp.stack([cp[:, 0], cp[::-1, 1]], axis=1)
    cp = cp.transpose(1, 2, 4, 0, 3).reshape(2, S5_NG, S5_P, T * S5_GROUP)
    c_cat = jnp.concatenate([jnp.real(cp), -jnp.imag(cp)], axis=2)

    def coef(z):
        zr, zi = jnp.real(z), jnp.imag(z)
        return jnp.stack([jnp.concatenate([zr, zr], -1), jnp.concatenate([-zi, zi], -1)], axis=-2)

    step = coef(powers(jnp.array(T)))
    seg_c, seg_l = (coef(powers(jnp.array(T * n))) for n in (jj_ctx, jj_lat))
    coefs = jnp.pad(jnp.concatenate([step, seg_c, seg_l], axis=2), ((0, 0), (0, 0), (0, 2), (0, 0)))
    ptab = coef(powers(T * jnp.arange(max(jj_ctx, jj_lat)))).transpose(1, 2, 0, 3, 4)
    return w_cat.astype(BF16), c_cat.astype(BF16), coefs, ptab


def _cmul(coef_a, coef_b, s):
    return coef_a * s + coef_b * pltpu.roll(s, S5_P, axis=1)


S5_GPB = LANE // S5_GROUP
S5_TC = S5_T * S5_GROUP


def _s5_in_kernel(z_ref, w_ref, yi_ref, ds_ref):
    nj = z_ref.shape[0] // S5_T
    ws = [z_ref[pl.ds(t, nj, stride=S5_T), :].T for t in range(S5_T)]
    for g in range(S5_GPB):
        vt = jnp.concatenate([w[S5_GROUP * g:S5_GROUP * (g + 1), :] for w in ws], axis=0)
        v = vt.T.astype(BF16)
        for d in range(2):
            o = jnp.dot(v, w_ref[d, g], preferred_element_type=F32)
            yi_ref[d, g] = o[:, :S5_TC]
            ds_ref[d, g] = o[:, S5_TC:]


def _s5_in(z, w_cat, nj):
    T = z.shape[0]
    R = T // S5_T
    col0 = HY_COLS // LANE
    return pl.pallas_call(
        _s5_in_kernel,
        grid=(R // nj, S5_CH // LANE),
        in_specs=[pl.BlockSpec((nj * S5_T, LANE), lambda i, c: (i, col0 + c)),
                  pl.BlockSpec((2, S5_GPB, S5_TC, S5_TC + 2 * S5_P), lambda i, c: (0, c, 0, 0))],
        out_specs=[pl.BlockSpec((2, S5_GPB, nj, S5_TC), lambda i, c: (0, c, i, 0)),
                   pl.BlockSpec((2, S5_GPB, nj, 2 * S5_P), lambda i, c: (0, c, i, 0))],
        out_shape=[jax.ShapeDtypeStruct((2, S5_NG, R, S5_TC), F32),
                   jax.ShapeDtypeStruct((2, S5_NG, R, 2 * S5_P), F32)],
        compiler_params=_cparams(("arbitrary", "arbitrary")),
    )(z, w_cat)


def _s5_scan_kernel(ds_ref, yi_ref, c_ref, cf_ref, p_ref, y_ref, sp_scr, *, parts, reverse):
    a1, a2 = cf_ref[0:1, :], cf_ref[1:2, :]
    rid = lax.broadcasted_iota(jnp.int32, (S5_SEG, 2 * S5_P), 0)
    first, last = (S5_SEG - 1, 0) if reverse else (0, S5_SEG - 1)
    shift = S5_SEG - 1 if reverse else 1
    nb = len(parts[0][0])
    zero = jnp.zeros((S5_SEG, 2 * S5_P), F32)
    s0 = [zero] * nb
    for pi, (bases, jj) in enumerate(parts):
        g1, g2 = cf_ref[2 + 2 * pi:3 + 2 * pi, :], cf_ref[3 + 2 * pi:4 + 2 * pi, :]

        def rows(b, k, bases=bases, jj=jj):
            return pl.ds(bases[b] + (jj - 1 - k if reverse else k), S5_SEG, stride=jj)

        def local_step(k, states, rows=rows):
            new = []
            for b in range(nb):
                sp_scr[rows(b, k), :] = states[b]
                new.append(_cmul(a1, a2, states[b]) + ds_ref[rows(b, k), :])
            return tuple(new)

        ends = lax.fori_loop(0, jj, local_step, (zero,) * nb)
        carries, nxt_s0 = [], []
        for b in range(nb):
            c = jnp.where(rid == first, s0[b], 0.0)
            for _ in range(S5_SEG - 1):
                c = jnp.where(rid == first, s0[b], pltpu.roll(ends[b] + _cmul(g1, g2, c), shift, axis=0))
            fin = ends[b] + _cmul(g1, g2, c)
            nxt_s0.append(jnp.broadcast_to(fin[last:last + 1, :], fin.shape))
            carries.append((c, pltpu.roll(c, S5_P, axis=1)))

        def fix_step(k, carry, rows=rows, carries=carries):
            p = p_ref[k]
            for b in range(nb):
                c, cs = carries[b]
                sp_scr[rows(b, k), :] += p[0:1, :] * c + p[1:2, :] * cs
            return carry

        lax.fori_loop(0, jj, fix_step, 0)
        s0 = nxt_s0
    y_ref[...] = yi_ref[...] + jnp.dot(sp_scr[...].astype(BF16), c_ref[...], preferred_element_type=F32)


def _s5_scan(d, ds, yi, c_cat, coefs, ptab, parts):
    R = ds.shape[2]
    jjm = ptab.shape[2]
    kern = functools.partial(_s5_scan_kernel, parts=parts, reverse=(d == 1))
    return pl.pallas_call(
        kern,
        grid=(S5_NG,),
        in_specs=[pl.BlockSpec((None, None, R, 2 * S5_P), lambda g: (d, g, 0, 0)),
                  pl.BlockSpec((None, None, R, S5_TC), lambda g: (d, g, 0, 0)),
                  pl.BlockSpec((None, None, 2 * S5_P, S5_TC), lambda g: (d, g, 0, 0)),
                  pl.BlockSpec((None, None, 8, 2 * S5_P), lambda g: (d, g, 0, 0)),
                  pl.BlockSpec((None, None, jjm, 2, 2 * S5_P), lambda g: (d, g, 0, 0, 0))],
        out_specs=pl.BlockSpec((None, R, S5_TC), lambda g: (g, 0, 0)),
        out_shape=jax.ShapeDtypeStruct((S5_NG, R, S5_TC), F32),
        scratch_shapes=[pltpu.VMEM((R, 2 * S5_P), F32)],
        compiler_params=_cparams(("arbitrary",)),
    )(ds, yi, c_cat, coefs, ptab)


def _s5_out_kernel(yf_ref, yb_ref, o_ref):
    nj = yf_ref.shape[1]
    yts = [(yf_ref[g] + yb_ref[g]).T for g in range(S5_GPB)]
    for t in range(S5_T):
        zt = jnp.concatenate([y[S5_GROUP * t:S5_GROUP * (t + 1), :] for y in yts], axis=0)
        o_ref[pl.ds(t, nj, stride=S5_T), :] = zt.T


def _s5_out(yf, yb, nj):
    R = yf.shape[1]
    spec = pl.BlockSpec((S5_GPB, nj, S5_TC), lambda i, c: (c, i, 0))
    return pl.pallas_call(
        _s5_out_kernel,
        grid=(R // nj, S5_CH // LANE),
        in_specs=[spec, spec],
        out_specs=pl.BlockSpec((nj * S5_T, LANE), lambda i, c: (i, c)),
        out_shape=jax.ShapeDtypeStruct((R * S5_T, S5_CH), F32),
        compiler_params=_cparams(("arbitrary", "arbitrary")),
    )(yf, yb)


def _s5_mixer(tok, z, s5):
    B = tok.B
    cl, cc = tok.L // S5_T, tok.CTX // S5_T
    jl, jc = cl // S5_SEG, cc // S5_SEG
    w_cat, c_cat, coefs, ptab = _s5_tables(*s5, jc, jl)
    yi, ds = _s5_in(z, w_cat, math.gcd(B * cl, B * cc, 64))
    lat = (tuple(b * cl for b in range(B)), jl)
    ctx = (tuple(B * cl + b * cc for b in range(B)), jc)
    yf = _s5_scan(0, ds, yi, c_cat, coefs, ptab, (ctx, lat))
    yb = _s5_scan(1, ds, yi, c_cat, coefs, ptab, (ctx, lat))
    return _s5_out(yf, yb, math.gcd(B * cl, B * cc, 64))


def _gelu_tanh(x):
    return 0.5 * x * (1.0 + jnp.tanh(math.sqrt(2.0 / math.pi) * (x + 0.044715 * (x * x * x))))


def _even_out_kernel(hl_ref, hc_ref, ys_ref, u_ref, h_ref, mod_ref, dsk_ref, wg_ref, wo_ref, o_ref, *, n_lat):
    y = _gelu_tanh(ys_ref[...] + dsk_ref[...] * u_ref[...])
    s = y * jax.nn.sigmoid(jnp.dot(y.astype(BF16), wg_ref[...], preferred_element_type=F32))
    hy = jnp.where(pl.program_id(0) < n_lat, hl_ref[...], hc_ref[...])
    ol = (jnp.dot(hy.astype(BF16), wo_ref[:HY_CH, :], preferred_element_type=F32)
          + jnp.dot(s.astype(BF16), wo_ref[HY_CH:, :], preferred_element_type=F32))
    o_ref[...] = h_ref[...] + mod_ref[2:3, :] * ol


def _even_out(tok, hl, hc, ys, z, h, mods_l, dsk, w_glu, w_out):
    tm = tok.tm
    return pl.pallas_call(
        functools.partial(_even_out_kernel, n_lat=tok.n_lat),
        grid=(tok.n_all,),
        in_specs=[pl.BlockSpec((tm, HY_CH), lambda i: (jnp.minimum(i, tok.n_lat - 1), 0)),
                  pl.BlockSpec((tm, HY_CH), lambda i: (jnp.maximum(i - tok.n_lat, 0), 0)),
                  pl.BlockSpec((tm, S5_CH), lambda i: (i, 0)),
                  pl.BlockSpec((tm, S5_CH), lambda i: (i, HY_COLS // S5_CH)),
                  pl.BlockSpec((tm, D), lambda i: (i, 0)),
                  pl.BlockSpec((None, N_MOD, D), lambda i: (tok.mod_row(i), 0, 0)),
                  pl.BlockSpec((1, S5_CH), lambda i: (0, 0)),
                  pl.BlockSpec((S5_CH, S5_CH), lambda i: (0, 0)),
                  pl.BlockSpec((D, D), lambda i: (0, 0))],
        out_specs=pl.BlockSpec((tm, D), lambda i: (i, 0)),
        out_shape=jax.ShapeDtypeStruct((tok.T, D), F32),
        compiler_params=_cparams(("arbitrary",)),
    )(hl, hc, ys, z, h, mods_l, dsk.reshape(1, S5_CH), w_glu, w_out)


def _ffn_kernel(h_ref, mod_ref, g_ref, wg_ref, wu_ref, wd_ref, o_ref, y_scr, acc_scr):
    j = pl.program_id(1)

    @pl.when(j == 0)
    def _():
        y_scr[...] = _ada_norm(h_ref[...], g_ref[...], mod_ref[...], 3, 4).astype(BF16)
        acc_scr[...] = jnp.zeros_like(acc_scr)

    y = y_scr[...]
    gate = jnp.dot(y, wg_ref[...], preferred_element_type=F32)
    up = jnp.dot(y, wu_ref[...], preferred_element_type=F32)
    act = (gate * jax.nn.sigmoid(gate) * up).astype(BF16)
    acc_scr[...] += jnp.dot(act, wd_ref[...], preferred_element_type=F32)

    @pl.when(j == pl.num_programs(1) - 1)
    def _():
        o_ref[...] = h_ref[...] + mod_ref[5:6, :] * acc_scr[...]


def _ffn(tok, h, mods_l, gain, wg, wu, wd):
    tm = tok.tm
    ff = wg.shape[1]
    tf = _pick(ff, (1408, 512, 256, 128))
    return pl.pallas_call(
        _ffn_kernel,
        grid=(tok.n_all, ff // tf),
        in_specs=[pl.BlockSpec((tm, D), lambda i, j: (i, 0)),
                  pl.BlockSpec((None, N_MOD, D), lambda i, j: (tok.mod_row(i), 0, 0)),
                  pl.BlockSpec((1, D), lambda i, j: (0, 0)),
                  pl.BlockSpec((D, tf), lambda i, j: (0, j)),
                  pl.BlockSpec((D, tf), lambda i, j: (0, j)),
                  pl.BlockSpec((tf, D), lambda i, j: (j, 0))],
        out_specs=pl.BlockSpec((tm, D), lambda i, j: (i, 0)),
        out_shape=jax.ShapeDtypeStruct((tok.T, D), F32),
        scratch_shapes=[pltpu.VMEM((tm, D), BF16), pltpu.VMEM((tm, D), F32)],
        compiler_params=_cparams(("arbitrary", "arbitrary")),
    )(h, mods_l, gain.reshape(1, D), wg, wu, wd)


def _even_layer(tok, h, mods_l, p):
    B, L, CTX = tok.B, tok.L, tok.CTX
    nl = B * L
    z = _norm_matmul(tok, h, mods_l, p['norm_mix'], p['w_in'].astype(BF16), EV_IN)
    v, x1, x2 = _short_conv(tok, z, p['conv_w'], p['conv_b'])
    ctx = lambda a: a[nl:].reshape(B, CTX, -1)
    if L % (16 * FFT_N2) == 0:
        hl = _hyena_long(v, x1, x2, B, L, p['hy'], p['hy_bias'])
    else:
        lat = lambda a: a[:nl].reshape(B, L, -1)
        hl = _hyena_sequence(lat(v), lat(x1), lat(x2), p['hy'], p['hy_bias']).reshape(nl, HY_CH)
    ys_all = _s5_mixer(tok, z, p['s5'])
    if p['need_ctx']:
        hc = _hyena_sequence(ctx(v), ctx(x1), ctx(x2), p['hy'], p['hy_bias']).reshape(B * CTX, HY_CH)
    else:
        hc = jnp.zeros((B * CTX, HY_CH), F32)
    h = _even_out(tok, hl, hc, ys_all, z, h, mods_l, p['s5_d'], p['s5_w_glu'].astype(BF16),
                  p['w_out'].astype(BF16))
    return _ffn(tok, h, mods_l, p['norm_ffn'], _to_bf16(*p['ff_wg']), _to_bf16(*p['ff_wu']), _to_bf16(*p['ff_wd']))


def _rope_tables(L, tm):
    t = jnp.arange(L)
    row = (t // GRID_W).astype(F32)[:, None]
    col = (t % GRID_W).astype(F32)[:, None]

    def pattern(dim):
        nf = dim // 4
        inv = ROPE_BASE ** (-jnp.arange(nf, dtype=F32) / nf)
        ar, ac = row * inv[None, :], col * inv[None, :]
        cos = jnp.concatenate([jnp.cos(ar)] * 2 + [jnp.cos(ac)] * 2, axis=1)
        z = jnp.zeros((L, nf), F32)
        s_up = jnp.concatenate([-jnp.sin(ar), z, -jnp.sin(ac), z], axis=1)
        s_dn = jnp.concatenate([z, jnp.sin(ar), z, jnp.sin(ac)], axis=1)
        return cos, s_up, s_dn

    def pad_mla(a, fill):
        return jnp.concatenate([jnp.full((L, MLA_NOPE), fill, F32), a,
                                jnp.full((L, HEAD_PAD - MLA_NOPE - MLA_ROPE), fill, F32)], axis=1)

    cm, um, dm = pattern(MLA_ROPE)
    cg, ug, dg = pattern(GQA_HD)
    mla = jnp.stack([pad_mla(cm, 1.0), pad_mla(um, 0.0), pad_mla(dm, 0.0)])
    gqa = jnp.stack([jnp.tile(cg, (1, 2)), jnp.tile(ug, (1, 2)), jnp.tile(dg, (1, 2))])
    ident = jnp.stack([jnp.ones((tm, LANE), F32), jnp.zeros((tm, LANE), F32), jnp.zeros((tm, LANE), F32)])
    return jnp.stack([jnp.concatenate([mla, ident], axis=1), jnp.concatenate([gqa, ident], axis=1)])


def _rope(x, tab, w):
    outs = []
    for h in range(x.shape[1] // LANE):
        xs = x[:, h * LANE:(h + 1) * LANE]
        outs.append(xs * tab[0] + pltpu.roll(xs, LANE - w, axis=1) * tab[1] + pltpu.roll(xs, w, axis=1) * tab[2])
    return outs[0] if len(outs) == 1 else jnp.concatenate(outs, axis=1)


def _rms(x, g):
    return x * lax.rsqrt(jnp.mean(x * x, axis=-1, keepdims=True) + EPS) * g


_O_CQ, _O_CKV, _O_GQ, _O_GK, _O_GV, _O_KR = 0, 256, 384, 896, 1024, 1152


def _odd_proj_kernel(z_ref, tab_ref, qn_ref, kvn_ref, wuq_ref, wuk_ref, wuv_ref, e_ref,
                     q_ref, k_ref, v_ref, gq_ref, gk_ref, gv_ref):
    z = z_ref[...]
    mt, gt = tab_ref[0], tab_ref[1]
    qn = _rms(z[:, _O_CQ:_O_CKV], qn_ref[...]).astype(BF16)
    q = jnp.dot(qn, wuq_ref[...], preferred_element_type=F32)
    q_ref[...] = (_rope(q, mt, MLA_ROPE // 4) * (MLA_SCALE * LOG2E)).astype(BF16)
    kvn = _rms(z[:, _O_CKV:_O_GQ], kvn_ref[...]).astype(BF16)
    k = (jnp.dot(kvn, wuk_ref[...], preferred_element_type=F32)
         + jnp.dot(z[:, _O_KR:], e_ref[...], precision=HI, preferred_element_type=F32))
    k_ref[...] = _rope(k, mt, MLA_ROPE // 4).astype(BF16)
    v_ref[...] = jnp.dot(kvn, wuv_ref[...], preferred_element_type=F32).astype(BF16)
    gq_ref[...] = (_rope(z[:, _O_GQ:_O_GK], gt, GQA_HD // 4) * (GQA_SCALE * LOG2E)).astype(BF16)
    gk_ref[...] = _rope(z[:, _O_GK:_O_GV], gt, GQA_HD // 4).astype(BF16)
    gv_ref[...] = z[:, _O_GV:_O_KR].astype(BF16)


def _odd_proj(tok, z, tabs, q_norm, kv_norm, w_uq, w_ukv):
    tm = tok.tm
    hq = MLA_HEADS * HEAD_PAD
    wq = jnp.pad(w_uq.reshape(Q_LORA, MLA_HEADS, MLA_NOPE + MLA_ROPE),
                 ((0, 0), (0, 0), (0, HEAD_PAD - MLA_NOPE - MLA_ROPE))).reshape(Q_LORA, hq).astype(BF16)
    wkv = w_ukv.reshape(KV_LORA, MLA_HEADS, MLA_NOPE + MLA_V)
    wk = jnp.pad(wkv[..., :MLA_NOPE], ((0, 0), (0, 0), (0, HEAD_PAD - MLA_NOPE))).reshape(KV_LORA, hq).astype(BF16)
    wv = wkv[..., MLA_NOPE:].reshape(KV_LORA, MLA_HEADS * MLA_V).astype(BF16)
    eye = jnp.eye(MLA_ROPE, dtype=F32)
    e_head = jnp.pad(eye, ((0, LANE - MLA_ROPE), (MLA_NOPE, HEAD_PAD - MLA_NOPE - MLA_ROPE)))
    e = jnp.tile(e_head, (1, MLA_HEADS))
    tab_blk = lambda i: (0, 0, jnp.where(i < tok.n_lat, i % tok.per_seq, tok.per_seq), 0)
    full = lambda shape: pl.BlockSpec(shape, lambda i: (0,) * len(shape))
    widths = (hq, hq, MLA_HEADS * MLA_V, GQA_HEADS * GQA_HD, GQA_KV * GQA_HD, GQA_KV * GQA_HD)
    return pl.pallas_call(
        _odd_proj_kernel,
        grid=(tok.n_all,),
        in_specs=[pl.BlockSpec((tm, OD_IN_PAD), lambda i: (i, 0)),
                  pl.BlockSpec((2, 3, tm, LANE), tab_blk),
                  full((1, Q_LORA)), full((1, KV_LORA)), full((Q_LORA, hq)), full((KV_LORA, hq)),
                  full((KV_LORA, MLA_HEADS * MLA_V)), full((LANE, hq))],
        out_specs=[pl.BlockSpec((tm, w), lambda i: (i, 0)) for w in widths],
        out_shape=[jax.ShapeDtypeStruct((tok.T, w), BF16) for w in widths],
        compiler_params=_cparams(("arbitrary",)),
    )(z, tabs, q_norm.reshape(1, -1), kv_norm.reshape(1, -1), wq, wk, wv, e)


def _mla_attn_kernel(q_ref, k_ref, vt_ref, o_ref, s_scr, p_scr, *, tk, nk, ks):
    tq = q_ref.shape[0]
    nslab = tk // ks
    kq, kv = math.gcd(tk, MLA_KS_QK), math.gcd(tk, MLA_KS_PV)
    dn = (((1,), (1,)), ((), ()))
    qs = [q_ref[:, h * HEAD_PAD:(h + 1) * HEAD_PAD] for h in range(2)]

    def qk_slab(h, c, j, mx):
        if (j * ks) % kq:
            return mx
        r = pl.multiple_of(c * tk + j * ks, ks)
        s = lax.dot_general(k_ref[pl.ds(r, kq), h * HEAD_PAD:(h + 1) * HEAD_PAD], qs[h], dn,
                            preferred_element_type=F32)
        s_scr[h, j * ks:j * ks + kq, :] = s
        return jnp.maximum(mx, jnp.max(s, axis=0, keepdims=True))

    def pv_slab(h, c, j):
        if (j * ks) % kv:
            return 0.0
        return jnp.dot(vt_ref[c, h * MLA_VP:(h + 1) * MLA_VP, j * ks:j * ks + kv], p_scr[h, j * ks:j * ks + kv, :],
                       preferred_element_type=F32)

    def step(x, c_sm, c_pv, c_qk, st):
        y = 1 - x
        m, acc, mc = st[x]
        m_new = jnp.maximum(m, mc)
        alpha = jnp.exp2(m - m_new)
        acc_y = st[y][1]
        mx_y = jnp.full((1, tq), NEG, F32)
        for j in range(nslab):
            acc_y = acc_y + pv_slab(y, c_pv, j)
            mx_y = qk_slab(y, c_qk, j, mx_y)
            p_scr[x, j * ks:(j + 1) * ks, :] = jnp.exp2(s_scr[x, j * ks:(j + 1) * ks, :] - m_new).astype(BF16)
        new = [None, None]
        new[x] = (m_new, alpha * acc, mc)
        new[y] = (st[y][0], acc_y, mx_y)
        return tuple(new)

    def body(c, st):
        st = step(0, c, jnp.maximum(c - 1, 0), c, st)
        return step(1, c, c, jnp.minimum(c + 1, nk - 1), st)

    neg = jnp.full((1, tq), NEG, F32)
    acc0 = jnp.zeros((MLA_VP, tq), F32)
    p_scr[1] = jnp.zeros(p_scr.shape[1:], BF16)
    mx0 = neg
    for j in range(nslab):
        mx0 = qk_slab(0, 0, j, mx0)
    def trip(i, st):
        for u in range(MLA_UNROLL):
            st = body(i * MLA_UNROLL + u, st)
        return st

    st = lax.fori_loop(0, nk // MLA_UNROLL, trip, ((neg, acc0, mx0), (neg, acc0, neg)))
    for c in range(nk - nk % MLA_UNROLL, nk):
        st = body(jnp.int32(c), st)
    acc1 = st[1][1]
    for j in range(nslab):
        acc1 = acc1 + pv_slab(1, nk - 1, j)
    out_t = jnp.concatenate([a[:MLA_V] / a[MLA_V:MLA_V + 1] for a in (st[0][1], acc1)], axis=0)
    o_ref[...] = out_t.T.astype(o_ref.dtype)


MLA_TQ = (512, 256, 128)
MLA_TK = (768, 512, 256, 128)
MLA_KS = 128
MLA_KS_QK = 768
MLA_KS_PV = 256
MLA_UNROLL = 11


def _mla_attention(q, row0, Lq, k, v):
    B, Nk = k.shape[0], k.shape[1]
    tq = _pick(Lq, MLA_TQ)
    tk = _pick(Nk, MLA_TK)
    nk = Nk // tk
    hp = MLA_HEADS // 2
    vt = v.reshape(B, nk, tk, hp, 2, MLA_V).transpose(0, 3, 1, 4, 5, 2)
    vt = jnp.concatenate([vt, jnp.ones((B, hp, nk, 2, MLA_VP - MLA_V, tk), v.dtype)], axis=4)
    vt = vt.reshape(B, hp, nk, 2 * MLA_VP, tk)
    kern = functools.partial(_mla_attn_kernel, tk=tk, nk=nk, ks=math.gcd(tk, MLA_KS))
    return pl.pallas_call(
        kern,
        grid=(B, hp, Lq // tq),
        in_specs=[pl.BlockSpec((tq, 2 * HEAD_PAD), lambda b, h, i: ((row0 + b * Lq) // tq + i, h)),
                  pl.BlockSpec((None, Nk, 2 * HEAD_PAD), lambda b, h, i: (b, 0, h)),
                  pl.BlockSpec((None, None, nk, 2 * MLA_VP, tk), lambda b, h, i: (b, h, 0, 0, 0))],
        out_specs=pl.BlockSpec((None, tq, 2 * MLA_V), lambda b, h, i: (b, i, h)),
        out_shape=jax.ShapeDtypeStruct((B, Lq, MLA_HEADS * MLA_V), BF16),
        scratch_shapes=[pltpu.VMEM((2, tk, tq), F32), pltpu.VMEM((2, tk, tq), BF16)],
        compiler_params=_cparams(("arbitrary", "arbitrary", "arbitrary")),
    )(q, k, vt)


def _gqa_kernel(sink_ref, q_ref, kc_ref, vct_ref, *rest, L, has_band):
    group = GQA_HEADS // GQA_KV
    gw = group * BLK
    if has_band:
        kp_ref, k_ref, kn_ref, vtp_ref, vt_ref, vtn_ref, bias_ref, o_ref = rest
        keys = jnp.concatenate([kp_ref[...], k_ref[...], kn_ref[...], kc_ref[...]], axis=0)
        vals_t = jnp.concatenate([vtp_ref[...], vt_ref[...], vtn_ref[...], vct_ref[...]], axis=1)
    else:
        (o_ref,) = rest
        keys, vals_t = kc_ref[...], vct_ref[...]
    dn = (((1,), (1,)), ((), ()))
    G = range(GQA_KV)
    qs = [jnp.concatenate([q_ref[:, (kh * group + g) * GQA_HD:(kh * group + g + 1) * GQA_HD] for g in range(group)],
                          axis=0) for kh in G]
    s = [lax.dot_general(keys[:, kh * GQA_HD:(kh + 1) * GQA_HD], qs[kh], dn, preferred_element_type=F32) for kh in G]
    if has_band:
        s = [x + bias_ref[...] for x in s]
    sink = [sink_ref[:, kh * gw:(kh + 1) * gw] for kh in G]
    m = [jnp.maximum(jnp.max(s[kh], axis=0, keepdims=True), sink[kh]) for kh in G]
    p = [jnp.exp2(s[kh] - m[kh]) for kh in G]
    den = [jnp.sum(p[kh], axis=0, keepdims=True) + jnp.exp2(sink[kh] - m[kh]) for kh in G]
    ot = [jnp.dot(vals_t[kh * GQA_HD:(kh + 1) * GQA_HD, :], p[kh].astype(BF16), preferred_element_type=F32)
          * (1.0 / den[kh]) for kh in G]
    o = jnp.concatenate(ot, axis=0).T
    o_ref[...] = jnp.concatenate([o[g * BLK:(g + 1) * BLK, kh * GQA_HD:(kh + 1) * GQA_HD]
                                  for kh in G for g in range(group)], axis=1).astype(o_ref.dtype)


def _gqa_attention(sink, q, row0, Lq, kc, vc, k=None, v=None):
    B, CTX = kc.shape[0], kc.shape[1]
    has_band = k is not None
    kw = GQA_KV * GQA_HD
    nb = Lq // BLK
    sink_row = jnp.repeat(sink.astype(F32) * LOG2E, BLK)[None, :]
    in_specs = [pl.BlockSpec((1, GQA_HEADS * BLK), lambda b, i: (0, 0)),
                pl.BlockSpec((BLK, GQA_HEADS * GQA_HD), lambda b, i: ((row0 + b * Lq) // BLK + i, 0)),
                pl.BlockSpec((None, CTX, kw), lambda b, i: (b, 0, 0)),
                pl.BlockSpec((None, kw, CTX), lambda b, i: (b, 0, 0))]
    args = [sink_row, q, kc, jnp.swapaxes(vc, 1, 2)]
    if has_band:
        prev = lambda i: jnp.maximum(i - 1, 0)
        nxt = lambda i: jnp.minimum(i + 1, nb - 1)
        in_specs += [pl.BlockSpec((None, BLK, kw), lambda b, i: (b, prev(i), 0)),
                     pl.BlockSpec((None, BLK, kw), lambda b, i: (b, i, 0)),
                     pl.BlockSpec((None, BLK, kw), lambda b, i: (b, nxt(i), 0)),
                     pl.BlockSpec((None, kw, BLK), lambda b, i: (b, 0, prev(i))),
                     pl.BlockSpec((None, kw, BLK), lambda b, i: (b, 0, i)),
                     pl.BlockSpec((None, kw, BLK), lambda b, i: (b, 0, nxt(i)))]
        vt = jnp.swapaxes(v, 1, 2)
        r = jnp.arange(3 * BLK + CTX)[:, None]
        c = jnp.arange(GQA_HEADS // GQA_KV * BLK)[None, :] % BLK
        band = (jnp.abs(r - BLK - c) <= WINDOW) | (r >= 3 * BLK)
        ok = jnp.stack([band & (r >= BLK), band, band & ((r < 2 * BLK) | (r >= 3 * BLK))])
        bias = jnp.where(ok, 0.0, NEG).astype(F32)
        in_specs += [pl.BlockSpec((None,) + bias.shape[1:],
                                  lambda b, i: (jnp.where(i == 0, 0, jnp.where(i == nb - 1, 2, 1)), 0, 0))]
        args += [k, k, k, vt, vt, vt, bias]
    kern = functools.partial(_gqa_kernel, L=Lq, has_band=has_band)
    return pl.pallas_call(
        kern,
        grid=(B, nb),
        in_specs=in_specs,
        out_specs=pl.BlockSpec((None, BLK, GQA_HEADS * GQA_HD), lambda b, i: (b, i, 0)),
        out_shape=jax.ShapeDtypeStruct((B, Lq, GQA_HEADS * GQA_HD), BF16),
        compiler_params=_cparams(("arbitrary", "arbitrary")),
    )(*args)


def _odd_out_kernel(al_ref, ac_ref, gl_ref, gc_ref, h_ref, mod_ref, wo_ref, o_ref, *, n_lat):
    half = al_ref.shape[1]
    is_lat = pl.program_id(0) < n_lat
    a = jnp.where(is_lat, al_ref[...], ac_ref[...])
    g = jnp.where(is_lat, gl_ref[...], gc_ref[...])
    ol = (jnp.dot(a, wo_ref[:half, :], preferred_element_type=F32)
          + jnp.dot(g, wo_ref[half:, :], preferred_element_type=F32))
    o_ref[...] = h_ref[...] + mod_ref[2:3, :] * ol


def _odd_out(tok, mla_l, mla_c, gqa_l, gqa_c, h, mods_l, w_out):
    tm = tok.tm
    half = mla_l.shape[1]
    lat = pl.BlockSpec((tm, half), lambda i: (jnp.minimum(i, tok.n_lat - 1), 0))
    ctx = pl.BlockSpec((tm, half), lambda i: (jnp.maximum(i - tok.n_lat, 0), 0))
    return pl.pallas_call(
        functools.partial(_odd_out_kernel, n_lat=tok.n_lat),
        grid=(tok.n_all,),
        in_specs=[lat, ctx, lat, ctx,
                  pl.BlockSpec((tm, D), lambda i: (i, 0)),
                  pl.BlockSpec((None, N_MOD, D), lambda i: (tok.mod_row(i), 0, 0)),
                  pl.BlockSpec((D, D), lambda i: (0, 0))],
        out_specs=pl.BlockSpec((tm, D), lambda i: (i, 0)),
        out_shape=jax.ShapeDtypeStruct((tok.T, D), F32),
        compiler_params=_cparams(("arbitrary",)),
    )(mla_l, mla_c, gqa_l, gqa_c, h, mods_l, w_out)


MOE_TR = 512
MOE_TF = 1792
MOE_IDX_BLK = 1024
MOE_NF = EXP_FF // MOE_TF
ROUTE_W = 8


def _router_kernel(h_ref, mod_ref, g_ref, r_ref, y_ref, route_ref):
    y = _ada_norm(h_ref[...], g_ref[...], mod_ref[...], 3, 4)
    y_ref[...] = y
    logits = jnp.dot(y, r_ref[...], precision=HI, preferred_element_type=F32)
    lane = lax.broadcasted_iota(jnp.int32, logits.shape, 1)
    lg = jnp.where(lane < N_EXP, logits, -jnp.inf)
    m1 = jnp.max(lg, axis=-1, keepdims=True)
    i1 = jnp.min(jnp.where(lg == m1, lane, LANE), axis=-1, keepdims=True)
    lg2 = jnp.where(lane == i1, -jnp.inf, lg)
    m2 = jnp.max(lg2, axis=-1, keepdims=True)
    i2 = jnp.min(jnp.where(lg2 == m2, lane, LANE), axis=-1, keepdims=True)
    e = jnp.exp(m2 - m1)
    w1 = 1.0 / (1.0 + e)
    route = (jnp.where(lane == 0, w1, 0.0) + jnp.where(lane == 1, e * w1, 0.0)
             + jnp.where(lane == 2, i1.astype(F32), 0.0) + jnp.where(lane == 3, i2.astype(F32), 0.0))
    route_ref[...] = route[:, :ROUTE_W]


def _moe_router(tok, h, mods_l, gain, router):
    tm = tok.tm
    rp = jnp.pad(router, ((0, 0), (0, LANE - N_EXP)))
    return pl.pallas_call(
        _router_kernel,
        grid=(tok.n_all,),
        in_specs=[pl.BlockSpec((tm, D), lambda i: (i, 0)),
                  pl.BlockSpec((None, N_MOD, D), lambda i: (tok.mod_row(i), 0, 0)),
                  pl.BlockSpec((1, D), lambda i: (0, 0)),
                  pl.BlockSpec((D, LANE), lambda i: (0, 0))],
        out_specs=[pl.BlockSpec((tm, D), lambda i: (i, 0)), pl.BlockSpec((tm, ROUTE_W), lambda i: (i, 0))],
        out_shape=[jax.ShapeDtypeStruct((tok.T, D), F32), jax.ShapeDtypeStruct((tok.T, ROUTE_W), F32)],
        compiler_params=_cparams(("arbitrary",)),
    )(h, mods_l, gain.reshape(1, D), rp)


def _moe_plan(route, tr):
    T = route.shape[0]
    flat = route[:, 2:4].astype(jnp.int32).reshape(-1)
    onehot = (flat[:, None] == jnp.arange(N_EXP, dtype=jnp.int32)[None, :]).astype(jnp.int32)
    csum = jnp.cumsum(onehot, axis=0)
    rank = jnp.sum((csum - onehot) * onehot, axis=1)
    padded = (csum[-1] + tr - 1) // tr * tr
    ends = jnp.cumsum(padded)
    pos = (ends - padded)[flat] + rank
    tpb = MOE_IDX_BLK // tr
    n_tiles = -(-((2 * T + N_EXP * (tr - 1)) // tr) // tpb) * tpb
    src = jnp.zeros((n_tiles * tr,), jnp.int32).at[pos].set(jnp.arange(2 * T, dtype=jnp.int32) // 2,
                                                            unique_indices=True)
    starts = jnp.arange(n_tiles, dtype=jnp.int32) * tr
    tile_expert = jnp.minimum(jnp.sum((starts[:, None] >= ends[None, :]).astype(jnp.int32), axis=1), N_EXP - 1)
    n_valid = (ends[-1] // tr).astype(jnp.int32).reshape(1)
    return src, tile_expert.astype(jnp.int32), n_valid, pos.reshape(T, 2)


def _gather_rows(idx_ref, src_hbm, dst, sem, n):
    def issue(r, carry):
        pltpu.make_async_copy(src_hbm.at[pl.ds(idx_ref[r], 1), :], dst.at[pl.ds(r, 1), :], sem).start()
        return carry

    lax.fori_loop(0, n, issue, 0, unroll=8)


def _wait_rows(src_hbm, dst, sem, n):
    pltpu.make_async_copy(src_hbm.at[pl.ds(0, n), :], dst, sem).wait()


def _moe_expert_kernel(te_ref, nv_ref, idx_ref, idxn_ref, y_hbm, wg_ref, wu_ref, wd_ref, o_ref,
                       xbuf, y_scr, acc_scr, sem, *, tr):
    t, f = pl.program_id(0), pl.program_id(1)
    nf = pl.num_programs(1)
    nv = nv_ref[0]
    valid = t < nv
    slot = lax.rem(t, 2)
    per_f = tr // MOE_NF
    tpb = MOE_IDX_BLK // tr
    nxt_tile = jnp.minimum(t + 1, jnp.maximum(nv - 1, 0))

    @pl.when((f == 0) & (t == 0))
    def _():
        _gather_rows(idx_ref, y_hbm, xbuf.at[0], sem.at[0], tr)

    @pl.when((f == 0) & valid)
    def _():
        _wait_rows(y_hbm, xbuf.at[slot], sem.at[slot], tr)
        y_scr[...] = xbuf[slot].astype(BF16)
        acc_scr[...] = jnp.zeros_like(acc_scr)

    @pl.when(valid)
    def _():
        base = f * per_f
        ibase = lax.rem(nxt_tile, tpb) * tr + base
        nxt = xbuf.at[1 - slot]
        for r in range(per_f):
            pltpu.make_async_copy(y_hbm.at[pl.ds(idxn_ref[ibase + r], 1), :], nxt.at[pl.ds(base + r, 1), :],
                                  sem.at[1 - slot]).start()
        y = y_scr[...]
        gate = jnp.dot(y, wg_ref[...], preferred_element_type=F32)
        up = jnp.dot(y, wu_ref[...], preferred_element_type=F32)
        act = (gate * jax.nn.sigmoid(gate) * up).astype(BF16)
        acc_scr[...] += jnp.dot(act, wd_ref[...], preferred_element_type=F32)

    @pl.when((f == nf - 1) & (t == nv - 1))
    def _():
        _wait_rows(y_hbm, xbuf.at[1 - slot], sem.at[1 - slot], tr)

    @pl.when(f == nf - 1)
    def _():
        o_ref[...] = jnp.where(valid, acc_scr[...], 0.0)


def _moe_experts(y, src, tile_expert, n_valid, wg, wu, wd, tr):
    n_tiles = src.shape[0] // tr
    tf = MOE_TF
    tpb = MOE_IDX_BLK // tr
    kern = functools.partial(_moe_expert_kernel, tr=tr)
    smem = functools.partial(pl.BlockSpec, memory_space=pltpu.SMEM)
    grid_spec = pltpu.PrefetchScalarGridSpec(
        num_scalar_prefetch=2,
        grid=(n_tiles, EXP_FF // tf),
        in_specs=[smem((MOE_IDX_BLK,), lambda t, f, te, nv: (t // tpb,)),
                  smem((MOE_IDX_BLK,), lambda t, f, te, nv: (jnp.minimum(t + 1, jnp.maximum(nv[0] - 1, 0)) // tpb,)),
                  pl.BlockSpec(memory_space=pl.ANY),
                  pl.BlockSpec((None, D, tf), lambda t, f, te, nv: (te[t], 0, f)),
                  pl.BlockSpec((None, D, tf), lambda t, f, te, nv: (te[t], 0, f)),
                  pl.BlockSpec((None, tf, D), lambda t, f, te, nv: (te[t], f, 0))],
        out_specs=pl.BlockSpec((tr, D), lambda t, f, te, nv: (t, 0)),
        scratch_shapes=[pltpu.VMEM((2, tr, D), F32), pltpu.VMEM((tr, D), BF16), pltpu.VMEM((tr, D), F32),
                        pltpu.SemaphoreType.DMA((2,))])
    return pl.pallas_call(
        kern,
        grid_spec=grid_spec,
        out_shape=jax.ShapeDtypeStruct((n_tiles * tr, D), F32),
        compiler_params=_cparams(("arbitrary", "arbitrary")),
    )(tile_expert, n_valid, src, src, y, wg, wu, wd)


def _moe_combine_kernel(idx_ref, idxn_ref, o_hbm, h_ref, route_ref, mod_ref, out_ref, buf, sem, *, tm):
    i = pl.program_id(0)
    n = pl.num_programs(0)
    slot = lax.rem(i, 2)

    @pl.when(i == 0)
    def _():
        _gather_rows(idx_ref, o_hbm, buf.at[0], sem.at[0], 2 * tm)

    nxt = buf.at[1 - slot]
    for r in range(2 * tm):
        pltpu.make_async_copy(o_hbm.at[pl.ds(idxn_ref[r], 1), :], nxt.at[pl.ds(r, 1), :],
                              sem.at[1 - slot]).start(priority=r % 2)
    _wait_rows(o_hbm, buf.at[slot], sem.at[slot], 2 * tm)
    r = route_ref[...]
    mix = r[:, 0:1] * buf[slot, :tm, :] + r[:, 1:2] * buf[slot, tm:, :]
    out_ref[...] = h_ref[...] + mod_ref[5:6, :] * mix

    @pl.when(i == n - 1)
    def _():
        _wait_rows(o_hbm, nxt, sem.at[1 - slot], 2 * tm)


def _moe_combine(tok, o_sorted, pos, h, route, mods_l):
    tm = tok.tm
    n = tok.n_all
    idx = pos.reshape(n, tm, 2).transpose(0, 2, 1).reshape(-1)
    kern = functools.partial(_moe_combine_kernel, tm=tm)
    smem = functools.partial(pl.BlockSpec, memory_space=pltpu.SMEM)
    return pl.pallas_call(
        kern,
        grid=(n,),
        in_specs=[smem((2 * tm,), lambda i: (i,)),
                  smem((2 * tm,), lambda i: (jnp.minimum(i + 1, n - 1),)),
                  pl.BlockSpec(memory_space=pl.ANY),
                  pl.BlockSpec((tm, D), lambda i: (i, 0)),
                  pl.BlockSpec((tm, ROUTE_W), lambda i: (i, 0)),
                  pl.BlockSpec((None, N_MOD, D), lambda i: (tok.mod_row(i), 0, 0))],
        out_specs=pl.BlockSpec((tm, D), lambda i: (i, 0)),
        out_shape=jax.ShapeDtypeStruct((tok.T, D), F32),
        scratch_shapes=[pltpu.VMEM((2, 2 * tm, D), F32), pltpu.SemaphoreType.DMA((2,))],
        compiler_params=_cparams(("arbitrary",)),
    )(idx, idx, o_sorted, h, route, mods_l)


def _moe(tok, h, mods_l, gain, router, wg, wu, wd):
    y, route = _moe_router(tok, h, mods_l, gain, router)
    src, tile_expert, n_valid, pos = _moe_plan(route, MOE_TR)
    o_sorted = _moe_experts(y, src, tile_expert, n_valid, wg, wu, wd, MOE_TR)
    return _moe_combine(tok, o_sorted, pos, h, route, mods_l)


def _odd_layer(tok, h, mods_l, tabs, p):
    B, L, CTX = tok.B, tok.L, tok.CTX
    nl = B * L
    w = p['w_in']
    w_in = jnp.concatenate([w[:, :Q_LORA + KV_LORA], w[:, 416:1184], w[:, 384:416],
                            jnp.zeros((D, OD_IN_PAD - 1184), w.dtype)], axis=1).astype(BF16)
    z = _norm_matmul(tok, h, mods_l, p['norm_mix'], w_in, OD_IN_PAD)
    q, k, v, gq, gk, gv = _odd_proj(tok, z, tabs, p['q_norm'], p['kv_norm'], p['w_uq'], p['w_ukv'])
    lat = lambda a: a[:nl].reshape(B, L, -1)
    ctx = lambda a: a[nl:].reshape(B, CTX, -1)
    cat = lambda a: jnp.concatenate([ctx(a), lat(a)], axis=1)
    mla_l = _mla_attention(q, 0, L, cat(k), cat(v))
    gqa_l = _gqa_attention(p['sink'], gq, 0, L, ctx(gk), ctx(gv), lat(gk), lat(gv))
    if p['need_ctx']:
        mla_c = _mla_attention(q, nl, CTX, ctx(k), ctx(v))
        gqa_c = _gqa_attention(p['sink'], gq, nl, CTX, ctx(gk), ctx(gv))
    else:
        mla_c = jnp.zeros((B, CTX, MLA_HEADS * MLA_V), BF16)
        gqa_c = jnp.zeros((B, CTX, GQA_HEADS * GQA_HD), BF16)
    rows = lambda a: a.reshape(-1, a.shape[-1])
    h = _odd_out(tok, rows(mla_l), rows(mla_c), rows(gqa_l), rows(gqa_c), h, mods_l, p['w_out'].astype(BF16))
    return _moe(tok, h, mods_l, p['norm_ffn'], p['router'], _to_bf16(*p['moe_wg']), _to_bf16(*p['moe_wu']),
                _to_bf16(*p['moe_wd']))


def _final_norm_kernel(h_ref, g_ref, o_ref):
    o_ref[...] = _rms(h_ref[...], g_ref[...])


def _final_norm(tok, h, gain):
    tm = tok.tm
    return pl.pallas_call(
        _final_norm_kernel,
        grid=(tok.n_lat,),
        in_specs=[pl.BlockSpec((tm, D), lambda i: (i, 0)), pl.BlockSpec((1, D), lambda i: (0, 0))],
        out_specs=pl.BlockSpec((tm, D), lambda i: (i, 0)),
        out_shape=jax.ShapeDtypeStruct((tok.B * tok.L, D), F32),
        compiler_params=_cparams(("arbitrary",)),
    )(h, gain.reshape(1, D))


def kernel(x, c, ctx, c_ctx, mod_w, mod_b, norm_mix, norm_ffn, final_norm,
           ev_w_in, ev_conv_w, ev_conv_b, hy_w1, hy_b1, hy_w2, hy_b2, hy_w3, hy_freq, hy_decay, hy_bias,
           s5_a_re, s5_a_im, s5_log_dt, s5_b_re, s5_b_im, s5_c_re, s5_c_im, s5_d, s5_w_glu, ev_w_out,
           ff_w_gate, ff_w_up, ff_w_down,
           od_w_in, mla_q_norm, mla_w_uq, mla_kv_norm, mla_w_ukv, gqa_sink, od_w_out,
           moe_router, moe_w_gate, moe_w_up, moe_w_down):
    B, L, _ = x.shape
    CTX = ctx.shape[1]
    tok = _Tok(B, L, CTX, _pick(math.gcd(L, B * CTX), (512, 256, 128)))
    cond_t = jnp.concatenate([c, c_ctx[None, :], jnp.zeros((8 - B - 1, D), F32)], axis=0).T
    mods = _modulations(cond_t, B + 1, mod_w, mod_b)
    tabs = _rope_tables(L, tok.tm)
    h = jnp.concatenate([x.reshape(B * L, D), ctx.reshape(B * CTX, D)], axis=0)
    for l in range(DEPTH):
        i = l // 2
        need_ctx = l < DEPTH - 1
        if l % 2 == 0:
            p = dict(norm_mix=norm_mix[l], norm_ffn=norm_ffn[l], w_in=ev_w_in[i], conv_w=ev_conv_w[i],
                     conv_b=ev_conv_b[i],
                     hy=(hy_w1[i], hy_b1[i], hy_w2[i], hy_b2[i], hy_w3[i], hy_freq[i], hy_decay[i]),
                     hy_bias=hy_bias[i],
                     s5=(s5_a_re[i], s5_a_im[i], s5_log_dt[i], s5_b_re[i], s5_b_im[i], s5_c_re[i], s5_c_im[i]),
                     s5_d=s5_d[i], s5_w_glu=s5_w_glu[i], w_out=ev_w_out[i],
                     ff_wg=(ff_w_gate, i), ff_wu=(ff_w_up, i), ff_wd=(ff_w_down, i), need_ctx=need_ctx)
            h = _even_layer(tok, h, mods[l], p)
        else:
            p = dict(norm_mix=norm_mix[l], norm_ffn=norm_ffn[l], w_in=od_w_in[i], q_norm=mla_q_norm[i],
                     w_uq=mla_w_uq[i], kv_norm=mla_kv_norm[i], w_ukv=mla_w_ukv[i], sink=gqa_sink[i],
                     w_out=od_w_out[i], router=moe_router[i], moe_wg=(moe_w_gate, i), moe_wu=(moe_w_up, i),
                     moe_wd=(moe_w_down, i), need_ctx=need_ctx)
            h = _odd_layer(tok, h, mods[l], tabs, p)
    return _final_norm(tok, h, final_norm).reshape(B, L, D)
```
